```python
import math
import jax
import jax.numpy as jnp
from jax import lax
import numpy as np

D_MODEL = 1024
BATCH = 16
SEQ = 2048
DEPTH = 1

N_HEADS_A = 8
HEAD_DIM = 64
WIDTH_A = N_HEADS_A * HEAD_DIM
KV_RANK = 128
IDX_HEADS = 8
IDX_DIM = 64
TOPK_MAX = 256
N_HEADS_B = 8
WIDTH_B = N_HEADS_B * HEAD_DIM
BLOCK_Q = 128
N_BUCKETS = 32
MAX_EXACT = N_BUCKETS // 2
MAX_DISTANCE = 128
ATTN_SCALE = HEAD_DIM ** -0.5
IDX_SCALE = (IDX_HEADS ** -0.5) * (IDX_DIM ** -0.5)
N_GROUPS = 4
EXPERTS_PER_GROUP = 8
N_EXPERTS = N_GROUPS * EXPERTS_PER_GROUP
D_EXPERT = 256
TOP_K_INNER = 2
PLE_DIM = 256
EPS = 1e-6

_COLS = [
    ("q_a", WIDTH_A),
    ("c_kv", KV_RANK),
    ("q_idx", IDX_HEADS * IDX_DIM),
    ("k_idx", IDX_DIM),
    ("w_idx", IDX_HEADS),
    ("qkv_b", 3 * WIDTH_B),
    ("gate_a", D_MODEL),
    ("gate_b", D_MODEL),
]
IN_COLS = sum(c for _, c in _COLS)

kernel_name = "hybrid_dsa_stickbreak_hmoe_block"


def _rmsnorm(x, g):
    xf = x.astype(jnp.float32)
    y = xf * lax.rsqrt(jnp.mean(xf * xf, axis=-1, keepdims=True) + EPS)
    return (y * g.astype(jnp.float32)).astype(x.dtype)


def _split_cols(proj):
    out = []
    start = 0
    for _, width in _COLS:
        out.append(proj[..., start:start + width])
        start += width
    return out


def _t5_bucket(dist):
    dist = jnp.maximum(dist, 0)
    d_f = jnp.maximum(dist, 1).astype(jnp.float32)
    large = MAX_EXACT + (jnp.log(d_f / MAX_EXACT) / math.log(MAX_DISTANCE / MAX_EXACT)
                         * (N_BUCKETS - MAX_EXACT)).astype(jnp.int32)
    large = jnp.minimum(large, N_BUCKETS - 1)
    return jnp.where(dist < MAX_EXACT, dist, large)


def _dsa_branch(q, c_kv, q_idx, k_idx, w_idx, w_uk, w_uv, rel_bias):
    b, s = q.shape[0], q.shape[1]
    n_sel = min(TOPK_MAX, s // 4)
    n_blocks = s // BLOCK_Q
    q_abs = jnp.einsum("bshd,hcd->bshc", q, w_uk)
    key_pos = jnp.arange(s, dtype=jnp.int32)
    gather = jax.vmap(lambda c, i: c[i])

    def block(i):
        start = i * BLOCK_Q
        t = start + jnp.arange(BLOCK_Q, dtype=jnp.int32)
        qa = lax.dynamic_slice_in_dim(q_abs, start, BLOCK_Q, axis=1)
        qi = lax.dynamic_slice_in_dim(q_idx, start, BLOCK_Q, axis=1)
        wi = lax.dynamic_slice_in_dim(w_idx, start, BLOCK_Q, axis=1).astype(jnp.float32)
        rel = jax.nn.relu(jnp.einsum("bqhe,bse->bqhs", qi, k_idx).astype(jnp.float32))
        score = jnp.einsum("bqhs,bqh->bqs", rel, wi)
        causal = key_pos[None, :] <= t[:, None]
        score = jnp.where(causal[None], score, -jnp.inf)
        _, idx = lax.top_k(score, n_sel)
        valid = idx <= t[None, :, None]
        c_sel = gather(c_kv, idx)
        logits = jnp.einsum("bqhc,bqkc->bhqk", qa, c_sel).astype(jnp.float32) * ATTN_SCALE
        bias = rel_bias[_t5_bucket(t[None, :, None] - idx)]
        logits = logits + jnp.transpose(bias, (0, 3, 1, 2)).astype(jnp.float32)
        logits = jnp.where(valid[:, None], logits, -jnp.inf)
        probs = jax.nn.softmax(logits, axis=-1).astype(c_sel.dtype)
        return jnp.einsum("bhqk,bqkc->bqhc", probs, c_sel)

    o = lax.map(block, jnp.arange(n_blocks, dtype=jnp.int32))
    o = jnp.moveaxis(o, 0, 1).reshape(b, s, N_HEADS_A, KV_RANK)
    return jnp.einsum("bshc,hcd->bshd", o, w_uv)


def _stick_breaking_branch(q, k, v):
    b, s = q.shape[0], q.shape[1]
    n_blocks = s // BLOCK_Q
    key_pos = jnp.arange(s, dtype=jnp.int32)

    def block(i):
        start = i * BLOCK_Q
        t = start + jnp.arange(BLOCK_Q, dtype=jnp.int32)
        qb = lax.dynamic_slice_in_dim(q, start, BLOCK_Q, axis=1)
        z = jnp.einsum("bqhd,bshd->bhqs", qb, k).astype(jnp.float32) * ATTN_SCALE
        mask = (key_pos[None, :] < t[:, None])[None, None]
        log_1m = jnp.where(mask, jax.nn.log_sigmoid(-z), 0.0)
        after = lax.cumsum(log_1m, axis=3, reverse=True) - log_1m
        log_a = jax.nn.log_sigmoid(z) + after
        a = jnp.where(mask, jnp.exp(log_a), 0.0).astype(v.dtype)
        return jnp.einsum("bhqs,bshd->bqhd", a, v)

    o = lax.map(block, jnp.arange(n_blocks, dtype=jnp.int32))
    return jnp.moveaxis(o, 0, 1).reshape(b, s, N_HEADS_B, HEAD_DIM)


def _hier_moe(h, w_r1, b_r1, w_r2, b_r2, w_gate, w_up, w_down):
    b, s, d = h.shape
    hf = h.reshape(b * s, d)
    n_tok = b * s
    g_logits = jnp.matmul(hf, w_r1).astype(jnp.float32) + b_r1.astype(jnp.float32)
    g_prob = jax.nn.softmax(g_logits, axis=-1)
    g_sel = jnp.argmax(g_logits, axis=-1)
    p_g = jnp.take_along_axis(g_prob, g_sel[:, None], axis=1)[:, 0]
    e_logits = jnp.einsum("nd,gde->nge", hf, w_r2).astype(jnp.float32) + b_r2.astype(jnp.float32)
    e_sel = jnp.take_along_axis(e_logits, g_sel[:, None, None], axis=1)[:, 0]
    top_v, top_i = lax.top_k(e_sel, TOP_K_INNER)
    w_inner = jax.nn.softmax(top_v, axis=-1)
    inner = jnp.sum(jax.nn.one_hot(top_i, EXPERTS_PER_GROUP, dtype=jnp.float32) * w_inner[..., None], axis=1)
    comb = (jax.nn.one_hot(g_sel, N_GROUPS, dtype=jnp.float32)[:, :, None]
            * inner[:, None, :] * p_g[:, None, None])
    comb = comb.reshape(n_tok, N_EXPERTS).astype(h.dtype)
    out = jnp.zeros_like(hf)
    for e in range(N_EXPERTS):
        hid = jax.nn.silu(jnp.matmul(hf, w_gate[e])) * jnp.matmul(hf, w_up[e])
        out = out + comb[:, e:e + 1] * jnp.matmul(hid, w_down[e])
    return out.reshape(b, s, d)


def setup_inputs(seed: int = 0) -> dict:
    key = jax.random.key(seed)
    ks = jax.random.split(key, 24)
    f32 = jnp.float32

    def nrm(k, shape, fan_in):
        return jax.random.normal(k, shape, f32) * (fan_in ** -0.5)

    def gain(k, shape):
        return 1.0 + 0.05 * jax.random.normal(k, shape, f32)

    return {
        "x": jax.random.normal(ks[0], (BATCH, SEQ, D_MODEL), f32),
        "p": jax.random.normal(ks[1], (DEPTH, BATCH, SEQ, PLE_DIM), f32),
        "attn_norm": gain(ks[2], (DEPTH, D_MODEL)),
        "w_in": nrm(ks[3], (DEPTH, D_MODEL, IN_COLS), D_MODEL),
        "kv_norm": gain(ks[4], (DEPTH, KV_RANK)),
        "w_uk": nrm(ks[5], (DEPTH, N_HEADS_A, KV_RANK, HEAD_DIM), KV_RANK),
        "w_uv": nrm(ks[6], (DEPTH, N_HEADS_A, KV_RANK, HEAD_DIM), KV_RANK),
        "rel_bias": 0.5 * jax.random.normal(ks[7], (N_BUCKETS, N_HEADS_A), f32),
        "w_branch_a": nrm(ks[8], (DEPTH, WIDTH_A, D_MODEL), WIDTH_A),
        "w_branch_b": nrm(ks[9], (DEPTH, WIDTH_B, D_MODEL), WIDTH_B),
        "w_out": nrm(ks[10], (DEPTH, D_MODEL, D_MODEL), D_MODEL),
        "ffn_norm": gain(ks[11], (DEPTH, D_MODEL)),
        "w_r1": nrm(ks[12], (DEPTH, D_MODEL, N_GROUPS), D_MODEL),
        "b_r1": 0.01 * jax.random.normal(ks[13], (DEPTH, N_GROUPS), f32),
        "w_r2": nrm(ks[14], (DEPTH, N_GROUPS, D_MODEL, EXPERTS_PER_GROUP), D_MODEL),
        "b_r2": 0.01 * jax.random.normal(ks[15], (DEPTH, N_GROUPS, EXPERTS_PER_GROUP), f32),
        "w_gate": nrm(ks[16], (DEPTH, N_EXPERTS, D_MODEL, D_EXPERT), D_MODEL),
        "w_up": nrm(ks[17], (DEPTH, N_EXPERTS, D_MODEL, D_EXPERT), D_MODEL),
        "w_down": nrm(ks[18], (DEPTH, N_EXPERTS, D_EXPERT, D_MODEL), D_EXPERT),
        "ple_norm": gain(ks[19], (DEPTH, D_MODEL)),
        "w_ple_gate": nrm(ks[20], (DEPTH, D_MODEL, D_MODEL), D_MODEL),
        "w_ple": nrm(ks[21], (DEPTH, PLE_DIM, D_MODEL), PLE_DIM),
        "final_norm": gain(ks[22], (D_MODEL,)),
    }


def reference(x, p, attn_norm, w_in, kv_norm, w_uk, w_uv, rel_bias, w_branch_a, w_branch_b,
              w_out, ffn_norm, w_r1, b_r1, w_r2, b_r2, w_gate, w_up, w_down,
              ple_norm, w_ple_gate, w_ple, final_norm):
    b, s, _ = x.shape
    for i in range(DEPTH):
        h = _rmsnorm(x, attn_norm[i])
        proj = jnp.matmul(h, w_in[i])
        q_a, c_kv, q_idx, k_idx, w_idx, qkv_b, gate_a, gate_b = _split_cols(proj)
        q_a = q_a.reshape(b, s, N_HEADS_A, HEAD_DIM)
        c_kv = _rmsnorm(c_kv, kv_norm[i])
        q_idx = q_idx.reshape(b, s, IDX_HEADS, IDX_DIM)
        w_idx = w_idx * IDX_SCALE
        q_b, k_b, v_b = jnp.split(qkv_b.reshape(b, s, 3, N_HEADS_B, HEAD_DIM), 3, axis=2)
        o_a = _dsa_branch(q_a, c_kv, q_idx, k_idx, w_idx, w_uk[i], w_uv[i], rel_bias)
        o_b = _stick_breaking_branch(q_b[:, :, 0], k_b[:, :, 0], v_b[:, :, 0])
        y_a = jnp.matmul(o_a.reshape(b, s, WIDTH_A), w_branch_a[i])
        y_b = jnp.matmul(o_b.reshape(b, s, WIDTH_B), w_branch_b[i])
        merged = jax.nn.sigmoid(gate_a) * y_a + jax.nn.sigmoid(gate_b) * y_b
        x = x + jnp.matmul(merged, w_out[i])
        h2 = _rmsnorm(x, ffn_norm[i])
        x = x + _hier_moe(h2, w_r1[i], b_r1[i], w_r2[i], b_r2[i], w_gate[i], w_up[i], w_down[i])
        h3 = _rmsnorm(x, ple_norm[i])
        x = x + jnp.matmul(p[i], w_ple[i]) * jax.nn.sigmoid(jnp.matmul(h3, w_ple_gate[i]))
    return _rmsnorm(x, final_norm)
```

```python
import functools
import math

import jax
import jax.numpy as jnp
from jax import lax
from jax.experimental import pallas as pl
from jax.experimental.pallas import tpu as pltpu

D_MODEL = 1024
N_HEADS_A = 8
HEAD_DIM = 64
WIDTH_A = N_HEADS_A * HEAD_DIM
KV_RANK = 128
IDX_HEADS = 8
IDX_DIM = 64
TOPK_MAX = 256
N_HEADS_B = 8
WIDTH_B = N_HEADS_B * HEAD_DIM
N_BUCKETS = 32
MAX_EXACT = N_BUCKETS // 2
MAX_DISTANCE = 128
ATTN_SCALE = HEAD_DIM ** -0.5
IDX_SCALE = (IDX_HEADS ** -0.5) * (IDX_DIM ** -0.5)
N_GROUPS = 4
EXPERTS_PER_GROUP = 8
N_EXPERTS = N_GROUPS * EXPERTS_PER_GROUP
D_EXPERT = 256
PLE_DIM = 256
EPS = 1e-6

LANES = 128
ROW_TILE = 512
MOE_ROW_TILE = 1024
ATT_TILE = 256
VMEM_LIMIT = 56 * 1024 * 1024
NEG_BIG = -1e30
INT_MIN = -2 ** 31

F32 = jnp.float32
BF16 = jnp.bfloat16


def _rms(x, g):
    return x * lax.rsqrt(jnp.mean(x * x, axis=-1, keepdims=True) + EPS) * g


def _dot(a, b):
    return jnp.dot(a, b, preferred_element_type=F32)


def _dot_nt(a, b):
    return lax.dot_general(a, b, (((1,), (1,)), ((), ())), preferred_element_type=F32)


def _full_spec(shape):
    nd = len(shape)
    return pl.BlockSpec(shape, lambda *_: (0,) * nd)


def _proj_kernel(x_ref, g_ref, wqa_ref, wuk_ref, wckv_ref, kvg_ref, wqi_ref, wkk_ref, wwi_ref,
                 wqkv_ref, wga_ref, wgb_ref,
                 qabs_ref, ckv_ref, qidx_ref, kk_ref, widx_ref, qb_ref, kb_ref, vb_ref, sa_ref, sb_ref):
    h = _rms(x_ref[...], g_ref[...]).astype(BF16)
    qa = _dot(h, wqa_ref[...]).astype(BF16)
    qabs_ref[...] = (_dot(qa, wuk_ref[...]) * ATTN_SCALE).astype(BF16)
    c = _dot(h, wckv_ref[...])
    c = _rms(c, kvg_ref[...]).astype(BF16)
    ckv_ref[...] = jnp.concatenate([c, jnp.ones_like(c)], axis=1)
    qidx_ref[...] = _dot(h, wqi_ref[...]).astype(BF16)
    kk_ref[...] = _dot(h, wkk_ref[...]).astype(BF16)
    widx_ref[...] = _dot(h, wwi_ref[...]) * IDX_SCALE
    qkv = _dot(h, wqkv_ref[...])
    qb_ref[...] = (qkv[:, :WIDTH_B] * ATTN_SCALE).astype(BF16)
    kb_ref[...] = qkv[:, WIDTH_B:2 * WIDTH_B].astype(BF16)
    vb_ref[...] = qkv[:, 2 * WIDTH_B:].astype(BF16)
    sa_ref[...] = jax.nn.sigmoid(_dot(h, wga_ref[...])).astype(BF16)
    sb_ref[...] = jax.nn.sigmoid(_dot(h, wgb_ref[...])).astype(BF16)


def _proj(x, g, wqa, wuk, wckv, kvg, wqi, wkk, wwi, wqkv, wga, wgb):
    n, d = x.shape
    tm = ROW_TILE
    row = lambda w: pl.BlockSpec((tm, w), lambda i: (i, 0))
    ws = (g, wqa, wuk, wckv, kvg, wqi, wkk, wwi, wqkv, wga, wgb)
    outs = [(N_HEADS_A * KV_RANK, BF16), (2 * KV_RANK, BF16), (IDX_HEADS * IDX_DIM, BF16), (LANES, BF16),
            (LANES, F32), (WIDTH_B, BF16), (WIDTH_B, BF16), (WIDTH_B, BF16), (d, BF16), (d, BF16)]
    return pl.pallas_call(
        _proj_kernel,
        grid=(n // tm,),
        in_specs=[row(d)] + [_full_spec(w.shape) for w in ws],
        out_specs=[row(w) for w, _ in outs],
        out_shape=[jax.ShapeDtypeStruct((n, w), dt) for w, dt in outs],
        compiler_params=pltpu.CompilerParams(dimension_semantics=("arbitrary",), vmem_limit_bytes=VMEM_LIMIT),
        name="proj",
    )(x, *ws)


def _sortable_key(s):
    bits = lax.bitcast_convert_type(s, jnp.int32)
    key = bits ^ ((bits >> 31) & jnp.int32(0x7FFFFFFF))
    return jnp.where(key == -1, 0, key)


def _dsa_kernel(qabs_ref, qidx_ref, widx_ref, ckv_ref, kk_ref, bias_ref, o_ref,
                qa_s, qi_s, w_s, key_s, m_s, acc_s, *, n_sel, seq):
    tq = tk = ATT_TILE
    nh = N_HEADS_A
    qi = pl.program_id(1)
    n_kt = qi + 1

    lane = lax.broadcasted_iota(jnp.int32, (tq, LANES), 1)
    lo_half = jnp.where(lane < IDX_DIM, 1.0, 0.0)
    hi_half = 1.0 - lo_half
    for h in range(nh):
        qa_s[h * tq:(h + 1) * tq, :] = qabs_ref[:, h * KV_RANK:(h + 1) * KV_RANK]
        pair = qidx_ref[:, (h // 2) * LANES:(h // 2 + 1) * LANES].astype(F32)
        qi_s[h * tq:(h + 1) * tq, :] = (pair * (lo_half if h % 2 == 0 else hi_half)).astype(BF16)
        w_s[h * tq:(h + 1) * tq, :] = jnp.broadcast_to(widx_ref[:, h:h + 1], (tq, LANES))

    row = lax.broadcasted_iota(jnp.int32, (tq, tk), 0)
    col = lax.broadcasted_iota(jnp.int32, (tq, tk), 1)

    def idx_body(kj, carry):
        ks = pl.multiple_of(kj * tk, tk)
        r = _dot_nt(qi_s[...], kk_ref[pl.ds(ks, tk), :])
        w = w_s[...]
        r = jnp.maximum(r, 0.0) * jnp.concatenate([w] * (tk // LANES), axis=1)
        s = jnp.sum(r.reshape(nh, tq, tk), axis=0)
        key = _sortable_key(s)
        key_s[kj] = jnp.where((col > row) & (kj == qi), INT_MIN, key)
        return carry

    lax.fori_loop(0, n_kt, idx_body, 0)

    def count(pred):
        def body(kj, acc):
            c = jnp.where(pred(key_s[kj], col + kj * tk), 1.0, 0.0)
            for j in range(tk // LANES):
                acc = acc + c[:, j * LANES:(j + 1) * LANES]
            return acc
        acc = lax.fori_loop(0, n_kt, body, jnp.zeros((tq, LANES), F32))
        return jnp.sum(acc, axis=1, keepdims=True)

    def thr_body(i, tau_u):
        cand_u = tau_u | jnp.left_shift(jnp.int32(1), 31 - i)
        cand = cand_u ^ INT_MIN
        cnt = count(lambda k, c: k >= cand)
        return jnp.where(cnt >= n_sel, cand_u, tau_u)

    tau_u = lax.fori_loop(0, 32, thr_body, jnp.zeros((tq, 1), jnp.int32))
    first = qi == 0
    tau = jnp.where(first, INT_MIN, tau_u ^ INT_MIN)
    need = n_sel - count(lambda k, c: k > tau)

    def tie_body(i, d):
        cand = d | jnp.left_shift(jnp.int32(1), (seq.bit_length() - 1) - 1 - i)
        cnt = count(lambda k, c: (k == tau) & (c < cand))
        return jnp.where(cnt < need, cand, d)

    cstar = lax.fori_loop(0, seq.bit_length() - 1, tie_body, jnp.zeros((tq, 1), jnp.int32))
    cstar = jnp.where(first, -1, cstar)

    m_s[...] = jnp.full(m_s.shape, NEG_BIG, F32)
    acc_s[...] = jnp.zeros(acc_s.shape, F32)

    def att_body(kj, carry):
        ks = pl.multiple_of(kj * tk, tk)
        c_t = ckv_ref[pl.ds(ks, tk), :]
        k = key_s[kj]
        sel = (k > tau) | ((k == tau) & (col + kj * tk <= cstar))
        selb = jnp.where(sel, 0.0, NEG_BIG)
        s = _dot_nt(qa_s[...], c_t[:, :KV_RANK])
        s = s + bias_ref[jnp.minimum(qi - kj, 2)]
        s = (s.reshape(nh, tq, tk) + selb[None]).reshape(nh * tq, tk)
        m_old = m_s[...]
        m_new = jnp.maximum(m_old, jnp.max(s, axis=1, keepdims=True))
        alpha = jnp.exp(m_old - m_new)
        p = jnp.exp(s - jnp.concatenate([m_new] * (tk // LANES), axis=1)).astype(BF16)
        acc_s[...] = acc_s[...] * jnp.concatenate([alpha, alpha], axis=1) + _dot(p, c_t)
        m_s[...] = m_new
        return carry

    lax.fori_loop(0, n_kt, att_body, 0)

    acc = acc_s[...]
    o = acc[:, :KV_RANK] / acc[:, KV_RANK:]
    for h in range(nh):
        o_ref[:, h * KV_RANK:(h + 1) * KV_RANK] = o[h * tq:(h + 1) * tq].astype(BF16)


def _dsa(qabs, qidx, widx, ckv, kk, bias, batch, seq):
    n = qabs.shape[0]
    tq = ATT_TILE
    nq = seq // tq
    n_sel = min(TOPK_MAX, seq // 4)
    assert n_sel == tq and seq % tq == 0 and seq & (seq - 1) == 0
    qrow = lambda w: pl.BlockSpec((tq, w), lambda b, q: (b * nq + q, 0))
    brow = lambda w: pl.BlockSpec((seq, w), lambda b, q: (b, 0))
    nhq = N_HEADS_A * tq
    return pl.pallas_call(
        functools.partial(_dsa_kernel, n_sel=n_sel, seq=seq),
        grid=(batch, nq),
        in_specs=[qrow(N_HEADS_A * KV_RANK), qrow(IDX_HEADS * IDX_DIM), qrow(LANES),
                  brow(2 * KV_RANK), brow(LANES), _full_spec(bias.shape)],
        out_specs=qrow(N_HEADS_A * KV_RANK),
        out_shape=jax.ShapeDtypeStruct((n, N_HEADS_A * KV_RANK), BF16),
        scratch_shapes=[pltpu.VMEM((nhq, KV_RANK), BF16), pltpu.VMEM((nhq, LANES), BF16),
                        pltpu.VMEM((nhq, LANES), F32), pltpu.VMEM((nq, tq, tq), jnp.int32),
                        pltpu.VMEM((nhq, LANES), F32), pltpu.VMEM((nhq, 2 * KV_RANK), F32)],
        compiler_params=pltpu.CompilerParams(dimension_semantics=("arbitrary", "arbitrary"),
                                             vmem_limit_bytes=VMEM_LIMIT),
        name="dsa",
    )(qabs, qidx, widx, ckv, kk, bias)


def _stick_kernel(q_ref, k_ref, v_ref, o_ref):
    tq = tk = ATT_TILE
    qi = pl.program_id(2)
    q2 = q_ref[...].astype(F32)
    lane = lax.broadcasted_iota(jnp.int32, (tq, LANES), 1)
    row = lax.broadcasted_iota(jnp.int32, (tq, tk), 0)
    col = lax.broadcasted_iota(jnp.int32, (tq, tk), 1)
    causal = col < row
    upper = jnp.where(row > col, 1.0, 0.0).astype(BF16)

    def tile(qh, kj, r_sum, o, masked):
        ks = pl.multiple_of(kj * tk, tk)
        z = _dot_nt(qh, k_ref[pl.ds(ks, tk), :])
        sp = jnp.maximum(z, 0.0) + jnp.log(1.0 + jnp.exp(-jnp.abs(z)))
        log_1m = -sp
        if masked:
            log_1m = jnp.where(causal, log_1m, 0.0)
        after = _dot(log_1m.astype(BF16), upper) + r_sum
        a = jnp.exp((z - sp) + after)
        if masked:
            a = jnp.where(causal, a, 0.0)
        o = o + _dot(a.astype(BF16), v_ref[pl.ds(ks, tk), :])
        r_sum = r_sum + jnp.sum(log_1m, axis=1, keepdims=True)
        return r_sum, o

    outs = []
    for hh in range(2):
        keep = jnp.where(lane < HEAD_DIM, 1.0, 0.0)
        if hh == 1:
            keep = 1.0 - keep
        qh = (q2 * keep).astype(BF16)
        carry = tile(qh, qi, jnp.zeros((tq, 1), F32), jnp.zeros((tq, LANES), F32), True)
        carry = lax.fori_loop(0, qi, lambda it, c: tile(qh, qi - 1 - it, c[0], c[1], False), carry)
        outs.append(carry[1])
    o_ref[...] = jnp.where(lane < HEAD_DIM, outs[0], outs[1]).astype(BF16)


def _stick(qb, kb, vb, batch, seq):
    n = qb.shape[0]
    tq = ATT_TILE
    nq = seq // tq
    npair = WIDTH_B // LANES
    qspec = pl.BlockSpec((tq, LANES), lambda b, hp, q: (b * nq + q, hp))
    kspec = pl.BlockSpec((seq, LANES), lambda b, hp, q: (b, hp))
    return pl.pallas_call(
        _stick_kernel,
        grid=(batch, npair, nq),
        in_specs=[qspec, kspec, kspec],
        out_specs=qspec,
        out_shape=jax.ShapeDtypeStruct((n, WIDTH_B), BF16),
        compiler_params=pltpu.CompilerParams(dimension_semantics=("arbitrary",) * 3, vmem_limit_bytes=VMEM_LIMIT),
        name="stick",
    )(qb, kb, vb)


def _merge_kernel(x_ref, ol_ref, ob_ref, sa_ref, sb_ref, wuv_ref, wba_ref, wbb_ref, wout_ref, g_ref,
                  wrh_ref, wrl_ref, br_ref, x1_ref, h2_ref, comb_ref):
    oa = _dot(ol_ref[...], wuv_ref[...]).astype(BF16)
    ya = _dot(oa, wba_ref[...])
    yb = _dot(ob_ref[...], wbb_ref[...])
    merged = sa_ref[...].astype(F32) * ya + sb_ref[...].astype(F32) * yb
    x1 = x_ref[...] + _dot(merged.astype(BF16), wout_ref[...])
    x1_ref[...] = x1
    h2 = _rms(x1, g_ref[...])
    h2_hi = h2.astype(BF16)
    h2_ref[...] = h2_hi
    h2_lo = (h2 - h2_hi.astype(F32)).astype(BF16)
    logits = (_dot(h2_hi, wrh_ref[...]) + _dot(h2_lo, wrh_ref[...]) + _dot(h2_hi, wrl_ref[...])) + br_ref[...]
    lane = lax.broadcasted_iota(jnp.int32, logits.shape, 1).astype(F32)
    ninf = -jnp.inf

    def first_max(v):
        m = jnp.max(v, axis=1, keepdims=True)
        return m, jnp.min(jnp.where(v == m, lane, 1e9), axis=1, keepdims=True)

    gmask = (lane >= N_EXPERTS) & (lane < N_EXPERTS + N_GROUPS)
    gl = jnp.where(gmask, logits, ninf)
    gmax, gidx = first_max(gl)
    p_g = 1.0 / jnp.sum(jnp.where(gmask, jnp.exp(gl - gmax), 0.0), axis=1, keepdims=True)
    e_lo = (gidx - N_EXPERTS) * EXPERTS_PER_GROUP
    el = jnp.where((lane >= e_lo) & (lane < e_lo + EXPERTS_PER_GROUP), logits, ninf)
    v1, i1 = first_max(el)
    el2 = jnp.where(lane == i1, ninf, el)
    v2, i2 = first_max(el2)
    e2 = jnp.exp(v2 - v1)
    w1 = 1.0 / (1.0 + e2)
    comb_ref[...] = jnp.where(lane == i1, w1 * p_g, jnp.where(lane == i2, (e2 * w1) * p_g, 0.0))


def _merge(x, olat, ob, sa, sb, wuv, wba, wbb, wout, g, wrh, wrl, br):
    n, d = x.shape
    tm = ROW_TILE
    row = lambda w: pl.BlockSpec((tm, w), lambda i: (i, 0))
    ws = (wuv, wba, wbb, wout, g, wrh, wrl, br)
    return pl.pallas_call(
        _merge_kernel,
        grid=(n // tm,),
        in_specs=[row(d), row(olat.shape[1]), row(ob.shape[1]), row(d), row(d)] + [_full_spec(w.shape) for w in ws],
        out_specs=[row(d), row(d), row(LANES)],
        out_shape=[jax.ShapeDtypeStruct((n, d), F32), jax.ShapeDtypeStruct((n, d), BF16),
                   jax.ShapeDtypeStruct((n, LANES), F32)],
        compiler_params=pltpu.CompilerParams(dimension_semantics=("arbitrary",), vmem_limit_bytes=VMEM_LIMIT),
        name="merge",
    )(x, olat, ob, sa, sb, *ws)


def _moe_kernel(h_ref, comb_ref, x1_ref, wgu_ref, wd_ref, o_ref):
    e = pl.program_id(1)

    @pl.when(e == 0)
    def _():
        o_ref[...] = x1_ref[...]

    gu = _dot(h_ref[...], wgu_ref[0])
    g = gu[:, :D_EXPERT]
    hid = g * jax.nn.sigmoid(g) * gu[:, D_EXPERT:]
    comb = comb_ref[...]
    lane = lax.broadcasted_iota(jnp.int32, comb.shape, 1)
    ce = jnp.sum(jnp.where(lane == e, comb, 0.0), axis=1, keepdims=True)
    o_ref[...] += _dot((hid * ce).astype(BF16), wd_ref[0])


def _moe(h2, comb, x1, wgu, wd):
    n, d = x1.shape
    tm = MOE_ROW_TILE
    row = lambda w: pl.BlockSpec((tm, w), lambda i, e: (i, 0))
    return pl.pallas_call(
        _moe_kernel,
        grid=(n // tm, N_EXPERTS),
        in_specs=[row(d), row(LANES), row(d),
                  pl.BlockSpec((1, d, 2 * D_EXPERT), lambda i, e: (e, 0, 0)),
                  pl.BlockSpec((1, D_EXPERT, d), lambda i, e: (e, 0, 0))],
        out_specs=row(d),
        out_shape=jax.ShapeDtypeStruct((n, d), F32),
        compiler_params=pltpu.CompilerParams(dimension_semantics=("arbitrary", "arbitrary"),
                                             vmem_limit_bytes=VMEM_LIMIT),
        name="moe",
    )(h2, comb, x1, wgu, wd)


def _ple_kernel(x_ref, p_ref, g_ref, wpg_ref, wple_ref, gf_ref, o_ref, *, last):
    x2 = x_ref[...]
    h3 = _rms(x2, g_ref[...]).astype(BF16)
    gate = jax.nn.sigmoid(_dot(h3, wpg_ref[...]))
    x3 = x2 + _dot(p_ref[...].astype(BF16), wple_ref[...]) * gate
    o_ref[...] = _rms(x3, gf_ref[...]) if last else x3


def _ple(x2, p, g, wpg, wple, gf, last):
    n, d = x2.shape
    tm = ROW_TILE
    row = lambda w: pl.BlockSpec((tm, w), lambda i: (i, 0))
    ws = (g, wpg, wple, gf)
    return pl.pallas_call(
        functools.partial(_ple_kernel, last=last),
        grid=(n // tm,),
        in_specs=[row(d), row(p.shape[1])] + [_full_spec(w.shape) for w in ws],
        out_specs=row(d),
        out_shape=jax.ShapeDtypeStruct((n, d), F32),
        compiler_params=pltpu.CompilerParams(dimension_semantics=("arbitrary",), vmem_limit_bytes=VMEM_LIMIT),
        name="ple",
    )(x2, p, *ws)


def _t5_bucket(dist):
    dist = jnp.maximum(dist, 0)
    d_f = jnp.maximum(dist, 1).astype(F32)
    large = MAX_EXACT + (jnp.log(d_f / MAX_EXACT) / math.log(MAX_DISTANCE / MAX_EXACT)
                         * (N_BUCKETS - MAX_EXACT)).astype(jnp.int32)
    large = jnp.minimum(large, N_BUCKETS - 1)
    return jnp.where(dist < MAX_EXACT, dist, large)


def _bias_tiles(rel_bias):
    t = ATT_TILE
    i = jnp.arange(t, dtype=jnp.int32)[:, None]
    j = jnp.arange(t, dtype=jnp.int32)[None, :]
    dist = jnp.stack([i - j, t + i - j, jnp.full((t, t), 2 * t, jnp.int32)])
    b = rel_bias.astype(F32)[_t5_bucket(dist)]
    return jnp.transpose(b, (0, 3, 1, 2)).reshape(3, N_HEADS_A * t, t)


def _block_diag(blocks):
    h, r, c = blocks.shape
    eye = jnp.eye(h, dtype=blocks.dtype)
    return (eye[:, None, :, None] * blocks[:, :, None, :]).reshape(h * r, h * c)


def kernel(x, p, attn_norm, w_in, kv_norm, w_uk, w_uv, rel_bias, w_branch_a, w_branch_b, w_out, ffn_norm,
           w_r1, b_r1, w_r2, b_r2, w_gate, w_up, w_down, ple_norm, w_ple_gate, w_ple, final_norm):
    batch, seq, d = x.shape
    n = batch * seq
    depth = w_in.shape[0]
    xf = x.reshape(n, d).astype(F32)
    bias = _bias_tiles(rel_bias)
    widths = [WIDTH_A, KV_RANK, IDX_HEADS * IDX_DIM, IDX_DIM, IDX_HEADS, 3 * WIDTH_B, d, d]
    starts = [sum(widths[:k]) for k in range(len(widths))]
    for i in range(depth):
        w_qa, w_ckv, w_qi, w_ki, w_wi, w_qkv, w_ga, w_gb = [
            w_in[i][:, s:s + w].astype(BF16) for s, w in zip(starts, widths)]
        w_kk = jnp.concatenate([w_ki, w_ki], axis=1)
        w_wi = jnp.pad(w_wi, ((0, 0), (0, LANES - IDX_HEADS)))
        wuk_bd = _block_diag(jnp.swapaxes(w_uk[i], 1, 2)).astype(BF16)
        wuv_bd = _block_diag(w_uv[i]).astype(BF16)
        qabs, ckv, qidx, kk, widx, qb, kb, vb, sa, sb = _proj(
            xf, attn_norm[i][None].astype(F32), w_qa, wuk_bd, w_ckv, kv_norm[i][None].astype(F32),
            w_qi, w_kk, w_wi, w_qkv, w_ga, w_gb)
        olat = _dsa(qabs, qidx, widx, ckv, kk, bias, batch, seq)
        ob = _stick(qb, kb, vb, batch, seq)
        w_r = jnp.concatenate([jnp.transpose(w_r2[i], (1, 0, 2)).reshape(d, N_EXPERTS), w_r1[i]], axis=1)
        w_r = jnp.pad(w_r.astype(F32), ((0, 0), (0, LANES - N_EXPERTS - N_GROUPS)))
        w_r_hi = w_r.astype(BF16)
        w_r_lo = (w_r - w_r_hi.astype(F32)).astype(BF16)
        b_r = jnp.pad(jnp.concatenate([b_r2[i].reshape(-1), b_r1[i]]).astype(F32),
                      (0, LANES - N_EXPERTS - N_GROUPS))[None]
        x1, h2, comb = _merge(xf, olat, ob, sa, sb, wuv_bd, w_branch_a[i].astype(BF16), w_branch_b[i].astype(BF16),
                              w_out[i].astype(BF16), ffn_norm[i][None].astype(F32), w_r_hi, w_r_lo, b_r)
        wgu = jnp.concatenate([w_gate[i], w_up[i]], axis=2).astype(BF16)
        x2 = _moe(h2, comb, x1, wgu, w_down[i].astype(BF16))
        xf = _ple(x2, p[i].reshape(n, -1).astype(F32), ple_norm[i][None].astype(F32), w_ple_gate[i].astype(BF16),
                  w_ple[i].astype(BF16), final_norm[None].astype(F32), last=(i == depth - 1))
    return xf.reshape(batch, seq, d).astype(x.dtype)
```

```python
import functools
import math

import jax
import jax.numpy as jnp
from jax import lax
from jax.experimental import pallas as pl
from jax.experimental.pallas import tpu as pltpu

D_MODEL = 1024
N_HEADS_A = 8
HEAD_DIM = 64
WIDTH_A = N_HEADS_A * HEAD_DIM
KV_RANK = 128
IDX_HEADS = 8
IDX_DIM = 64
TOPK_MAX = 256
N_HEADS_B = 8
WIDTH_B = N_HEADS_B * HEAD_DIM
N_BUCKETS = 32
MAX_EXACT = N_BUCKETS // 2
MAX_DISTANCE = 128
ATTN_SCALE = HEAD_DIM ** -0.5
IDX_SCALE = (IDX_HEADS ** -0.5) * (IDX_DIM ** -0.5)
N_GROUPS = 4
EXPERTS_PER_GROUP = 8
N_EXPERTS = N_GROUPS * EXPERTS_PER_GROUP
D_EXPERT = 256
PLE_DIM = 256
EPS = 1e-6

LANES = 128
ROW_TILE = 512
MOE_ROW_TILE = 1024
ATT_TILE = 256
VMEM_LIMIT = 56 * 1024 * 1024
NEG_BIG = -1e30
EXP_UNDERFLOW = -120.0
INT_MIN = -2 ** 31

F32 = jnp.float32
BF16 = jnp.bfloat16


def _rms(x, g):
    return x * lax.rsqrt(jnp.mean(x * x, axis=-1, keepdims=True) + EPS) * g


def _dot(a, b):
    return jnp.dot(a, b, preferred_element_type=F32)


def _dot_nt(a, b):
    return lax.dot_general(a, b, (((1,), (1,)), ((), ())), preferred_element_type=F32)


def _full_spec(shape):
    nd = len(shape)
    return pl.BlockSpec(shape, lambda *_: (0,) * nd)


def _proj_kernel(x_ref, g_ref, wqa_ref, wuk_ref, wckv_ref, kvg_ref, wqi_ref, wkk_ref, wwi_ref,
                 wqkv_ref, wga_ref, wgb_ref,
                 qabs_ref, ckv_ref, qidx_ref, kk_ref, widx_ref, qb_ref, kb_ref, vb_ref, sa_ref, sb_ref):
    h = _rms(x_ref[...], g_ref[...]).astype(BF16)
    qa = _dot(h, wqa_ref[...]).astype(BF16)
    qabs_ref[...] = (_dot(qa, wuk_ref[...]) * ATTN_SCALE).astype(BF16)
    c = _dot(h, wckv_ref[...])
    c = _rms(c, kvg_ref[...]).astype(BF16)
    ckv_ref[...] = jnp.concatenate([c, jnp.ones_like(c)], axis=1)
    qidx_ref[...] = _dot(h, wqi_ref[...]).astype(BF16)
    kk_ref[...] = _dot(h, wkk_ref[...]).astype(BF16)
    widx_ref[...] = _dot(h, wwi_ref[...]) * IDX_SCALE
    qkv = _dot(h, wqkv_ref[...])
    qb_ref[...] = (qkv[:, :WIDTH_B] * ATTN_SCALE).astype(BF16)
    kb_ref[...] = qkv[:, WIDTH_B:2 * WIDTH_B].astype(BF16)
    vb_ref[...] = qkv[:, 2 * WIDTH_B:].astype(BF16)
    sa_ref[...] = jax.nn.sigmoid(_dot(h, wga_ref[...])).astype(BF16)
    sb_ref[...] = jax.nn.sigmoid(_dot(h, wgb_ref[...])).astype(BF16)


def _proj(x, g, wqa, wuk, wckv, kvg, wqi, wkk, wwi, wqkv, wga, wgb):
    n, d = x.shape
    tm = ROW_TILE
    row = lambda w: pl.BlockSpec((tm, w), lambda i: (i, 0))
    ws = (g, wqa, wuk, wckv, kvg, wqi, wkk, wwi, wqkv, wga, wgb)
    outs = [(N_HEADS_A * KV_RANK, BF16), (2 * KV_RANK, BF16), (IDX_HEADS * IDX_DIM, BF16), (LANES, BF16),
            (LANES, F32), (WIDTH_B, BF16), (WIDTH_B, BF16), (WIDTH_B, BF16), (d, BF16), (d, BF16)]
    return pl.pallas_call(
        _proj_kernel,
        grid=(n // tm,),
        in_specs=[row(d)] + [_full_spec(w.shape) for w in ws],
        out_specs=[row(w) for w, _ in outs],
        out_shape=[jax.ShapeDtypeStruct((n, w), dt) for w, dt in outs],
        compiler_params=pltpu.CompilerParams(dimension_semantics=("arbitrary",), vmem_limit_bytes=VMEM_LIMIT),
        name="proj",
    )(x, *ws)


def _sortable_key(s):
    bits = lax.bitcast_convert_type(s, jnp.int32)
    key = bits ^ ((bits >> 31) & jnp.int32(0x7FFFFFFF))
    return jnp.where(key == -1, 0, key)


def _t5_bucket(dist):
    dist = jnp.maximum(dist, 0)
    d_f = jnp.maximum(dist, 1).astype(F32)
    large = MAX_EXACT + (jnp.log(d_f / MAX_EXACT) / math.log(MAX_DISTANCE / MAX_EXACT)
                         * (N_BUCKETS - MAX_EXACT)).astype(jnp.int32)
    large = jnp.minimum(large, N_BUCKETS - 1)
    return jnp.where(dist < MAX_EXACT, dist, large)


def _rows_to_cols(v_row):
    return jnp.transpose(jnp.broadcast_to(v_row, (LANES, v_row.shape[1])))


def _dsa_kernel(relb_ref, qabs_ref, qidx_ref, widx_ref, ckv_ref, kk_ref, o_ref,
                qa_s, qi_s, w_s, key_s, keyt_s, cs_s, bias_s, m_s, acc_s, *, n_sel, seq):
    tq = tk = ATT_TILE
    nh = N_HEADS_A
    qi = pl.program_id(1)
    n_kt = qi + 1
    row = lax.broadcasted_iota(jnp.int32, (tq, tk), 0)
    col = lax.broadcasted_iota(jnp.int32, (tq, tk), 1)

    @pl.when((pl.program_id(0) == 0) & (qi == 0))
    def _():
        for t in range(3):
            bucket = _t5_bucket(t * tq + row - col if t < 2 else jnp.full((tq, tk), 2 * tq, jnp.int32))
            for h in range(nh):
                b = jnp.zeros((tq, tk), F32)
                for bk in range(N_BUCKETS):
                    b = jnp.where(bucket == bk, relb_ref[bk, h], b)
                bias_s[t, h * tq:(h + 1) * tq, :] = b

    lane = lax.broadcasted_iota(jnp.int32, (tq, LANES), 1)
    lo_half = jnp.where(lane < IDX_DIM, 1.0, 0.0)
    hi_half = 1.0 - lo_half
    for h in range(nh):
        qa_s[h * tq:(h + 1) * tq, :] = qabs_ref[:, h * KV_RANK:(h + 1) * KV_RANK]
        pair = qidx_ref[:, (h // 2) * LANES:(h // 2 + 1) * LANES].astype(F32)
        qi_s[h * tq:(h + 1) * tq, :] = (pair * (lo_half if h % 2 == 0 else hi_half)).astype(BF16)
        w_s[h * tq:(h + 1) * tq, :] = jnp.broadcast_to(widx_ref[:, h:h + 1], (tq, LANES))

    def idx_body(kj, carry):
        ks = pl.multiple_of(kj * tk, tk)
        r = _dot_nt(qi_s[...], kk_ref[pl.ds(ks, tk), :])
        w = w_s[...]
        r = jnp.maximum(r, 0.0) * jnp.concatenate([w] * (tk // LANES), axis=1)
        s = jnp.sum(r.reshape(nh, tq, tk), axis=0)
        diag = kj == qi
        key_s[kj] = jnp.where((col > row) & diag, INT_MIN, _sortable_key(s))
        keyt_s[kj] = jnp.where((row > col) & diag, INT_MIN, _sortable_key(jnp.transpose(s)))
        return carry

    lax.fori_loop(0, n_kt, idx_body, 0)

    def count(pred):
        def body(kj, acc):
            c = jnp.where(pred(keyt_s[kj], row + kj * tk), 1.0, 0.0)
            return acc + jnp.sum(c.reshape(tk // 8, 8, tq), axis=0)
        acc = lax.fori_loop(0, n_kt, body, jnp.zeros((8, tq), F32))
        return jnp.sum(acc, axis=0, keepdims=True)

    def thr_body(i, tau_u):
        cand_u = tau_u | jnp.left_shift(jnp.int32(1), 31 - i)
        cand = cand_u ^ INT_MIN
        cnt = count(lambda k, pos: k >= cand)
        return jnp.where(cnt >= n_sel, cand_u, tau_u)

    tau_u = lax.fori_loop(0, 32, thr_body, jnp.zeros((1, tq), jnp.int32))
    first = qi == 0
    tau = jnp.where(first, INT_MIN, tau_u ^ INT_MIN)
    n_ge = count(lambda k, pos: k >= tau)

    cs_s[...] = jnp.full(cs_s.shape, seq, jnp.int32)

    @pl.when(jnp.max(n_ge) > n_sel)
    def _():
        need = n_sel - count(lambda k, pos: k > tau)
        nbits = seq.bit_length() - 1

        def tie_body(i, d):
            cand = d | jnp.left_shift(jnp.int32(1), nbits - 1 - i)
            cnt = count(lambda k, pos: (k == tau) & (pos < cand))
            return jnp.where(cnt < need, cand, d)

        d = lax.fori_loop(0, nbits, tie_body, jnp.zeros((1, tq), jnp.int32))
        cs_s[...] = jnp.broadcast_to(d, cs_s.shape)

    cstar = jnp.where(first, -1, cs_s[0:1, :])
    tau_c = ((_rows_to_cols((tau >> 16).astype(F32)).astype(jnp.int32) << 16)
             | _rows_to_cols((tau & 0xFFFF).astype(F32)).astype(jnp.int32))
    cstar_c = _rows_to_cols(cstar.astype(F32)).astype(jnp.int32)
    tau_c = jnp.concatenate([tau_c] * (tk // LANES), axis=1)
    cstar_c = jnp.concatenate([cstar_c] * (tk // LANES), axis=1)

    m_s[...] = jnp.full(m_s.shape, NEG_BIG, F32)
    acc_s[...] = jnp.zeros(acc_s.shape, F32)

    def att_body(kj, carry):
        ks = pl.multiple_of(kj * tk, tk)
        c_t = ckv_ref[pl.ds(ks, tk), :]
        k = key_s[kj]
        sel = (k > tau_c) | ((k == tau_c) & (col + kj * tk <= cstar_c))
        selb = jnp.where(sel, 0.0, NEG_BIG)
        s = _dot_nt(qa_s[...], c_t[:, :KV_RANK])
        s = s + bias_s[jnp.minimum(qi - kj, 2)]
        s = (s.reshape(nh, tq, tk) + selb[None]).reshape(nh * tq, tk)
        m_old = m_s[...]
        m_new = jnp.maximum(m_old, jnp.max(s, axis=1, keepdims=True))
        alpha = jnp.exp(m_old - m_new)
        p = jnp.exp(s - jnp.concatenate([m_new] * (tk // LANES), axis=1)).astype(BF16)
        acc_s[...] = acc_s[...] * jnp.concatenate([alpha, alpha], axis=1) + _dot(p, c_t)
        m_s[...] = m_new
        return carry

    lax.fori_loop(0, n_kt, att_body, 0)

    acc = acc_s[...]
    o = acc[:, :KV_RANK] / acc[:, KV_RANK:]
    for h in range(nh):
        o_ref[:, h * KV_RANK:(h + 1) * KV_RANK] = o[h * tq:(h + 1) * tq].astype(BF16)


def _dsa(rel_bias, qabs, qidx, widx, ckv, kk, batch, seq):
    n = qabs.shape[0]
    tq = ATT_TILE
    nq = seq // tq
    n_sel = min(TOPK_MAX, seq // 4)
    assert n_sel == tq and seq % tq == 0 and seq & (seq - 1) == 0
    qrow = lambda w: pl.BlockSpec((tq, w), lambda b, q: (b * nq + q, 0))
    brow = lambda w: pl.BlockSpec((seq, w), lambda b, q: (b, 0))
    nhq = N_HEADS_A * tq
    return pl.pallas_call(
        functools.partial(_dsa_kernel, n_sel=n_sel, seq=seq),
        grid=(batch, nq),
        in_specs=[pl.BlockSpec(memory_space=pltpu.SMEM),
                  qrow(N_HEADS_A * KV_RANK), qrow(IDX_HEADS * IDX_DIM), qrow(LANES),
                  brow(2 * KV_RANK), brow(LANES)],
        out_specs=qrow(N_HEADS_A * KV_RANK),
        out_shape=jax.ShapeDtypeStruct((n, N_HEADS_A * KV_RANK), BF16),
        scratch_shapes=[pltpu.VMEM((nhq, KV_RANK), BF16), pltpu.VMEM((nhq, LANES), BF16),
                        pltpu.VMEM((nhq, LANES), F32), pltpu.VMEM((nq, tq, tq), jnp.int32),
                        pltpu.VMEM((nq, tq, tq), jnp.int32), pltpu.VMEM((8, tq), jnp.int32),
                        pltpu.VMEM((3, nhq, tq), F32),
                        pltpu.VMEM((nhq, LANES), F32), pltpu.VMEM((nhq, 2 * KV_RANK), F32)],
        compiler_params=pltpu.CompilerParams(dimension_semantics=("arbitrary", "arbitrary"),
                                             vmem_limit_bytes=VMEM_LIMIT),
        name="dsa",
    )(rel_bias, qabs, qidx, widx, ckv, kk)


def _stick_kernel(q_ref, k_ref, v_ref, o_ref):
    tq = tk = ATT_TILE
    qi = pl.program_id(1)
    npair = WIDTH_B // LANES
    lane = lax.broadcasted_iota(jnp.int32, (tq, LANES), 1)
    row = lax.broadcasted_iota(jnp.int32, (tq, tk), 0)
    col = lax.broadcasted_iota(jnp.int32, (tq, tk), 1)
    causal = col < row
    upper = jnp.where(row > col, 1.0, 0.0).astype(BF16)
    upper_ones = jnp.concatenate([upper, jnp.ones((tk, LANES), BF16)], axis=1)
    lo_half = jnp.where(lane < HEAD_DIM, 1.0, 0.0)
    causal2 = jnp.concatenate([causal, causal], axis=0)
    q_pairs = []
    for p in range(npair):
        q2 = q_ref[:, p * LANES:(p + 1) * LANES].astype(F32)
        q_pairs.append(jnp.concatenate([q2 * lo_half, q2 * (1.0 - lo_half)], axis=0).astype(BF16))

    def tile(kj, r_sums, outs, masked):
        ks = pl.multiple_of(kj * tk, tk)
        log_sig, log_1m = [], []
        for p in range(npair):
            z = _dot_nt(q_pairs[p], k_ref[pl.ds(ks, tk), p * LANES:(p + 1) * LANES])
            sp = jnp.maximum(z, 0.0) + jnp.log(1.0 + jnp.exp(-jnp.abs(z)))
            lm = jnp.where(causal2, -sp, 0.0) if masked else -sp
            log_sig.append(z - sp)
            log_1m.append(lm.astype(BF16))
        sums = _dot(jnp.concatenate(log_1m, axis=0), upper_ones)
        new_r, new_o = [], []
        for p in range(npair):
            sm = sums[p * 2 * tq:(p + 1) * 2 * tq]
            after = sm[:, :tk] + jnp.concatenate([r_sums[p]] * (tk // LANES), axis=1)
            a = jnp.exp(log_sig[p] + after)
            if masked:
                a = jnp.where(causal2, a, 0.0)
            res = _dot(a.astype(BF16), v_ref[pl.ds(ks, tk), p * LANES:(p + 1) * LANES])
            new_o.append(outs[p] + jnp.where(lane < HEAD_DIM, res[:tq], res[tq:]))
            new_r.append(r_sums[p] + sm[:, tk:])
        return tuple(new_r), tuple(new_o)

    zeros = jnp.zeros((tq, LANES), F32)
    r_sums, outs = tile(qi, (jnp.zeros((2 * tq, LANES), F32),) * npair, (zeros,) * npair, True)

    def cond(c):
        it, r_sums, _ = c
        return (it < qi) & (jnp.max(functools.reduce(jnp.maximum, r_sums)) > EXP_UNDERFLOW)

    def body(c):
        it, r_sums, outs = c
        r_sums, outs = tile(qi - 1 - it, r_sums, outs, False)
        return it + 1, r_sums, outs

    _, _, outs = lax.while_loop(cond, body, (jnp.int32(0), r_sums, outs))
    for p in range(npair):
        o_ref[:, p * LANES:(p + 1) * LANES] = outs[p].astype(BF16)


def _stick(qb, kb, vb, batch, seq):
    n = qb.shape[0]
    tq = ATT_TILE
    nq = seq // tq
    qspec = pl.BlockSpec((tq, WIDTH_B), lambda b, q: (b * nq + q, 0))
    kspec = pl.BlockSpec((seq, WIDTH_B), lambda b, q: (b, 0))
    return pl.pallas_call(
        _stick_kernel,
        grid=(batch, nq),
        in_specs=[qspec, kspec, kspec],
        out_specs=qspec,
        out_shape=jax.ShapeDtypeStruct((n, WIDTH_B), BF16),
        compiler_params=pltpu.CompilerParams(dimension_semantics=("arbitrary",) * 2, vmem_limit_bytes=VMEM_LIMIT),
        name="stick",
    )(qb, kb, vb)


def _merge_kernel(x_ref, ol_ref, ob_ref, sa_ref, sb_ref, wuv_ref, wba_ref, wbb_ref, wout_ref, g_ref,
                  wrh_ref, wrl_ref, br_ref, x1_ref, h2_ref, comb_ref):
    oa = _dot(ol_ref[...], wuv_ref[...]).astype(BF16)
    ya = _dot(oa, wba_ref[...])
    yb = _dot(ob_ref[...], wbb_ref[...])
    merged = sa_ref[...].astype(F32) * ya + sb_ref[...].astype(F32) * yb
    x1 = x_ref[...] + _dot(merged.astype(BF16), wout_ref[...])
    x1_ref[...] = x1
    h2 = _rms(x1, g_ref[...])
    h2_hi = h2.astype(BF16)
    h2_ref[...] = h2_hi
    h2_lo = (h2 - h2_hi.astype(F32)).astype(BF16)
    logits = (_dot(h2_hi, wrh_ref[...]) + _dot(h2_lo, wrh_ref[...]) + _dot(h2_hi, wrl_ref[...])) + br_ref[...]
    lane = lax.broadcasted_iota(jnp.int32, logits.shape, 1).astype(F32)
    ninf = -jnp.inf

    def first_max(v):
        m = jnp.max(v, axis=1, keepdims=True)
        return m, jnp.min(jnp.where(v == m, lane, 1e9), axis=1, keepdims=True)

    gmask = (lane >= N_EXPERTS) & (lane < N_EXPERTS + N_GROUPS)
    gl = jnp.where(gmask, logits, ninf)
    gmax, gidx = first_max(gl)
    p_g = 1.0 / jnp.sum(jnp.where(gmask, jnp.exp(gl - gmax), 0.0), axis=1, keepdims=True)
    e_lo = (gidx - N_EXPERTS) * EXPERTS_PER_GROUP
    el = jnp.where((lane >= e_lo) & (lane < e_lo + EXPERTS_PER_GROUP), logits, ninf)
    v1, i1 = first_max(el)
    el2 = jnp.where(lane == i1, ninf, el)
    v2, i2 = first_max(el2)
    e2 = jnp.exp(v2 - v1)
    w1 = 1.0 / (1.0 + e2)
    comb_ref[...] = jnp.where(lane == i1, w1 * p_g, jnp.where(lane == i2, (e2 * w1) * p_g, 0.0))


def _merge(x, olat, ob, sa, sb, wuv, wba, wbb, wout, g, wrh, wrl, br):
    n, d = x.shape
    tm = ROW_TILE
    row = lambda w: pl.BlockSpec((tm, w), lambda i: (i, 0))
    ws = (wuv, wba, wbb, wout, g, wrh, wrl, br)
    return pl.pallas_call(
        _merge_kernel,
        grid=(n // tm,),
        in_specs=[row(d), row(olat.shape[1]), row(ob.shape[1]), row(d), row(d)] + [_full_spec(w.shape) for w in ws],
        out_specs=[row(d), row(d), row(LANES)],
        out_shape=[jax.ShapeDtypeStruct((n, d), F32), jax.ShapeDtypeStruct((n, d), BF16),
                   jax.ShapeDtypeStruct((n, LANES), F32)],
        compiler_params=pltpu.CompilerParams(dimension_semantics=("arbitrary",), vmem_limit_bytes=VMEM_LIMIT),
        name="merge",
    )(x, olat, ob, sa, sb, *ws)


def _moe_kernel(h_ref, comb_ref, x1_ref, wgu_ref, wd_ref, o_ref):
    e = pl.program_id(1)

    @pl.when(e == 0)
    def _():
        o_ref[...] = x1_ref[...]

    gu = _dot(h_ref[...], wgu_ref[0])
    g = gu[:, :D_EXPERT]
    hid = g * jax.nn.sigmoid(g) * gu[:, D_EXPERT:]
    comb = comb_ref[...]
    lane = lax.broadcasted_iota(jnp.int32, comb.shape, 1)
    ce = jnp.sum(jnp.where(lane == e, comb, 0.0), axis=1, keepdims=True)
    o_ref[...] += _dot((hid * ce).astype(BF16), wd_ref[0])


def _moe(h2, comb, x1, wgu, wd):
    n, d = x1.shape
    tm = MOE_ROW_TILE
    row = lambda w: pl.BlockSpec((tm, w), lambda i, e: (i, 0))
    return pl.pallas_call(
        _moe_kernel,
        grid=(n // tm, N_EXPERTS),
        in_specs=[row(d), row(LANES), row(d),
                  pl.BlockSpec((1, d, 2 * D_EXPERT), lambda i, e: (e, 0, 0)),
                  pl.BlockSpec((1, D_EXPERT, d), lambda i, e: (e, 0, 0))],
        out_specs=row(d),
        out_shape=jax.ShapeDtypeStruct((n, d), F32),
        compiler_params=pltpu.CompilerParams(dimension_semantics=("arbitrary", "arbitrary"),
                                             vmem_limit_bytes=VMEM_LIMIT),
        name="moe",
    )(h2, comb, x1, wgu, wd)


def _ple_kernel(x_ref, p_ref, g_ref, wpg_ref, wple_ref, gf_ref, o_ref, *, last):
    x2 = x_ref[...]
    h3 = _rms(x2, g_ref[...]).astype(BF16)
    gate = jax.nn.sigmoid(_dot(h3, wpg_ref[...]))
    x3 = x2 + _dot(p_ref[...].astype(BF16), wple_ref[...]) * gate
    o_ref[...] = _rms(x3, gf_ref[...]) if last else x3


def _ple(x2, p, g, wpg, wple, gf, last):
    n, d = x2.shape
    tm = ROW_TILE
    row = lambda w: pl.BlockSpec((tm, w), lambda i: (i, 0))
    ws = (g, wpg, wple, gf)
    return pl.pallas_call(
        functools.partial(_ple_kernel, last=last),
        grid=(n // tm,),
        in_specs=[row(d), row(p.shape[1])] + [_full_spec(w.shape) for w in ws],
        out_specs=row(d),
        out_shape=jax.ShapeDtypeStruct((n, d), F32),
        compiler_params=pltpu.CompilerParams(dimension_semantics=("arbitrary",), vmem_limit_bytes=VMEM_LIMIT),
        name="ple",
    )(x2, p, *ws)


def _block_diag(blocks):
    h, r, c = blocks.shape
    eye = jnp.eye(h, dtype=blocks.dtype)
    return (eye[:, None, :, None] * blocks[:, :, None, :]).reshape(h * r, h * c)


def kernel(x, p, attn_norm, w_in, kv_norm, w_uk, w_uv, rel_bias, w_branch_a, w_branch_b, w_out, ffn_norm,
           w_r1, b_r1, w_r2, b_r2, w_gate, w_up, w_down, ple_norm, w_ple_gate, w_ple, final_norm):
    batch, seq, d = x.shape
    n = batch * seq
    depth = w_in.shape[0]
    xf = x.reshape(n, d).astype(F32)
    widths = [WIDTH_A, KV_RANK, IDX_HEADS * IDX_DIM, IDX_DIM, IDX_HEADS, 3 * WIDTH_B, d, d]
    starts = [sum(widths[:k]) for k in range(len(widths))]
    for i in range(depth):
        w_qa, w_ckv, w_qi, w_ki, w_wi, w_qkv, w_ga, w_gb = [
            w_in[i][:, s:s + w].astype(BF16) for s, w in zip(starts, widths)]
        w_kk = jnp.concatenate([w_ki, w_ki], axis=1)
        w_wi = jnp.pad(w_wi, ((0, 0), (0, LANES - IDX_HEADS)))
        wuk_bd = _block_diag(jnp.swapaxes(w_uk[i], 1, 2)).astype(BF16)
        wuv_bd = _block_diag(w_uv[i]).astype(BF16)
        qabs, ckv, qidx, kk, widx, qb, kb, vb, sa, sb = _proj(
            xf, attn_norm[i][None].astype(F32), w_qa, wuk_bd, w_ckv, kv_norm[i][None].astype(F32),
            w_qi, w_kk, w_wi, w_qkv, w_ga, w_gb)
        olat = _dsa(rel_bias.astype(F32), qabs, qidx, widx, ckv, kk, batch, seq)
        ob = _stick(qb, kb, vb, batch, seq)
        w_r = jnp.concatenate([jnp.transpose(w_r2[i], (1, 0, 2)).reshape(d, N_EXPERTS), w_r1[i]], axis=1)
        w_r = jnp.pad(w_r.astype(F32), ((0, 0), (0, LANES - N_EXPERTS - N_GROUPS)))
        w_r_hi = w_r.astype(BF16)
        w_r_lo = (w_r - w_r_hi.astype(F32)).astype(BF16)
        b_r = jnp.pad(jnp.concatenate([b_r2[i].reshape(-1), b_r1[i]]).astype(F32),
                      (0, LANES - N_EXPERTS - N_GROUPS))[None]
        x1, h2, comb = _merge(xf, olat, ob, sa, sb, wuv_bd, w_branch_a[i].astype(BF16), w_branch_b[i].astype(BF16),
                              w_out[i].astype(BF16), ffn_norm[i][None].astype(F32), w_r_hi, w_r_lo, b_r)
        wgu = jnp.concatenate([w_gate[i], w_up[i]], axis=2).astype(BF16)
        x2 = _moe(h2, comb, x1, wgu, w_down[i].astype(BF16))
        xf = _ple(x2, p[i].reshape(n, -1).astype(F32), ple_norm[i][None].astype(F32), w_ple_gate[i].astype(BF16),
                  w_ple[i].astype(BF16), final_norm[None].astype(F32), last=(i == depth - 1))
    return xf.reshape(batch, seq, d).astype(x.dtype)
```

```python
import functools
import math

import jax
import jax.numpy as jnp
from jax import lax
from jax.experimental import pallas as pl
from jax.experimental.pallas import tpu as pltpu

D_MODEL = 1024
N_HEADS_A = 8
HEAD_DIM = 64
WIDTH_A = N_HEADS_A * HEAD_DIM
KV_RANK = 128
IDX_HEADS = 8
IDX_DIM = 64
TOPK_MAX = 256
N_HEADS_B = 8
WIDTH_B = N_HEADS_B * HEAD_DIM
N_BUCKETS = 32
MAX_EXACT = N_BUCKETS // 2
MAX_DISTANCE = 128
ATTN_SCALE = HEAD_DIM ** -0.5
IDX_SCALE = (IDX_HEADS ** -0.5) * (IDX_DIM ** -0.5)
N_GROUPS = 4
EXPERTS_PER_GROUP = 8
N_EXPERTS = N_GROUPS * EXPERTS_PER_GROUP
D_EXPERT = 256
PLE_DIM = 256
EPS = 1e-6

LANES = 128
ROW_TILE = 512
MOE_ROW_TILE = 1024
MOE_CHUNK = 256
ATT_TILE = 256
VMEM_LIMIT = 56 * 1024 * 1024
NEG_BIG = -1e30
EXP_UNDERFLOW = -120.0
INT_MIN = -2 ** 31

F32 = jnp.float32
BF16 = jnp.bfloat16


def _rms(x, g):
    return x * lax.rsqrt(jnp.mean(x * x, axis=-1, keepdims=True) + EPS) * g


def _dot(a, b):
    return jnp.dot(a, b, preferred_element_type=F32)


def _dot_nt(a, b):
    return lax.dot_general(a, b, (((1,), (1,)), ((), ())), preferred_element_type=F32)


def _full_spec(shape):
    nd = len(shape)
    return pl.BlockSpec(shape, lambda *_: (0,) * nd)


def _proj_kernel(x_ref, g_ref, wqa_ref, wuk_ref, wckv_ref, kvg_ref, wqi_ref, wkk_ref, wwi_ref,
                 wqkv_ref, wga_ref, wgb_ref,
                 qabs_ref, ckv_ref, qidx_ref, kk_ref, widx_ref, qb_ref, kb_ref, vb_ref, sa_ref, sb_ref):
    h = _rms(x_ref[...], g_ref[...]).astype(BF16)
    qa = _dot(h, wqa_ref[...]).astype(BF16)
    qabs_ref[...] = (_dot(qa, wuk_ref[...]) * ATTN_SCALE).astype(BF16)
    c = _dot(h, wckv_ref[...])
    c = _rms(c, kvg_ref[...]).astype(BF16)
    ckv_ref[...] = jnp.concatenate([c, jnp.ones_like(c)], axis=1)
    qidx_ref[...] = _dot(h, wqi_ref[...]).astype(BF16)
    kk_ref[...] = _dot(h, wkk_ref[...]).astype(BF16)
    widx_ref[...] = _dot(h, wwi_ref[...]) * IDX_SCALE
    qkv = _dot(h, wqkv_ref[...])
    qb_ref[...] = (qkv[:, :WIDTH_B] * ATTN_SCALE).astype(BF16)
    kb_ref[...] = qkv[:, WIDTH_B:2 * WIDTH_B].astype(BF16)
    vb_ref[...] = qkv[:, 2 * WIDTH_B:].astype(BF16)
    sa_ref[...] = jax.nn.sigmoid(_dot(h, wga_ref[...])).astype(BF16)
    sb_ref[...] = jax.nn.sigmoid(_dot(h, wgb_ref[...])).astype(BF16)


def _proj(x, g, wqa, wuk, wckv, kvg, wqi, wkk, wwi, wqkv, wga, wgb):
    n, d = x.shape
    tm = ROW_TILE
    row = lambda w: pl.BlockSpec((tm, w), lambda i: (i, 0))
    ws = (g, wqa, wuk, wckv, kvg, wqi, wkk, wwi, wqkv, wga, wgb)
    outs = [(N_HEADS_A * KV_RANK, BF16), (2 * KV_RANK, BF16), (IDX_HEADS * IDX_DIM, BF16), (LANES, BF16),
            (LANES, F32), (WIDTH_B, BF16), (WIDTH_B, BF16), (WIDTH_B, BF16), (d, BF16), (d, BF16)]
    return pl.pallas_call(
        _proj_kernel,
        grid=(n // tm,),
        in_specs=[row(d)] + [_full_spec(w.shape) for w in ws],
        out_specs=[row(w) for w, _ in outs],
        out_shape=[jax.ShapeDtypeStruct((n, w), dt) for w, dt in outs],
        compiler_params=pltpu.CompilerParams(dimension_semantics=("arbitrary",), vmem_limit_bytes=VMEM_LIMIT),
        name="proj",
    )(x, *ws)


def _sortable_key(s):
    bits = lax.bitcast_convert_type(s, jnp.int32)
    key = bits ^ ((bits >> 31) & jnp.int32(0x7FFFFFFF))
    return jnp.where(key == -1, 0, key)


def _t5_bucket(dist):
    dist = jnp.maximum(dist, 0)
    d_f = jnp.maximum(dist, 1).astype(F32)
    large = MAX_EXACT + (jnp.log(d_f / MAX_EXACT) / math.log(MAX_DISTANCE / MAX_EXACT)
                         * (N_BUCKETS - MAX_EXACT)).astype(jnp.int32)
    large = jnp.minimum(large, N_BUCKETS - 1)
    return jnp.where(dist < MAX_EXACT, dist, large)


def _rows_to_cols(v_row):
    return jnp.transpose(jnp.broadcast_to(v_row, (LANES, v_row.shape[1])))


def _dsa_kernel(relb_ref, qabs_ref, qidx_ref, widx_ref, ckv_ref, kk_ref, o_ref,
                qa_s, qi_s, w_s, key_s, keyt_s, cs_s, bias_s, m_s, acc_s, *, n_sel, seq):
    tq = tk = ATT_TILE
    nh = N_HEADS_A
    qi = pl.program_id(1)
    n_kt = qi + 1
    row = lax.broadcasted_iota(jnp.int32, (tq, tk), 0)
    col = lax.broadcasted_iota(jnp.int32, (tq, tk), 1)

    @pl.when((pl.program_id(0) == 0) & (qi == 0))
    def _():
        for t in range(3):
            bucket = _t5_bucket(t * tq + row - col if t < 2 else jnp.full((tq, tk), 2 * tq, jnp.int32))
            for h in range(nh):
                b = jnp.zeros((tq, tk), F32)
                for bk in range(N_BUCKETS):
                    b = jnp.where(bucket == bk, relb_ref[bk, h], b)
                bias_s[t, h * tq:(h + 1) * tq, :] = b

    lane = lax.broadcasted_iota(jnp.int32, (tq, LANES), 1)
    lo_half = jnp.where(lane < IDX_DIM, 1.0, 0.0)
    hi_half = 1.0 - lo_half
    for h in range(nh):
        qa_s[h * tq:(h + 1) * tq, :] = qabs_ref[:, h * KV_RANK:(h + 1) * KV_RANK]
        pair = qidx_ref[:, (h // 2) * LANES:(h // 2 + 1) * LANES].astype(F32)
        qi_s[h * tq:(h + 1) * tq, :] = (pair * (lo_half if h % 2 == 0 else hi_half)).astype(BF16)
        w_s[h * tq:(h + 1) * tq, :] = jnp.broadcast_to(widx_ref[:, h:h + 1], (tq, LANES))

    def idx_body(kj, carry):
        ks = pl.multiple_of(kj * tk, tk)
        r = _dot_nt(qi_s[...], kk_ref[pl.ds(ks, tk), :])
        w = w_s[...]
        r = jnp.maximum(r, 0.0) * jnp.concatenate([w] * (tk // LANES), axis=1)
        s = jnp.sum(r.reshape(nh, tq, tk), axis=0)
        diag = kj == qi
        key_s[kj] = jnp.where((col > row) & diag, INT_MIN, _sortable_key(s))
        keyt_s[kj] = jnp.where((row > col) & diag, INT_MIN, _sortable_key(jnp.transpose(s)))
        return carry

    lax.fori_loop(0, n_kt, idx_body, 0)

    def count(pred):
        def body(kj, acc):
            c = jnp.where(pred(keyt_s[kj], row + kj * tk), 1.0, 0.0)
            return acc + jnp.sum(c.reshape(tk // 8, 8, tq), axis=0)
        acc = lax.fori_loop(0, n_kt, body, jnp.zeros((8, tq), F32))
        return jnp.sum(acc, axis=0, keepdims=True)

    def thr_body(i, tau_u):
        cand_u = tau_u | jnp.left_shift(jnp.int32(1), 31 - i)
        cand = cand_u ^ INT_MIN
        cnt = count(lambda k, pos: k >= cand)
        return jnp.where(cnt >= n_sel, cand_u, tau_u)

    tau_u = lax.fori_loop(0, 32, thr_body, jnp.zeros((1, tq), jnp.int32))
    first = qi == 0
    tau = jnp.where(first, INT_MIN, tau_u ^ INT_MIN)
    n_ge = count(lambda k, pos: k >= tau)

    cs_s[...] = jnp.full(cs_s.shape, seq, jnp.int32)

    @pl.when(jnp.max(n_ge) > n_sel)
    def _():
        need = n_sel - count(lambda k, pos: k > tau)
        nbits = seq.bit_length() - 1

        def tie_body(i, d):
            cand = d | jnp.left_shift(jnp.int32(1), nbits - 1 - i)
            cnt = count(lambda k, pos: (k == tau) & (pos < cand))
            return jnp.where(cnt < need, cand, d)

        d = lax.fori_loop(0, nbits, tie_body, jnp.zeros((1, tq), jnp.int32))
        cs_s[...] = jnp.broadcast_to(d, cs_s.shape)

    cstar = jnp.where(first, -1, cs_s[0:1, :])
    tau_c = ((_rows_to_cols((tau >> 16).astype(F32)).astype(jnp.int32) << 16)
             | _rows_to_cols((tau & 0xFFFF).astype(F32)).astype(jnp.int32))
    cstar_c = _rows_to_cols(cstar.astype(F32)).astype(jnp.int32)
    tau_c = jnp.concatenate([tau_c] * (tk // LANES), axis=1)
    cstar_c = jnp.concatenate([cstar_c] * (tk // LANES), axis=1)

    m_s[...] = jnp.full(m_s.shape, NEG_BIG, F32)
    acc_s[...] = jnp.zeros(acc_s.shape, F32)

    def att_body(kj, carry):
        ks = pl.multiple_of(kj * tk, tk)
        c_t = ckv_ref[pl.ds(ks, tk), :]
        k = key_s[kj]
        sel = (k > tau_c) | ((k == tau_c) & (col + kj * tk <= cstar_c))
        selb = jnp.where(sel, 0.0, NEG_BIG)
        s = _dot_nt(qa_s[...], c_t[:, :KV_RANK])
        s = s + bias_s[jnp.minimum(qi - kj, 2)]
        s = (s.reshape(nh, tq, tk) + selb[None]).reshape(nh * tq, tk)
        m_old = m_s[...]
        m_new = jnp.maximum(m_old, jnp.max(s, axis=1, keepdims=True))
        alpha = jnp.exp(m_old - m_new)
        p = jnp.exp(s - jnp.concatenate([m_new] * (tk // LANES), axis=1)).astype(BF16)
        acc_s[...] = acc_s[...] * jnp.concatenate([alpha, alpha], axis=1) + _dot(p, c_t)
        m_s[...] = m_new
        return carry

    lax.fori_loop(0, n_kt, att_body, 0)

    acc = acc_s[...]
    o = acc[:, :KV_RANK] / acc[:, KV_RANK:]
    for h in range(nh):
        o_ref[:, h * KV_RANK:(h + 1) * KV_RANK] = o[h * tq:(h + 1) * tq].astype(BF16)


def _dsa(rel_bias, qabs, qidx, widx, ckv, kk, batch, seq):
    n = qabs.shape[0]
    tq = ATT_TILE
    nq = seq // tq
    n_sel = min(TOPK_MAX, seq // 4)
    assert n_sel == tq and seq % tq == 0 and seq & (seq - 1) == 0
    qrow = lambda w: pl.BlockSpec((tq, w), lambda b, q: (b * nq + q, 0))
    brow = lambda w: pl.BlockSpec((seq, w), lambda b, q: (b, 0))
    nhq = N_HEADS_A * tq
    return pl.pallas_call(
        functools.partial(_dsa_kernel, n_sel=n_sel, seq=seq),
        grid=(batch, nq),
        in_specs=[pl.BlockSpec(memory_space=pltpu.SMEM),
                  qrow(N_HEADS_A * KV_RANK), qrow(IDX_HEADS * IDX_DIM), qrow(LANES),
                  brow(2 * KV_RANK), brow(LANES)],
        out_specs=qrow(N_HEADS_A * KV_RANK),
        out_shape=jax.ShapeDtypeStruct((n, N_HEADS_A * KV_RANK), BF16),
        scratch_shapes=[pltpu.VMEM((nhq, KV_RANK), BF16), pltpu.VMEM((nhq, LANES), BF16),
                        pltpu.VMEM((nhq, LANES), F32), pltpu.VMEM((nq, tq, tq), jnp.int32),
                        pltpu.VMEM((nq, tq, tq), jnp.int32), pltpu.VMEM((8, tq), jnp.int32),
                        pltpu.VMEM((3, nhq, tq), F32),
                        pltpu.VMEM((nhq, LANES), F32), pltpu.VMEM((nhq, 2 * KV_RANK), F32)],
        compiler_params=pltpu.CompilerParams(dimension_semantics=("arbitrary", "arbitrary"),
                                             vmem_limit_bytes=VMEM_LIMIT),
        name="dsa",
    )(rel_bias, qabs, qidx, widx, ckv, kk)


def _stick_kernel(q_ref, k_ref, v_ref, o_ref):
    tq = tk = ATT_TILE
    qi = pl.program_id(1)
    npair = WIDTH_B // LANES
    lane = lax.broadcasted_iota(jnp.int32, (tq, LANES), 1)
    row = lax.broadcasted_iota(jnp.int32, (tq, tk), 0)
    col = lax.broadcasted_iota(jnp.int32, (tq, tk), 1)
    causal = col < row
    upper = jnp.where(row > col, 1.0, 0.0).astype(BF16)
    upper_ones = jnp.concatenate([upper, jnp.ones((tk, LANES), BF16)], axis=1)
    lo_half = jnp.where(lane < HEAD_DIM, 1.0, 0.0)
    causal2 = jnp.concatenate([causal, causal], axis=0)
    q_pairs = []
    for p in range(npair):
        q2 = q_ref[:, p * LANES:(p + 1) * LANES].astype(F32)
        q_pairs.append(jnp.concatenate([q2 * lo_half, q2 * (1.0 - lo_half)], axis=0).astype(BF16))

    def tile(kj, r_sums, outs, masked):
        ks = pl.multiple_of(kj * tk, tk)
        log_sig, log_1m = [], []
        for p in range(npair):
            z = _dot_nt(q_pairs[p], k_ref[pl.ds(ks, tk), p * LANES:(p + 1) * LANES])
            sp = jnp.maximum(z, 0.0) + jnp.log(1.0 + jnp.exp(-jnp.abs(z)))
            lm = jnp.where(causal2, -sp, 0.0) if masked else -sp
            log_sig.append(z - sp)
            log_1m.append(lm.astype(BF16))
        sums = _dot(jnp.concatenate(log_1m, axis=0), upper_ones)
        new_r, new_o = [], []
        for p in range(npair):
            sm = sums[p * 2 * tq:(p + 1) * 2 * tq]
            after = sm[:, :tk] + jnp.concatenate([r_sums[p]] * (tk // LANES), axis=1)
            a = jnp.exp(log_sig[p] + after)
            if masked:
                a = jnp.where(causal2, a, 0.0)
            res = _dot(a.astype(BF16), v_ref[pl.ds(ks, tk), p * LANES:(p + 1) * LANES])
            new_o.append(outs[p] + jnp.where(lane < HEAD_DIM, res[:tq], res[tq:]))
            new_r.append(r_sums[p] + sm[:, tk:])
        return tuple(new_r), tuple(new_o)

    zeros = jnp.zeros((tq, LANES), F32)
    r_sums, outs = tile(qi, (jnp.zeros((2 * tq, LANES), F32),) * npair, (zeros,) * npair, True)

    def cond(c):
        it, r_sums, _ = c
        return (it < qi) & (jnp.max(functools.reduce(jnp.maximum, r_sums)) > EXP_UNDERFLOW)

    def body(c):
        it, r_sums, outs = c
        r_sums, outs = tile(qi - 1 - it, r_sums, outs, False)
        return it + 1, r_sums, outs

    _, _, outs = lax.while_loop(cond, body, (jnp.int32(0), r_sums, outs))
    for p in range(npair):
        o_ref[:, p * LANES:(p + 1) * LANES] = outs[p].astype(BF16)


def _stick(qb, kb, vb, batch, seq):
    n = qb.shape[0]
    tq = ATT_TILE
    nq = seq // tq
    qspec = pl.BlockSpec((tq, WIDTH_B), lambda b, q: (b * nq + q, 0))
    kspec = pl.BlockSpec((seq, WIDTH_B), lambda b, q: (b, 0))
    return pl.pallas_call(
        _stick_kernel,
        grid=(batch, nq),
        in_specs=[qspec, kspec, kspec],
        out_specs=qspec,
        out_shape=jax.ShapeDtypeStruct((n, WIDTH_B), BF16),
        compiler_params=pltpu.CompilerParams(dimension_semantics=("arbitrary",) * 2, vmem_limit_bytes=VMEM_LIMIT),
        name="stick",
    )(qb, kb, vb)


def _merge_kernel(x_ref, ol_ref, ob_ref, sa_ref, sb_ref, wuv_ref, wba_ref, wbb_ref, wout_ref, g_ref,
                  wrh_ref, wrl_ref, br_ref, x1_ref, h2_ref, comb_ref):
    oa = _dot(ol_ref[...], wuv_ref[...]).astype(BF16)
    ya = _dot(oa, wba_ref[...])
    yb = _dot(ob_ref[...], wbb_ref[...])
    merged = sa_ref[...].astype(F32) * ya + sb_ref[...].astype(F32) * yb
    x1 = x_ref[...] + _dot(merged.astype(BF16), wout_ref[...])
    x1_ref[...] = x1
    h2 = _rms(x1, g_ref[...])
    h2_hi = h2.astype(BF16)
    h2_ref[...] = h2_hi
    h2_lo = (h2 - h2_hi.astype(F32)).astype(BF16)
    logits = (_dot(h2_hi, wrh_ref[...]) + _dot(h2_lo, wrh_ref[...]) + _dot(h2_hi, wrl_ref[...])) + br_ref[...]
    lane = lax.broadcasted_iota(jnp.int32, logits.shape, 1).astype(F32)
    ninf = -jnp.inf

    def first_max(v):
        m = jnp.max(v, axis=1, keepdims=True)
        return m, jnp.min(jnp.where(v == m, lane, 1e9), axis=1, keepdims=True)

    gmask = (lane >= N_EXPERTS) & (lane < N_EXPERTS + N_GROUPS)
    gl = jnp.where(gmask, logits, ninf)
    gmax, gidx = first_max(gl)
    p_g = 1.0 / jnp.sum(jnp.where(gmask, jnp.exp(gl - gmax), 0.0), axis=1, keepdims=True)
    e_lo = (gidx - N_EXPERTS) * EXPERTS_PER_GROUP
    el = jnp.where((lane >= e_lo) & (lane < e_lo + EXPERTS_PER_GROUP), logits, ninf)
    v1, i1 = first_max(el)
    el2 = jnp.where(lane == i1, ninf, el)
    v2, i2 = first_max(el2)
    e2 = jnp.exp(v2 - v1)
    w1 = 1.0 / (1.0 + e2)
    comb_ref[...] = jnp.where(lane == i1, w1 * p_g,
                              jnp.where(lane == i2, (e2 * w1) * p_g, jnp.where(lane == gidx, 1.0, 0.0)))


def _merge(x, olat, ob, sa, sb, wuv, wba, wbb, wout, g, wrh, wrl, br):
    n, d = x.shape
    tm = ROW_TILE
    row = lambda w: pl.BlockSpec((tm, w), lambda i: (i, 0))
    ws = (wuv, wba, wbb, wout, g, wrh, wrl, br)
    return pl.pallas_call(
        _merge_kernel,
        grid=(n // tm,),
        in_specs=[row(d), row(olat.shape[1]), row(ob.shape[1]), row(d), row(d)] + [_full_spec(w.shape) for w in ws],
        out_specs=[row(d), row(d), row(LANES)],
        out_shape=[jax.ShapeDtypeStruct((n, d), F32), jax.ShapeDtypeStruct((n, d), BF16),
                   jax.ShapeDtypeStruct((n, LANES), F32)],
        compiler_params=pltpu.CompilerParams(dimension_semantics=("arbitrary",), vmem_limit_bytes=VMEM_LIMIT),
        name="merge",
    )(x, olat, ob, sa, sb, *ws)


def _moe_kernel(h_ref, comb_ref, wgu_ref, wd_ref, o_ref, slotc_s, slotr_s, split_s, acc_s, rng_s):
    t, d = h_ref.shape
    c = MOE_CHUNK
    g = pl.program_id(1)
    lane = lax.broadcasted_iota(jnp.int32, (c, LANES), 1)
    group_lanes = (lane >= N_EXPERTS) & (lane < N_EXPERTS + N_GROUPS)

    @pl.when(g == 0)
    def _():
        r_i = lax.broadcasted_iota(jnp.int32, (c, c), 0)
        c_i = lax.broadcasted_iota(jnp.int32, (c, c), 1)
        lower = jnp.where(c_i < r_i, 1.0, 0.0).astype(BF16)
        counts = jnp.zeros((1, LANES), F32)
        ranks = []
        for blk in range(t // c):
            oh = jnp.where(group_lanes, comb_ref[blk * c:(blk + 1) * c, :], 0.0)
            before = _dot(lower, oh.astype(BF16)) + counts
            ranks.append(jnp.sum(before * oh, axis=1, keepdims=True))
            counts = counts + jnp.sum(oh, axis=0, keepdims=True)
        lane1 = lax.broadcasted_iota(jnp.int32, (1, LANES), 1)
        start = jnp.int32(0)
        seg = jnp.zeros((1, LANES), F32)
        for k in range(N_GROUPS):
            n_k = jnp.sum(jnp.where(lane1 == N_EXPERTS + k, counts, 0.0)).astype(jnp.int32)
            seg = jnp.where(lane1 == N_EXPERTS + k, (start * c).astype(F32), seg)
            rng_s[k] = start
            start = start + ((n_k + (c - 1)) >> (c.bit_length() - 1))
            rng_s[N_GROUPS + k] = start
        for blk in range(t // c):
            rows = slice(blk * c, (blk + 1) * c)
            oh = jnp.where(group_lanes, comb_ref[rows, :], 0.0)
            slot = ranks[blk] + jnp.sum(oh * seg, axis=1, keepdims=True)
            slotc_s[rows, :] = jnp.broadcast_to(slot, (c, LANES))
        slotr_s[...] = jnp.transpose(slotc_s[...])
        cb = comb_ref[...]
        hi = cb.astype(BF16)
        r1 = cb - hi.astype(F32)
        mid = r1.astype(BF16)
        split_s[0] = hi
        split_s[1] = mid
        split_s[2] = (r1 - mid.astype(F32)).astype(BF16)
        acc_s[...] = jnp.zeros(acc_s.shape, F32)

    def chunk_body(ci, carry):
        base = (ci * c).astype(F32)
        row_slot = lax.broadcasted_iota(jnp.int32, (c, t), 0).astype(F32) + base
        pc = jnp.where(slotr_s[0:1, :] == row_slot, 1.0, 0.0).astype(BF16)
        xs = _dot(pc, h_ref[...]).astype(BF16)
        cw = _dot(pc, split_s[0]) + _dot(pc, split_s[1]) + _dot(pc, split_s[2])
        y = jnp.zeros((c, d), F32)
        for e in range(EXPERTS_PER_GROUP):
            gu = _dot(xs, wgu_ref[e])
            gate = gu[:, :D_EXPERT]
            hid = gate * jax.nn.sigmoid(gate) * gu[:, D_EXPERT:]
            ce = jnp.sum(jnp.where(lane == g * EXPERTS_PER_GROUP + e, cw, 0.0), axis=1, keepdims=True)
            y = y + _dot((hid * ce).astype(BF16), wd_ref[e])
        col_slot = lax.broadcasted_iota(jnp.int32, (t, c), 1).astype(F32) + base
        pct = jnp.where(jnp.concatenate([slotc_s[...]] * (c // LANES), axis=1) == col_slot, 1.0, 0.0).astype(BF16)
        acc_s[...] += _dot(pct, y.astype(BF16))
        return carry

    lax.fori_loop(rng_s[g], rng_s[N_GROUPS + g], chunk_body, 0)

    @pl.when(g == N_GROUPS - 1)
    def _():
        o_ref[...] = acc_s[...].astype(BF16)


def _moe(h2, comb, wgu, wd):
    n, d = h2.shape
    tm = MOE_ROW_TILE
    row = lambda w: pl.BlockSpec((tm, w), lambda i, g: (i, 0))
    return pl.pallas_call(
        _moe_kernel,
        grid=(n // tm, N_GROUPS),
        in_specs=[row(d), row(LANES),
                  pl.BlockSpec((EXPERTS_PER_GROUP, d, 2 * D_EXPERT), lambda i, g: (g, 0, 0)),
                  pl.BlockSpec((EXPERTS_PER_GROUP, D_EXPERT, d), lambda i, g: (g, 0, 0))],
        out_specs=row(d),
        out_shape=jax.ShapeDtypeStruct((n, d), BF16),
        scratch_shapes=[pltpu.VMEM((tm, LANES), F32), pltpu.VMEM((LANES, tm), F32),
                        pltpu.VMEM((3, tm, LANES), BF16), pltpu.VMEM((tm, d), F32),
                        pltpu.SMEM((2 * N_GROUPS,), jnp.int32)],
        compiler_params=pltpu.CompilerParams(dimension_semantics=("arbitrary", "arbitrary"),
                                             vmem_limit_bytes=VMEM_LIMIT),
        name="moe",
    )(h2, comb, wgu, wd)


def _ple_kernel(x_ref, m_ref, p_ref, g_ref, wpg_ref, wple_ref, gf_ref, o_ref, *, last):
    x2 = x_ref[...] + m_ref[...].astype(F32)
    h3 = _rms(x2, g_ref[...]).astype(BF16)
    gate = jax.nn.sigmoid(_dot(h3, wpg_ref[...]))
    x3 = x2 + _dot(p_ref[...].astype(BF16), wple_ref[...]) * gate
    o_ref[...] = _rms(x3, gf_ref[...]) if last else x3


def _ple(x1, moe, p, g, wpg, wple, gf, last):
    n, d = x1.shape
    tm = ROW_TILE
    row = lambda w: pl.BlockSpec((tm, w), lambda i: (i, 0))
    ws = (g, wpg, wple, gf)
    return pl.pallas_call(
        functools.partial(_ple_kernel, last=last),
        grid=(n // tm,),
        in_specs=[row(d), row(d), row(p.shape[1])] + [_full_spec(w.shape) for w in ws],
        out_specs=row(d),
        out_shape=jax.ShapeDtypeStruct((n, d), F32),
        compiler_params=pltpu.CompilerParams(dimension_semantics=("arbitrary",), vmem_limit_bytes=VMEM_LIMIT),
        name="ple",
    )(x1, moe, p, *ws)


def _block_diag(blocks):
    h, r, c = blocks.shape
    eye = jnp.eye(h, dtype=blocks.dtype)
    return (eye[:, None, :, None] * blocks[:, :, None, :]).reshape(h * r, h * c)


def kernel(x, p, attn_norm, w_in, kv_norm, w_uk, w_uv, rel_bias, w_branch_a, w_branch_b, w_out, ffn_norm,
           w_r1, b_r1, w_r2, b_r2, w_gate, w_up, w_down, ple_norm, w_ple_gate, w_ple, final_norm):
    batch, seq, d = x.shape
    n = batch * seq
    depth = w_in.shape[0]
    xf = x.reshape(n, d).astype(F32)
    widths = [WIDTH_A, KV_RANK, IDX_HEADS * IDX_DIM, IDX_DIM, IDX_HEADS, 3 * WIDTH_B, d, d]
    starts = [sum(widths[:k]) for k in range(len(widths))]
    for i in range(depth):
        w_qa, w_ckv, w_qi, w_ki, w_wi, w_qkv, w_ga, w_gb = [
            w_in[i][:, s:s + w].astype(BF16) for s, w in zip(starts, widths)]
        w_kk = jnp.concatenate([w_ki, w_ki], axis=1)
        w_wi = jnp.pad(w_wi, ((0, 0), (0, LANES - IDX_HEADS)))
        wuk_bd = _block_diag(jnp.swapaxes(w_uk[i], 1, 2)).astype(BF16)
        wuv_bd = _block_diag(w_uv[i]).astype(BF16)
        qabs, ckv, qidx, kk, widx, qb, kb, vb, sa, sb = _proj(
            xf, attn_norm[i][None].astype(F32), w_qa, wuk_bd, w_ckv, kv_norm[i][None].astype(F32),
            w_qi, w_kk, w_wi, w_qkv, w_ga, w_gb)
        olat = _dsa(rel_bias.astype(F32), qabs, qidx, widx, ckv, kk, batch, seq)
        ob = _stick(qb, kb, vb, batch, seq)
        w_r = jnp.concatenate([jnp.transpose(w_r2[i], (1, 0, 2)).reshape(d, N_EXPERTS), w_r1[i]], axis=1)
        w_r = jnp.pad(w_r.astype(F32), ((0, 0), (0, LANES - N_EXPERTS - N_GROUPS)))
        w_r_hi = w_r.astype(BF16)
        w_r_lo = (w_r - w_r_hi.astype(F32)).astype(BF16)
        b_r = jnp.pad(jnp.concatenate([b_r2[i].reshape(-1), b_r1[i]]).astype(F32),
                      (0, LANES - N_EXPERTS - N_GROUPS))[None]
        x1, h2, comb = _merge(xf, olat, ob, sa, sb, wuv_bd, w_branch_a[i].astype(BF16), w_branch_b[i].astype(BF16),
                              w_out[i].astype(BF16), ffn_norm[i][None].astype(F32), w_r_hi, w_r_lo, b_r)
        wgu = jnp.concatenate([w_gate[i], w_up[i]], axis=2).astype(BF16)
        moe = _moe(h2, comb, wgu, w_down[i].astype(BF16))
        xf = _ple(x1, moe, p[i].reshape(n, -1).astype(F32), ple_norm[i][None].astype(F32), w_ple_gate[i].astype(BF16),
                  w_ple[i].astype(BF16), final_norm[None].astype(F32), last=(i == depth - 1))
    return xf.reshape(batch, seq, d).astype(x.dtype)
```

```python
import functools
import math

import jax
import jax.numpy as jnp
from jax import lax
from jax.experimental import pallas as pl
from jax.experimental.pallas import tpu as pltpu

D_MODEL = 1024
N_HEADS_A = 8
HEAD_DIM = 64
WIDTH_A = N_HEADS_A * HEAD_DIM
KV_RANK = 128
IDX_HEADS = 8
IDX_DIM = 64
TOPK_MAX = 256
N_HEADS_B = 8
WIDTH_B = N_HEADS_B * HEAD_DIM
N_BUCKETS = 32
MAX_EXACT = N_BUCKETS // 2
MAX_DISTANCE = 128
ATTN_SCALE = HEAD_DIM ** -0.5
IDX_SCALE = (IDX_HEADS ** -0.5) * (IDX_DIM ** -0.5)
N_GROUPS = 4
EXPERTS_PER_GROUP = 8
N_EXPERTS = N_GROUPS * EXPERTS_PER_GROUP
D_EXPERT = 256
PLE_DIM = 256
EPS = 1e-6
LOG2E = 1.4426950408889634

LANES = 128
ROW_TILE = 512
MOE_ROW_TILE = 1024
MOE_CHUNK = 256
ATT_TILE = 256
VMEM_LIMIT = 56 * 1024 * 1024
NEG_BIG = -1e30
EXP_UNDERFLOW = -120.0
INT_MIN = -2 ** 31

F32 = jnp.float32
BF16 = jnp.bfloat16


def _rms(x, g):
    return x * lax.rsqrt(jnp.mean(x * x, axis=-1, keepdims=True) + EPS) * g


def _dot(a, b):
    return jnp.dot(a, b, preferred_element_type=F32)


def _dot_nt(a, b):
    return lax.dot_general(a, b, (((1,), (1,)), ((), ())), preferred_element_type=F32)


def _full_spec(shape):
    nd = len(shape)
    return pl.BlockSpec(shape, lambda *_: (0,) * nd)


def _proj_kernel(x_ref, g_ref, wqa_ref, wuk_ref, wckv_ref, kvg_ref, wqi_ref, wkk_ref, wwi_ref,
                 wqkv_ref, wga_ref, wgb_ref,
                 qabs_ref, ckv_ref, ckvt_ref, qidx_ref, kk_ref, widx_ref, qb_ref, kb_ref, vb_ref, sa_ref, sb_ref):
    h = _rms(x_ref[...], g_ref[...]).astype(BF16)
    qa = _dot(h, wqa_ref[...]).astype(BF16)
    qabs_ref[...] = (_dot(qa, wuk_ref[...]) * (ATTN_SCALE * LOG2E)).astype(BF16)
    c = _rms(_dot(h, wckv_ref[...]), kvg_ref[...])
    ckv_ref[...] = c.astype(BF16)
    c_ext = jnp.concatenate([c, jnp.ones_like(c)], axis=1)
    for j in range(ckvt_ref.shape[0]):
        ckvt_ref[j] = jnp.transpose(c_ext[j * ATT_TILE:(j + 1) * ATT_TILE, :]).astype(BF16)
    qidx_ref[...] = _dot(h, wqi_ref[...]).astype(BF16)
    kk_ref[...] = _dot(h, wkk_ref[...]).astype(BF16)
    widx_ref[...] = _dot(h, wwi_ref[...]) * IDX_SCALE
    qkv = _dot(h, wqkv_ref[...])
    qb_ref[...] = (qkv[:, :WIDTH_B] * ATTN_SCALE).astype(BF16)
    kb_ref[...] = qkv[:, WIDTH_B:2 * WIDTH_B].astype(BF16)
    vb_ref[...] = qkv[:, 2 * WIDTH_B:].astype(BF16)
    sa_ref[...] = jax.nn.sigmoid(_dot(h, wga_ref[...])).astype(BF16)
    sb_ref[...] = jax.nn.sigmoid(_dot(h, wgb_ref[...])).astype(BF16)


def _proj(x, g, wqa, wuk, wckv, kvg, wqi, wkk, wwi, wqkv, wga, wgb):
    n, d = x.shape
    tm = ROW_TILE
    row = lambda w: pl.BlockSpec((tm, w), lambda i: (i, 0))
    ws = (g, wqa, wuk, wckv, kvg, wqi, wkk, wwi, wqkv, wga, wgb)
    outs = [(N_HEADS_A * KV_RANK, BF16), (KV_RANK, BF16), None, (IDX_HEADS * IDX_DIM, BF16), (LANES, BF16),
            (LANES, F32), (WIDTH_B, BF16), (WIDTH_B, BF16), (WIDTH_B, BF16), (d, BF16), (d, BF16)]
    t = ATT_TILE
    ckvt_spec = pl.BlockSpec((tm // t, 2 * KV_RANK, t), lambda i: (i, 0, 0))
    ckvt_shape = jax.ShapeDtypeStruct((n // t, 2 * KV_RANK, t), BF16)
    return pl.pallas_call(
        _proj_kernel,
        grid=(n // tm,),
        in_specs=[row(d)] + [_full_spec(w.shape) for w in ws],
        out_specs=[ckvt_spec if o is None else row(o[0]) for o in outs],
        out_shape=[ckvt_shape if o is None else jax.ShapeDtypeStruct((n, o[0]), o[1]) for o in outs],
        compiler_params=pltpu.CompilerParams(dimension_semantics=("arbitrary",), vmem_limit_bytes=VMEM_LIMIT),
        name="proj",
    )(x, *ws)


def _sortable_key(s):
    bits = lax.bitcast_convert_type(s, jnp.int32)
    key = bits ^ ((bits >> 31) & jnp.int32(0x7FFFFFFF))
    return jnp.where(key == -1, 0, key)


def _t5_bucket(dist):
    dist = jnp.maximum(dist, 0)
    d_f = jnp.maximum(dist, 1).astype(F32)
    large = MAX_EXACT + (jnp.log(d_f / MAX_EXACT) / math.log(MAX_DISTANCE / MAX_EXACT)
                         * (N_BUCKETS - MAX_EXACT)).astype(jnp.int32)
    large = jnp.minimum(large, N_BUCKETS - 1)
    return jnp.where(dist < MAX_EXACT, dist, large)


def _dsa_kernel(relb_ref, qabs_ref, qidx_ref, widx_ref, ckv_ref, ckvt_ref, kk_ref, o_ref,
                qa_s, qi_s, keyt_s, dig_s, act_s, cs_s, bias_s, m_s, acc_s, *, n_sel, seq):
    tq = tk = ATT_TILE
    nh = N_HEADS_A
    nhq = nh * tq
    qi = pl.program_id(1)
    n_kt = qi + 1
    kpos = lax.broadcasted_iota(jnp.int32, (tk, tq), 0)
    qpos = lax.broadcasted_iota(jnp.int32, (tk, tq), 1)

    @pl.when((pl.program_id(0) == 0) & (qi == 0))
    def _():
        for t in range(3):
            bucket = _t5_bucket(t * tq + qpos - kpos if t < 2 else jnp.full((tk, tq), 2 * tq, jnp.int32))
            for h in range(nh):
                b = jnp.zeros((tk, tq), F32)
                for bk in range(N_BUCKETS):
                    b = jnp.where(bucket == bk, relb_ref[bk, h], b)
                bias_s[t, :, h * tq:(h + 1) * tq] = b * LOG2E

    lane = lax.broadcasted_iota(jnp.int32, (tq, LANES), 1)
    lo_half = jnp.where(lane < IDX_DIM, 1.0, 0.0)
    hi_half = 1.0 - lo_half
    for h in range(nh):
        qa_s[h * tq:(h + 1) * tq, :] = qabs_ref[:, h * KV_RANK:(h + 1) * KV_RANK]
        pair = qidx_ref[:, (h // 2) * LANES:(h // 2 + 1) * LANES].astype(F32)
        qi_s[h * tq:(h + 1) * tq, :] = (pair * (lo_half if h % 2 == 0 else hi_half)).astype(BF16)
    w_t = jnp.transpose(widx_ref[...])
    w_row = jnp.concatenate([w_t[h:h + 1, :] for h in range(nh)], axis=1)

    def idx_body(kj, carry):
        ks = pl.multiple_of(kj * tk, tk)
        r = _dot_nt(kk_ref[pl.ds(ks, tk), :], qi_s[...])
        r = jnp.maximum(r, 0.0) * w_row
        s = r[:, :tq]
        for h in range(1, nh):
            s = s + r[:, h * tq:(h + 1) * tq]
        key = jnp.where((kpos > qpos) & (kj == qi), INT_MIN, _sortable_key(s))
        keyt_s[kj] = key
        u = key ^ INT_MIN
        for p in range(4):
            dig_s[p, kj] = (lax.shift_right_logical(u, 8 * p) & 0xFF).astype(F32).astype(BF16)
        return carry

    lax.fori_loop(0, n_kt, idx_body, 0)

    def count_active(pred):
        one, zero = jnp.ones((), BF16), jnp.zeros((), BF16)

        def body(kj, acc):
            c = jnp.where(pred(act_s[kj]), one, zero).reshape(tk // 16, 16, tq)
            part = c[0]
            for j in range(1, tk // 16):
                part = part + c[j]
            return acc + part.astype(F32)
        acc = lax.fori_loop(0, n_kt, body, jnp.zeros((16, tq), F32))
        return jnp.sum(acc, axis=0, keepdims=True)

    def to_digit(v):
        return v.astype(F32).astype(BF16)

    target = jnp.full((1, tq), float(n_sel), F32)
    tau_u = jnp.zeros((1, tq), jnp.int32)
    digit = jnp.zeros((1, tq), jnp.int32)
    for p in (3, 2, 1, 0):
        if p == 3:
            def init_body(kj, carry):
                act_s[kj] = dig_s[3, kj]
                return carry
            lax.fori_loop(0, n_kt, init_body, 0)
        else:
            prev = to_digit(digit)
            target = target - count_active(lambda a: a > prev)

            def narrow_body(kj, carry, prev=prev, p=p):
                act_s[kj] = jnp.where(act_s[kj] == prev, dig_s[p, kj], jnp.full((), -1.0, BF16))
                return carry
            lax.fori_loop(0, n_kt, narrow_body, 0)

        def bit_body(i, dgt, target=target):
            cand = dgt | jnp.left_shift(jnp.int32(1), 7 - i)
            cand_d = to_digit(cand)
            cnt = count_active(lambda a: a >= cand_d)
            return jnp.where(cnt >= target, cand, dgt)
        digit = lax.fori_loop(0, 8, bit_body, jnp.zeros((1, tq), jnp.int32))
        tau_u = tau_u | jnp.left_shift(digit, 8 * p)

    first = qi == 0
    tau = jnp.where(first, INT_MIN, tau_u ^ INT_MIN)
    last = to_digit(digit)
    n_gt = count_active(lambda a: a > last)
    n_eq = count_active(lambda a: a == last)
    need = target - n_gt

    def count_keys(pred):
        def body(kj, acc):
            c = jnp.where(pred(keyt_s[kj], kpos + kj * tk), 1.0, 0.0)
            return acc + jnp.sum(c.reshape(tk // 8, 8, tq), axis=0)
        acc = lax.fori_loop(0, n_kt, body, jnp.zeros((8, tq), F32))
        return jnp.sum(acc, axis=0, keepdims=True)

    cs_s[...] = jnp.full(cs_s.shape, seq, jnp.int32)

    @pl.when(jnp.logical_not(first) & (jnp.max(n_eq - need) > 0.0))
    def _():
        nbits = seq.bit_length() - 1

        def tie_body(i, d):
            cand = d | jnp.left_shift(jnp.int32(1), nbits - 1 - i)
            cnt = count_keys(lambda k, pos: (k == tau) & (pos < cand))
            return jnp.where(cnt < need, cand, d)

        d = lax.fori_loop(0, nbits, tie_body, jnp.zeros((1, tq), jnp.int32))
        cs_s[...] = jnp.broadcast_to(d, cs_s.shape)

    cstar = jnp.where(first, -1, cs_s[0:1, :])

    m_s[...] = jnp.full(m_s.shape, NEG_BIG, F32)
    acc_s[...] = jnp.zeros(acc_s.shape, F32)

    def att_body(kj, carry):
        ks = pl.multiple_of(kj * tk, tk)
        k = keyt_s[kj]
        sel = (k > tau) | ((k == tau) & (kpos + kj * tk <= cstar))
        selb = jnp.where(sel, 0.0, NEG_BIG)
        s = _dot_nt(ckv_ref[pl.ds(ks, tk), :], qa_s[...])
        s = s + bias_s[jnp.minimum(qi - kj, 2)] + jnp.concatenate([selb] * nh, axis=1)
        m_old = m_s[0:1, :]
        m_new = jnp.maximum(m_old, jnp.max(s, axis=0, keepdims=True))
        alpha = jnp.exp2(m_old - m_new)
        p = jnp.exp2(s - m_new).astype(BF16)
        acc_s[...] = acc_s[...] * alpha + _dot(ckvt_ref[kj], p)
        m_s[...] = jnp.broadcast_to(m_new, m_s.shape)
        return carry

    lax.fori_loop(0, n_kt, att_body, 0)

    acc = acc_s[...]
    o_t = acc[:KV_RANK] / acc[KV_RANK:]
    for h in range(nh):
        o_ref[:, h * KV_RANK:(h + 1) * KV_RANK] = jnp.transpose(o_t[:, h * tq:(h + 1) * tq]).astype(BF16)


def _dsa(rel_bias, qabs, qidx, widx, ckv, ckvt, kk, batch, seq):
    n = qabs.shape[0]
    tq = ATT_TILE
    nq = seq // tq
    n_sel = min(TOPK_MAX, seq // 4)
    assert n_sel == tq and seq % tq == 0 and seq & (seq - 1) == 0
    qrow = lambda w: pl.BlockSpec((tq, w), lambda b, q: (b * nq + q, 0))
    brow = lambda w: pl.BlockSpec((seq, w), lambda b, q: (b, 0))
    nhq = N_HEADS_A * tq
    return pl.pallas_call(
        functools.partial(_dsa_kernel, n_sel=n_sel, seq=seq),
        grid=(batch, nq),
        in_specs=[pl.BlockSpec(memory_space=pltpu.SMEM),
                  qrow(N_HEADS_A * KV_RANK), qrow(IDX_HEADS * IDX_DIM), qrow(LANES),
                  brow(KV_RANK), pl.BlockSpec((nq, 2 * KV_RANK, tq), lambda b, q: (b, 0, 0)), brow(LANES)],
        out_specs=qrow(N_HEADS_A * KV_RANK),
        out_shape=jax.ShapeDtypeStruct((n, N_HEADS_A * KV_RANK), BF16),
        scratch_shapes=[pltpu.VMEM((nhq, KV_RANK), BF16), pltpu.VMEM((nhq, LANES), BF16),
                        pltpu.VMEM((nq, tq, tq), jnp.int32), pltpu.VMEM((4, nq, tq, tq), BF16),
                        pltpu.VMEM((nq, tq, tq), BF16), pltpu.VMEM((8, tq), jnp.int32),
                        pltpu.VMEM((3, tq, nhq), F32),
                        pltpu.VMEM((8, nhq), F32), pltpu.VMEM((2 * KV_RANK, nhq), F32)],
        compiler_params=pltpu.CompilerParams(dimension_semantics=("arbitrary", "arbitrary"),
                                             vmem_limit_bytes=VMEM_LIMIT),
        name="dsa",
    )(rel_bias, qabs, qidx, widx, ckv, ckvt, kk)


def _stick_kernel(q_ref, k_ref, v_ref, o_ref):
    tq = tk = ATT_TILE
    qi = pl.program_id(1)
    npair = WIDTH_B // LANES
    lane = lax.broadcasted_iota(jnp.int32, (tq, LANES), 1)
    row = lax.broadcasted_iota(jnp.int32, (tq, tk), 0)
    col = lax.broadcasted_iota(jnp.int32, (tq, tk), 1)
    causal = col < row
    upper = jnp.where(row > col, 1.0, 0.0).astype(BF16)
    upper_ones = jnp.concatenate([upper, jnp.ones((tk, LANES), BF16)], axis=1)
    lo_half = jnp.where(lane < HEAD_DIM, 1.0, 0.0)
    causal2 = jnp.concatenate([causal, causal], axis=0)
    q_pairs = []
    for p in range(npair):
        q2 = q_ref[:, p * LANES:(p + 1) * LANES].astype(F32)
        q_pairs.append(jnp.concatenate([q2 * lo_half, q2 * (1.0 - lo_half)], axis=0).astype(BF16))

    def tile(kj, r_sums, outs, masked):
        ks = pl.multiple_of(kj * tk, tk)
        log_sig, log_1m = [], []
        for p in range(npair):
            z = _dot_nt(q_pairs[p], k_ref[pl.ds(ks, tk), p * LANES:(p + 1) * LANES])
            sp = jnp.maximum(z, 0.0) + jnp.log(1.0 + jnp.exp(-jnp.abs(z)))
            lm = jnp.where(causal2, -sp, 0.0) if masked else -sp
            log_sig.append(z - sp)
            log_1m.append(lm.astype(BF16))
        sums = _dot(jnp.concatenate(log_1m, axis=0), upper_ones)
        new_r, new_o = [], []
        for p in range(npair):
            sm = sums[p * 2 * tq:(p + 1) * 2 * tq]
            after = sm[:, :tk] + jnp.concatenate([r_sums[p]] * (tk // LANES), axis=1)
            a = jnp.exp(log_sig[p] + after)
            if masked:
                a = jnp.where(causal2, a, 0.0)
            res = _dot(a.astype(BF16), v_ref[pl.ds(ks, tk), p * LANES:(p + 1) * LANES])
            new_o.append(outs[p] + jnp.where(lane < HEAD_DIM, res[:tq], res[tq:]))
            new_r.append(r_sums[p] + sm[:, tk:])
        return tuple(new_r), tuple(new_o)

    zeros = jnp.zeros((tq, LANES), F32)
    r_sums, outs = tile(qi, (jnp.zeros((2 * tq, LANES), F32),) * npair, (zeros,) * npair, True)

    def cond(c):
        it, r_sums, _ = c
        return (it < qi) & (jnp.max(functools.reduce(jnp.maximum, r_sums)) > EXP_UNDERFLOW)

    def body(c):
        it, r_sums, outs = c
        r_sums, outs = tile(qi - 1 - it, r_sums, outs, False)
        return it + 1, r_sums, outs

    _, _, outs = lax.while_loop(cond, body, (jnp.int32(0), r_sums, outs))
    for p in range(npair):
        o_ref[:, p * LANES:(p + 1) * LANES] = outs[p].astype(BF16)


def _stick(qb, kb, vb, batch, seq):
    n = qb.shape[0]
    tq = ATT_TILE
    nq = seq // tq
    qspec = pl.BlockSpec((tq, WIDTH_B), lambda b, q: (b * nq + q, 0))
    kspec = pl.BlockSpec((seq, WIDTH_B), lambda b, q: (b, 0))
    return pl.pallas_call(
        _stick_kernel,
        grid=(batch, nq),
        in_specs=[qspec, kspec, kspec],
        out_specs=qspec,
        out_shape=jax.ShapeDtypeStruct((n, WIDTH_B), BF16),
        compiler_params=pltpu.CompilerParams(dimension_semantics=("arbitrary",) * 2, vmem_limit_bytes=VMEM_LIMIT),
        name="stick",
    )(qb, kb, vb)


def _merge_kernel(x_ref, ol_ref, ob_ref, sa_ref, sb_ref, wuv_ref, wba_ref, wbb_ref, wout_ref, g_ref,
                  wrh_ref, wrl_ref, br_ref, x1_ref, h2_ref, comb_ref):
    oa = _dot(ol_ref[...], wuv_ref[...]).astype(BF16)
    ya = _dot(oa, wba_ref[...])
    yb = _dot(ob_ref[...], wbb_ref[...])
    merged = sa_ref[...].astype(F32) * ya + sb_ref[...].astype(F32) * yb
    x1 = x_ref[...] + _dot(merged.astype(BF16), wout_ref[...])
    x1_ref[...] = x1
    h2 = _rms(x1, g_ref[...])
    h2_hi = h2.astype(BF16)
    h2_ref[...] = h2_hi
    h2_lo = (h2 - h2_hi.astype(F32)).astype(BF16)
    logits = (_dot(h2_hi, wrh_ref[...]) + _dot(h2_lo, wrh_ref[...]) + _dot(h2_hi, wrl_ref[...])) + br_ref[...]
    lane = lax.broadcasted_iota(jnp.int32, logits.shape, 1).astype(F32)
    ninf = -jnp.inf

    def first_max(v):
        m = jnp.max(v, axis=1, keepdims=True)
        return m, jnp.min(jnp.where(v == m, lane, 1e9), axis=1, keepdims=True)

    gmask = (lane >= N_EXPERTS) & (lane < N_EXPERTS + N_GROUPS)
    gl = jnp.where(gmask, logits, ninf)
    gmax, gidx = first_max(gl)
    p_g = 1.0 / jnp.sum(jnp.where(gmask, jnp.exp(gl - gmax), 0.0), axis=1, keepdims=True)
    e_lo = (gidx - N_EXPERTS) * EXPERTS_PER_GROUP
    el = jnp.where((lane >= e_lo) & (lane < e_lo + EXPERTS_PER_GROUP), logits, ninf)
    v1, i1 = first_max(el)
    el2 = jnp.where(lane == i1, ninf, el)
    v2, i2 = first_max(el2)
    e2 = jnp.exp(v2 - v1)
    w1 = 1.0 / (1.0 + e2)
    comb_ref[...] = jnp.where(lane == i1, w1 * p_g,
                              jnp.where(lane == i2, (e2 * w1) * p_g, jnp.where(lane == gidx, 1.0, 0.0)))


def _merge(x, olat, ob, sa, sb, wuv, wba, wbb, wout, g, wrh, wrl, br):
    n, d = x.shape
    tm = ROW_TILE
    row = lambda w: pl.BlockSpec((tm, w), lambda i: (i, 0))
    ws = (wuv, wba, wbb, wout, g, wrh, wrl, br)
    return pl.pallas_call(
        _merge_kernel,
        grid=(n // tm,),
        in_specs=[row(d), row(olat.shape[1]), row(ob.shape[1]), row(d), row(d)] + [_full_spec(w.shape) for w in ws],
        out_specs=[row(d), row(d), row(LANES)],
        out_shape=[jax.ShapeDtypeStruct((n, d), F32), jax.ShapeDtypeStruct((n, d), BF16),
                   jax.ShapeDtypeStruct((n, LANES), F32)],
        compiler_params=pltpu.CompilerParams(dimension_semantics=("arbitrary",), vmem_limit_bytes=VMEM_LIMIT),
        name="merge",
    )(x, olat, ob, sa, sb, *ws)


def _moe_kernel(h_ref, comb_ref, wgu_ref, wd_ref, o_ref, slotc_s, slotr_s, split_s, acc_s, rng_s):
    t, d = h_ref.shape
    c = MOE_CHUNK
    g = pl.program_id(1)
    lane = lax.broadcasted_iota(jnp.int32, (c, LANES), 1)
    group_lanes = (lane >= N_EXPERTS) & (lane < N_EXPERTS + N_GROUPS)

    @pl.when(g == 0)
    def _():
        r_i = lax.broadcasted_iota(jnp.int32, (c, c), 0)
        c_i = lax.broadcasted_iota(jnp.int32, (c, c), 1)
        lower = jnp.where(c_i < r_i, 1.0, 0.0).astype(BF16)
        counts = jnp.zeros((1, LANES), F32)
        ranks = []
        for blk in range(t // c):
            oh = jnp.where(group_lanes, comb_ref[blk * c:(blk + 1) * c, :], 0.0)
            before = _dot(lower, oh.astype(BF16)) + counts
            ranks.append(jnp.sum(before * oh, axis=1, keepdims=True))
            counts = counts + jnp.sum(oh, axis=0, keepdims=True)
        lane1 = lax.broadcasted_iota(jnp.int32, (1, LANES), 1)
        start = jnp.int32(0)
        seg = jnp.zeros((1, LANES), F32)
        for k in range(N_GROUPS):
            n_k = jnp.sum(jnp.where(lane1 == N_EXPERTS + k, counts, 0.0)).astype(jnp.int32)
            seg = jnp.where(lane1 == N_EXPERTS + k, (start * c).astype(F32), seg)
            rng_s[k] = start
            start = start + ((n_k + (c - 1)) >> (c.bit_length() - 1))
            rng_s[N_GROUPS + k] = start
        for blk in range(t // c):
            rows = slice(blk * c, (blk + 1) * c)
            oh = jnp.where(group_lanes, comb_ref[rows, :], 0.0)
            slot = ranks[blk] + jnp.sum(oh * seg, axis=1, keepdims=True)
            slotc_s[rows, :] = jnp.broadcast_to(slot, (c, LANES))
        slotr_s[...] = jnp.transpose(slotc_s[...])
        cb = comb_ref[...]
        hi = cb.astype(BF16)
        r1 = cb - hi.astype(F32)
        mid = r1.astype(BF16)
        split_s[0] = hi
        split_s[1] = mid
        split_s[2] = (r1 - mid.astype(F32)).astype(BF16)
        acc_s[...] = jnp.zeros(acc_s.shape, F32)

    def chunk_body(ci, carry):
        base = (ci * c).astype(F32)
        row_slot = lax.broadcasted_iota(jnp.int32, (c, t), 0).astype(F32) + base
        pc = jnp.where(slotr_s[0:1, :] == row_slot, 1.0, 0.0).astype(BF16)
        xs = _dot(pc, h_ref[...]).astype(BF16)
        cw = _dot(pc, split_s[0]) + _dot(pc, split_s[1]) + _dot(pc, split_s[2])
        y = jnp.zeros((c, d), F32)
        for e in range(EXPERTS_PER_GROUP):
            gu = _dot(xs, wgu_ref[e])
            gate = gu[:, :D_EXPERT]
            hid = gate * jax.nn.sigmoid(gate) * gu[:, D_EXPERT:]
            ce = jnp.sum(jnp.where(lane == g * EXPERTS_PER_GROUP + e, cw, 0.0), axis=1, keepdims=True)
            y = y + _dot((hid * ce).astype(BF16), wd_ref[e])
        col_slot = lax.broadcasted_iota(jnp.int32, (t, c), 1).astype(F32) + base
        pct = jnp.where(jnp.concatenate([slotc_s[...]] * (c // LANES), axis=1) == col_slot, 1.0, 0.0).astype(BF16)
        acc_s[...] += _dot(pct, y.astype(BF16))
        return carry

    lax.fori_loop(rng_s[g], rng_s[N_GROUPS + g], chunk_body, 0)

    @pl.when(g == N_GROUPS - 1)
    def _():
        o_ref[...] = acc_s[...].astype(BF16)


def _moe(h2, comb, wgu, wd):
    n, d = h2.shape
    tm = MOE_ROW_TILE
    row = lambda w: pl.BlockSpec((tm, w), lambda i, g: (i, 0))
    return pl.pallas_call(
        _moe_kernel,
        grid=(n // tm, N_GROUPS),
        in_specs=[row(d), row(LANES),
                  pl.BlockSpec((EXPERTS_PER_GROUP, d, 2 * D_EXPERT), lambda i, g: (g, 0, 0)),
                  pl.BlockSpec((EXPERTS_PER_GROUP, D_EXPERT, d), lambda i, g: (g, 0, 0))],
        out_specs=row(d),
        out_shape=jax.ShapeDtypeStruct((n, d), BF16),
        scratch_shapes=[pltpu.VMEM((tm, LANES), F32), pltpu.VMEM((LANES, tm), F32),
                        pltpu.VMEM((3, tm, LANES), BF16), pltpu.VMEM((tm, d), F32),
                        pltpu.SMEM((2 * N_GROUPS,), jnp.int32)],
        compiler_params=pltpu.CompilerParams(dimension_semantics=("arbitrary", "arbitrary"),
                                             vmem_limit_bytes=VMEM_LIMIT),
        name="moe",
    )(h2, comb, wgu, wd)


def _ple_kernel(x_ref, m_ref, p_ref, g_ref, wpg_ref, wple_ref, gf_ref, o_ref, *, last):
    x2 = x_ref[...] + m_ref[...].astype(F32)
    h3 = _rms(x2, g_ref[...]).astype(BF16)
    gate = jax.nn.sigmoid(_dot(h3, wpg_ref[...]))
    x3 = x2 + _dot(p_ref[...].astype(BF16), wple_ref[...]) * gate
    o_ref[...] = _rms(x3, gf_ref[...]) if last else x3


def _ple(x1, moe, p, g, wpg, wple, gf, last):
    n, d = x1.shape
    tm = ROW_TILE
    row = lambda w: pl.BlockSpec((tm, w), lambda i: (i, 0))
    ws = (g, wpg, wple, gf)
    return pl.pallas_call(
        functools.partial(_ple_kernel, last=last),
        grid=(n // tm,),
        in_specs=[row(d), row(d), row(p.shape[1])] + [_full_spec(w.shape) for w in ws],
        out_specs=row(d),
        out_shape=jax.ShapeDtypeStruct((n, d), F32),
        compiler_params=pltpu.CompilerParams(dimension_semantics=("arbitrary",), vmem_limit_bytes=VMEM_LIMIT),
        name="ple",
    )(x1, moe, p, *ws)


def _block_diag(blocks):
    h, r, c = blocks.shape
    eye = jnp.eye(h, dtype=blocks.dtype)
    return (eye[:, None, :, None] * blocks[:, :, None, :]).reshape(h * r, h * c)


def kernel(x, p, attn_norm, w_in, kv_norm, w_uk, w_uv, rel_bias, w_branch_a, w_branch_b, w_out, ffn_norm,
           w_r1, b_r1, w_r2, b_r2, w_gate, w_up, w_down, ple_norm, w_ple_gate, w_ple, final_norm):
    batch, seq, d = x.shape
    n = batch * seq
    depth = w_in.shape[0]
    xf = x.reshape(n, d).astype(F32)
    widths = [WIDTH_A, KV_RANK, IDX_HEADS * IDX_DIM, IDX_DIM, IDX_HEADS, 3 * WIDTH_B, d, d]
    starts = [sum(widths[:k]) for k in range(len(widths))]
    for i in range(depth):
        w_qa, w_ckv, w_qi, w_ki, w_wi, w_qkv, w_ga, w_gb = [
            w_in[i][:, s:s + w].astype(BF16) for s, w in zip(starts, widths)]
        w_kk = jnp.concatenate([w_ki, w_ki], axis=1)
        w_wi = jnp.pad(w_wi, ((0, 0), (0, LANES - IDX_HEADS)))
        wuk_bd = _block_diag(jnp.swapaxes(w_uk[i], 1, 2)).astype(BF16)
        wuv_bd = _block_diag(w_uv[i]).astype(BF16)
        qabs, ckv, ckvt, qidx, kk, widx, qb, kb, vb, sa, sb = _proj(
            xf, attn_norm[i][None].astype(F32), w_qa, wuk_bd, w_ckv, kv_norm[i][None].astype(F32),
            w_qi, w_kk, w_wi, w_qkv, w_ga, w_gb)
        olat = _dsa(rel_bias.astype(F32), qabs, qidx, widx, ckv, ckvt, kk, batch, seq)
        ob = _stick(qb, kb, vb, batch, seq)
        w_r = jnp.concatenate([jnp.transpose(w_r2[i], (1, 0, 2)).reshape(d, N_EXPERTS), w_r1[i]], axis=1)
        w_r = jnp.pad(w_r.astype(F32), ((0, 0), (0, LANES - N_EXPERTS - N_GROUPS)))
        w_r_hi = w_r.astype(BF16)
        w_r_lo = (w_r - w_r_hi.astype(F32)).astype(BF16)
        b_r = jnp.pad(jnp.concatenate([b_r2[i].reshape(-1), b_r1[i]]).astype(F32),
                      (0, LANES - N_EXPERTS - N_GROUPS))[None]
        x1, h2, comb = _merge(xf, olat, ob, sa, sb, wuv_bd, w_branch_a[i].astype(BF16), w_branch_b[i].astype(BF16),
                              w_out[i].astype(BF16), ffn_norm[i][None].astype(F32), w_r_hi, w_r_lo, b_r)
        wgu = jnp.concatenate([w_gate[i], w_up[i]], axis=2).astype(BF16)
        moe = _moe(h2, comb, wgu, w_down[i].astype(BF16))
        xf = _ple(x1, moe, p[i].reshape(n, -1).astype(F32), ple_norm[i][None].astype(F32), w_ple_gate[i].astype(BF16),
                  w_ple[i].astype(BF16), final_norm[None].astype(F32), last=(i == depth - 1))
    return xf.reshape(batch, seq, d).astype(x.dtype)
```

```python
import functools
import math

import jax
import jax.numpy as jnp
from jax import lax
from jax.experimental import pallas as pl
from jax.experimental.pallas import tpu as pltpu

D_MODEL = 1024
N_HEADS_A = 8
HEAD_DIM = 64
WIDTH_A = N_HEADS_A * HEAD_DIM
KV_RANK = 128
IDX_HEADS = 8
IDX_DIM = 64
TOPK_MAX = 256
N_HEADS_B = 8
WIDTH_B = N_HEADS_B * HEAD_DIM
N_BUCKETS = 32
MAX_EXACT = N_BUCKETS // 2
MAX_DISTANCE = 128
ATTN_SCALE = HEAD_DIM ** -0.5
IDX_SCALE = (IDX_HEADS ** -0.5) * (IDX_DIM ** -0.5)
N_GROUPS = 4
EXPERTS_PER_GROUP = 8
N_EXPERTS = N_GROUPS * EXPERTS_PER_GROUP
D_EXPERT = 256
PLE_DIM = 256
EPS = 1e-6
LOG2E = 1.4426950408889634

LANES = 128
ROW_TILE = 512
MOE_ROW_TILE = 1024
MOE_CHUNK = 288
MOE_RANK_BLOCK = 256
ATT_TILE = 256
VMEM_LIMIT = 56 * 1024 * 1024
NEG_BIG = -1e30
EXP_UNDERFLOW = -120.0
INT_MIN = -2 ** 31

F32 = jnp.float32
BF16 = jnp.bfloat16


def _rms(x, g):
    return x * lax.rsqrt(jnp.mean(x * x, axis=-1, keepdims=True) + EPS) * g


def _dot(a, b):
    return jnp.dot(a, b, preferred_element_type=F32)


def _dot_nt(a, b):
    return lax.dot_general(a, b, (((1,), (1,)), ((), ())), preferred_element_type=F32)


def _full_spec(shape):
    nd = len(shape)
    return pl.BlockSpec(shape, lambda *_: (0,) * nd)


def _proj_kernel(x_ref, g_ref, wqa_ref, wuk_ref, wckv_ref, kvg_ref, wqi_ref, wkk_ref, wwi_ref,
                 wqkv_ref, wga_ref, wgb_ref,
                 qabs_ref, ckv_ref, ckvt_ref, qidx_ref, kk_ref, widx_ref, qb_ref, kb_ref, vb_ref, sa_ref, sb_ref):
    h = _rms(x_ref[...], g_ref[...]).astype(BF16)
    qa = _dot(h, wqa_ref[...]).astype(BF16)
    qabs_ref[...] = (_dot(qa, wuk_ref[...]) * (ATTN_SCALE * LOG2E)).astype(BF16)
    c = _rms(_dot(h, wckv_ref[...]), kvg_ref[...])
    ckv_ref[...] = c.astype(BF16)
    c_ext = jnp.concatenate([c, jnp.ones_like(c)], axis=1)
    for j in range(ckvt_ref.shape[0]):
        ckvt_ref[j] = jnp.transpose(c_ext[j * ATT_TILE:(j + 1) * ATT_TILE, :]).astype(BF16)
    qidx_ref[...] = _dot(h, wqi_ref[...]).astype(BF16)
    kk_ref[...] = _dot(h, wkk_ref[...]).astype(BF16)
    widx_ref[...] = _dot(h, wwi_ref[...]) * IDX_SCALE
    qkv = _dot(h, wqkv_ref[...])
    qb_ref[...] = (qkv[:, :WIDTH_B] * ATTN_SCALE).astype(BF16)
    kb_ref[...] = qkv[:, WIDTH_B:2 * WIDTH_B].astype(BF16)
    vb_ref[...] = qkv[:, 2 * WIDTH_B:].astype(BF16)
    sa_ref[...] = jax.nn.sigmoid(_dot(h, wga_ref[...])).astype(BF16)
    sb_ref[...] = jax.nn.sigmoid(_dot(h, wgb_ref[...])).astype(BF16)


def _proj(x, g, wqa, wuk, wckv, kvg, wqi, wkk, wwi, wqkv, wga, wgb):
    n, d = x.shape
    tm = ROW_TILE
    row = lambda w: pl.BlockSpec((tm, w), lambda i: (i, 0))
    ws = (g, wqa, wuk, wckv, kvg, wqi, wkk, wwi, wqkv, wga, wgb)
    outs = [(N_HEADS_A * KV_RANK, BF16), (KV_RANK, BF16), None, (IDX_HEADS * IDX_DIM, BF16), (LANES, BF16),
            (LANES, F32), (WIDTH_B, BF16), (WIDTH_B, BF16), (WIDTH_B, BF16), (d, BF16), (d, BF16)]
    t = ATT_TILE
    ckvt_spec = pl.BlockSpec((tm // t, 2 * KV_RANK, t), lambda i: (i, 0, 0))
    ckvt_shape = jax.ShapeDtypeStruct((n // t, 2 * KV_RANK, t), BF16)
    return pl.pallas_call(
        _proj_kernel,
        grid=(n // tm,),
        in_specs=[row(d)] + [_full_spec(w.shape) for w in ws],
        out_specs=[ckvt_spec if o is None else row(o[0]) for o in outs],
        out_shape=[ckvt_shape if o is None else jax.ShapeDtypeStruct((n, o[0]), o[1]) for o in outs],
        compiler_params=pltpu.CompilerParams(dimension_semantics=("arbitrary",), vmem_limit_bytes=VMEM_LIMIT),
        name="proj",
    )(x, *ws)


def _sortable_key(s):
    bits = lax.bitcast_convert_type(s, jnp.int32)
    key = bits ^ ((bits >> 31) & jnp.int32(0x7FFFFFFF))
    return jnp.where(key == -1, 0, key)


def _t5_bucket(dist):
    dist = jnp.maximum(dist, 0)
    d_f = jnp.maximum(dist, 1).astype(F32)
    large = MAX_EXACT + (jnp.log(d_f / MAX_EXACT) / math.log(MAX_DISTANCE / MAX_EXACT)
                         * (N_BUCKETS - MAX_EXACT)).astype(jnp.int32)
    large = jnp.minimum(large, N_BUCKETS - 1)
    return jnp.where(dist < MAX_EXACT, dist, large)


def _dsa_kernel(relb_ref, qabs_ref, qidx_ref, widx_ref, ckv_ref, ckvt_ref, kk_ref, o_ref,
                qa_s, qi_s, keyt_s, dig_s, act_s, cs_s, bias_s, m_s, acc_s, *, n_sel, seq):
    tq = tk = ATT_TILE
    nh = N_HEADS_A
    nhq = nh * tq
    qi = pl.program_id(1)
    n_kt = qi + 1
    kpos = lax.broadcasted_iota(jnp.int32, (tk, tq), 0)
    qpos = lax.broadcasted_iota(jnp.int32, (tk, tq), 1)

    @pl.when((pl.program_id(0) == 0) & (qi == 0))
    def _():
        for t in range(3):
            bucket = _t5_bucket(t * tq + qpos - kpos if t < 2 else jnp.full((tk, tq), 2 * tq, jnp.int32))
            for h in range(nh):
                b = jnp.zeros((tk, tq), F32)
                for bk in range(N_BUCKETS):
                    b = jnp.where(bucket == bk, relb_ref[bk, h], b)
                bias_s[t, :, h * tq:(h + 1) * tq] = b * LOG2E

    lane = lax.broadcasted_iota(jnp.int32, (tq, LANES), 1)
    lo_half = jnp.where(lane < IDX_DIM, 1.0, 0.0)
    hi_half = 1.0 - lo_half
    for h in range(nh):
        qa_s[h * tq:(h + 1) * tq, :] = qabs_ref[:, h * KV_RANK:(h + 1) * KV_RANK]
        pair = qidx_ref[:, (h // 2) * LANES:(h // 2 + 1) * LANES].astype(F32)
        qi_s[h * tq:(h + 1) * tq, :] = (pair * (lo_half if h % 2 == 0 else hi_half)).astype(BF16)
    w_t = jnp.transpose(widx_ref[...])
    w_row = jnp.concatenate([w_t[h:h + 1, :] for h in range(nh)], axis=1)

    def idx_body(kj, carry):
        ks = pl.multiple_of(kj * tk, tk)
        r = _dot_nt(kk_ref[pl.ds(ks, tk), :], qi_s[...])
        r = jnp.maximum(r, 0.0) * w_row
        s = r[:, :tq]
        for h in range(1, nh):
            s = s + r[:, h * tq:(h + 1) * tq]
        key = jnp.where((kpos > qpos) & (kj == qi), INT_MIN, _sortable_key(s))
        keyt_s[kj] = key
        u = key ^ INT_MIN
        for p in range(4):
            dig_s[p, kj] = (lax.shift_right_logical(u, 8 * p) & 0xFF).astype(F32).astype(BF16)
        return carry

    lax.fori_loop(0, n_kt, idx_body, 0)

    def count_active(pred):
        one, zero = jnp.ones((), BF16), jnp.zeros((), BF16)

        def body(kj, acc):
            c = jnp.where(pred(act_s[kj]), one, zero).reshape(tk // 16, 16, tq)
            part = c[0]
            for j in range(1, tk // 16):
                part = part + c[j]
            return acc + part.astype(F32)
        acc = lax.fori_loop(0, n_kt, body, jnp.zeros((16, tq), F32))
        return jnp.sum(acc, axis=0, keepdims=True)

    def to_digit(v):
        return v.astype(F32).astype(BF16)

    target = jnp.full((1, tq), float(n_sel), F32)
    tau_u = jnp.zeros((1, tq), jnp.int32)
    digit = jnp.zeros((1, tq), jnp.int32)
    for p in (3, 2, 1, 0):
        if p == 3:
            def init_body(kj, carry):
                act_s[kj] = dig_s[3, kj]
                return carry
            lax.fori_loop(0, n_kt, init_body, 0)
        else:
            prev = to_digit(digit)
            target = target - count_active(lambda a: a > prev)

            def narrow_body(kj, carry, prev=prev, p=p):
                act_s[kj] = jnp.where(act_s[kj] == prev, dig_s[p, kj], jnp.full((), -1.0, BF16))
                return carry
            lax.fori_loop(0, n_kt, narrow_body, 0)

        def bit_body(i, dgt, target=target):
            cand = dgt | jnp.left_shift(jnp.int32(1), 7 - i)
            cand_d = to_digit(cand)
            cnt = count_active(lambda a: a >= cand_d)
            return jnp.where(cnt >= target, cand, dgt)
        digit = lax.fori_loop(0, 8, bit_body, jnp.zeros((1, tq), jnp.int32))
        tau_u = tau_u | jnp.left_shift(digit, 8 * p)

    first = qi == 0
    tau = jnp.where(first, INT_MIN, tau_u ^ INT_MIN)
    last = to_digit(digit)
    n_gt = count_active(lambda a: a > last)
    n_eq = count_active(lambda a: a == last)
    need = target - n_gt

    def count_keys(pred):
        def body(kj, acc):
            c = jnp.where(pred(keyt_s[kj], kpos + kj * tk), 1.0, 0.0)
            return acc + jnp.sum(c.reshape(tk // 8, 8, tq), axis=0)
        acc = lax.fori_loop(0, n_kt, body, jnp.zeros((8, tq), F32))
        return jnp.sum(acc, axis=0, keepdims=True)

    cs_s[...] = jnp.full(cs_s.shape, seq, jnp.int32)

    @pl.when(jnp.logical_not(first) & (jnp.max(n_eq - need) > 0.0))
    def _():
        nbits = seq.bit_length() - 1

        def tie_body(i, d):
            cand = d | jnp.left_shift(jnp.int32(1), nbits - 1 - i)
            cnt = count_keys(lambda k, pos: (k == tau) & (pos < cand))
            return jnp.where(cnt < need, cand, d)

        d = lax.fori_loop(0, nbits, tie_body, jnp.zeros((1, tq), jnp.int32))
        cs_s[...] = jnp.broadcast_to(d, cs_s.shape)

    cstar = jnp.where(first, -1, cs_s[0:1, :])

    m_s[...] = jnp.full(m_s.shape, NEG_BIG, F32)
    acc_s[...] = jnp.zeros(acc_s.shape, F32)

    def att_body(kj, carry):
        ks = pl.multiple_of(kj * tk, tk)
        k = keyt_s[kj]
        sel = (k > tau) | ((k == tau) & (kpos + kj * tk <= cstar))
        selb = jnp.where(sel, 0.0, NEG_BIG)
        s = _dot_nt(ckv_ref[pl.ds(ks, tk), :], qa_s[...])
        s = s + bias_s[jnp.minimum(qi - kj, 2)] + jnp.concatenate([selb] * nh, axis=1)
        m_old = m_s[0:1, :]
        m_new = jnp.maximum(m_old, jnp.max(s, axis=0, keepdims=True))
        alpha = jnp.exp2(m_old - m_new)
        p = jnp.exp2(s - m_new).astype(BF16)
        acc_s[...] = acc_s[...] * alpha + _dot(ckvt_ref[kj], p)
        m_s[...] = jnp.broadcast_to(m_new, m_s.shape)
        return carry

    lax.fori_loop(0, n_kt, att_body, 0)

    acc = acc_s[...]
    o_t = acc[:KV_RANK] / acc[KV_RANK:]
    for h in range(nh):
        o_ref[:, h * KV_RANK:(h + 1) * KV_RANK] = jnp.transpose(o_t[:, h * tq:(h + 1) * tq]).astype(BF16)


def _dsa(rel_bias, qabs, qidx, widx, ckv, ckvt, kk, batch, seq):
    n = qabs.shape[0]
    tq = ATT_TILE
    nq = seq // tq
    n_sel = min(TOPK_MAX, seq // 4)
    assert n_sel == tq and seq % tq == 0 and seq & (seq - 1) == 0
    qrow = lambda w: pl.BlockSpec((tq, w), lambda b, q: (b * nq + q, 0))
    brow = lambda w: pl.BlockSpec((seq, w), lambda b, q: (b, 0))
    nhq = N_HEADS_A * tq
    return pl.pallas_call(
        functools.partial(_dsa_kernel, n_sel=n_sel, seq=seq),
        grid=(batch, nq),
        in_specs=[pl.BlockSpec(memory_space=pltpu.SMEM),
                  qrow(N_HEADS_A * KV_RANK), qrow(IDX_HEADS * IDX_DIM), qrow(LANES),
                  brow(KV_RANK), pl.BlockSpec((nq, 2 * KV_RANK, tq), lambda b, q: (b, 0, 0)), brow(LANES)],
        out_specs=qrow(N_HEADS_A * KV_RANK),
        out_shape=jax.ShapeDtypeStruct((n, N_HEADS_A * KV_RANK), BF16),
        scratch_shapes=[pltpu.VMEM((nhq, KV_RANK), BF16), pltpu.VMEM((nhq, LANES), BF16),
                        pltpu.VMEM((nq, tq, tq), jnp.int32), pltpu.VMEM((4, nq, tq, tq), BF16),
                        pltpu.VMEM((nq, tq, tq), BF16), pltpu.VMEM((8, tq), jnp.int32),
                        pltpu.VMEM((3, tq, nhq), F32),
                        pltpu.VMEM((8, nhq), F32), pltpu.VMEM((2 * KV_RANK, nhq), F32)],
        compiler_params=pltpu.CompilerParams(dimension_semantics=("arbitrary", "arbitrary"),
                                             vmem_limit_bytes=VMEM_LIMIT),
        name="dsa",
    )(rel_bias, qabs, qidx, widx, ckv, ckvt, kk)


def _stick_kernel(q_ref, k_ref, v_ref, o_ref):
    tq = tk = ATT_TILE
    qi = pl.program_id(1)
    npair = WIDTH_B // LANES
    lane = lax.broadcasted_iota(jnp.int32, (tq, LANES), 1)
    row = lax.broadcasted_iota(jnp.int32, (tq, tk), 0)
    col = lax.broadcasted_iota(jnp.int32, (tq, tk), 1)
    causal = col < row
    upper = jnp.where(row > col, 1.0, 0.0).astype(BF16)
    upper_ones = jnp.concatenate([upper, jnp.ones((tk, LANES), BF16)], axis=1)
    lo_half = jnp.where(lane < HEAD_DIM, 1.0, 0.0)
    causal2 = jnp.concatenate([causal, causal], axis=0)
    q_pairs = []
    for p in range(npair):
        q2 = q_ref[:, p * LANES:(p + 1) * LANES].astype(F32)
        q_pairs.append(jnp.concatenate([q2 * lo_half, q2 * (1.0 - lo_half)], axis=0).astype(BF16))

    def tile(kj, r_sums, outs, masked):
        ks = pl.multiple_of(kj * tk, tk)
        log_sig, log_1m = [], []
        for p in range(npair):
            z = _dot_nt(q_pairs[p], k_ref[pl.ds(ks, tk), p * LANES:(p + 1) * LANES])
            sp = jnp.maximum(z, 0.0) + jnp.log(1.0 + jnp.exp(-jnp.abs(z)))
            lm = jnp.where(causal2, -sp, 0.0) if masked else -sp
            log_sig.append(z - sp)
            log_1m.append(lm.astype(BF16))
        sums = _dot(jnp.concatenate(log_1m, axis=0), upper_ones)
        new_r, new_o = [], []
        for p in range(npair):
            sm = sums[p * 2 * tq:(p + 1) * 2 * tq]
            after = sm[:, :tk] + jnp.concatenate([r_sums[p]] * (tk // LANES), axis=1)
            a = jnp.exp(log_sig[p] + after)
            if masked:
                a = jnp.where(causal2, a, 0.0)
            res = _dot(a.astype(BF16), v_ref[pl.ds(ks, tk), p * LANES:(p + 1) * LANES])
            new_o.append(outs[p] + jnp.where(lane < HEAD_DIM, res[:tq], res[tq:]))
            new_r.append(r_sums[p] + sm[:, tk:])
        return tuple(new_r), tuple(new_o)

    zeros = jnp.zeros((tq, LANES), F32)
    r_sums, outs = tile(qi, (jnp.zeros((2 * tq, LANES), F32),) * npair, (zeros,) * npair, True)

    def cond(c):
        it, r_sums, _ = c
        return (it < qi) & (jnp.max(functools.reduce(jnp.maximum, r_sums)) > EXP_UNDERFLOW)

    def body(c):
        it, r_sums, outs = c
        r_sums, outs = tile(qi - 1 - it, r_sums, outs, False)
        return it + 1, r_sums, outs

    _, _, outs = lax.while_loop(cond, body, (jnp.int32(0), r_sums, outs))
    for p in range(npair):
        o_ref[:, p * LANES:(p + 1) * LANES] = outs[p].astype(BF16)


def _stick(qb, kb, vb, batch, seq):
    n = qb.shape[0]
    tq = ATT_TILE
    nq = seq // tq
    qspec = pl.BlockSpec((tq, WIDTH_B), lambda b, q: (b * nq + q, 0))
    kspec = pl.BlockSpec((seq, WIDTH_B), lambda b, q: (b, 0))
    return pl.pallas_call(
        _stick_kernel,
        grid=(batch, nq),
        in_specs=[qspec, kspec, kspec],
        out_specs=qspec,
        out_shape=jax.ShapeDtypeStruct((n, WIDTH_B), BF16),
        compiler_params=pltpu.CompilerParams(dimension_semantics=("arbitrary",) * 2, vmem_limit_bytes=VMEM_LIMIT),
        name="stick",
    )(qb, kb, vb)


def _merge_kernel(x_ref, ol_ref, ob_ref, sa_ref, sb_ref, wuv_ref, wba_ref, wbb_ref, wout_ref, g_ref,
                  wrh_ref, wrl_ref, br_ref, x1_ref, h2_ref, comb_ref):
    oa = _dot(ol_ref[...], wuv_ref[...]).astype(BF16)
    ya = _dot(oa, wba_ref[...])
    yb = _dot(ob_ref[...], wbb_ref[...])
    merged = sa_ref[...].astype(F32) * ya + sb_ref[...].astype(F32) * yb
    x1 = x_ref[...] + _dot(merged.astype(BF16), wout_ref[...])
    x1_ref[...] = x1
    h2 = _rms(x1, g_ref[...])
    h2_hi = h2.astype(BF16)
    h2_ref[...] = h2_hi
    h2_lo = (h2 - h2_hi.astype(F32)).astype(BF16)
    logits = (_dot(h2_hi, wrh_ref[...]) + _dot(h2_lo, wrh_ref[...]) + _dot(h2_hi, wrl_ref[...])) + br_ref[...]
    lane = lax.broadcasted_iota(jnp.int32, logits.shape, 1).astype(F32)
    ninf = -jnp.inf

    def first_max(v):
        m = jnp.max(v, axis=1, keepdims=True)
        return m, jnp.min(jnp.where(v == m, lane, 1e9), axis=1, keepdims=True)

    gmask = (lane >= N_EXPERTS) & (lane < N_EXPERTS + N_GROUPS)
    gl = jnp.where(gmask, logits, ninf)
    gmax, gidx = first_max(gl)
    p_g = 1.0 / jnp.sum(jnp.where(gmask, jnp.exp(gl - gmax), 0.0), axis=1, keepdims=True)
    e_lo = (gidx - N_EXPERTS) * EXPERTS_PER_GROUP
    el = jnp.where((lane >= e_lo) & (lane < e_lo + EXPERTS_PER_GROUP), logits, ninf)
    v1, i1 = first_max(el)
    el2 = jnp.where(lane == i1, ninf, el)
    v2, i2 = first_max(el2)
    e2 = jnp.exp(v2 - v1)
    w1 = 1.0 / (1.0 + e2)
    comb_ref[...] = jnp.where(lane == i1, w1 * p_g,
                              jnp.where(lane == i2, (e2 * w1) * p_g, jnp.where(lane == gidx, 1.0, 0.0)))


def _merge(x, olat, ob, sa, sb, wuv, wba, wbb, wout, g, wrh, wrl, br):
    n, d = x.shape
    tm = ROW_TILE
    row = lambda w: pl.BlockSpec((tm, w), lambda i: (i, 0))
    ws = (wuv, wba, wbb, wout, g, wrh, wrl, br)
    return pl.pallas_call(
        _merge_kernel,
        grid=(n // tm,),
        in_specs=[row(d), row(olat.shape[1]), row(ob.shape[1]), row(d), row(d)] + [_full_spec(w.shape) for w in ws],
        out_specs=[row(d), row(d), row(LANES)],
        out_shape=[jax.ShapeDtypeStruct((n, d), F32), jax.ShapeDtypeStruct((n, d), BF16),
                   jax.ShapeDtypeStruct((n, LANES), F32)],
        compiler_params=pltpu.CompilerParams(dimension_semantics=("arbitrary",), vmem_limit_bytes=VMEM_LIMIT),
        name="merge",
    )(x, olat, ob, sa, sb, *ws)


def _moe_kernel(h_ref, comb_ref, wgu_ref, wd_ref, o_ref, slotc_s, slotr_s, split_s, acc_s, rng_s):
    t, d = h_ref.shape
    c = MOE_CHUNK
    rb = MOE_RANK_BLOCK
    g = pl.program_id(1)
    lane = lax.broadcasted_iota(jnp.int32, (c, LANES), 1)

    @pl.when(g == 0)
    def _():
        lane_b = lax.broadcasted_iota(jnp.int32, (rb, LANES), 1)
        group_lanes = (lane_b >= N_EXPERTS) & (lane_b < N_EXPERTS + N_GROUPS)
        r_i = lax.broadcasted_iota(jnp.int32, (rb, rb), 0)
        c_i = lax.broadcasted_iota(jnp.int32, (rb, rb), 1)
        lower = jnp.where(c_i < r_i, 1.0, 0.0).astype(BF16)
        counts = jnp.zeros((1, LANES), F32)
        ranks = []
        for blk in range(t // rb):
            oh = jnp.where(group_lanes, comb_ref[blk * rb:(blk + 1) * rb, :], 0.0)
            before = _dot(lower, oh.astype(BF16)) + counts
            ranks.append(jnp.sum(before * oh, axis=1, keepdims=True))
            counts = counts + jnp.sum(oh, axis=0, keepdims=True)
        lane1 = lax.broadcasted_iota(jnp.int32, (1, LANES), 1)
        start = jnp.int32(0)
        seg = jnp.zeros((1, LANES), F32)
        for k in range(N_GROUPS):
            n_k = jnp.sum(jnp.where(lane1 == N_EXPERTS + k, counts, 0.0)).astype(jnp.int32)
            seg = jnp.where(lane1 == N_EXPERTS + k, (start * c).astype(F32), seg)
            rng_s[k] = start
            for j in range(-(-t // c)):
                start = start + (n_k > j * c).astype(jnp.int32)
            rng_s[N_GROUPS + k] = start
        for blk in range(t // rb):
            rows = slice(blk * rb, (blk + 1) * rb)
            oh = jnp.where(group_lanes, comb_ref[rows, :], 0.0)
            slot = ranks[blk] + jnp.sum(oh * seg, axis=1, keepdims=True)
            slotc_s[rows, :] = jnp.broadcast_to(slot, (rb, LANES))
        slotr_s[...] = jnp.transpose(slotc_s[...])
        cb = comb_ref[...]
        hi = cb.astype(BF16)
        r1 = cb - hi.astype(F32)
        mid = r1.astype(BF16)
        split_s[0] = hi
        split_s[1] = mid
        split_s[2] = (r1 - mid.astype(F32)).astype(BF16)
        acc_s[...] = jnp.zeros(acc_s.shape, F32)

    def chunk_body(ci, carry):
        base = (ci * c).astype(F32)
        row_slot = lax.broadcasted_iota(jnp.int32, (c, t), 0).astype(F32) + base
        pc = jnp.where(slotr_s[0:1, :] == row_slot, 1.0, 0.0).astype(BF16)
        xs = _dot(pc, h_ref[...]).astype(BF16)
        cw = _dot(pc, split_s[0]) + _dot(pc, split_s[1]) + _dot(pc, split_s[2])
        y = jnp.zeros((c, d), F32)
        for e in range(EXPERTS_PER_GROUP):
            gu = _dot(xs, wgu_ref[e])
            gate = gu[:, :D_EXPERT]
            hid = gate * jax.nn.sigmoid(gate) * gu[:, D_EXPERT:]
            ce = jnp.sum(jnp.where(lane == g * EXPERTS_PER_GROUP + e, cw, 0.0), axis=1, keepdims=True)
            y = y + _dot((hid * ce).astype(BF16), wd_ref[e])
        col_slot = lax.broadcasted_iota(jnp.int32, (t, c), 1).astype(F32) + base
        pct = jnp.where(jnp.broadcast_to(slotc_s[:, 0:1], (t, c)) == col_slot, 1.0, 0.0).astype(BF16)
        acc_s[...] += _dot(pct, y.astype(BF16))
        return carry

    lax.fori_loop(rng_s[g], rng_s[N_GROUPS + g], chunk_body, 0)

    @pl.when(g == N_GROUPS - 1)
    def _():
        o_ref[...] = acc_s[...].astype(BF16)


def _moe(h2, comb, wgu, wd):
    n, d = h2.shape
    tm = MOE_ROW_TILE
    row = lambda w: pl.BlockSpec((tm, w), lambda i, g: (i, 0))
    return pl.pallas_call(
        _moe_kernel,
        grid=(n // tm, N_GROUPS),
        in_specs=[row(d), row(LANES),
                  pl.BlockSpec((EXPERTS_PER_GROUP, d, 2 * D_EXPERT), lambda i, g: (g, 0, 0)),
                  pl.BlockSpec((EXPERTS_PER_GROUP, D_EXPERT, d), lambda i, g: (g, 0, 0))],
        out_specs=row(d),
        out_shape=jax.ShapeDtypeStruct((n, d), BF16),
        scratch_shapes=[pltpu.VMEM((tm, LANES), F32), pltpu.VMEM((LANES, tm), F32),
                        pltpu.VMEM((3, tm, LANES), BF16), pltpu.VMEM((tm, d), F32),
                        pltpu.SMEM((2 * N_GROUPS,), jnp.int32)],
        compiler_params=pltpu.CompilerParams(dimension_semantics=("arbitrary", "arbitrary"),
                                             vmem_limit_bytes=VMEM_LIMIT),
        name="moe",
    )(h2, comb, wgu, wd)


def _ple_kernel(x_ref, m_ref, p_ref, g_ref, wpg_ref, wple_ref, gf_ref, o_ref, *, last):
    x2 = x_ref[...] + m_ref[...].astype(F32)
    h3 = _rms(x2, g_ref[...]).astype(BF16)
    gate = jax.nn.sigmoid(_dot(h3, wpg_ref[...]))
    x3 = x2 + _dot(p_ref[...].astype(BF16), wple_ref[...]) * gate
    o_ref[...] = _rms(x3, gf_ref[...]) if last else x3


def _ple(x1, moe, p, g, wpg, wple, gf, last):
    n, d = x1.shape
    tm = ROW_TILE
    row = lambda w: pl.BlockSpec((tm, w), lambda i: (i, 0))
    ws = (g, wpg, wple, gf)
    return pl.pallas_call(
        functools.partial(_ple_kernel, last=last),
        grid=(n // tm,),
        in_specs=[row(d), row(d), row(p.shape[1])] + [_full_spec(w.shape) for w in ws],
        out_specs=row(d),
        out_shape=jax.ShapeDtypeStruct((n, d), F32),
        compiler_params=pltpu.CompilerParams(dimension_semantics=("arbitrary",), vmem_limit_bytes=VMEM_LIMIT),
        name="ple",
    )(x1, moe, p, *ws)


def _block_diag(blocks):
    h, r, c = blocks.shape
    eye = jnp.eye(h, dtype=blocks.dtype)
    return (eye[:, None, :, None] * blocks[:, :, None, :]).reshape(h * r, h * c)


def kernel(x, p, attn_norm, w_in, kv_norm, w_uk, w_uv, rel_bias, w_branch_a, w_branch_b, w_out, ffn_norm,
           w_r1, b_r1, w_r2, b_r2, w_gate, w_up, w_down, ple_norm, w_ple_gate, w_ple, final_norm):
    batch, seq, d = x.shape
    n = batch * seq
    depth = w_in.shape[0]
    xf = x.reshape(n, d).astype(F32)
    widths = [WIDTH_A, KV_RANK, IDX_HEADS * IDX_DIM, IDX_DIM, IDX_HEADS, 3 * WIDTH_B, d, d]
    starts = [sum(widths[:k]) for k in range(len(widths))]
    for i in range(depth):
        w_qa, w_ckv, w_qi, w_ki, w_wi, w_qkv, w_ga, w_gb = [
            w_in[i][:, s:s + w].astype(BF16) for s, w in zip(starts, widths)]
        w_kk = jnp.concatenate([w_ki, w_ki], axis=1)
        w_wi = jnp.pad(w_wi, ((0, 0), (0, LANES - IDX_HEADS)))
        wuk_bd = _block_diag(jnp.swapaxes(w_uk[i], 1, 2)).astype(BF16)
        wuv_bd = _block_diag(w_uv[i]).astype(BF16)
        qabs, ckv, ckvt, qidx, kk, widx, qb, kb, vb, sa, sb = _proj(
            xf, attn_norm[i][None].astype(F32), w_qa, wuk_bd, w_ckv, kv_norm[i][None].astype(F32),
            w_qi, w_kk, w_wi, w_qkv, w_ga, w_gb)
        olat = _dsa(rel_bias.astype(F32), qabs, qidx, widx, ckv, ckvt, kk, batch, seq)
        ob = _stick(qb, kb, vb, batch, seq)
        w_r = jnp.concatenate([jnp.transpose(w_r2[i], (1, 0, 2)).reshape(d, N_EXPERTS), w_r1[i]], axis=1)
        w_r = jnp.pad(w_r.astype(F32), ((0, 0), (0, LANES - N_EXPERTS - N_GROUPS)))
        w_r_hi = w_r.astype(BF16)
        w_r_lo = (w_r - w_r_hi.astype(F32)).astype(BF16)
        b_r = jnp.pad(jnp.concatenate([b_r2[i].reshape(-1), b_r1[i]]).astype(F32),
                      (0, LANES - N_EXPERTS - N_GROUPS))[None]
        x1, h2, comb = _merge(xf, olat, ob, sa, sb, wuv_bd, w_branch_a[i].astype(BF16), w_branch_b[i].astype(BF16),
                              w_out[i].astype(BF16), ffn_norm[i][None].astype(F32), w_r_hi, w_r_lo, b_r)
        wgu = jnp.concatenate([w_gate[i], w_up[i]], axis=2).astype(BF16)
        moe = _moe(h2, comb, wgu, w_down[i].astype(BF16))
        xf = _ple(x1, moe, p[i].reshape(n, -1).astype(F32), ple_norm[i][None].astype(F32), w_ple_gate[i].astype(BF16),
                  w_ple[i].astype(BF16), final_norm[None].astype(F32), last=(i == depth - 1))
    return xf.reshape(batch, seq, d).astype(x.dtype)
```

```python
import functools
import math

import jax
import jax.numpy as jnp
from jax import lax
from jax.experimental import pallas as pl
from jax.experimental.pallas import tpu as pltpu

D_MODEL = 1024
N_HEADS_A = 8
HEAD_DIM = 64
WIDTH_A = N_HEADS_A * HEAD_DIM
KV_RANK = 128
IDX_HEADS = 8
IDX_DIM = 64
TOPK_MAX = 256
N_HEADS_B = 8
WIDTH_B = N_HEADS_B * HEAD_DIM
N_BUCKETS = 32
MAX_EXACT = N_BUCKETS // 2
MAX_DISTANCE = 128
ATTN_SCALE = HEAD_DIM ** -0.5
IDX_SCALE = (IDX_HEADS ** -0.5) * (IDX_DIM ** -0.5)
N_GROUPS = 4
EXPERTS_PER_GROUP = 8
N_EXPERTS = N_GROUPS * EXPERTS_PER_GROUP
D_EXPERT = 256
PLE_DIM = 256
EPS = 1e-6
LOG2E = 1.4426950408889634

LANES = 128
ROW_TILE = 512
MOE_ROW_TILE = 1024
MOE_CHUNK = 288
MOE_RANK_BLOCK = 256
ATT_TILE = 256
VMEM_LIMIT = 56 * 1024 * 1024
NEG_BIG = -1e30
EXP_UNDERFLOW = -120.0
KEY_TOP = 127.0
KEY_MIN_RANGE = 1e-30

F32 = jnp.float32
BF16 = jnp.bfloat16


def _rms(x, g):
    return x * lax.rsqrt(jnp.mean(x * x, axis=-1, keepdims=True) + EPS) * g


def _dot(a, b):
    return jnp.dot(a, b, preferred_element_type=F32)


def _dot_nt(a, b):
    return lax.dot_general(a, b, (((1,), (1,)), ((), ())), preferred_element_type=F32)


def _full_spec(shape):
    nd = len(shape)
    return pl.BlockSpec(shape, lambda *_: (0,) * nd)


def _proj_kernel(x_ref, g_ref, wqa_ref, wuk_ref, wckv_ref, kvg_ref, wqi_ref, wkk_ref, wwi_ref,
                 wqkv_ref, wga_ref, wgb_ref,
                 qabs_ref, ckv_ref, ckvt_ref, qidx_ref, kk_ref, widx_ref, qb_ref, kb_ref, vb_ref, sa_ref, sb_ref):
    h = _rms(x_ref[...], g_ref[...]).astype(BF16)
    qa = _dot(h, wqa_ref[...]).astype(BF16)
    qabs_ref[...] = (_dot(qa, wuk_ref[...]) * (ATTN_SCALE * LOG2E)).astype(BF16)
    c = _rms(_dot(h, wckv_ref[...]), kvg_ref[...])
    ckv_ref[...] = c.astype(BF16)
    c_ext = jnp.concatenate([c, jnp.ones_like(c)], axis=1)
    for j in range(ckvt_ref.shape[0]):
        ckvt_ref[j] = jnp.transpose(c_ext[j * ATT_TILE:(j + 1) * ATT_TILE, :]).astype(BF16)
    qidx_ref[...] = _dot(h, wqi_ref[...]).astype(BF16)
    kk_ref[...] = _dot(h, wkk_ref[...]).astype(BF16)
    widx_ref[...] = _dot(h, wwi_ref[...]) * IDX_SCALE
    qkv = _dot(h, wqkv_ref[...])
    qb_ref[...] = (qkv[:, :WIDTH_B] * ATTN_SCALE).astype(BF16)
    kb_ref[...] = qkv[:, WIDTH_B:2 * WIDTH_B].astype(BF16)
    vb_ref[...] = qkv[:, 2 * WIDTH_B:].astype(BF16)
    sa_ref[...] = jax.nn.sigmoid(_dot(h, wga_ref[...])).astype(BF16)
    sb_ref[...] = jax.nn.sigmoid(_dot(h, wgb_ref[...])).astype(BF16)


def _proj(x, g, wqa, wuk, wckv, kvg, wqi, wkk, wwi, wqkv, wga, wgb):
    n, d = x.shape
    tm = ROW_TILE
    row = lambda w: pl.BlockSpec((tm, w), lambda i: (i, 0))
    ws = (g, wqa, wuk, wckv, kvg, wqi, wkk, wwi, wqkv, wga, wgb)
    outs = [(N_HEADS_A * KV_RANK, BF16), (KV_RANK, BF16), None, (IDX_HEADS * IDX_DIM, BF16), (LANES, BF16),
            (LANES, F32), (WIDTH_B, BF16), (WIDTH_B, BF16), (WIDTH_B, BF16), (d, BF16), (d, BF16)]
    t = ATT_TILE
    ckvt_spec = pl.BlockSpec((tm // t, 2 * KV_RANK, t), lambda i: (i, 0, 0))
    ckvt_shape = jax.ShapeDtypeStruct((n // t, 2 * KV_RANK, t), BF16)
    return pl.pallas_call(
        _proj_kernel,
        grid=(n // tm,),
        in_specs=[row(d)] + [_full_spec(w.shape) for w in ws],
        out_specs=[ckvt_spec if o is None else row(o[0]) for o in outs],
        out_shape=[ckvt_shape if o is None else jax.ShapeDtypeStruct((n, o[0]), o[1]) for o in outs],
        compiler_params=pltpu.CompilerParams(dimension_semantics=("arbitrary",), vmem_limit_bytes=VMEM_LIMIT),
        name="proj",
    )(x, *ws)


def _t5_bucket(dist):
    dist = jnp.maximum(dist, 0)
    d_f = jnp.maximum(dist, 1).astype(F32)
    large = MAX_EXACT + jnp.floor(jnp.log(d_f / MAX_EXACT) / math.log(MAX_DISTANCE / MAX_EXACT)
                                  * (N_BUCKETS - MAX_EXACT)).astype(jnp.int32)
    large = jnp.minimum(large, N_BUCKETS - 1)
    return jnp.where(dist < MAX_EXACT, dist, large)


def _dsa_kernel(relb_ref, qabs_ref, qidx_ref, widx_ref, ckv_ref, ckvt_ref, kk_ref, o_ref,
                qa_s, qi_s, key_s, dig_s, act_s, cs_s, mn_s, mx_s, bias_s, m_s, acc_s, sa_s, sb_s, *, n_sel, seq):
    tq = tk = ATT_TILE
    nh = N_HEADS_A
    nhq = nh * tq
    qi = pl.program_id(1)
    n_kt = qi + 1
    kpos = lax.broadcasted_iota(jnp.int32, (tk, tq), 0)
    qpos = lax.broadcasted_iota(jnp.int32, (tk, tq), 1)

    @pl.when((pl.program_id(0) == 0) & (qi == 0))
    def _():
        for t in range(3):
            bucket = _t5_bucket(t * tq + qpos - kpos if t < 2 else jnp.full((tk, tq), 2 * tq, jnp.int32))
            for h in range(nh):
                b = jnp.zeros((tk, tq), F32)
                for bk in range(N_BUCKETS):
                    b = jnp.where(bucket == bk, relb_ref[bk, h], b)
                bias_s[t, :, h * tq:(h + 1) * tq] = b * LOG2E

    lane = lax.broadcasted_iota(jnp.int32, (tq, LANES), 1)
    lo_half = jnp.where(lane < IDX_DIM, 1.0, 0.0)
    hi_half = 1.0 - lo_half
    for h in range(nh):
        qa_s[h * tq:(h + 1) * tq, :] = qabs_ref[:, h * KV_RANK:(h + 1) * KV_RANK]
        pair = qidx_ref[:, (h // 2) * LANES:(h // 2 + 1) * LANES].astype(F32)
        qi_s[h * tq:(h + 1) * tq, :] = (pair * (lo_half if h % 2 == 0 else hi_half)).astype(BF16)
    w_t = jnp.transpose(widx_ref[...])
    w_row = jnp.concatenate([w_t[h:h + 1, :] for h in range(nh)], axis=1)

    mn_s[...] = jnp.full(mn_s.shape, -NEG_BIG, F32)
    mx_s[...] = jnp.full(mx_s.shape, NEG_BIG, F32)

    def idx_dots(kj, r_ref):
        ks = pl.multiple_of(jnp.minimum(kj, n_kt - 1) * tk, tk)
        r_ref[...] = _dot_nt(kk_ref[pl.ds(ks, tk), :], qi_s[...])

    def idx_scores(kj, r_ref):
        r = jnp.maximum(r_ref[...], 0.0) * w_row
        s = r[:, :tq]
        for h in range(1, nh):
            s = s + r[:, h * tq:(h + 1) * tq]
        masked = (kpos > qpos) & (kj == qi)
        key_s[kj] = jnp.where(masked, NEG_BIG, s)
        mn_s[...] = jnp.minimum(mn_s[...], jnp.min(jnp.where(masked, -NEG_BIG, s).reshape(tk // 8, 8, tq), axis=0))
        mx_s[...] = jnp.maximum(mx_s[...], jnp.max(jnp.where(masked, NEG_BIG, s).reshape(tk // 8, 8, tq), axis=0))

    idx_dots(0, sa_s)

    def idx_pair(i, carry):
        idx_dots(2 * i + 1, sb_s)
        idx_scores(2 * i, sa_s)
        idx_dots(2 * i + 2, sa_s)
        idx_scores(2 * i + 1, sb_s)
        return carry

    lax.fori_loop(0, n_kt // 2, idx_pair, 0)

    @pl.when(n_kt % 2 == 1)
    def _():
        idx_scores(n_kt - 1, sa_s)

    lo = jnp.min(mn_s[...], axis=0, keepdims=True)
    hi = jnp.max(mx_s[...], axis=0, keepdims=True)
    scale = KEY_TOP / jnp.maximum(hi - lo, KEY_MIN_RANGE)

    def key_body(kj, carry):
        y = jnp.maximum((key_s[kj] - lo) * scale, -1.0)
        rem = y
        for p in (3, 2, 1, 0):
            d = jnp.floor(rem)
            dig_s[p, kj] = d.astype(BF16)
            rem = (rem - d) * 256.0
        key_s[kj] = y - rem * (2.0 ** -32)
        return carry

    lax.fori_loop(0, n_kt, key_body, 0)

    def count_active(pred):
        one, zero = jnp.ones((), BF16), jnp.zeros((), BF16)

        def body(kj, acc):
            c = jnp.where(pred(act_s[kj]), one, zero).reshape(tk // 16, 16, tq)
            part = c[0]
            for j in range(1, tk // 16):
                part = part + c[j]
            return acc + part.astype(F32)
        acc = lax.fori_loop(0, n_kt, body, jnp.zeros((16, tq), F32))
        return jnp.sum(acc, axis=0, keepdims=True)

    def to_digit(v):
        return v.astype(F32).astype(BF16)

    target = jnp.full((1, tq), float(n_sel), F32)
    tau = jnp.zeros((1, tq), F32)
    digit = jnp.zeros((1, tq), jnp.int32)
    for p in (3, 2, 1, 0):
        if p == 3:
            def init_body(kj, carry):
                act_s[kj] = dig_s[3, kj]
                return carry
            lax.fori_loop(0, n_kt, init_body, 0)
        else:
            prev = to_digit(digit)
            target = target - count_active(lambda a: a > prev)

            def narrow_body(kj, carry, prev=prev, p=p):
                act_s[kj] = jnp.where(act_s[kj] == prev, dig_s[p, kj], jnp.full((), -1.0, BF16))
                return carry
            lax.fori_loop(0, n_kt, narrow_body, 0)

        def bit_body(i, dgt, target=target):
            cand = dgt | jnp.left_shift(jnp.int32(1), 7 - i)
            cand_d = to_digit(cand)
            cnt = count_active(lambda a: a >= cand_d)
            return jnp.where(cnt >= target, cand, dgt)
        digit = lax.fori_loop(0, 8, bit_body, jnp.zeros((1, tq), jnp.int32))
        tau = tau + digit.astype(F32) * (256.0 ** (p - 3))

    first = qi == 0
    tau = jnp.where(first, -0.5, tau)
    last = to_digit(digit)
    n_gt = count_active(lambda a: a > last)
    n_eq = count_active(lambda a: a == last)
    need = target - n_gt

    def count_keys(pred):
        def body(kj, acc):
            c = jnp.where(pred(key_s[kj], kpos + kj * tk), 1.0, 0.0)
            return acc + jnp.sum(c.reshape(tk // 8, 8, tq), axis=0)
        acc = lax.fori_loop(0, n_kt, body, jnp.zeros((8, tq), F32))
        return jnp.sum(acc, axis=0, keepdims=True)

    cs_s[...] = jnp.full(cs_s.shape, seq, jnp.int32)

    @pl.when(jnp.logical_not(first) & (jnp.max(n_eq - need) > 0.0))
    def _():
        nbits = seq.bit_length() - 1

        def tie_body(i, d):
            cand = d | jnp.left_shift(jnp.int32(1), nbits - 1 - i)
            cnt = count_keys(lambda k, pos: (k == tau) & (pos < cand))
            return jnp.where(cnt < need, cand, d)

        d = lax.fori_loop(0, nbits, tie_body, jnp.zeros((1, tq), jnp.int32))
        cs_s[...] = jnp.broadcast_to(d, cs_s.shape)

    cstar = jnp.where(first, -1, cs_s[0:1, :])

    m_s[...] = jnp.full(m_s.shape, NEG_BIG, F32)
    acc_s[...] = jnp.zeros(acc_s.shape, F32)

    def scores(kj, s_ref):
        kc = jnp.minimum(kj, n_kt - 1)
        ks = pl.multiple_of(kc * tk, tk)
        k = key_s[kc]
        sel = ((k > tau) | ((k == tau) & (kpos + kc * tk <= cstar))) & (kj < n_kt)
        selb = jnp.where(sel, 0.0, NEG_BIG)
        s = _dot_nt(ckv_ref[pl.ds(ks, tk), :], qa_s[...])
        s_ref[...] = s + bias_s[jnp.minimum(qi - kc, 2)] + jnp.concatenate([selb] * nh, axis=1)

    def update(kj, s_ref):
        s = s_ref[...]
        m_old = m_s[0:1, :]
        m_new = jnp.maximum(m_old, jnp.max(s, axis=0, keepdims=True))
        alpha = jnp.exp2(m_old - m_new)
        p = jnp.exp2(s - m_new).astype(BF16)
        acc_s[...] = acc_s[...] * alpha + _dot(ckvt_ref[jnp.minimum(kj, n_kt - 1)], p)
        m_s[...] = jnp.broadcast_to(m_new, m_s.shape)

    scores(0, sa_s)

    def pair_body(i, carry):
        scores(2 * i + 1, sb_s)
        update(2 * i, sa_s)
        scores(2 * i + 2, sa_s)
        update(2 * i + 1, sb_s)
        return carry

    lax.fori_loop(0, n_kt // 2, pair_body, 0)

    @pl.when(n_kt % 2 == 1)
    def _():
        update(n_kt - 1, sa_s)

    acc = acc_s[...]
    o_t = acc[:KV_RANK] / acc[KV_RANK:]
    for h in range(nh):
        o_ref[:, h * KV_RANK:(h + 1) * KV_RANK] = jnp.transpose(o_t[:, h * tq:(h + 1) * tq]).astype(BF16)


def _dsa(rel_bias, qabs, qidx, widx, ckv, ckvt, kk, batch, seq):
    n = qabs.shape[0]
    tq = ATT_TILE
    nq = seq // tq
    n_sel = min(TOPK_MAX, seq // 4)
    assert n_sel == tq and seq % tq == 0 and seq & (seq - 1) == 0
    qrow = lambda w: pl.BlockSpec((tq, w), lambda b, q: (b * nq + q, 0))
    brow = lambda w: pl.BlockSpec((seq, w), lambda b, q: (b, 0))
    nhq = N_HEADS_A * tq
    return pl.pallas_call(
        functools.partial(_dsa_kernel, n_sel=n_sel, seq=seq),
        grid=(batch, nq),
        in_specs=[pl.BlockSpec(memory_space=pltpu.SMEM),
                  qrow(N_HEADS_A * KV_RANK), qrow(IDX_HEADS * IDX_DIM), qrow(LANES),
                  brow(KV_RANK), pl.BlockSpec((nq, 2 * KV_RANK, tq), lambda b, q: (b, 0, 0)), brow(LANES)],
        out_specs=qrow(N_HEADS_A * KV_RANK),
        out_shape=jax.ShapeDtypeStruct((n, N_HEADS_A * KV_RANK), BF16),
        scratch_shapes=[pltpu.VMEM((nhq, KV_RANK), BF16), pltpu.VMEM((nhq, LANES), BF16),
                        pltpu.VMEM((nq, tq, tq), F32), pltpu.VMEM((4, nq, tq, tq), BF16),
                        pltpu.VMEM((nq, tq, tq), BF16), pltpu.VMEM((8, tq), jnp.int32),
                        pltpu.VMEM((8, tq), F32), pltpu.VMEM((8, tq), F32),
                        pltpu.VMEM((3, tq, nhq), F32),
                        pltpu.VMEM((8, nhq), F32), pltpu.VMEM((2 * KV_RANK, nhq), F32),
                        pltpu.VMEM((tq, nhq), F32), pltpu.VMEM((tq, nhq), F32)],
        compiler_params=pltpu.CompilerParams(dimension_semantics=("arbitrary", "arbitrary"),
                                             vmem_limit_bytes=VMEM_LIMIT),
        name="dsa",
    )(rel_bias, qabs, qidx, widx, ckv, ckvt, kk)


def _stick_kernel(q_ref, k_ref, v_ref, o_ref):
    tq = tk = ATT_TILE
    qi = pl.program_id(1)
    npair = WIDTH_B // LANES
    lane = lax.broadcasted_iota(jnp.int32, (tq, LANES), 1)
    row = lax.broadcasted_iota(jnp.int32, (tq, tk), 0)
    col = lax.broadcasted_iota(jnp.int32, (tq, tk), 1)
    causal = col < row
    upper = jnp.where(row > col, 1.0, 0.0).astype(BF16)
    upper_ones = jnp.concatenate([upper, jnp.ones((tk, LANES), BF16)], axis=1)
    lo_half = jnp.where(lane < HEAD_DIM, 1.0, 0.0)
    causal2 = jnp.concatenate([causal, causal], axis=0)
    q_pairs = []
    for p in range(npair):
        q2 = q_ref[:, p * LANES:(p + 1) * LANES].astype(F32)
        q_pairs.append(jnp.concatenate([q2 * lo_half, q2 * (1.0 - lo_half)], axis=0).astype(BF16))

    def tile(kj, r_sums, outs, masked):
        ks = pl.multiple_of(kj * tk, tk)
        log_sig, log_1m = [], []
        for p in range(npair):
            z = _dot_nt(q_pairs[p], k_ref[pl.ds(ks, tk), p * LANES:(p + 1) * LANES])
            sp = jnp.maximum(z, 0.0) + jnp.log(1.0 + jnp.exp(-jnp.abs(z)))
            lm = jnp.where(causal2, -sp, 0.0) if masked else -sp
            log_sig.append(z - sp)
            log_1m.append(lm.astype(BF16))
        sums = _dot(jnp.concatenate(log_1m, axis=0), upper_ones)
        new_r, new_o = [], []
        for p in range(npair):
            sm = sums[p * 2 * tq:(p + 1) * 2 * tq]
            after = sm[:, :tk] + jnp.concatenate([r_sums[p]] * (tk // LANES), axis=1)
            a = jnp.exp(log_sig[p] + after)
            if masked:
                a = jnp.where(causal2, a, 0.0)
            res = _dot(a.astype(BF16), v_ref[pl.ds(ks, tk), p * LANES:(p + 1) * LANES])
            new_o.append(outs[p] + jnp.where(lane < HEAD_DIM, res[:tq], res[tq:]))
            new_r.append(r_sums[p] + sm[:, tk:])
        return tuple(new_r), tuple(new_o)

    zeros = jnp.zeros((tq, LANES), F32)
    r_sums, outs = tile(qi, (jnp.zeros((2 * tq, LANES), F32),) * npair, (zeros,) * npair, True)

    def cond(c):
        it, r_sums, _ = c
        return (it < qi) & (jnp.max(functools.reduce(jnp.maximum, r_sums)) > EXP_UNDERFLOW)

    def body(c):
        it, r_sums, outs = c
        r_sums, outs = tile(qi - 1 - it, r_sums, outs, False)
        return it + 1, r_sums, outs

    _, _, outs = lax.while_loop(cond, body, (jnp.int32(0), r_sums, outs))
    for p in range(npair):
        o_ref[:, p * LANES:(p + 1) * LANES] = outs[p].astype(BF16)


def _stick(qb, kb, vb, batch, seq):
    n = qb.shape[0]
    tq = ATT_TILE
    nq = seq // tq
    qspec = pl.BlockSpec((tq, WIDTH_B), lambda b, q: (b * nq + q, 0))
    kspec = pl.BlockSpec((seq, WIDTH_B), lambda b, q: (b, 0))
    return pl.pallas_call(
        _stick_kernel,
        grid=(batch, nq),
        in_specs=[qspec, kspec, kspec],
        out_specs=qspec,
        out_shape=jax.ShapeDtypeStruct((n, WIDTH_B), BF16),
        compiler_params=pltpu.CompilerParams(dimension_semantics=("arbitrary",) * 2, vmem_limit_bytes=VMEM_LIMIT),
        name="stick",
    )(qb, kb, vb)


def _merge_kernel(x_ref, ol_ref, ob_ref, sa_ref, sb_ref, wuv_ref, wba_ref, wbb_ref, wout_ref, g_ref,
                  wrh_ref, wrl_ref, br_ref, x1_ref, h2_ref, comb_ref):
    oa = _dot(ol_ref[...], wuv_ref[...]).astype(BF16)
    ya = _dot(oa, wba_ref[...])
    yb = _dot(ob_ref[...], wbb_ref[...])
    merged = sa_ref[...].astype(F32) * ya + sb_ref[...].astype(F32) * yb
    x1 = x_ref[...] + _dot(merged.astype(BF16), wout_ref[...])
    x1_ref[...] = x1
    h2 = _rms(x1, g_ref[...])
    h2_hi = h2.astype(BF16)
    h2_ref[...] = h2_hi
    h2_lo = (h2 - h2_hi.astype(F32)).astype(BF16)
    logits = (_dot(h2_hi, wrh_ref[...]) + _dot(h2_lo, wrh_ref[...]) + _dot(h2_hi, wrl_ref[...])) + br_ref[...]
    lane = lax.broadcasted_iota(jnp.int32, logits.shape, 1).astype(F32)
    ninf = -jnp.inf

    def first_max(v):
        m = jnp.max(v, axis=1, keepdims=True)
        return m, jnp.min(jnp.where(v == m, lane, 1e9), axis=1, keepdims=True)

    gmask = (lane >= N_EXPERTS) & (lane < N_EXPERTS + N_GROUPS)
    gl = jnp.where(gmask, logits, ninf)
    gmax, gidx = first_max(gl)
    p_g = 1.0 / jnp.sum(jnp.where(gmask, jnp.exp(gl - gmax), 0.0), axis=1, keepdims=True)
    e_lo = (gidx - N_EXPERTS) * EXPERTS_PER_GROUP
    el = jnp.where((lane >= e_lo) & (lane < e_lo + EXPERTS_PER_GROUP), logits, ninf)
    v1, i1 = first_max(el)
    el2 = jnp.where(lane == i1, ninf, el)
    v2, i2 = first_max(el2)
    e2 = jnp.exp(v2 - v1)
    w1 = 1.0 / (1.0 + e2)
    comb_ref[...] = jnp.where(lane == i1, w1 * p_g,
                              jnp.where(lane == i2, (e2 * w1) * p_g, jnp.where(lane == gidx, 1.0, 0.0)))


def _merge(x, olat, ob, sa, sb, wuv, wba, wbb, wout, g, wrh, wrl, br):
    n, d = x.shape
    tm = ROW_TILE
    row = lambda w: pl.BlockSpec((tm, w), lambda i: (i, 0))
    ws = (wuv, wba, wbb, wout, g, wrh, wrl, br)
    return pl.pallas_call(
        _merge_kernel,
        grid=(n // tm,),
        in_specs=[row(d), row(olat.shape[1]), row(ob.shape[1]), row(d), row(d)] + [_full_spec(w.shape) for w in ws],
        out_specs=[row(d), row(d), row(LANES)],
        out_shape=[jax.ShapeDtypeStruct((n, d), F32), jax.ShapeDtypeStruct((n, d), BF16),
                   jax.ShapeDtypeStruct((n, LANES), F32)],
        compiler_params=pltpu.CompilerParams(dimension_semantics=("arbitrary",), vmem_limit_bytes=VMEM_LIMIT),
        name="merge",
    )(x, olat, ob, sa, sb, *ws)


def _moe_kernel(h_ref, comb_ref, wgu_ref, wd_ref, o_ref, slotc_s, slotr_s, split_s, acc_s, rng_s):
    t, d = h_ref.shape
    c = MOE_CHUNK
    rb = MOE_RANK_BLOCK
    g = pl.program_id(1)
    lane = lax.broadcasted_iota(jnp.int32, (c, LANES), 1)

    @pl.when(g == 0)
    def _():
        lane_b = lax.broadcasted_iota(jnp.int32, (rb, LANES), 1)
        group_lanes = (lane_b >= N_EXPERTS) & (lane_b < N_EXPERTS + N_GROUPS)
        r_i = lax.broadcasted_iota(jnp.int32, (rb, rb), 0)
        c_i = lax.broadcasted_iota(jnp.int32, (rb, rb), 1)
        lower = jnp.where(c_i < r_i, 1.0, 0.0).astype(BF16)
        counts = jnp.zeros((1, LANES), F32)
        ranks = []
        for blk in range(t // rb):
            oh = jnp.where(group_lanes, comb_ref[blk * rb:(blk + 1) * rb, :], 0.0)
            before = _dot(lower, oh.astype(BF16)) + counts
            ranks.append(jnp.sum(before * oh, axis=1, keepdims=True))
            counts = counts + jnp.sum(oh, axis=0, keepdims=True)
        lane1 = lax.broadcasted_iota(jnp.int32, (1, LANES), 1)
        start = jnp.int32(0)
        seg = jnp.zeros((1, LANES), F32)
        for k in range(N_GROUPS):
            n_k = jnp.sum(jnp.where(lane1 == N_EXPERTS + k, counts, 0.0)).astype(jnp.int32)
            seg = jnp.where(lane1 == N_EXPERTS + k, (start * c).astype(F32), seg)
            rng_s[k] = start
            for j in range(-(-t // c)):
                start = start + (n_k > j * c).astype(jnp.int32)
            rng_s[N_GROUPS + k] = start
        for blk in range(t // rb):
            rows = slice(blk * rb, (blk + 1) * rb)
            oh = jnp.where(group_lanes, comb_ref[rows, :], 0.0)
            slot = ranks[blk] + jnp.sum(oh * seg, axis=1, keepdims=True)
            slotc_s[rows, :] = jnp.broadcast_to(slot, (rb, LANES))
        slotr_s[...] = jnp.transpose(slotc_s[...])
        cb = comb_ref[...]
        hi = cb.astype(BF16)
        r1 = cb - hi.astype(F32)
        mid = r1.astype(BF16)
        split_s[0] = hi
        split_s[1] = mid
        split_s[2] = (r1 - mid.astype(F32)).astype(BF16)
        acc_s[...] = jnp.zeros(acc_s.shape, F32)

    def chunk_body(ci, carry):
        base = (ci * c).astype(F32)
        row_slot = lax.broadcasted_iota(jnp.int32, (c, t), 0).astype(F32) + base
        pc = jnp.where(slotr_s[0:1, :] == row_slot, 1.0, 0.0).astype(BF16)
        xs = _dot(pc, h_ref[...]).astype(BF16)
        cw = _dot(pc, split_s[0]) + _dot(pc, split_s[1]) + _dot(pc, split_s[2])
        y = jnp.zeros((c, d), F32)
        for e in range(EXPERTS_PER_GROUP):
            gu = _dot(xs, wgu_ref[e])
            gate = gu[:, :D_EXPERT]
            hid = gate * jax.nn.sigmoid(gate) * gu[:, D_EXPERT:]
            ce = jnp.sum(jnp.where(lane == g * EXPERTS_PER_GROUP + e, cw, 0.0), axis=1, keepdims=True)
            y = y + _dot((hid * ce).astype(BF16), wd_ref[e])
        col_slot = lax.broadcasted_iota(jnp.int32, (t, c), 1).astype(F32) + base
        pct = jnp.where(jnp.broadcast_to(slotc_s[:, 0:1], (t, c)) == col_slot, 1.0, 0.0).astype(BF16)
        acc_s[...] += _dot(pct, y.astype(BF16))
        return carry

    lax.fori_loop(rng_s[g], rng_s[N_GROUPS + g], chunk_body, 0)

    @pl.when(g == N_GROUPS - 1)
    def _():
        o_ref[...] = acc_s[...].astype(BF16)


def _moe(h2, comb, wgu, wd):
    n, d = h2.shape
    tm = MOE_ROW_TILE
    row = lambda w: pl.BlockSpec((tm, w), lambda i, g: (i, 0))
    return pl.pallas_call(
        _moe_kernel,
        grid=(n // tm, N_GROUPS),
        in_specs=[row(d), row(LANES),
                  pl.BlockSpec((EXPERTS_PER_GROUP, d, 2 * D_EXPERT), lambda i, g: (g, 0, 0)),
                  pl.BlockSpec((EXPERTS_PER_GROUP, D_EXPERT, d), lambda i, g: (g, 0, 0))],
        out_specs=row(d),
        out_shape=jax.ShapeDtypeStruct((n, d), BF16),
        scratch_shapes=[pltpu.VMEM((tm, LANES), F32), pltpu.VMEM((LANES, tm), F32),
                        pltpu.VMEM((3, tm, LANES), BF16), pltpu.VMEM((tm, d), F32),
                        pltpu.SMEM((2 * N_GROUPS,), jnp.int32)],
        compiler_params=pltpu.CompilerParams(dimension_semantics=("arbitrary", "arbitrary"),
                                             vmem_limit_bytes=VMEM_LIMIT),
        name="moe",
    )(h2, comb, wgu, wd)


def _ple_kernel(x_ref, m_ref, p_ref, g_ref, wpg_ref, wple_ref, gf_ref, o_ref, *, last):
    x2 = x_ref[...] + m_ref[...].astype(F32)
    h3 = _rms(x2, g_ref[...]).astype(BF16)
    gate = jax.nn.sigmoid(_dot(h3, wpg_ref[...]))
    x3 = x2 + _dot(p_ref[...].astype(BF16), wple_ref[...]) * gate
    o_ref[...] = _rms(x3, gf_ref[...]) if last else x3


def _ple(x1, moe, p, g, wpg, wple, gf, last):
    n, d = x1.shape
    tm = ROW_TILE
    row = lambda w: pl.BlockSpec((tm, w), lambda i: (i, 0))
    ws = (g, wpg, wple, gf)
    return pl.pallas_call(
        functools.partial(_ple_kernel, last=last),
        grid=(n // tm,),
        in_specs=[row(d), row(d), row(p.shape[1])] + [_full_spec(w.shape) for w in ws],
        out_specs=row(d),
        out_shape=jax.ShapeDtypeStruct((n, d), F32),
        compiler_params=pltpu.CompilerParams(dimension_semantics=("arbitrary",), vmem_limit_bytes=VMEM_LIMIT),
        name="ple",
    )(x1, moe, p, *ws)


def _block_diag(blocks):
    h, r, c = blocks.shape
    eye = jnp.eye(h, dtype=blocks.dtype)
    return (eye[:, None, :, None] * blocks[:, :, None, :]).reshape(h * r, h * c)


def kernel(x, p, attn_norm, w_in, kv_norm, w_uk, w_uv, rel_bias, w_branch_a, w_branch_b, w_out, ffn_norm,
           w_r1, b_r1, w_r2, b_r2, w_gate, w_up, w_down, ple_norm, w_ple_gate, w_ple, final_norm):
    batch, seq, d = x.shape
    n = batch * seq
    depth = w_in.shape[0]
    xf = x.reshape(n, d).astype(F32)
    widths = [WIDTH_A, KV_RANK, IDX_HEADS * IDX_DIM, IDX_DIM, IDX_HEADS, 3 * WIDTH_B, d, d]
    starts = [sum(widths[:k]) for k in range(len(widths))]
    for i in range(depth):
        w_qa, w_ckv, w_qi, w_ki, w_wi, w_qkv, w_ga, w_gb = [
            w_in[i][:, s:s + w].astype(BF16) for s, w in zip(starts, widths)]
        w_kk = jnp.concatenate([w_ki, w_ki], axis=1)
        w_wi = jnp.pad(w_wi, ((0, 0), (0, LANES - IDX_HEADS)))
        wuk_bd = _block_diag(jnp.swapaxes(w_uk[i], 1, 2)).astype(BF16)
        wuv_bd = _block_diag(w_uv[i]).astype(BF16)
        qabs, ckv, ckvt, qidx, kk, widx, qb, kb, vb, sa, sb = _proj(
            xf, attn_norm[i][None].astype(F32), w_qa, wuk_bd, w_ckv, kv_norm[i][None].astype(F32),
            w_qi, w_kk, w_wi, w_qkv, w_ga, w_gb)
        olat = _dsa(rel_bias.astype(F32), qabs, qidx, widx, ckv, ckvt, kk, batch, seq)
        ob = _stick(qb, kb, vb, batch, seq)
        w_r = jnp.concatenate([jnp.transpose(w_r2[i], (1, 0, 2)).reshape(d, N_EXPERTS), w_r1[i]], axis=1)
        w_r = jnp.pad(w_r.astype(F32), ((0, 0), (0, LANES - N_EXPERTS - N_GROUPS)))
        w_r_hi = w_r.astype(BF16)
        w_r_lo = (w_r - w_r_hi.astype(F32)).astype(BF16)
        b_r = jnp.pad(jnp.concatenate([b_r2[i].reshape(-1), b_r1[i]]).astype(F32),
                      (0, LANES - N_EXPERTS - N_GROUPS))[None]
        x1, h2, comb = _merge(xf, olat, ob, sa, sb, wuv_bd, w_branch_a[i].astype(BF16), w_branch_b[i].astype(BF16),
                              w_out[i].astype(BF16), ffn_norm[i][None].astype(F32), w_r_hi, w_r_lo, b_r)
        wgu = jnp.concatenate([w_gate[i], w_up[i]], axis=2).astype(BF16)
        moe = _moe(h2, comb, wgu, w_down[i].astype(BF16))
        xf = _ple(x1, moe, p[i].reshape(n, -1).astype(F32), ple_norm[i][None].astype(F32), w_ple_gate[i].astype(BF16),
                  w_ple[i].astype(BF16), final_norm[None].astype(F32), last=(i == depth - 1))
    return xf.reshape(batch, seq, d).astype(x.dtype)
```

```python
import functools
import math

import jax
import jax.numpy as jnp
from jax import lax
from jax.experimental import pallas as pl
from jax.experimental.pallas import tpu as pltpu

D_MODEL = 1024
N_HEADS_A = 8
HEAD_DIM = 64
WIDTH_A = N_HEADS_A * HEAD_DIM
KV_RANK = 128
IDX_HEADS = 8
IDX_DIM = 64
TOPK_MAX = 256
N_HEADS_B = 8
WIDTH_B = N_HEADS_B * HEAD_DIM
N_BUCKETS = 32
MAX_EXACT = N_BUCKETS // 2
MAX_DISTANCE = 128
ATTN_SCALE = HEAD_DIM ** -0.5
IDX_SCALE = (IDX_HEADS ** -0.5) * (IDX_DIM ** -0.5)
N_GROUPS = 4
EXPERTS_PER_GROUP = 8
N_EXPERTS = N_GROUPS * EXPERTS_PER_GROUP
D_EXPERT = 256
PLE_DIM = 256
EPS = 1e-6
LOG2E = 1.4426950408889634

LANES = 128
ROW_TILE = 512
MOE_ROW_TILE = 1024
MOE_CHUNK = 288
MOE_RANK_BLOCK = 256
ATT_TILE = 256
VMEM_LIMIT = 56 * 1024 * 1024
NEG_BIG = -1e30
EXP_UNDERFLOW = -173.0
KEY_TOP = 127.0
KEY_MIN_RANGE = 1e-30

F32 = jnp.float32
BF16 = jnp.bfloat16


def _rms(x, g):
    return x * lax.rsqrt(jnp.mean(x * x, axis=-1, keepdims=True) + EPS) * g


def _dot(a, b):
    return jnp.dot(a, b, preferred_element_type=F32)


def _dot_nt(a, b):
    return lax.dot_general(a, b, (((1,), (1,)), ((), ())), preferred_element_type=F32)


def _full_spec(shape):
    nd = len(shape)
    return pl.BlockSpec(shape, lambda *_: (0,) * nd)


def _proj_kernel(x_ref, g_ref, wqa_ref, wuk_ref, wckv_ref, kvg_ref, wqi_ref, wkk_ref, wwi_ref,
                 wqkv_ref, wga_ref, wgb_ref,
                 qabs_ref, ckv_ref, ckvt_ref, qidx_ref, kk_ref, widx_ref, qb_ref, kb_ref, vb_ref, sa_ref, sb_ref):
    h = _rms(x_ref[...], g_ref[...]).astype(BF16)
    qa = _dot(h, wqa_ref[...]).astype(BF16)
    half = WIDTH_A // 2
    qabs = jnp.concatenate([_dot(qa[:, :half], wuk_ref[:half, :N_HEADS_A * KV_RANK // 2]),
                            _dot(qa[:, half:], wuk_ref[half:, N_HEADS_A * KV_RANK // 2:])], axis=1)
    qabs_ref[...] = (qabs * (ATTN_SCALE * LOG2E)).astype(BF16)
    c = _rms(_dot(h, wckv_ref[...]), kvg_ref[...])
    ckv_ref[...] = c.astype(BF16)
    c_ext = jnp.concatenate([c, jnp.ones_like(c)], axis=1)
    for j in range(ckvt_ref.shape[0]):
        ckvt_ref[j] = jnp.transpose(c_ext[j * ATT_TILE:(j + 1) * ATT_TILE, :]).astype(BF16)
    qidx_ref[...] = _dot(h, wqi_ref[...]).astype(BF16)
    kk_ref[...] = _dot(h, wkk_ref[...]).astype(BF16)
    widx_ref[...] = _dot(h, wwi_ref[...]) * IDX_SCALE
    qkv = _dot(h, wqkv_ref[...])
    qb_ref[...] = (qkv[:, :WIDTH_B] * (ATTN_SCALE * LOG2E)).astype(BF16)
    kb_ref[...] = qkv[:, WIDTH_B:2 * WIDTH_B].astype(BF16)
    vb_ref[...] = qkv[:, 2 * WIDTH_B:].astype(BF16)
    sa_ref[...] = jax.nn.sigmoid(_dot(h, wga_ref[...])).astype(BF16)
    sb_ref[...] = jax.nn.sigmoid(_dot(h, wgb_ref[...])).astype(BF16)


def _proj(x, g, wqa, wuk, wckv, kvg, wqi, wkk, wwi, wqkv, wga, wgb):
    n, d = x.shape
    tm = ROW_TILE
    row = lambda w: pl.BlockSpec((tm, w), lambda i: (i, 0))
    ws = (g, wqa, wuk, wckv, kvg, wqi, wkk, wwi, wqkv, wga, wgb)
    outs = [(N_HEADS_A * KV_RANK, BF16), (KV_RANK, BF16), None, (IDX_HEADS * IDX_DIM, BF16), (LANES, BF16),
            (LANES, F32), (WIDTH_B, BF16), (WIDTH_B, BF16), (WIDTH_B, BF16), (d, BF16), (d, BF16)]
    t = ATT_TILE
    ckvt_spec = pl.BlockSpec((tm // t, 2 * KV_RANK, t), lambda i: (i, 0, 0))
    ckvt_shape = jax.ShapeDtypeStruct((n // t, 2 * KV_RANK, t), BF16)
    return pl.pallas_call(
        _proj_kernel,
        grid=(n // tm,),
        in_specs=[row(d)] + [_full_spec(w.shape) for w in ws],
        out_specs=[ckvt_spec if o is None else row(o[0]) for o in outs],
        out_shape=[ckvt_shape if o is None else jax.ShapeDtypeStruct((n, o[0]), o[1]) for o in outs],
        compiler_params=pltpu.CompilerParams(dimension_semantics=("arbitrary",), vmem_limit_bytes=VMEM_LIMIT),
        name="proj",
    )(x, *ws)


def _t5_bucket(dist):
    dist = jnp.maximum(dist, 0)
    d_f = jnp.maximum(dist, 1).astype(F32)
    large = MAX_EXACT + jnp.floor(jnp.log(d_f / MAX_EXACT) / math.log(MAX_DISTANCE / MAX_EXACT)
                                  * (N_BUCKETS - MAX_EXACT)).astype(jnp.int32)
    large = jnp.minimum(large, N_BUCKETS - 1)
    return jnp.where(dist < MAX_EXACT, dist, large)


def _dsa_kernel(relb_ref, qabs_ref, qidx_ref, widx_ref, ckv_ref, ckvt_ref, kk_ref, o_ref,
                qa_s, qi_s, key_s, dig_s, act_s, cs_s, mn_s, mx_s, bias_s, m_s, acc_s, sa_s, sb_s, *, n_sel, seq):
    tq = tk = ATT_TILE
    nh = N_HEADS_A
    nhq = nh * tq
    qi = pl.program_id(1)
    n_kt = qi + 1
    kpos = lax.broadcasted_iota(jnp.int32, (tk, tq), 0)
    qpos = lax.broadcasted_iota(jnp.int32, (tk, tq), 1)

    @pl.when((pl.program_id(0) == 0) & (qi == 0))
    def _():
        for t in range(3):
            bucket = _t5_bucket(t * tq + qpos - kpos if t < 2 else jnp.full((tk, tq), 2 * tq, jnp.int32))
            for h in range(nh):
                b = jnp.zeros((tk, tq), F32)
                for bk in range(N_BUCKETS):
                    b = jnp.where(bucket == bk, relb_ref[bk, h], b)
                bias_s[t, :, h * tq:(h + 1) * tq] = b * LOG2E

    lane = lax.broadcasted_iota(jnp.int32, (tq, LANES), 1)
    lo_half = jnp.where(lane < IDX_DIM, 1.0, 0.0)
    hi_half = 1.0 - lo_half
    for h in range(nh):
        qa_s[h * tq:(h + 1) * tq, :] = qabs_ref[:, h * KV_RANK:(h + 1) * KV_RANK]
        pair = qidx_ref[:, (h // 2) * LANES:(h // 2 + 1) * LANES].astype(F32)
        qi_s[h * tq:(h + 1) * tq, :] = (pair * (lo_half if h % 2 == 0 else hi_half)).astype(BF16)
    w_t = jnp.transpose(widx_ref[...])
    w_row = jnp.concatenate([w_t[h:h + 1, :] for h in range(nh)], axis=1)

    mn_s[...] = jnp.full(mn_s.shape, -NEG_BIG, F32)
    mx_s[...] = jnp.full(mx_s.shape, NEG_BIG, F32)

    def idx_dots(kj, r_ref):
        ks = pl.multiple_of(jnp.minimum(kj, n_kt - 1) * tk, tk)
        r_ref[...] = _dot_nt(kk_ref[pl.ds(ks, tk), :], qi_s[...])

    def idx_scores(kj, r_ref):
        r = jnp.maximum(r_ref[...], 0.0) * w_row
        s = r[:, :tq]
        for h in range(1, nh):
            s = s + r[:, h * tq:(h + 1) * tq]
        masked = (kpos > qpos) & (kj == qi)
        key_s[kj] = jnp.where(masked, NEG_BIG, s)
        mn_s[...] = jnp.minimum(mn_s[...], jnp.min(jnp.where(masked, -NEG_BIG, s).reshape(tk // 8, 8, tq), axis=0))
        mx_s[...] = jnp.maximum(mx_s[...], jnp.max(jnp.where(masked, NEG_BIG, s).reshape(tk // 8, 8, tq), axis=0))

    idx_dots(0, sa_s)

    def idx_pair(i, carry):
        idx_dots(2 * i + 1, sb_s)
        idx_scores(2 * i, sa_s)
        idx_dots(2 * i + 2, sa_s)
        idx_scores(2 * i + 1, sb_s)
        return carry

    lax.fori_loop(0, n_kt // 2, idx_pair, 0)

    @pl.when(n_kt % 2 == 1)
    def _():
        idx_scores(n_kt - 1, sa_s)

    lo = jnp.min(mn_s[...], axis=0, keepdims=True)
    hi = jnp.max(mx_s[...], axis=0, keepdims=True)
    scale = KEY_TOP / jnp.maximum(hi - lo, KEY_MIN_RANGE)

    def key_body(kj, carry):
        y = jnp.maximum((key_s[kj] - lo) * scale, -1.0)
        rem = y
        for p in (3, 2, 1, 0):
            d = jnp.floor(rem)
            dig_s[p, kj] = d.astype(BF16)
            rem = (rem - d) * 256.0
        key_s[kj] = y - rem * (2.0 ** -32)
        return carry

    lax.fori_loop(0, n_kt, key_body, 0)

    def count_active(pred):
        one, zero = jnp.ones((), BF16), jnp.zeros((), BF16)

        def body(kj, acc):
            c = jnp.where(pred(act_s[kj]), one, zero).reshape(tk // 16, 16, tq)
            part = c[0]
            for j in range(1, tk // 16):
                part = part + c[j]
            return acc + part.astype(F32)
        acc = lax.fori_loop(0, n_kt, body, jnp.zeros((16, tq), F32))
        return jnp.sum(acc, axis=0, keepdims=True)

    def to_digit(v):
        return v.astype(F32).astype(BF16)

    target = jnp.full((1, tq), float(n_sel), F32)
    tau = jnp.zeros((1, tq), F32)
    digit = jnp.zeros((1, tq), jnp.int32)
    for p in (3, 2, 1, 0):
        if p == 3:
            def init_body(kj, carry):
                act_s[kj] = dig_s[3, kj]
                return carry
            lax.fori_loop(0, n_kt, init_body, 0)
        else:
            prev = to_digit(digit)
            target = target - count_active(lambda a: a > prev)

            def narrow_body(kj, carry, prev=prev, p=p):
                act_s[kj] = jnp.where(act_s[kj] == prev, dig_s[p, kj], jnp.full((), -1.0, BF16))
                return carry
            lax.fori_loop(0, n_kt, narrow_body, 0)

        def bit_body(i, dgt, target=target):
            cand = dgt | jnp.left_shift(jnp.int32(1), 7 - i)
            cand_d = to_digit(cand)
            cnt = count_active(lambda a: a >= cand_d)
            return jnp.where(cnt >= target, cand, dgt)
        digit = lax.fori_loop(0, 8, bit_body, jnp.zeros((1, tq), jnp.int32))
        tau = tau + digit.astype(F32) * (256.0 ** (p - 3))

    first = qi == 0
    tau = jnp.where(first, -0.5, tau)
    last = to_digit(digit)
    n_gt = count_active(lambda a: a > last)
    n_eq = count_active(lambda a: a == last)
    need = target - n_gt

    def count_keys(pred):
        def body(kj, acc):
            c = jnp.where(pred(key_s[kj], kpos + kj * tk), 1.0, 0.0)
            return acc + jnp.sum(c.reshape(tk // 8, 8, tq), axis=0)
        acc = lax.fori_loop(0, n_kt, body, jnp.zeros((8, tq), F32))
        return jnp.sum(acc, axis=0, keepdims=True)

    cs_s[...] = jnp.full(cs_s.shape, seq, jnp.int32)

    @pl.when(jnp.logical_not(first) & (jnp.max(n_eq - need) > 0.0))
    def _():
        nbits = seq.bit_length() - 1

        def tie_body(i, d):
            cand = d | jnp.left_shift(jnp.int32(1), nbits - 1 - i)
            cnt = count_keys(lambda k, pos: (k == tau) & (pos < cand))
            return jnp.where(cnt < need, cand, d)

        d = lax.fori_loop(0, nbits, tie_body, jnp.zeros((1, tq), jnp.int32))
        cs_s[...] = jnp.broadcast_to(d, cs_s.shape)

    cstar = jnp.where(first, -1, cs_s[0:1, :])

    m_s[...] = jnp.full(m_s.shape, NEG_BIG, F32)
    acc_s[...] = jnp.zeros(acc_s.shape, F32)

    def scores(kj, s_ref):
        kc = jnp.minimum(kj, n_kt - 1)
        ks = pl.multiple_of(kc * tk, tk)
        k = key_s[kc]
        sel = ((k > tau) | ((k == tau) & (kpos + kc * tk <= cstar))) & (kj < n_kt)
        selb = jnp.where(sel, 0.0, NEG_BIG)
        s = _dot_nt(ckv_ref[pl.ds(ks, tk), :], qa_s[...])
        s_ref[...] = s + bias_s[jnp.minimum(qi - kc, 2)] + jnp.concatenate([selb] * nh, axis=1)

    def update(kj, s_ref):
        s = s_ref[...]
        m_old = m_s[0:1, :]
        m_new = jnp.maximum(m_old, jnp.max(s, axis=0, keepdims=True))
        alpha = jnp.exp2(m_old - m_new)
        p = jnp.exp2(s - m_new).astype(BF16)
        acc_s[...] = acc_s[...] * alpha + _dot(ckvt_ref[jnp.minimum(kj, n_kt - 1)], p)
        m_s[...] = jnp.broadcast_to(m_new, m_s.shape)

    scores(0, sa_s)

    def pair_body(i, carry):
        scores(2 * i + 1, sb_s)
        update(2 * i, sa_s)
        scores(2 * i + 2, sa_s)
        update(2 * i + 1, sb_s)
        return carry

    lax.fori_loop(0, n_kt // 2, pair_body, 0)

    @pl.when(n_kt % 2 == 1)
    def _():
        update(n_kt - 1, sa_s)

    acc = acc_s[...]
    o_t = acc[:KV_RANK] / acc[KV_RANK:]
    for h in range(nh):
        o_ref[:, h * KV_RANK:(h + 1) * KV_RANK] = jnp.transpose(o_t[:, h * tq:(h + 1) * tq]).astype(BF16)


def _dsa(rel_bias, qabs, qidx, widx, ckv, ckvt, kk, batch, seq):
    n = qabs.shape[0]
    tq = ATT_TILE
    nq = seq // tq
    n_sel = min(TOPK_MAX, seq // 4)
    assert n_sel == tq and seq % tq == 0 and seq & (seq - 1) == 0
    qrow = lambda w: pl.BlockSpec((tq, w), lambda b, q: (b * nq + q, 0))
    brow = lambda w: pl.BlockSpec((seq, w), lambda b, q: (b, 0))
    nhq = N_HEADS_A * tq
    return pl.pallas_call(
        functools.partial(_dsa_kernel, n_sel=n_sel, seq=seq),
        grid=(batch, nq),
        in_specs=[pl.BlockSpec(memory_space=pltpu.SMEM),
                  qrow(N_HEADS_A * KV_RANK), qrow(IDX_HEADS * IDX_DIM), qrow(LANES),
                  brow(KV_RANK), pl.BlockSpec((nq, 2 * KV_RANK, tq), lambda b, q: (b, 0, 0)), brow(LANES)],
        out_specs=qrow(N_HEADS_A * KV_RANK),
        out_shape=jax.ShapeDtypeStruct((n, N_HEADS_A * KV_RANK), BF16),
        scratch_shapes=[pltpu.VMEM((nhq, KV_RANK), BF16), pltpu.VMEM((nhq, LANES), BF16),
                        pltpu.VMEM((nq, tq, tq), F32), pltpu.VMEM((4, nq, tq, tq), BF16),
                        pltpu.VMEM((nq, tq, tq), BF16), pltpu.VMEM((8, tq), jnp.int32),
                        pltpu.VMEM((8, tq), F32), pltpu.VMEM((8, tq), F32),
                        pltpu.VMEM((3, tq, nhq), F32),
                        pltpu.VMEM((8, nhq), F32), pltpu.VMEM((2 * KV_RANK, nhq), F32),
                        pltpu.VMEM((tq, nhq), F32), pltpu.VMEM((tq, nhq), F32)],
        compiler_params=pltpu.CompilerParams(dimension_semantics=("arbitrary", "arbitrary"),
                                             vmem_limit_bytes=VMEM_LIMIT),
        name="dsa",
    )(rel_bias, qabs, qidx, widx, ckv, ckvt, kk)


def _stick_kernel(q_ref, k_ref, v_ref, o_ref):
    tq = tk = ATT_TILE
    qi = pl.program_id(1)
    npair = WIDTH_B // LANES
    lane = lax.broadcasted_iota(jnp.int32, (tq, LANES), 1)
    row = lax.broadcasted_iota(jnp.int32, (tq, tk), 0)
    col = lax.broadcasted_iota(jnp.int32, (tq, tk), 1)
    causal = col < row
    upper = jnp.where(row > col, 1.0, 0.0).astype(BF16)
    upper_ones = jnp.concatenate([upper, jnp.ones((tk, LANES), BF16)], axis=1)
    lo_half = jnp.where(lane < HEAD_DIM, 1.0, 0.0)
    causal2 = jnp.concatenate([causal, causal], axis=0)
    q_pairs = []
    for p in range(npair):
        q2 = q_ref[:, p * LANES:(p + 1) * LANES].astype(F32)
        q_pairs.append(jnp.concatenate([q2 * lo_half, q2 * (1.0 - lo_half)], axis=0).astype(BF16))

    def tile(kj, r_sums, outs, masked):
        ks = pl.multiple_of(kj * tk, tk)
        log_sig, log_1m = [], []
        for p in range(npair):
            z = _dot_nt(q_pairs[p], k_ref[pl.ds(ks, tk), p * LANES:(p + 1) * LANES])
            sp = jnp.maximum(z, 0.0) + jnp.log2(1.0 + jnp.exp2(-jnp.abs(z)))
            lm = jnp.where(causal2, -sp, 0.0) if masked else -sp
            log_sig.append(z - sp)
            log_1m.append(lm.astype(BF16))
        sums = _dot(jnp.concatenate(log_1m, axis=0), upper_ones)
        new_r, new_o = [], []
        for p in range(npair):
            sm = sums[p * 2 * tq:(p + 1) * 2 * tq]
            after = sm[:, :tk] + jnp.concatenate([r_sums[p]] * (tk // LANES), axis=1)
            a = jnp.exp2(log_sig[p] + after)
            if masked:
                a = jnp.where(causal2, a, 0.0)
            res = _dot(a.astype(BF16), v_ref[pl.ds(ks, tk), p * LANES:(p + 1) * LANES])
            new_o.append(outs[p] + jnp.where(lane < HEAD_DIM, res[:tq], res[tq:]))
            new_r.append(r_sums[p] + sm[:, tk:])
        return tuple(new_r), tuple(new_o)

    zeros = jnp.zeros((tq, LANES), F32)
    r_sums, outs = tile(qi, (jnp.zeros((2 * tq, LANES), F32),) * npair, (zeros,) * npair, True)

    def cond(c):
        it, r_sums, _ = c
        return (it < qi) & (jnp.max(functools.reduce(jnp.maximum, r_sums)) > EXP_UNDERFLOW)

    def body(c):
        it, r_sums, outs = c
        r_sums, outs = tile(qi - 1 - it, r_sums, outs, False)
        return it + 1, r_sums, outs

    _, _, outs = lax.while_loop(cond, body, (jnp.int32(0), r_sums, outs))
    for p in range(npair):
        o_ref[:, p * LANES:(p + 1) * LANES] = outs[p].astype(BF16)


def _stick(qb, kb, vb, batch, seq):
    n = qb.shape[0]
    tq = ATT_TILE
    nq = seq // tq
    qspec = pl.BlockSpec((tq, WIDTH_B), lambda b, q: (b * nq + q, 0))
    kspec = pl.BlockSpec((seq, WIDTH_B), lambda b, q: (b, 0))
    return pl.pallas_call(
        _stick_kernel,
        grid=(batch, nq),
        in_specs=[qspec, kspec, kspec],
        out_specs=qspec,
        out_shape=jax.ShapeDtypeStruct((n, WIDTH_B), BF16),
        compiler_params=pltpu.CompilerParams(dimension_semantics=("arbitrary",) * 2, vmem_limit_bytes=VMEM_LIMIT),
        name="stick",
    )(qb, kb, vb)


def _merge_kernel(x_ref, ol_ref, ob_ref, sa_ref, sb_ref, wuv_ref, wba_ref, wbb_ref, wout_ref, g_ref,
                  wr_ref, br_ref, x1_ref, h2_ref, comb_ref):
    half = N_HEADS_A * KV_RANK // 2
    oa = jnp.concatenate([_dot(ol_ref[:, :half], wuv_ref[:half, :WIDTH_A // 2]),
                          _dot(ol_ref[:, half:], wuv_ref[half:, WIDTH_A // 2:])], axis=1).astype(BF16)
    ya = _dot(oa, wba_ref[...])
    yb = _dot(ob_ref[...], wbb_ref[...])
    merged = sa_ref[...].astype(F32) * ya + sb_ref[...].astype(F32) * yb
    x1 = x_ref[...] + _dot(merged.astype(BF16), wout_ref[...])
    x1_ref[...] = x1
    h2 = _rms(x1, g_ref[...])
    h2_hi = h2.astype(BF16)
    h2_ref[...] = h2_hi
    h2_lo = (h2 - h2_hi.astype(F32)).astype(BF16)
    hi_part = _dot(h2_hi, wr_ref[...])
    logits = (hi_part[:, :LANES] + _dot(h2_lo, wr_ref[:, :LANES]) + hi_part[:, LANES:]) + br_ref[...]
    lane = lax.broadcasted_iota(jnp.int32, logits.shape, 1).astype(F32)
    ninf = -jnp.inf

    def first_max(v):
        m = jnp.max(v, axis=1, keepdims=True)
        return m, jnp.min(jnp.where(v == m, lane, 1e9), axis=1, keepdims=True)

    gmask = (lane >= N_EXPERTS) & (lane < N_EXPERTS + N_GROUPS)
    gl = jnp.where(gmask, logits, ninf)
    gmax, gidx = first_max(gl)
    p_g = 1.0 / jnp.sum(jnp.where(gmask, jnp.exp(gl - gmax), 0.0), axis=1, keepdims=True)
    e_lo = (gidx - N_EXPERTS) * EXPERTS_PER_GROUP
    el = jnp.where((lane >= e_lo) & (lane < e_lo + EXPERTS_PER_GROUP), logits, ninf)
    v1, i1 = first_max(el)
    el2 = jnp.where(lane == i1, ninf, el)
    v2, i2 = first_max(el2)
    e2 = jnp.exp(v2 - v1)
    w1 = 1.0 / (1.0 + e2)
    comb_ref[...] = jnp.where(lane == i1, w1 * p_g,
                              jnp.where(lane == i2, (e2 * w1) * p_g, jnp.where(lane == gidx, 1.0, 0.0)))


def _merge(x, olat, ob, sa, sb, wuv, wba, wbb, wout, g, wr, br):
    n, d = x.shape
    tm = ROW_TILE
    row = lambda w: pl.BlockSpec((tm, w), lambda i: (i, 0))
    ws = (wuv, wba, wbb, wout, g, wr, br)
    return pl.pallas_call(
        _merge_kernel,
        grid=(n // tm,),
        in_specs=[row(d), row(olat.shape[1]), row(ob.shape[1]), row(d), row(d)] + [_full_spec(w.shape) for w in ws],
        out_specs=[row(d), row(d), row(LANES)],
        out_shape=[jax.ShapeDtypeStruct((n, d), F32), jax.ShapeDtypeStruct((n, d), BF16),
                   jax.ShapeDtypeStruct((n, LANES), F32)],
        compiler_params=pltpu.CompilerParams(dimension_semantics=("arbitrary",), vmem_limit_bytes=VMEM_LIMIT),
        name="merge",
    )(x, olat, ob, sa, sb, *ws)


def _moe_kernel(h_ref, comb_ref, wgu_ref, wd_ref, o_ref, slotc_s, slotr_s, split_s, acc_s, rng_s):
    t, d = h_ref.shape
    c = MOE_CHUNK
    rb = MOE_RANK_BLOCK
    g = pl.program_id(1)
    lane = lax.broadcasted_iota(jnp.int32, (c, LANES), 1)

    @pl.when(g == 0)
    def _():
        lane_b = lax.broadcasted_iota(jnp.int32, (rb, LANES), 1)
        group_lanes = (lane_b >= N_EXPERTS) & (lane_b < N_EXPERTS + N_GROUPS)
        r_i = lax.broadcasted_iota(jnp.int32, (rb, rb), 0)
        c_i = lax.broadcasted_iota(jnp.int32, (rb, rb), 1)
        lower = jnp.where(c_i < r_i, 1.0, 0.0).astype(BF16)
        counts = jnp.zeros((1, LANES), F32)
        ranks = []
        for blk in range(t // rb):
            oh = jnp.where(group_lanes, comb_ref[blk * rb:(blk + 1) * rb, :], 0.0)
            before = _dot(lower, oh.astype(BF16)) + counts
            ranks.append(jnp.sum(before * oh, axis=1, keepdims=True))
            counts = counts + jnp.sum(oh, axis=0, keepdims=True)
        lane1 = lax.broadcasted_iota(jnp.int32, (1, LANES), 1)
        start = jnp.int32(0)
        seg = jnp.zeros((1, LANES), F32)
        for k in range(N_GROUPS):
            n_k = jnp.sum(jnp.where(lane1 == N_EXPERTS + k, counts, 0.0)).astype(jnp.int32)
            seg = jnp.where(lane1 == N_EXPERTS + k, (start * c).astype(F32), seg)
            rng_s[k] = start
            for j in range(-(-t // c)):
                start = start + (n_k > j * c).astype(jnp.int32)
            rng_s[N_GROUPS + k] = start
        for blk in range(t // rb):
            rows = slice(blk * rb, (blk + 1) * rb)
            oh = jnp.where(group_lanes, comb_ref[rows, :], 0.0)
            slot = ranks[blk] + jnp.sum(oh * seg, axis=1, keepdims=True)
            slotc_s[rows, :] = jnp.broadcast_to(slot, (rb, LANES))
        slotr_s[...] = jnp.transpose(slotc_s[...])
        cb = comb_ref[...]
        hi = cb.astype(BF16)
        r1 = cb - hi.astype(F32)
        mid = r1.astype(BF16)
        split_s[0] = hi
        split_s[1] = mid
        split_s[2] = (r1 - mid.astype(F32)).astype(BF16)
        acc_s[...] = jnp.zeros(acc_s.shape, F32)

    def chunk_body(ci, carry):
        base = (ci * c).astype(F32)
        row_slot = lax.broadcasted_iota(jnp.int32, (c, t), 0).astype(F32) + base
        pc = jnp.where(slotr_s[0:1, :] == row_slot, 1.0, 0.0).astype(BF16)
        xs = _dot(pc, h_ref[...]).astype(BF16)
        cw = _dot(pc, split_s[0]) + _dot(pc, split_s[1]) + _dot(pc, split_s[2])
        y = jnp.zeros((c, d), F32)
        for e in range(EXPERTS_PER_GROUP):
            gu = _dot(xs, wgu_ref[e])
            gate = gu[:, :D_EXPERT]
            hid = gate * jax.nn.sigmoid(gate) * gu[:, D_EXPERT:]
            ce = jnp.sum(jnp.where(lane == g * EXPERTS_PER_GROUP + e, cw, 0.0), axis=1, keepdims=True)
            y = y + _dot((hid * ce).astype(BF16), wd_ref[e])
        col_slot = lax.broadcasted_iota(jnp.int32, (t, c), 1).astype(F32) + base
        pct = jnp.where(jnp.broadcast_to(slotc_s[:, 0:1], (t, c)) == col_slot, 1.0, 0.0).astype(BF16)
        acc_s[...] += _dot(pct, y.astype(BF16))
        return carry

    lax.fori_loop(rng_s[g], rng_s[N_GROUPS + g], chunk_body, 0)

    @pl.when(g == N_GROUPS - 1)
    def _():
        o_ref[...] = acc_s[...].astype(BF16)


def _moe(h2, comb, wgu, wd):
    n, d = h2.shape
    tm = MOE_ROW_TILE
    row = lambda w: pl.BlockSpec((tm, w), lambda i, g: (i, 0))
    return pl.pallas_call(
        _moe_kernel,
        grid=(n // tm, N_GROUPS),
        in_specs=[row(d), row(LANES),
                  pl.BlockSpec((EXPERTS_PER_GROUP, d, 2 * D_EXPERT), lambda i, g: (g, 0, 0)),
                  pl.BlockSpec((EXPERTS_PER_GROUP, D_EXPERT, d), lambda i, g: (g, 0, 0))],
        out_specs=row(d),
        out_shape=jax.ShapeDtypeStruct((n, d), BF16),
        scratch_shapes=[pltpu.VMEM((tm, LANES), F32), pltpu.VMEM((LANES, tm), F32),
                        pltpu.VMEM((3, tm, LANES), BF16), pltpu.VMEM((tm, d), F32),
                        pltpu.SMEM((2 * N_GROUPS,), jnp.int32)],
        compiler_params=pltpu.CompilerParams(dimension_semantics=("arbitrary", "arbitrary"),
                                             vmem_limit_bytes=VMEM_LIMIT),
        name="moe",
    )(h2, comb, wgu, wd)


def _ple_kernel(x_ref, m_ref, p_ref, g_ref, wpg_ref, wple_ref, gf_ref, o_ref, *, last):
    x2 = x_ref[...] + m_ref[...].astype(F32)
    h3 = _rms(x2, g_ref[...]).astype(BF16)
    gate = jax.nn.sigmoid(_dot(h3, wpg_ref[...]))
    x3 = x2 + _dot(p_ref[...].astype(BF16), wple_ref[...]) * gate
    o_ref[...] = _rms(x3, gf_ref[...]) if last else x3


def _ple(x1, moe, p, g, wpg, wple, gf, last):
    n, d = x1.shape
    tm = ROW_TILE
    row = lambda w: pl.BlockSpec((tm, w), lambda i: (i, 0))
    ws = (g, wpg, wple, gf)
    return pl.pallas_call(
        functools.partial(_ple_kernel, last=last),
        grid=(n // tm,),
        in_specs=[row(d), row(d), row(p.shape[1])] + [_full_spec(w.shape) for w in ws],
        out_specs=row(d),
        out_shape=jax.ShapeDtypeStruct((n, d), F32),
        compiler_params=pltpu.CompilerParams(dimension_semantics=("arbitrary",), vmem_limit_bytes=VMEM_LIMIT),
        name="ple",
    )(x1, moe, p, *ws)


def _block_diag(blocks):
    h, r, c = blocks.shape
    eye = jnp.eye(h, dtype=blocks.dtype)
    return (eye[:, None, :, None] * blocks[:, :, None, :]).reshape(h * r, h * c)


def kernel(x, p, attn_norm, w_in, kv_norm, w_uk, w_uv, rel_bias, w_branch_a, w_branch_b, w_out, ffn_norm,
           w_r1, b_r1, w_r2, b_r2, w_gate, w_up, w_down, ple_norm, w_ple_gate, w_ple, final_norm):
    batch, seq, d = x.shape
    n = batch * seq
    depth = w_in.shape[0]
    xf = x.reshape(n, d).astype(F32)
    widths = [WIDTH_A, KV_RANK, IDX_HEADS * IDX_DIM, IDX_DIM, IDX_HEADS, 3 * WIDTH_B, d, d]
    starts = [sum(widths[:k]) for k in range(len(widths))]
    for i in range(depth):
        w_qa, w_ckv, w_qi, w_ki, w_wi, w_qkv, w_ga, w_gb = [
            w_in[i][:, s:s + w].astype(BF16) for s, w in zip(starts, widths)]
        w_kk = jnp.concatenate([w_ki, w_ki], axis=1)
        w_wi = jnp.pad(w_wi, ((0, 0), (0, LANES - IDX_HEADS)))
        wuk_bd = _block_diag(jnp.swapaxes(w_uk[i], 1, 2)).astype(BF16)
        wuv_bd = _block_diag(w_uv[i]).astype(BF16)
        qabs, ckv, ckvt, qidx, kk, widx, qb, kb, vb, sa, sb = _proj(
            xf, attn_norm[i][None].astype(F32), w_qa, wuk_bd, w_ckv, kv_norm[i][None].astype(F32),
            w_qi, w_kk, w_wi, w_qkv, w_ga, w_gb)
        olat = _dsa(rel_bias.astype(F32), qabs, qidx, widx, ckv, ckvt, kk, batch, seq)
        ob = _stick(qb, kb, vb, batch, seq)
        w_r = jnp.concatenate([jnp.transpose(w_r2[i], (1, 0, 2)).reshape(d, N_EXPERTS), w_r1[i]], axis=1)
        w_r = jnp.pad(w_r.astype(F32), ((0, 0), (0, LANES - N_EXPERTS - N_GROUPS)))
        w_r_hi = w_r.astype(BF16)
        w_r_lo = (w_r - w_r_hi.astype(F32)).astype(BF16)
        b_r = jnp.pad(jnp.concatenate([b_r2[i].reshape(-1), b_r1[i]]).astype(F32),
                      (0, LANES - N_EXPERTS - N_GROUPS))[None]
        x1, h2, comb = _merge(xf, olat, ob, sa, sb, wuv_bd, w_branch_a[i].astype(BF16), w_branch_b[i].astype(BF16),
                              w_out[i].astype(BF16), ffn_norm[i][None].astype(F32),
                              jnp.concatenate([w_r_hi, w_r_lo], axis=1), b_r)
        wgu = jnp.concatenate([w_gate[i], w_up[i]], axis=2).astype(BF16)
        moe = _moe(h2, comb, wgu, w_down[i].astype(BF16))
        xf = _ple(x1, moe, p[i].reshape(n, -1).astype(F32), ple_norm[i][None].astype(F32), w_ple_gate[i].astype(BF16),
                  w_ple[i].astype(BF16), final_norm[None].astype(F32), last=(i == depth - 1))
    return xf.reshape(batch, seq, d).astype(x.dtype)
```

```python
import functools
import math

import jax
import jax.numpy as jnp
from jax import lax
from jax.experimental import pallas as pl
from jax.experimental.pallas import tpu as pltpu

D_MODEL = 1024
N_HEADS_A = 8
HEAD_DIM = 64
WIDTH_A = N_HEADS_A * HEAD_DIM
KV_RANK = 128
IDX_HEADS = 8
IDX_DIM = 64
TOPK_MAX = 256
N_HEADS_B = 8
WIDTH_B = N_HEADS_B * HEAD_DIM
N_BUCKETS = 32
MAX_EXACT = N_BUCKETS // 2
MAX_DISTANCE = 128
ATTN_SCALE = HEAD_DIM ** -0.5
IDX_SCALE = (IDX_HEADS ** -0.5) * (IDX_DIM ** -0.5)
N_GROUPS = 4
EXPERTS_PER_GROUP = 8
N_EXPERTS = N_GROUPS * EXPERTS_PER_GROUP
D_EXPERT = 256
PLE_DIM = 256
EPS = 1e-6
LOG2E = 1.4426950408889634

LANES = 128
ROW_TILE = 512
MOE_ROW_TILE = 1024
MOE_SUBTILES = 2
MOE_CHUNK = 160
MOE_RANK_BLOCK = 256
ATT_TILE = 256
VMEM_LIMIT = 56 * 1024 * 1024
NEG_BIG = -1e30
EXP_UNDERFLOW = -173.0
KEY_TOP = 127.0
KEY_MIN_RANGE = 1e-30

F32 = jnp.float32
BF16 = jnp.bfloat16


def _rms(x, g):
    return x * lax.rsqrt(jnp.mean(x * x, axis=-1, keepdims=True) + EPS) * g


def _dot(a, b):
    return jnp.dot(a, b, preferred_element_type=F32)


def _dot_nt(a, b):
    return lax.dot_general(a, b, (((1,), (1,)), ((), ())), preferred_element_type=F32)


def _full_spec(shape):
    nd = len(shape)
    return pl.BlockSpec(shape, lambda *_: (0,) * nd)


def _proj_kernel(x_ref, g_ref, wqa_ref, wuk_ref, wckv_ref, kvg_ref, wqi_ref, wkk_ref, wwi_ref,
                 wqkv_ref, wga_ref, wgb_ref,
                 qabs_ref, ckv_ref, ckvt_ref, qidx_ref, kk_ref, widx_ref, qb_ref, kb_ref, vb_ref, sa_ref, sb_ref):
    h = _rms(x_ref[...], g_ref[...]).astype(BF16)
    qa = _dot(h, wqa_ref[...]).astype(BF16)
    half = WIDTH_A // 2
    qabs = jnp.concatenate([_dot(qa[:, :half], wuk_ref[:half, :N_HEADS_A * KV_RANK // 2]),
                            _dot(qa[:, half:], wuk_ref[half:, N_HEADS_A * KV_RANK // 2:])], axis=1)
    qabs_ref[...] = (qabs * (ATTN_SCALE * LOG2E)).astype(BF16)
    c = _rms(_dot(h, wckv_ref[...]), kvg_ref[...])
    ckv_ref[...] = c.astype(BF16)
    c_ext = jnp.concatenate([c, jnp.ones_like(c)], axis=1)
    for j in range(ckvt_ref.shape[0]):
        ckvt_ref[j] = jnp.transpose(c_ext[j * ATT_TILE:(j + 1) * ATT_TILE, :]).astype(BF16)
    qidx_ref[...] = _dot(h, wqi_ref[...]).astype(BF16)
    kk_ref[...] = _dot(h, wkk_ref[...]).astype(BF16)
    widx_ref[...] = _dot(h, wwi_ref[...]) * IDX_SCALE
    qkv = _dot(h, wqkv_ref[...])
    qb_ref[...] = (qkv[:, :WIDTH_B] * (ATTN_SCALE * LOG2E)).astype(BF16)
    kb_ref[...] = qkv[:, WIDTH_B:2 * WIDTH_B].astype(BF16)
    vb_ref[...] = qkv[:, 2 * WIDTH_B:].astype(BF16)
    sa_ref[...] = jax.nn.sigmoid(_dot(h, wga_ref[...])).astype(BF16)
    sb_ref[...] = jax.nn.sigmoid(_dot(h, wgb_ref[...])).astype(BF16)


def _proj(x, g, wqa, wuk, wckv, kvg, wqi, wkk, wwi, wqkv, wga, wgb):
    n, d = x.shape
    tm = ROW_TILE
    row = lambda w: pl.BlockSpec((tm, w), lambda i: (i, 0))
    ws = (g, wqa, wuk, wckv, kvg, wqi, wkk, wwi, wqkv, wga, wgb)
    outs = [(N_HEADS_A * KV_RANK, BF16), (KV_RANK, BF16), None, (IDX_HEADS * IDX_DIM, BF16), (LANES, BF16),
            (LANES, F32), (WIDTH_B, BF16), (WIDTH_B, BF16), (WIDTH_B, BF16), (d, BF16), (d, BF16)]
    t = ATT_TILE
    ckvt_spec = pl.BlockSpec((tm // t, 2 * KV_RANK, t), lambda i: (i, 0, 0))
    ckvt_shape = jax.ShapeDtypeStruct((n // t, 2 * KV_RANK, t), BF16)
    return pl.pallas_call(
        _proj_kernel,
        grid=(n // tm,),
        in_specs=[row(d)] + [_full_spec(w.shape) for w in ws],
        out_specs=[ckvt_spec if o is None else row(o[0]) for o in outs],
        out_shape=[ckvt_shape if o is None else jax.ShapeDtypeStruct((n, o[0]), o[1]) for o in outs],
        compiler_params=pltpu.CompilerParams(dimension_semantics=("arbitrary",), vmem_limit_bytes=VMEM_LIMIT),
        name="proj",
    )(x, *ws)


def _t5_bucket(dist):
    dist = jnp.maximum(dist, 0)
    d_f = jnp.maximum(dist, 1).astype(F32)
    large = MAX_EXACT + jnp.floor(jnp.log(d_f / MAX_EXACT) / math.log(MAX_DISTANCE / MAX_EXACT)
                                  * (N_BUCKETS - MAX_EXACT)).astype(jnp.int32)
    large = jnp.minimum(large, N_BUCKETS - 1)
    return jnp.where(dist < MAX_EXACT, dist, large)


def _dsa_kernel(relb_ref, qabs_ref, qidx_ref, widx_ref, ckv_ref, ckvt_ref, kk_ref, o_ref,
                qa_s, qi_s, key_s, dig_s, act_s, cs_s, mn_s, mx_s, bias_s, m_s, acc_s, sa_s, sb_s, *, n_sel, seq):
    tq = tk = ATT_TILE
    nh = N_HEADS_A
    nhq = nh * tq
    qi = pl.program_id(1)
    n_kt = qi + 1
    kpos = lax.broadcasted_iota(jnp.int32, (tk, tq), 0)
    qpos = lax.broadcasted_iota(jnp.int32, (tk, tq), 1)

    @pl.when((pl.program_id(0) == 0) & (qi == 0))
    def _():
        for t in range(3):
            bucket = _t5_bucket(t * tq + qpos - kpos if t < 2 else jnp.full((tk, tq), 2 * tq, jnp.int32))
            for h in range(nh):
                b = jnp.zeros((tk, tq), F32)
                for bk in range(N_BUCKETS):
                    b = jnp.where(bucket == bk, relb_ref[bk, h], b)
                bias_s[t, :, h * tq:(h + 1) * tq] = b * LOG2E

    lane = lax.broadcasted_iota(jnp.int32, (tq, LANES), 1)
    lo_half = jnp.where(lane < IDX_DIM, 1.0, 0.0)
    hi_half = 1.0 - lo_half
    for h in range(nh):
        qa_s[h * tq:(h + 1) * tq, :] = qabs_ref[:, h * KV_RANK:(h + 1) * KV_RANK]
        pair = qidx_ref[:, (h // 2) * LANES:(h // 2 + 1) * LANES].astype(F32)
        qi_s[h * tq:(h + 1) * tq, :] = (pair * (lo_half if h % 2 == 0 else hi_half)).astype(BF16)
    w_t = jnp.transpose(widx_ref[...])
    w_row = jnp.concatenate([w_t[h:h + 1, :] for h in range(nh)], axis=1)

    mn_s[...] = jnp.full(mn_s.shape, -NEG_BIG, F32)
    mx_s[...] = jnp.full(mx_s.shape, NEG_BIG, F32)

    def idx_dots(kj, r_ref):
        ks = pl.multiple_of(jnp.minimum(kj, n_kt - 1) * tk, tk)
        r_ref[...] = _dot_nt(kk_ref[pl.ds(ks, tk), :], qi_s[...])

    def idx_scores(kj, r_ref):
        r = jnp.maximum(r_ref[...], 0.0) * w_row
        s = r[:, :tq]
        for h in range(1, nh):
            s = s + r[:, h * tq:(h + 1) * tq]
        masked = (kpos > qpos) & (kj == qi)
        key_s[kj] = jnp.where(masked, NEG_BIG, s)
        mn_s[...] = jnp.minimum(mn_s[...], jnp.min(jnp.where(masked, -NEG_BIG, s).reshape(tk // 8, 8, tq), axis=0))
        mx_s[...] = jnp.maximum(mx_s[...], jnp.max(jnp.where(masked, NEG_BIG, s).reshape(tk // 8, 8, tq), axis=0))

    idx_dots(0, sa_s)

    def idx_pair(i, carry):
        idx_dots(2 * i + 1, sb_s)
        idx_scores(2 * i, sa_s)
        idx_dots(2 * i + 2, sa_s)
        idx_scores(2 * i + 1, sb_s)
        return carry

    lax.fori_loop(0, n_kt // 2, idx_pair, 0)

    @pl.when(n_kt % 2 == 1)
    def _():
        idx_scores(n_kt - 1, sa_s)

    lo = jnp.min(mn_s[...], axis=0, keepdims=True)
    hi = jnp.max(mx_s[...], axis=0, keepdims=True)
    scale = KEY_TOP / jnp.maximum(hi - lo, KEY_MIN_RANGE)

    def key_body(kj, carry):
        y = jnp.maximum((key_s[kj] - lo) * scale, -1.0)
        rem = y
        for p in (3, 2, 1, 0):
            d = jnp.floor(rem)
            dig_s[p, kj] = d.astype(BF16)
            rem = (rem - d) * 256.0
        key_s[kj] = y - rem * (2.0 ** -32)
        return carry

    lax.fori_loop(0, n_kt, key_body, 0)

    def count_active(pred):
        one, zero = jnp.ones((), BF16), jnp.zeros((), BF16)

        def body(kj, acc):
            c = jnp.where(pred(act_s[kj]), one, zero).reshape(tk // 16, 16, tq)
            part = c[0]
            for j in range(1, tk // 16):
                part = part + c[j]
            return acc + part.astype(F32)
        acc = lax.fori_loop(0, n_kt, body, jnp.zeros((16, tq), F32))
        return jnp.sum(acc, axis=0, keepdims=True)

    def to_digit(v):
        return v.astype(F32).astype(BF16)

    target = jnp.full((1, tq), float(n_sel), F32)
    tau = jnp.zeros((1, tq), F32)
    digit = jnp.zeros((1, tq), jnp.int32)
    for p in (3, 2, 1, 0):
        if p == 3:
            def init_body(kj, carry):
                act_s[kj] = dig_s[3, kj]
                return carry
            lax.fori_loop(0, n_kt, init_body, 0)
        else:
            prev = to_digit(digit)
            target = target - count_active(lambda a: a > prev)

            def narrow_body(kj, carry, prev=prev, p=p):
                act_s[kj] = jnp.where(act_s[kj] == prev, dig_s[p, kj], jnp.full((), -1.0, BF16))
                return carry
            lax.fori_loop(0, n_kt, narrow_body, 0)

        def bit_body(i, dgt, target=target):
            cand = dgt | jnp.left_shift(jnp.int32(1), 7 - i)
            cand_d = to_digit(cand)
            cnt = count_active(lambda a: a >= cand_d)
            return jnp.where(cnt >= target, cand, dgt)
        digit = lax.fori_loop(0, 8, bit_body, jnp.zeros((1, tq), jnp.int32))
        tau = tau + digit.astype(F32) * (256.0 ** (p - 3))

    first = qi == 0
    tau = jnp.where(first, -0.5, tau)
    last = to_digit(digit)
    n_gt = count_active(lambda a: a > last)
    n_eq = count_active(lambda a: a == last)
    need = target - n_gt

    def count_keys(pred):
        def body(kj, acc):
            c = jnp.where(pred(key_s[kj], kpos + kj * tk), 1.0, 0.0)
            return acc + jnp.sum(c.reshape(tk // 8, 8, tq), axis=0)
        acc = lax.fori_loop(0, n_kt, body, jnp.zeros((8, tq), F32))
        return jnp.sum(acc, axis=0, keepdims=True)

    cs_s[...] = jnp.full(cs_s.shape, seq, jnp.int32)

    @pl.when(jnp.logical_not(first) & (jnp.max(n_eq - need) > 0.0))
    def _():
        nbits = seq.bit_length() - 1

        def tie_body(i, d):
            cand = d | jnp.left_shift(jnp.int32(1), nbits - 1 - i)
            cnt = count_keys(lambda k, pos: (k == tau) & (pos < cand))
            return jnp.where(cnt < need, cand, d)

        d = lax.fori_loop(0, nbits, tie_body, jnp.zeros((1, tq), jnp.int32))
        cs_s[...] = jnp.broadcast_to(d, cs_s.shape)

    cstar = jnp.where(first, -1, cs_s[0:1, :])

    m_s[...] = jnp.full(m_s.shape, NEG_BIG, F32)
    acc_s[...] = jnp.zeros(acc_s.shape, F32)

    def scores(kj, s_ref):
        kc = jnp.minimum(kj, n_kt - 1)
        ks = pl.multiple_of(kc * tk, tk)
        k = key_s[kc]
        sel = ((k > tau) | ((k == tau) & (kpos + kc * tk <= cstar))) & (kj < n_kt)
        selb = jnp.where(sel, 0.0, NEG_BIG)
        s = _dot_nt(ckv_ref[pl.ds(ks, tk), :], qa_s[...])
        s_ref[...] = s + bias_s[jnp.minimum(qi - kc, 2)] + jnp.concatenate([selb] * nh, axis=1)

    def update(kj, s_ref):
        s = s_ref[...]
        m_old = m_s[0:1, :]
        m_new = jnp.maximum(m_old, jnp.max(s, axis=0, keepdims=True))
        alpha = jnp.exp2(m_old - m_new)
        p = jnp.exp2(s - m_new).astype(BF16)
        acc_s[...] = acc_s[...] * alpha + _dot(ckvt_ref[jnp.minimum(kj, n_kt - 1)], p)
        m_s[...] = jnp.broadcast_to(m_new, m_s.shape)

    scores(0, sa_s)

    def pair_body(i, carry):
        scores(2 * i + 1, sb_s)
        update(2 * i, sa_s)
        scores(2 * i + 2, sa_s)
        update(2 * i + 1, sb_s)
        return carry

    lax.fori_loop(0, n_kt // 2, pair_body, 0)

    @pl.when(n_kt % 2 == 1)
    def _():
        update(n_kt - 1, sa_s)

    acc = acc_s[...]
    o_t = acc[:KV_RANK] / acc[KV_RANK:]
    for h in range(nh):
        o_ref[:, h * KV_RANK:(h + 1) * KV_RANK] = jnp.transpose(o_t[:, h * tq:(h + 1) * tq]).astype(BF16)


def _dsa(rel_bias, qabs, qidx, widx, ckv, ckvt, kk, batch, seq):
    n = qabs.shape[0]
    tq = ATT_TILE
    nq = seq // tq
    n_sel = min(TOPK_MAX, seq // 4)
    assert n_sel == tq and seq % tq == 0 and seq & (seq - 1) == 0
    qrow = lambda w: pl.BlockSpec((tq, w), lambda b, q: (b * nq + q, 0))
    brow = lambda w: pl.BlockSpec((seq, w), lambda b, q: (b, 0))
    nhq = N_HEADS_A * tq
    return pl.pallas_call(
        functools.partial(_dsa_kernel, n_sel=n_sel, seq=seq),
        grid=(batch, nq),
        in_specs=[pl.BlockSpec(memory_space=pltpu.SMEM),
                  qrow(N_HEADS_A * KV_RANK), qrow(IDX_HEADS * IDX_DIM), qrow(LANES),
                  brow(KV_RANK), pl.BlockSpec((nq, 2 * KV_RANK, tq), lambda b, q: (b, 0, 0)), brow(LANES)],
        out_specs=qrow(N_HEADS_A * KV_RANK),
        out_shape=jax.ShapeDtypeStruct((n, N_HEADS_A * KV_RANK), BF16),
        scratch_shapes=[pltpu.VMEM((nhq, KV_RANK), BF16), pltpu.VMEM((nhq, LANES), BF16),
                        pltpu.VMEM((nq, tq, tq), F32), pltpu.VMEM((4, nq, tq, tq), BF16),
                        pltpu.VMEM((nq, tq, tq), BF16), pltpu.VMEM((8, tq), jnp.int32),
                        pltpu.VMEM((8, tq), F32), pltpu.VMEM((8, tq), F32),
                        pltpu.VMEM((3, tq, nhq), F32),
                        pltpu.VMEM((8, nhq), F32), pltpu.VMEM((2 * KV_RANK, nhq), F32),
                        pltpu.VMEM((tq, nhq), F32), pltpu.VMEM((tq, nhq), F32)],
        compiler_params=pltpu.CompilerParams(dimension_semantics=("arbitrary", "arbitrary"),
                                             vmem_limit_bytes=VMEM_LIMIT),
        name="dsa",
    )(rel_bias, qabs, qidx, widx, ckv, ckvt, kk)


def _stick_kernel(q_ref, k_ref, v_ref, o_ref):
    tq = tk = ATT_TILE
    qi = pl.program_id(1)
    npair = WIDTH_B // LANES
    lane = lax.broadcasted_iota(jnp.int32, (tq, LANES), 1)
    row = lax.broadcasted_iota(jnp.int32, (tq, tk), 0)
    col = lax.broadcasted_iota(jnp.int32, (tq, tk), 1)
    causal = col < row
    upper = jnp.where(row > col, 1.0, 0.0).astype(BF16)
    upper_ones = jnp.concatenate([upper, jnp.ones((tk, LANES), BF16)], axis=1)
    lo_half = jnp.where(lane < HEAD_DIM, 1.0, 0.0)
    causal2 = jnp.concatenate([causal, causal], axis=0)
    q_pairs = []
    for p in range(npair):
        q2 = q_ref[:, p * LANES:(p + 1) * LANES].astype(F32)
        q_pairs.append(jnp.concatenate([q2 * lo_half, q2 * (1.0 - lo_half)], axis=0).astype(BF16))

    def tile(kj, r_sums, outs, masked):
        ks = pl.multiple_of(kj * tk, tk)
        log_sig, log_1m = [], []
        for p in range(npair):
            z = _dot_nt(q_pairs[p], k_ref[pl.ds(ks, tk), p * LANES:(p + 1) * LANES])
            sp = jnp.maximum(z, 0.0) + jnp.log2(1.0 + jnp.exp2(-jnp.abs(z)))
            lm = jnp.where(causal2, -sp, 0.0) if masked else -sp
            log_sig.append(z - sp)
            log_1m.append(lm.astype(BF16))
        sums = _dot(jnp.concatenate(log_1m, axis=0), upper_ones)
        new_r, new_o = [], []
        for p in range(npair):
            sm = sums[p * 2 * tq:(p + 1) * 2 * tq]
            after = sm[:, :tk] + jnp.concatenate([r_sums[p]] * (tk // LANES), axis=1)
            a = jnp.exp2(log_sig[p] + after)
            if masked:
                a = jnp.where(causal2, a, 0.0)
            res = _dot(a.astype(BF16), v_ref[pl.ds(ks, tk), p * LANES:(p + 1) * LANES])
            new_o.append(outs[p] + jnp.where(lane < HEAD_DIM, res[:tq], res[tq:]))
            new_r.append(r_sums[p] + sm[:, tk:])
        return tuple(new_r), tuple(new_o)

    zeros = jnp.zeros((tq, LANES), F32)
    r_sums, outs = tile(qi, (jnp.zeros((2 * tq, LANES), F32),) * npair, (zeros,) * npair, True)

    def cond(c):
        it, r_sums, _ = c
        return (it < qi) & (jnp.max(functools.reduce(jnp.maximum, r_sums)) > EXP_UNDERFLOW)

    def body(c):
        it, r_sums, outs = c
        r_sums, outs = tile(qi - 1 - it, r_sums, outs, False)
        return it + 1, r_sums, outs

    _, _, outs = lax.while_loop(cond, body, (jnp.int32(0), r_sums, outs))
    for p in range(npair):
        o_ref[:, p * LANES:(p + 1) * LANES] = outs[p].astype(BF16)


def _stick(qb, kb, vb, batch, seq):
    n = qb.shape[0]
    tq = ATT_TILE
    nq = seq // tq
    qspec = pl.BlockSpec((tq, WIDTH_B), lambda b, q: (b * nq + q, 0))
    kspec = pl.BlockSpec((seq, WIDTH_B), lambda b, q: (b, 0))
    return pl.pallas_call(
        _stick_kernel,
        grid=(batch, nq),
        in_specs=[qspec, kspec, kspec],
        out_specs=qspec,
        out_shape=jax.ShapeDtypeStruct((n, WIDTH_B), BF16),
        compiler_params=pltpu.CompilerParams(dimension_semantics=("arbitrary",) * 2, vmem_limit_bytes=VMEM_LIMIT),
        name="stick",
    )(qb, kb, vb)


def _merge_kernel(x_ref, ol_ref, ob_ref, sa_ref, sb_ref, wuv_ref, wba_ref, wbb_ref, wout_ref, g_ref,
                  wr_ref, br_ref, x1_ref, h2_ref, comb_ref):
    half = N_HEADS_A * KV_RANK // 2
    oa = jnp.concatenate([_dot(ol_ref[:, :half], wuv_ref[:half, :WIDTH_A // 2]),
                          _dot(ol_ref[:, half:], wuv_ref[half:, WIDTH_A // 2:])], axis=1).astype(BF16)
    ya = _dot(oa, wba_ref[...])
    yb = _dot(ob_ref[...], wbb_ref[...])
    merged = sa_ref[...].astype(F32) * ya + sb_ref[...].astype(F32) * yb
    x1 = x_ref[...] + _dot(merged.astype(BF16), wout_ref[...])
    x1_ref[...] = x1
    h2 = _rms(x1, g_ref[...])
    h2_hi = h2.astype(BF16)
    h2_ref[...] = h2_hi
    h2_lo = (h2 - h2_hi.astype(F32)).astype(BF16)
    hi_part = _dot(h2_hi, wr_ref[...])
    logits = (hi_part[:, :LANES] + _dot(h2_lo, wr_ref[:, :LANES]) + hi_part[:, LANES:]) + br_ref[...]
    lane = lax.broadcasted_iota(jnp.int32, logits.shape, 1).astype(F32)
    ninf = -jnp.inf

    def first_max(v):
        m = jnp.max(v, axis=1, keepdims=True)
        return m, jnp.min(jnp.where(v == m, lane, 1e9), axis=1, keepdims=True)

    gmask = (lane >= N_EXPERTS) & (lane < N_EXPERTS + N_GROUPS)
    gl = jnp.where(gmask, logits, ninf)
    gmax, gidx = first_max(gl)
    p_g = 1.0 / jnp.sum(jnp.where(gmask, jnp.exp(gl - gmax), 0.0), axis=1, keepdims=True)
    e_lo = (gidx - N_EXPERTS) * EXPERTS_PER_GROUP
    el = jnp.where((lane >= e_lo) & (lane < e_lo + EXPERTS_PER_GROUP), logits, ninf)
    v1, i1 = first_max(el)
    el2 = jnp.where(lane == i1, ninf, el)
    v2, i2 = first_max(el2)
    e2 = jnp.exp(v2 - v1)
    w1 = 1.0 / (1.0 + e2)
    comb_ref[...] = jnp.where(lane == i1, w1 * p_g,
                              jnp.where(lane == i2, (e2 * w1) * p_g, jnp.where(lane == gidx, 1.0, 0.0)))


def _merge(x, olat, ob, sa, sb, wuv, wba, wbb, wout, g, wr, br):
    n, d = x.shape
    tm = ROW_TILE
    row = lambda w: pl.BlockSpec((tm, w), lambda i: (i, 0))
    ws = (wuv, wba, wbb, wout, g, wr, br)
    return pl.pallas_call(
        _merge_kernel,
        grid=(n // tm,),
        in_specs=[row(d), row(olat.shape[1]), row(ob.shape[1]), row(d), row(d)] + [_full_spec(w.shape) for w in ws],
        out_specs=[row(d), row(d), row(LANES)],
        out_shape=[jax.ShapeDtypeStruct((n, d), F32), jax.ShapeDtypeStruct((n, d), BF16),
                   jax.ShapeDtypeStruct((n, LANES), F32)],
        compiler_params=pltpu.CompilerParams(dimension_semantics=("arbitrary",), vmem_limit_bytes=VMEM_LIMIT),
        name="merge",
    )(x, olat, ob, sa, sb, *ws)


def _moe_kernel(h_ref, comb_ref, wgu_ref, wd_ref, o_ref, slotc_s, slotr_s, split_s, acc_s, rng_s):
    t, d = h_ref.shape
    nsub = MOE_SUBTILES
    ts = t // nsub
    c = MOE_CHUNK
    rb = MOE_RANK_BLOCK
    g = pl.program_id(1)

    @pl.when(g == 0)
    def _():
        lane_b = lax.broadcasted_iota(jnp.int32, (rb, LANES), 1)
        group_lanes = (lane_b >= N_EXPERTS) & (lane_b < N_EXPERTS + N_GROUPS)
        r_i = lax.broadcasted_iota(jnp.int32, (rb, rb), 0)
        c_i = lax.broadcasted_iota(jnp.int32, (rb, rb), 1)
        lower = jnp.where(c_i < r_i, 1.0, 0.0).astype(BF16)
        lane1 = lax.broadcasted_iota(jnp.int32, (1, LANES), 1)
        for sub in range(nsub):
            counts = jnp.zeros((1, LANES), F32)
            ranks = []
            for blk in range(ts // rb):
                rows = slice(sub * ts + blk * rb, sub * ts + (blk + 1) * rb)
                oh = jnp.where(group_lanes, comb_ref[rows, :], 0.0)
                before = _dot(lower, oh.astype(BF16)) + counts
                ranks.append(jnp.sum(before * oh, axis=1, keepdims=True))
                counts = counts + jnp.sum(oh, axis=0, keepdims=True)
            start = jnp.int32(0)
            seg = jnp.zeros((1, LANES), F32)
            for k in range(N_GROUPS):
                n_k = jnp.sum(jnp.where(lane1 == N_EXPERTS + k, counts, 0.0)).astype(jnp.int32)
                seg = jnp.where(lane1 == N_EXPERTS + k, (start * c).astype(F32), seg)
                rng_s[sub * 2 * N_GROUPS + k] = start
                for j in range(-(-ts // c)):
                    start = start + (n_k > j * c).astype(jnp.int32)
                rng_s[sub * 2 * N_GROUPS + N_GROUPS + k] = start
            for blk in range(ts // rb):
                rows = slice(sub * ts + blk * rb, sub * ts + (blk + 1) * rb)
                oh = jnp.where(group_lanes, comb_ref[rows, :], 0.0)
                slot = ranks[blk] + jnp.sum(oh * seg, axis=1, keepdims=True)
                slotc_s[rows, :] = jnp.broadcast_to(slot, (rb, LANES))
        slotr_s[...] = jnp.transpose(slotc_s[...])
        cb = comb_ref[...]
        hi = cb.astype(BF16)
        split_s[...] = jnp.concatenate([hi, (cb - hi.astype(F32)).astype(BF16)], axis=1)
        acc_s[...] = jnp.zeros(acc_s.shape, F32)

    lane = lax.broadcasted_iota(jnp.int32, (nsub * c, LANES), 1)
    first = [rng_s[sub * 2 * N_GROUPS + g] for sub in range(nsub)]
    n_chunks = [rng_s[sub * 2 * N_GROUPS + N_GROUPS + g] - first[sub] for sub in range(nsub)]

    def chunk_body(j, carry):
        bases = [((first[sub] + j) * c).astype(F32) for sub in range(nsub)]
        xs, cw = [], []
        for sub in range(nsub):
            rows = slice(sub * ts, (sub + 1) * ts)
            row_slot = lax.broadcasted_iota(jnp.int32, (c, ts), 0).astype(F32) + bases[sub]
            pc = jnp.where(slotr_s[0:1, rows] == row_slot, 1.0, 0.0).astype(BF16)
            xs.append(_dot(pc, h_ref[rows, :]).astype(BF16))
            cw2 = _dot(pc, split_s[rows, :])
            cw.append(cw2[:, :LANES] + cw2[:, LANES:])
        xs = jnp.concatenate(xs, axis=0)
        cw = jnp.concatenate(cw, axis=0)
        y = jnp.zeros((nsub * c, d), F32)
        for e in range(EXPERTS_PER_GROUP):
            gu = _dot(xs, wgu_ref[e])
            gate = gu[:, :D_EXPERT]
            hid = gate * jax.nn.sigmoid(gate) * gu[:, D_EXPERT:]
            ce = jnp.sum(jnp.where(lane == g * EXPERTS_PER_GROUP + e, cw, 0.0), axis=1, keepdims=True)
            y = y + _dot((hid * ce).astype(BF16), wd_ref[e])
        y = y.astype(BF16)
        for sub in range(nsub):
            rows = slice(sub * ts, (sub + 1) * ts)
            col_slot = lax.broadcasted_iota(jnp.int32, (ts, c), 1).astype(F32) + bases[sub]
            pct = jnp.where(jnp.broadcast_to(slotc_s[rows, 0:1], (ts, c)) == col_slot, 1.0, 0.0).astype(BF16)
            acc_s[rows, :] += _dot(pct, y[sub * c:(sub + 1) * c])
        return carry

    lax.fori_loop(0, functools.reduce(jnp.maximum, n_chunks), chunk_body, 0)

    @pl.when(g == N_GROUPS - 1)
    def _():
        o_ref[...] = acc_s[...].astype(BF16)


def _moe(h2, comb, wgu, wd):
    n, d = h2.shape
    tm = MOE_ROW_TILE
    row = lambda w: pl.BlockSpec((tm, w), lambda i, g: (i, 0))
    return pl.pallas_call(
        _moe_kernel,
        grid=(n // tm, N_GROUPS),
        in_specs=[row(d), row(LANES),
                  pl.BlockSpec((EXPERTS_PER_GROUP, d, 2 * D_EXPERT), lambda i, g: (g, 0, 0)),
                  pl.BlockSpec((EXPERTS_PER_GROUP, D_EXPERT, d), lambda i, g: (g, 0, 0))],
        out_specs=row(d),
        out_shape=jax.ShapeDtypeStruct((n, d), BF16),
        scratch_shapes=[pltpu.VMEM((tm, LANES), F32), pltpu.VMEM((LANES, tm), F32),
                        pltpu.VMEM((tm, 2 * LANES), BF16), pltpu.VMEM((tm, d), F32),
                        pltpu.SMEM((MOE_SUBTILES * 2 * N_GROUPS,), jnp.int32)],
        compiler_params=pltpu.CompilerParams(dimension_semantics=("arbitrary", "arbitrary"),
                                             vmem_limit_bytes=VMEM_LIMIT),
        name="moe",
    )(h2, comb, wgu, wd)


def _ple_kernel(x_ref, m_ref, p_ref, g_ref, wpg_ref, wple_ref, gf_ref, o_ref, *, last):
    x2 = x_ref[...] + m_ref[...].astype(F32)
    h3 = _rms(x2, g_ref[...]).astype(BF16)
    gate = jax.nn.sigmoid(_dot(h3, wpg_ref[...]))
    x3 = x2 + _dot(p_ref[...].astype(BF16), wple_ref[...]) * gate
    o_ref[...] = _rms(x3, gf_ref[...]) if last else x3


def _ple(x1, moe, p, g, wpg, wple, gf, last):
    n, d = x1.shape
    tm = ROW_TILE
    row = lambda w: pl.BlockSpec((tm, w), lambda i: (i, 0))
    ws = (g, wpg, wple, gf)
    return pl.pallas_call(
        functools.partial(_ple_kernel, last=last),
        grid=(n // tm,),
        in_specs=[row(d), row(d), row(p.shape[1])] + [_full_spec(w.shape) for w in ws],
        out_specs=row(d),
        out_shape=jax.ShapeDtypeStruct((n, d), F32),
        compiler_params=pltpu.CompilerParams(dimension_semantics=("arbitrary",), vmem_limit_bytes=VMEM_LIMIT),
        name="ple",
    )(x1, moe, p, *ws)


def _block_diag(blocks):
    h, r, c = blocks.shape
    eye = jnp.eye(h, dtype=blocks.dtype)
    return (eye[:, None, :, None] * blocks[:, :, None, :]).reshape(h * r, h * c)


def kernel(x, p, attn_norm, w_in, kv_norm, w_uk, w_uv, rel_bias, w_branch_a, w_branch_b, w_out, ffn_norm,
           w_r1, b_r1, w_r2, b_r2, w_gate, w_up, w_down, ple_norm, w_ple_gate, w_ple, final_norm):
    batch, seq, d = x.shape
    n = batch * seq
    depth = w_in.shape[0]
    xf = x.reshape(n, d).astype(F32)
    widths = [WIDTH_A, KV_RANK, IDX_HEADS * IDX_DIM, IDX_DIM, IDX_HEADS, 3 * WIDTH_B, d, d]
    starts = [sum(widths[:k]) for k in range(len(widths))]
    for i in range(depth):
        w_qa, w_ckv, w_qi, w_ki, w_wi, w_qkv, w_ga, w_gb = [
            w_in[i][:, s:s + w].astype(BF16) for s, w in zip(starts, widths)]
        w_kk = jnp.concatenate([w_ki, w_ki], axis=1)
        w_wi = jnp.pad(w_wi, ((0, 0), (0, LANES - IDX_HEADS)))
        wuk_bd = _block_diag(jnp.swapaxes(w_uk[i], 1, 2)).astype(BF16)
        wuv_bd = _block_diag(w_uv[i]).astype(BF16)
        qabs, ckv, ckvt, qidx, kk, widx, qb, kb, vb, sa, sb = _proj(
            xf, attn_norm[i][None].astype(F32), w_qa, wuk_bd, w_ckv, kv_norm[i][None].astype(F32),
            w_qi, w_kk, w_wi, w_qkv, w_ga, w_gb)
        olat = _dsa(rel_bias.astype(F32), qabs, qidx, widx, ckv, ckvt, kk, batch, seq)
        ob = _stick(qb, kb, vb, batch, seq)
        w_r = jnp.concatenate([jnp.transpose(w_r2[i], (1, 0, 2)).reshape(d, N_EXPERTS), w_r1[i]], axis=1)
        w_r = jnp.pad(w_r.astype(F32), ((0, 0), (0, LANES - N_EXPERTS - N_GROUPS)))
        w_r_hi = w_r.astype(BF16)
        w_r_lo = (w_r - w_r_hi.astype(F32)).astype(BF16)
        b_r = jnp.pad(jnp.concatenate([b_r2[i].reshape(-1), b_r1[i]]).astype(F32),
                      (0, LANES - N_EXPERTS - N_GROUPS))[None]
        x1, h2, comb = _merge(xf, olat, ob, sa, sb, wuv_bd, w_branch_a[i].astype(BF16), w_branch_b[i].astype(BF16),
                              w_out[i].astype(BF16), ffn_norm[i][None].astype(F32),
                              jnp.concatenate([w_r_hi, w_r_lo], axis=1), b_r)
        wgu = jnp.concatenate([w_gate[i], w_up[i]], axis=2).astype(BF16)
        moe = _moe(h2, comb, wgu, w_down[i].astype(BF16))
        xf = _ple(x1, moe, p[i].reshape(n, -1).astype(F32), ple_norm[i][None].astype(F32), w_ple_gate[i].astype(BF16),
                  w_ple[i].astype(BF16), final_norm[None].astype(F32), last=(i == depth - 1))
    return xf.reshape(batch, seq, d).astype(x.dtype)
```

```python
import functools
import math

import jax
import jax.numpy as jnp
from jax import lax
from jax.experimental import pallas as pl
from jax.experimental.pallas import tpu as pltpu

D_MODEL = 1024
N_HEADS_A = 8
HEAD_DIM = 64
WIDTH_A = N_HEADS_A * HEAD_DIM
KV_RANK = 128
IDX_HEADS = 8
IDX_DIM = 64
TOPK_MAX = 256
N_HEADS_B = 8
WIDTH_B = N_HEADS_B * HEAD_DIM
N_BUCKETS = 32
MAX_EXACT = N_BUCKETS // 2
MAX_DISTANCE = 128
ATTN_SCALE = HEAD_DIM ** -0.5
IDX_SCALE = (IDX_HEADS ** -0.5) * (IDX_DIM ** -0.5)
N_GROUPS = 4
EXPERTS_PER_GROUP = 8
N_EXPERTS = N_GROUPS * EXPERTS_PER_GROUP
D_EXPERT = 256
PLE_DIM = 256
EPS = 1e-6
LOG2E = 1.4426950408889634

LANES = 128
ROW_TILE = 512
MOE_ROW_TILE = 1024
MOE_SUBTILES = 2
MOE_CHUNK = 160
MOE_RANK_BLOCK = 256
ATT_TILE = 256
VMEM_LIMIT = 56 * 1024 * 1024
NEG_BIG = -1e30
EXP_UNDERFLOW = -173.0
KEY_TOP = 127.0
KEY_MIN_RANGE = 1e-30

F32 = jnp.float32
BF16 = jnp.bfloat16


def _rms(x, g):
    return x * lax.rsqrt(jnp.mean(x * x, axis=-1, keepdims=True) + EPS) * g


def _dot(a, b):
    return jnp.dot(a, b, preferred_element_type=F32)


def _dot_nt(a, b):
    return lax.dot_general(a, b, (((1,), (1,)), ((), ())), preferred_element_type=F32)


def _full_spec(shape):
    nd = len(shape)
    return pl.BlockSpec(shape, lambda *_: (0,) * nd)


def _proj_kernel(x_ref, g_ref, wqa_ref, wuk_ref, wcw_ref, kvg_ref, wqi_ref, wkk_ref,
                 wqkv_ref, wga_ref, wgb_ref,
                 qabs_ref, ckv_ref, ckvt_ref, qidx_ref, kk_ref, widx_ref, qb_ref, kb_ref, vb_ref, sa_ref, sb_ref):
    h = _rms(x_ref[...], g_ref[...]).astype(BF16)
    qa = _dot(h, wqa_ref[...]).astype(BF16)
    half = WIDTH_A // 2
    qabs = jnp.concatenate([_dot(qa[:, :half], wuk_ref[:half, :N_HEADS_A * KV_RANK // 2]),
                            _dot(qa[:, half:], wuk_ref[half:, N_HEADS_A * KV_RANK // 2:])], axis=1)
    qabs_ref[...] = (qabs * (ATTN_SCALE * LOG2E)).astype(BF16)
    cw = _dot(h, wcw_ref[...])
    c = _rms(cw[:, :KV_RANK], kvg_ref[...])
    ckv_ref[...] = c.astype(BF16)
    c_ext = jnp.concatenate([c, jnp.ones_like(c)], axis=1)
    for j in range(ckvt_ref.shape[0]):
        ckvt_ref[j] = jnp.transpose(c_ext[j * ATT_TILE:(j + 1) * ATT_TILE, :]).astype(BF16)
    qidx_ref[...] = _dot(h, wqi_ref[...]).astype(BF16)
    kk_ref[...] = _dot(h, wkk_ref[...]).astype(BF16)
    widx_ref[...] = cw[:, KV_RANK:] * IDX_SCALE
    qkv = _dot(h, wqkv_ref[...])
    qb_ref[...] = (qkv[:, :WIDTH_B] * (ATTN_SCALE * LOG2E)).astype(BF16)
    kb_ref[...] = qkv[:, WIDTH_B:2 * WIDTH_B].astype(BF16)
    vb_ref[...] = qkv[:, 2 * WIDTH_B:].astype(BF16)
    sa_ref[...] = jax.nn.sigmoid(_dot(h, wga_ref[...])).astype(BF16)
    sb_ref[...] = jax.nn.sigmoid(_dot(h, wgb_ref[...])).astype(BF16)


def _proj(x, g, wqa, wuk, wcw, kvg, wqi, wkk, wqkv, wga, wgb):
    n, d = x.shape
    tm = ROW_TILE
    row = lambda w: pl.BlockSpec((tm, w), lambda i: (i, 0))
    ws = (g, wqa, wuk, wcw, kvg, wqi, wkk, wqkv, wga, wgb)
    outs = [(N_HEADS_A * KV_RANK, BF16), (KV_RANK, BF16), None, (IDX_HEADS * IDX_DIM, BF16), (LANES, BF16),
            (LANES, F32), (WIDTH_B, BF16), (WIDTH_B, BF16), (WIDTH_B, BF16), (d, BF16), (d, BF16)]
    t = ATT_TILE
    ckvt_spec = pl.BlockSpec((tm // t, 2 * KV_RANK, t), lambda i: (i, 0, 0))
    ckvt_shape = jax.ShapeDtypeStruct((n // t, 2 * KV_RANK, t), BF16)
    return pl.pallas_call(
        _proj_kernel,
        grid=(n // tm,),
        in_specs=[row(d)] + [_full_spec(w.shape) for w in ws],
        out_specs=[ckvt_spec if o is None else row(o[0]) for o in outs],
        out_shape=[ckvt_shape if o is None else jax.ShapeDtypeStruct((n, o[0]), o[1]) for o in outs],
        compiler_params=pltpu.CompilerParams(dimension_semantics=("arbitrary",), vmem_limit_bytes=VMEM_LIMIT),
        name="proj",
    )(x, *ws)


def _t5_bucket(dist):
    dist = jnp.maximum(dist, 0)
    d_f = jnp.maximum(dist, 1).astype(F32)
    large = MAX_EXACT + jnp.floor(jnp.log(d_f / MAX_EXACT) / math.log(MAX_DISTANCE / MAX_EXACT)
                                  * (N_BUCKETS - MAX_EXACT)).astype(jnp.int32)
    large = jnp.minimum(large, N_BUCKETS - 1)
    return jnp.where(dist < MAX_EXACT, dist, large)


def _dsa_kernel(relb_ref, qabs_ref, qidx_ref, widx_ref, ckv_ref, ckvt_ref, kk_ref, o_ref,
                qa_s, qi_s, key_s, dig_s, act_s, cs_s, mn_s, mx_s, bias_s, m_s, acc_s, sa_s, sb_s, *, n_sel, seq):
    tq = tk = ATT_TILE
    nh = N_HEADS_A
    nhq = nh * tq
    qi = pl.program_id(1)
    n_kt = qi + 1
    kpos = lax.broadcasted_iota(jnp.int32, (tk, tq), 0)
    qpos = lax.broadcasted_iota(jnp.int32, (tk, tq), 1)

    @pl.when((pl.program_id(0) == 0) & (qi == 0))
    def _():
        for t in range(3):
            bucket = _t5_bucket(t * tq + qpos - kpos if t < 2 else jnp.full((tk, tq), 2 * tq, jnp.int32))
            for h in range(nh):
                b = jnp.zeros((tk, tq), F32)
                for bk in range(N_BUCKETS):
                    b = jnp.where(bucket == bk, relb_ref[bk, h], b)
                bias_s[t, :, h * tq:(h + 1) * tq] = b * LOG2E

    lane = lax.broadcasted_iota(jnp.int32, (tq, LANES), 1)
    lo_half = jnp.where(lane < IDX_DIM, 1.0, 0.0)
    hi_half = 1.0 - lo_half
    for h in range(nh):
        qa_s[h * tq:(h + 1) * tq, :] = qabs_ref[:, h * KV_RANK:(h + 1) * KV_RANK]
        pair = qidx_ref[:, (h // 2) * LANES:(h // 2 + 1) * LANES].astype(F32)
        qi_s[h * tq:(h + 1) * tq, :] = (pair * (lo_half if h % 2 == 0 else hi_half)).astype(BF16)
    w_t = jnp.transpose(widx_ref[...])
    w_row = jnp.concatenate([w_t[h:h + 1, :] for h in range(nh)], axis=1)

    mn_s[...] = jnp.full(mn_s.shape, -NEG_BIG, F32)
    mx_s[...] = jnp.full(mx_s.shape, NEG_BIG, F32)

    def idx_dots(kj, r_ref):
        ks = pl.multiple_of(jnp.minimum(kj, n_kt - 1) * tk, tk)
        r_ref[...] = _dot_nt(kk_ref[pl.ds(ks, tk), :], qi_s[...])

    def idx_scores(kj, r_ref):
        r = jnp.maximum(r_ref[...], 0.0) * w_row
        s = r[:, :tq]
        for h in range(1, nh):
            s = s + r[:, h * tq:(h + 1) * tq]
        masked = (kpos > qpos) & (kj == qi)
        key_s[kj] = jnp.where(masked, NEG_BIG, s)
        mn_s[...] = jnp.minimum(mn_s[...], jnp.min(jnp.where(masked, -NEG_BIG, s).reshape(tk // 8, 8, tq), axis=0))
        mx_s[...] = jnp.maximum(mx_s[...], jnp.max(jnp.where(masked, NEG_BIG, s).reshape(tk // 8, 8, tq), axis=0))

    idx_dots(0, sa_s)

    def idx_pair(i, carry):
        idx_dots(2 * i + 1, sb_s)
        idx_scores(2 * i, sa_s)
        idx_dots(2 * i + 2, sa_s)
        idx_scores(2 * i + 1, sb_s)
        return carry

    lax.fori_loop(0, n_kt // 2, idx_pair, 0)

    @pl.when(n_kt % 2 == 1)
    def _():
        idx_scores(n_kt - 1, sa_s)

    lo = jnp.min(mn_s[...], axis=0, keepdims=True)
    hi = jnp.max(mx_s[...], axis=0, keepdims=True)
    scale = KEY_TOP / jnp.maximum(hi - lo, KEY_MIN_RANGE)

    def key_body(kj, carry):
        y = jnp.maximum((key_s[kj] - lo) * scale, -1.0)
        rem = y
        for p in (3, 2, 1, 0):
            d = jnp.floor(rem)
            dig_s[p, kj] = d.astype(BF16)
            rem = (rem - d) * 256.0
        key_s[kj] = y - rem * (2.0 ** -32)
        return carry

    lax.fori_loop(0, n_kt, key_body, 0)

    def count_active(pred, then=None):
        one, zero = jnp.ones((), BF16), jnp.zeros((), BF16)

        def body(kj, acc):
            a = act_s[kj]
            c = jnp.where(pred(a), one, zero).reshape(tk // 16, 16, tq)
            part = c[0]
            for j in range(1, tk // 16):
                part = part + c[j]
            if then is not None:
                then(kj, a)
            return acc + part.astype(F32)
        acc = lax.fori_loop(0, n_kt, body, jnp.zeros((16, tq), F32))
        return jnp.sum(acc, axis=0, keepdims=True)

    def to_digit(v):
        return v.astype(F32).astype(BF16)

    target = jnp.full((1, tq), float(n_sel), F32)
    tau = jnp.zeros((1, tq), F32)
    digit = jnp.zeros((1, tq), jnp.int32)
    for p in (3, 2, 1, 0):
        if p == 3:
            def init_body(kj, carry):
                act_s[kj] = dig_s[3, kj]
                return carry
            lax.fori_loop(0, n_kt, init_body, 0)
        else:
            prev = to_digit(digit)

            def narrow(kj, a, prev=prev, p=p):
                act_s[kj] = jnp.where(a == prev, dig_s[p, kj], jnp.full((), -1.0, BF16))
            target = target - count_active(lambda a: a > prev, then=narrow)

        n_bits = 7 if p == 3 else 8

        def bit_body(i, dgt, target=target, n_bits=n_bits):
            cand = dgt | jnp.left_shift(jnp.int32(1), n_bits - 1 - i)
            cand_d = to_digit(cand)
            cnt = count_active(lambda a: a >= cand_d)
            return jnp.where(cnt >= target, cand, dgt)
        digit = lax.fori_loop(0, n_bits, bit_body, jnp.zeros((1, tq), jnp.int32))
        tau = tau + digit.astype(F32) * (256.0 ** (p - 3))

    first = qi == 0
    tau = jnp.where(first, -0.5, tau)
    last = to_digit(digit)
    n_gt = count_active(lambda a: a > last)
    n_eq = count_active(lambda a: a == last)
    need = target - n_gt

    def count_keys(pred):
        def body(kj, acc):
            c = jnp.where(pred(key_s[kj], kpos + kj * tk), 1.0, 0.0)
            return acc + jnp.sum(c.reshape(tk // 8, 8, tq), axis=0)
        acc = lax.fori_loop(0, n_kt, body, jnp.zeros((8, tq), F32))
        return jnp.sum(acc, axis=0, keepdims=True)

    cs_s[...] = jnp.full(cs_s.shape, seq, jnp.int32)

    @pl.when(jnp.logical_not(first) & (jnp.max(n_eq - need) > 0.0))
    def _():
        nbits = seq.bit_length() - 1

        def tie_body(i, d):
            cand = d | jnp.left_shift(jnp.int32(1), nbits - 1 - i)
            cnt = count_keys(lambda k, pos: (k == tau) & (pos < cand))
            return jnp.where(cnt < need, cand, d)

        d = lax.fori_loop(0, nbits, tie_body, jnp.zeros((1, tq), jnp.int32))
        cs_s[...] = jnp.broadcast_to(d, cs_s.shape)

    cstar = jnp.where(first, -1, cs_s[0:1, :])

    m_s[...] = jnp.full(m_s.shape, NEG_BIG, F32)
    acc_s[...] = jnp.zeros(acc_s.shape, F32)

    def scores(kj, s_ref):
        kc = jnp.minimum(kj, n_kt - 1)
        ks = pl.multiple_of(kc * tk, tk)
        k = key_s[kc]
        sel = ((k > tau) | ((k == tau) & (kpos + kc * tk <= cstar))) & (kj < n_kt)
        selb = jnp.where(sel, 0.0, NEG_BIG)
        s = _dot_nt(ckv_ref[pl.ds(ks, tk), :], qa_s[...])
        s_ref[...] = s + bias_s[jnp.minimum(qi - kc, 2)] + jnp.concatenate([selb] * nh, axis=1)

    def update(kj, s_ref):
        s = s_ref[...]
        m_old = m_s[0:1, :]
        m_new = jnp.maximum(m_old, jnp.max(s, axis=0, keepdims=True))
        alpha = jnp.exp2(m_old - m_new)
        p = jnp.exp2(s - m_new).astype(BF16)
        acc_s[...] = acc_s[...] * alpha + _dot(ckvt_ref[jnp.minimum(kj, n_kt - 1)], p)
        m_s[...] = jnp.broadcast_to(m_new, m_s.shape)

    scores(0, sa_s)

    def pair_body(i, carry):
        scores(2 * i + 1, sb_s)
        update(2 * i, sa_s)
        scores(2 * i + 2, sa_s)
        update(2 * i + 1, sb_s)
        return carry

    lax.fori_loop(0, n_kt // 2, pair_body, 0)

    @pl.when(n_kt % 2 == 1)
    def _():
        update(n_kt - 1, sa_s)

    acc = acc_s[...]
    o_t = acc[:KV_RANK] / acc[KV_RANK:]
    for h in range(nh):
        o_ref[:, h * KV_RANK:(h + 1) * KV_RANK] = jnp.transpose(o_t[:, h * tq:(h + 1) * tq]).astype(BF16)


def _dsa(rel_bias, qabs, qidx, widx, ckv, ckvt, kk, batch, seq):
    n = qabs.shape[0]
    tq = ATT_TILE
    nq = seq // tq
    n_sel = min(TOPK_MAX, seq // 4)
    assert n_sel == tq and seq % tq == 0 and seq & (seq - 1) == 0
    qrow = lambda w: pl.BlockSpec((tq, w), lambda b, q: (b * nq + q, 0))
    brow = lambda w: pl.BlockSpec((seq, w), lambda b, q: (b, 0))
    nhq = N_HEADS_A * tq
    return pl.pallas_call(
        functools.partial(_dsa_kernel, n_sel=n_sel, seq=seq),
        grid=(batch, nq),
        in_specs=[pl.BlockSpec(memory_space=pltpu.SMEM),
                  qrow(N_HEADS_A * KV_RANK), qrow(IDX_HEADS * IDX_DIM), qrow(LANES),
                  brow(KV_RANK), pl.BlockSpec((nq, 2 * KV_RANK, tq), lambda b, q: (b, 0, 0)), brow(LANES)],
        out_specs=qrow(N_HEADS_A * KV_RANK),
        out_shape=jax.ShapeDtypeStruct((n, N_HEADS_A * KV_RANK), BF16),
        scratch_shapes=[pltpu.VMEM((nhq, KV_RANK), BF16), pltpu.VMEM((nhq, LANES), BF16),
                        pltpu.VMEM((nq, tq, tq), F32), pltpu.VMEM((4, nq, tq, tq), BF16),
                        pltpu.VMEM((nq, tq, tq), BF16), pltpu.VMEM((8, tq), jnp.int32),
                        pltpu.VMEM((8, tq), F32), pltpu.VMEM((8, tq), F32),
                        pltpu.VMEM((3, tq, nhq), F32),
                        pltpu.VMEM((8, nhq), F32), pltpu.VMEM((2 * KV_RANK, nhq), F32),
                        pltpu.VMEM((tq, nhq), F32), pltpu.VMEM((tq, nhq), F32)],
        compiler_params=pltpu.CompilerParams(dimension_semantics=("arbitrary", "arbitrary"),
                                             vmem_limit_bytes=VMEM_LIMIT),
        name="dsa",
    )(rel_bias, qabs, qidx, widx, ckv, ckvt, kk)


def _stick_kernel(q_ref, k_ref, v_ref, o_ref):
    tq = tk = ATT_TILE
    qi = pl.program_id(1)
    npair = WIDTH_B // LANES
    lane = lax.broadcasted_iota(jnp.int32, (tq, LANES), 1)
    row = lax.broadcasted_iota(jnp.int32, (tq, tk), 0)
    col = lax.broadcasted_iota(jnp.int32, (tq, tk), 1)
    causal = col < row
    upper = jnp.where(row > col, 1.0, 0.0).astype(BF16)
    upper_ones = jnp.concatenate([upper, jnp.ones((tk, LANES), BF16)], axis=1)
    lo_half = jnp.where(lane < HEAD_DIM, 1.0, 0.0)
    causal2 = jnp.concatenate([causal, causal], axis=0)
    q_pairs = []
    for p in range(npair):
        q2 = q_ref[:, p * LANES:(p + 1) * LANES].astype(F32)
        q_pairs.append(jnp.concatenate([q2 * lo_half, q2 * (1.0 - lo_half)], axis=0).astype(BF16))

    def tile(kj, r_sums, outs, masked):
        ks = pl.multiple_of(kj * tk, tk)
        log_sig, log_1m = [], []
        for p in range(npair):
            z = _dot_nt(q_pairs[p], k_ref[pl.ds(ks, tk), p * LANES:(p + 1) * LANES])
            sp = jnp.maximum(z, 0.0) + jnp.log2(1.0 + jnp.exp2(-jnp.abs(z)))
            lm = jnp.where(causal2, -sp, 0.0) if masked else -sp
            log_sig.append(z - sp)
            log_1m.append(lm.astype(BF16))
        sums = _dot(jnp.concatenate(log_1m, axis=0), upper_ones)
        new_r, new_o = [], []
        for p in range(npair):
            sm = sums[p * 2 * tq:(p + 1) * 2 * tq]
            after = sm[:, :tk] + jnp.concatenate([r_sums[p]] * (tk // LANES), axis=1)
            a = jnp.exp2(log_sig[p] + after)
            if masked:
                a = jnp.where(causal2, a, 0.0)
            res = _dot(a.astype(BF16), v_ref[pl.ds(ks, tk), p * LANES:(p + 1) * LANES])
            new_o.append(outs[p] + jnp.where(lane < HEAD_DIM, res[:tq], res[tq:]))
            new_r.append(r_sums[p] + sm[:, tk:])
        return tuple(new_r), tuple(new_o)

    zeros = jnp.zeros((tq, LANES), F32)
    r_sums, outs = tile(qi, (jnp.zeros((2 * tq, LANES), F32),) * npair, (zeros,) * npair, True)

    def cond(c):
        it, r_sums, _ = c
        return (it < qi) & (jnp.max(functools.reduce(jnp.maximum, r_sums)) > EXP_UNDERFLOW)

    def body(c):
        it, r_sums, outs = c
        r_sums, outs = tile(qi - 1 - it, r_sums, outs, False)
        return it + 1, r_sums, outs

    _, _, outs = lax.while_loop(cond, body, (jnp.int32(0), r_sums, outs))
    for p in range(npair):
        o_ref[:, p * LANES:(p + 1) * LANES] = outs[p].astype(BF16)


def _stick(qb, kb, vb, batch, seq):
    n = qb.shape[0]
    tq = ATT_TILE
    nq = seq // tq
    qspec = pl.BlockSpec((tq, WIDTH_B), lambda b, q: (b * nq + q, 0))
    kspec = pl.BlockSpec((seq, WIDTH_B), lambda b, q: (b, 0))
    return pl.pallas_call(
        _stick_kernel,
        grid=(batch, nq),
        in_specs=[qspec, kspec, kspec],
        out_specs=qspec,
        out_shape=jax.ShapeDtypeStruct((n, WIDTH_B), BF16),
        compiler_params=pltpu.CompilerParams(dimension_semantics=("arbitrary",) * 2, vmem_limit_bytes=VMEM_LIMIT),
        name="stick",
    )(qb, kb, vb)


def _merge_kernel(x_ref, ol_ref, ob_ref, sa_ref, sb_ref, wuv_ref, wba_ref, wbb_ref, wout_ref, g_ref,
                  wr_ref, br_ref, x1_ref, h2_ref, comb_ref):
    half = N_HEADS_A * KV_RANK // 2
    oa = jnp.concatenate([_dot(ol_ref[:, :half], wuv_ref[:half, :WIDTH_A // 2]),
                          _dot(ol_ref[:, half:], wuv_ref[half:, WIDTH_A // 2:])], axis=1).astype(BF16)
    ya = _dot(oa, wba_ref[...])
    yb = _dot(ob_ref[...], wbb_ref[...])
    merged = sa_ref[...].astype(F32) * ya + sb_ref[...].astype(F32) * yb
    x1 = x_ref[...] + _dot(merged.astype(BF16), wout_ref[...])
    x1_ref[...] = x1
    h2 = _rms(x1, g_ref[...])
    h2_hi = h2.astype(BF16)
    h2_ref[...] = h2_hi
    h2_lo = (h2 - h2_hi.astype(F32)).astype(BF16)
    hi_part = _dot(h2_hi, wr_ref[...])
    logits = (hi_part[:, :LANES] + _dot(h2_lo, wr_ref[:, :LANES]) + hi_part[:, LANES:]) + br_ref[...]
    lane = lax.broadcasted_iota(jnp.int32, logits.shape, 1).astype(F32)
    ninf = -jnp.inf

    def first_max(v):
        m = jnp.max(v, axis=1, keepdims=True)
        return m, jnp.min(jnp.where(v == m, lane, 1e9), axis=1, keepdims=True)

    gmask = (lane >= N_EXPERTS) & (lane < N_EXPERTS + N_GROUPS)
    gl = jnp.where(gmask, logits, ninf)
    gmax, gidx = first_max(gl)
    p_g = 1.0 / jnp.sum(jnp.where(gmask, jnp.exp(gl - gmax), 0.0), axis=1, keepdims=True)
    e_lo = (gidx - N_EXPERTS) * EXPERTS_PER_GROUP
    el = jnp.where((lane >= e_lo) & (lane < e_lo + EXPERTS_PER_GROUP), logits, ninf)
    v1, i1 = first_max(el)
    el2 = jnp.where(lane == i1, ninf, el)
    v2, i2 = first_max(el2)
    e2 = jnp.exp(v2 - v1)
    w1 = 1.0 / (1.0 + e2)
    comb_ref[...] = jnp.where(lane == i1, w1 * p_g,
                              jnp.where(lane == i2, (e2 * w1) * p_g, jnp.where(lane == gidx, 1.0, 0.0)))


def _merge(x, olat, ob, sa, sb, wuv, wba, wbb, wout, g, wr, br):
    n, d = x.shape
    tm = ROW_TILE
    row = lambda w: pl.BlockSpec((tm, w), lambda i: (i, 0))
    ws = (wuv, wba, wbb, wout, g, wr, br)
    return pl.pallas_call(
        _merge_kernel,
        grid=(n // tm,),
        in_specs=[row(d), row(olat.shape[1]), row(ob.shape[1]), row(d), row(d)] + [_full_spec(w.shape) for w in ws],
        out_specs=[row(d), row(d), row(LANES)],
        out_shape=[jax.ShapeDtypeStruct((n, d), F32), jax.ShapeDtypeStruct((n, d), BF16),
                   jax.ShapeDtypeStruct((n, LANES), F32)],
        compiler_params=pltpu.CompilerParams(dimension_semantics=("arbitrary",), vmem_limit_bytes=VMEM_LIMIT),
        name="merge",
    )(x, olat, ob, sa, sb, *ws)


def _moe_kernel(h_ref, comb_ref, wgu_ref, wd_ref, o_ref, slotc_s, slotr_s, split_s, acc_s, rng_s):
    t, d = h_ref.shape
    nsub = MOE_SUBTILES
    ts = t // nsub
    c = MOE_CHUNK
    rb = MOE_RANK_BLOCK
    g = pl.program_id(1)

    @pl.when(g == 0)
    def _():
        lane_b = lax.broadcasted_iota(jnp.int32, (rb, LANES), 1)
        group_lanes = (lane_b >= N_EXPERTS) & (lane_b < N_EXPERTS + N_GROUPS)
        r_i = lax.broadcasted_iota(jnp.int32, (rb, rb), 0)
        c_i = lax.broadcasted_iota(jnp.int32, (rb, rb), 1)
        lower = jnp.where(c_i < r_i, 1.0, 0.0).astype(BF16)
        lane1 = lax.broadcasted_iota(jnp.int32, (1, LANES), 1)
        for sub in range(nsub):
            counts = jnp.zeros((1, LANES), F32)
            ranks = []
            for blk in range(ts // rb):
                rows = slice(sub * ts + blk * rb, sub * ts + (blk + 1) * rb)
                oh = jnp.where(group_lanes, comb_ref[rows, :], 0.0)
                before = _dot(lower, oh.astype(BF16)) + counts
                ranks.append(jnp.sum(before * oh, axis=1, keepdims=True))
                counts = counts + jnp.sum(oh, axis=0, keepdims=True)
            start = jnp.int32(0)
            seg = jnp.zeros((1, LANES), F32)
            for k in range(N_GROUPS):
                n_k = jnp.sum(jnp.where(lane1 == N_EXPERTS + k, counts, 0.0)).astype(jnp.int32)
                seg = jnp.where(lane1 == N_EXPERTS + k, (start * c).astype(F32), seg)
                rng_s[sub * 2 * N_GROUPS + k] = start
                for j in range(-(-ts // c)):
                    start = start + (n_k > j * c).astype(jnp.int32)
                rng_s[sub * 2 * N_GROUPS + N_GROUPS + k] = start
            for blk in range(ts // rb):
                rows = slice(sub * ts + blk * rb, sub * ts + (blk + 1) * rb)
                oh = jnp.where(group_lanes, comb_ref[rows, :], 0.0)
                slot = ranks[blk] + jnp.sum(oh * seg, axis=1, keepdims=True)
                slotc_s[rows, :] = jnp.broadcast_to(slot, (rb, LANES))
        slotr_s[...] = jnp.transpose(slotc_s[...])
        cb = comb_ref[...]
        hi = cb.astype(BF16)
        split_s[...] = jnp.concatenate([hi, (cb - hi.astype(F32)).astype(BF16)], axis=1)
        acc_s[...] = jnp.zeros(acc_s.shape, F32)

    lane = lax.broadcasted_iota(jnp.int32, (nsub * c, LANES), 1)
    first = [rng_s[sub * 2 * N_GROUPS + g] for sub in range(nsub)]
    n_chunks = [rng_s[sub * 2 * N_GROUPS + N_GROUPS + g] - first[sub] for sub in range(nsub)]

    def chunk_body(j, carry):
        bases = [((first[sub] + j) * c).astype(F32) for sub in range(nsub)]
        xs, cw = [], []
        for sub in range(nsub):
            rows = slice(sub * ts, (sub + 1) * ts)
            row_slot = lax.broadcasted_iota(jnp.int32, (c, ts), 0).astype(F32) + bases[sub]
            pc = jnp.where(slotr_s[0:1, rows] == row_slot, 1.0, 0.0).astype(BF16)
            xs.append(_dot(pc, h_ref[rows, :]).astype(BF16))
            cw2 = _dot(pc, split_s[rows, :])
            cw.append(cw2[:, :LANES] + cw2[:, LANES:])
        xs = jnp.concatenate(xs, axis=0)
        cw = jnp.concatenate(cw, axis=0)
        y = jnp.zeros((nsub * c, d), F32)
        for e in range(EXPERTS_PER_GROUP):
            gu = _dot(xs, wgu_ref[e])
            gate = gu[:, :D_EXPERT]
            hid = gate * jax.nn.sigmoid(gate) * gu[:, D_EXPERT:]
            ce = jnp.sum(jnp.where(lane == g * EXPERTS_PER_GROUP + e, cw, 0.0), axis=1, keepdims=True)
            y = y + _dot((hid * ce).astype(BF16), wd_ref[e])
        y = y.astype(BF16)
        for sub in range(nsub):
            rows = slice(sub * ts, (sub + 1) * ts)
            col_slot = lax.broadcasted_iota(jnp.int32, (ts, c), 1).astype(F32) + bases[sub]
            pct = jnp.where(jnp.broadcast_to(slotc_s[rows, 0:1], (ts, c)) == col_slot, 1.0, 0.0).astype(BF16)
            acc_s[rows, :] += _dot(pct, y[sub * c:(sub + 1) * c])
        return carry

    lax.fori_loop(0, functools.reduce(jnp.maximum, n_chunks), chunk_body, 0)

    @pl.when(g == N_GROUPS - 1)
    def _():
        o_ref[...] = acc_s[...].astype(BF16)


def _moe(h2, comb, wgu, wd):
    n, d = h2.shape
    tm = MOE_ROW_TILE
    row = lambda w: pl.BlockSpec((tm, w), lambda i, g: (i, 0))
    return pl.pallas_call(
        _moe_kernel,
        grid=(n // tm, N_GROUPS),
        in_specs=[row(d), row(LANES),
                  pl.BlockSpec((EXPERTS_PER_GROUP, d, 2 * D_EXPERT), lambda i, g: (g, 0, 0)),
                  pl.BlockSpec((EXPERTS_PER_GROUP, D_EXPERT, d), lambda i, g: (g, 0, 0))],
        out_specs=row(d),
        out_shape=jax.ShapeDtypeStruct((n, d), BF16),
        scratch_shapes=[pltpu.VMEM((tm, LANES), F32), pltpu.VMEM((LANES, tm), F32),
                        pltpu.VMEM((tm, 2 * LANES), BF16), pltpu.VMEM((tm, d), F32),
                        pltpu.SMEM((MOE_SUBTILES * 2 * N_GROUPS,), jnp.int32)],
        compiler_params=pltpu.CompilerParams(dimension_semantics=("arbitrary", "arbitrary"),
                                             vmem_limit_bytes=VMEM_LIMIT),
        name="moe",
    )(h2, comb, wgu, wd)


def _ple_kernel(x_ref, m_ref, p_ref, g_ref, wpg_ref, wple_ref, gf_ref, o_ref, *, last):
    x2 = x_ref[...] + m_ref[...].astype(F32)
    h3 = _rms(x2, g_ref[...]).astype(BF16)
    gate = jax.nn.sigmoid(_dot(h3, wpg_ref[...]))
    x3 = x2 + _dot(p_ref[...].astype(BF16), wple_ref[...]) * gate
    o_ref[...] = _rms(x3, gf_ref[...]) if last else x3


def _ple(x1, moe, p, g, wpg, wple, gf, last):
    n, d = x1.shape
    tm = ROW_TILE
    row = lambda w: pl.BlockSpec((tm, w), lambda i: (i, 0))
    ws = (g, wpg, wple, gf)
    return pl.pallas_call(
        functools.partial(_ple_kernel, last=last),
        grid=(n // tm,),
        in_specs=[row(d), row(d), row(p.shape[1])] + [_full_spec(w.shape) for w in ws],
        out_specs=row(d),
        out_shape=jax.ShapeDtypeStruct((n, d), F32),
        compiler_params=pltpu.CompilerParams(dimension_semantics=("arbitrary",), vmem_limit_bytes=VMEM_LIMIT),
        name="ple",
    )(x1, moe, p, *ws)


def _block_diag(blocks):
    h, r, c = blocks.shape
    eye = jnp.eye(h, dtype=blocks.dtype)
    return (eye[:, None, :, None] * blocks[:, :, None, :]).reshape(h * r, h * c)


def kernel(x, p, attn_norm, w_in, kv_norm, w_uk, w_uv, rel_bias, w_branch_a, w_branch_b, w_out, ffn_norm,
           w_r1, b_r1, w_r2, b_r2, w_gate, w_up, w_down, ple_norm, w_ple_gate, w_ple, final_norm):
    batch, seq, d = x.shape
    n = batch * seq
    depth = w_in.shape[0]
    xf = x.reshape(n, d).astype(F32)
    widths = [WIDTH_A, KV_RANK, IDX_HEADS * IDX_DIM, IDX_DIM, IDX_HEADS, 3 * WIDTH_B, d, d]
    starts = [sum(widths[:k]) for k in range(len(widths))]
    for i in range(depth):
        w_qa, w_ckv, w_qi, w_ki, w_wi, w_qkv, w_ga, w_gb = [
            w_in[i][:, s:s + w].astype(BF16) for s, w in zip(starts, widths)]
        w_kk = jnp.concatenate([w_ki, w_ki], axis=1)
        w_cw = jnp.concatenate([w_ckv, jnp.pad(w_wi, ((0, 0), (0, LANES - IDX_HEADS)))], axis=1)
        wuk_bd = _block_diag(jnp.swapaxes(w_uk[i], 1, 2)).astype(BF16)
        wuv_bd = _block_diag(w_uv[i]).astype(BF16)
        qabs, ckv, ckvt, qidx, kk, widx, qb, kb, vb, sa, sb = _proj(
            xf, attn_norm[i][None].astype(F32), w_qa, wuk_bd, w_cw, kv_norm[i][None].astype(F32),
            w_qi, w_kk, w_qkv, w_ga, w_gb)
        olat = _dsa(rel_bias.astype(F32), qabs, qidx, widx, ckv, ckvt, kk, batch, seq)
        ob = _stick(qb, kb, vb, batch, seq)
        w_r = jnp.concatenate([jnp.transpose(w_r2[i], (1, 0, 2)).reshape(d, N_EXPERTS), w_r1[i]], axis=1)
        w_r = jnp.pad(w_r.astype(F32), ((0, 0), (0, LANES - N_EXPERTS - N_GROUPS)))
        w_r_hi = w_r.astype(BF16)
        w_r_lo = (w_r - w_r_hi.astype(F32)).astype(BF16)
        b_r = jnp.pad(jnp.concatenate([b_r2[i].reshape(-1), b_r1[i]]).astype(F32),
                      (0, LANES - N_EXPERTS - N_GROUPS))[None]
        x1, h2, comb = _merge(xf, olat, ob, sa, sb, wuv_bd, w_branch_a[i].astype(BF16), w_branch_b[i].astype(BF16),
                              w_out[i].astype(BF16), ffn_norm[i][None].astype(F32),
                              jnp.concatenate([w_r_hi, w_r_lo], axis=1), b_r)
        wgu = jnp.concatenate([w_gate[i], w_up[i]], axis=2).astype(BF16)
        moe = _moe(h2, comb, wgu, w_down[i].astype(BF16))
        xf = _ple(x1, moe, p[i].reshape(n, -1).astype(F32), ple_norm[i][None].astype(F32), w_ple_gate[i].astype(BF16),
                  w_ple[i].astype(BF16), final_norm[None].astype(F32), last=(i == depth - 1))
    return xf.reshape(batch, seq, d).astype(x.dtype)
```

```python
import functools
import math

import jax
import jax.numpy as jnp
from jax import lax
from jax.experimental import pallas as pl
from jax.experimental.pallas import tpu as pltpu

D_MODEL = 1024
N_HEADS_A = 8
HEAD_DIM = 64
WIDTH_A = N_HEADS_A * HEAD_DIM
KV_RANK = 128
IDX_HEADS = 8
IDX_DIM = 64
TOPK_MAX = 256
N_HEADS_B = 8
WIDTH_B = N_HEADS_B * HEAD_DIM
N_BUCKETS = 32
MAX_EXACT = N_BUCKETS // 2
MAX_DISTANCE = 128
ATTN_SCALE = HEAD_DIM ** -0.5
IDX_SCALE = (IDX_HEADS ** -0.5) * (IDX_DIM ** -0.5)
N_GROUPS = 4
EXPERTS_PER_GROUP = 8
N_EXPERTS = N_GROUPS * EXPERTS_PER_GROUP
D_EXPERT = 256
PLE_DIM = 256
EPS = 1e-6
LOG2E = 1.4426950408889634

LANES = 128
ROW_TILE = 512
MOE_ROW_TILE = 1024
MOE_SUBTILES = 2
MOE_CHUNK = 160
MOE_RANK_BLOCK = 256
ATT_TILE = 256
VMEM_LIMIT = 56 * 1024 * 1024
NEG_BIG = -1e30
EXP_UNDERFLOW = -173.0
KEY_TOP = 127.0
KEY_MIN_RANGE = 1e-30

F32 = jnp.float32
BF16 = jnp.bfloat16


def _rms(x, g):
    return x * lax.rsqrt(jnp.mean(x * x, axis=-1, keepdims=True) + EPS) * g


def _dot(a, b):
    return jnp.dot(a, b, preferred_element_type=F32)


def _dot_nt(a, b):
    return lax.dot_general(a, b, (((1,), (1,)), ((), ())), preferred_element_type=F32)


def _full_spec(shape):
    nd = len(shape)
    return pl.BlockSpec(shape, lambda *_: (0,) * nd)


def _proj_kernel(x_ref, g_ref, wqa_ref, wuk_ref, wcw_ref, kvg_ref, wqi_ref, wkk_ref,
                 wqkv_ref, wga_ref, wgb_ref,
                 qabs_ref, ckv_ref, ckvt_ref, qidx_ref, kk_ref, widx_ref, qb_ref, kb_ref, vb_ref, sa_ref, sb_ref):
    h = _rms(x_ref[...], g_ref[...]).astype(BF16)
    qa = _dot(h, wqa_ref[...]).astype(BF16)
    half = WIDTH_A // 2
    qabs = jnp.concatenate([_dot(qa[:, :half], wuk_ref[:half, :N_HEADS_A * KV_RANK // 2]),
                            _dot(qa[:, half:], wuk_ref[half:, N_HEADS_A * KV_RANK // 2:])], axis=1)
    qabs_ref[...] = (qabs * (ATTN_SCALE * LOG2E)).astype(BF16)
    cw = _dot(h, wcw_ref[...])
    c = _rms(cw[:, :KV_RANK], kvg_ref[...])
    ckv_ref[...] = c.astype(BF16)
    c_ext = jnp.concatenate([c, jnp.ones_like(c)], axis=1)
    for j in range(ckvt_ref.shape[0]):
        ckvt_ref[j] = jnp.transpose(c_ext[j * ATT_TILE:(j + 1) * ATT_TILE, :]).astype(BF16)
    qidx_ref[...] = _dot(h, wqi_ref[...]).astype(BF16)
    kk_ref[...] = _dot(h, wkk_ref[...]).astype(BF16)
    widx_ref[...] = cw[:, KV_RANK:] * IDX_SCALE
    qkv = _dot(h, wqkv_ref[...])
    qb_ref[...] = (qkv[:, :WIDTH_B] * (ATTN_SCALE * LOG2E)).astype(BF16)
    kb_ref[...] = qkv[:, WIDTH_B:2 * WIDTH_B].astype(BF16)
    vb_ref[...] = qkv[:, 2 * WIDTH_B:].astype(BF16)
    sa_ref[...] = jax.nn.sigmoid(_dot(h, wga_ref[...])).astype(BF16)
    sb_ref[...] = jax.nn.sigmoid(_dot(h, wgb_ref[...])).astype(BF16)


def _proj(x, g, wqa, wuk, wcw, kvg, wqi, wkk, wqkv, wga, wgb):
    n, d = x.shape
    tm = ROW_TILE
    row = lambda w: pl.BlockSpec((tm, w), lambda i: (i, 0))
    ws = (g, wqa, wuk, wcw, kvg, wqi, wkk, wqkv, wga, wgb)
    outs = [(N_HEADS_A * KV_RANK, BF16), (KV_RANK, BF16), None, (IDX_HEADS * IDX_DIM, BF16), (LANES, BF16),
            (LANES, F32), (WIDTH_B, BF16), (WIDTH_B, BF16), (WIDTH_B, BF16), (d, BF16), (d, BF16)]
    t = ATT_TILE
    ckvt_spec = pl.BlockSpec((tm // t, 2 * KV_RANK, t), lambda i: (i, 0, 0))
    ckvt_shape = jax.ShapeDtypeStruct((n // t, 2 * KV_RANK, t), BF16)
    return pl.pallas_call(
        _proj_kernel,
        grid=(n // tm,),
        in_specs=[row(d)] + [_full_spec(w.shape) for w in ws],
        out_specs=[ckvt_spec if o is None else row(o[0]) for o in outs],
        out_shape=[ckvt_shape if o is None else jax.ShapeDtypeStruct((n, o[0]), o[1]) for o in outs],
        compiler_params=pltpu.CompilerParams(dimension_semantics=("arbitrary",), vmem_limit_bytes=VMEM_LIMIT),
        name="proj",
    )(x, *ws)


def _t5_bucket(dist):
    dist = jnp.maximum(dist, 0)
    d_f = jnp.maximum(dist, 1).astype(F32)
    large = MAX_EXACT + jnp.floor(jnp.log(d_f / MAX_EXACT) / math.log(MAX_DISTANCE / MAX_EXACT)
                                  * (N_BUCKETS - MAX_EXACT)).astype(jnp.int32)
    large = jnp.minimum(large, N_BUCKETS - 1)
    return jnp.where(dist < MAX_EXACT, dist, large)


def _dsa_kernel(relb_ref, qabs_ref, qidx_ref, widx_ref, ckv_ref, ckvt_ref, kk_ref, o_ref,
                qa_s, qi_s, key_s, dig_s, act_s, cs_s, mn_s, mx_s, bias_s, m_s, acc_s, sa_s, sb_s, *, n_sel, seq):
    tq = tk = ATT_TILE
    nh = N_HEADS_A
    nhq = nh * tq
    qi = pl.program_id(1)
    n_kt = qi + 1
    kpos = lax.broadcasted_iota(jnp.int32, (tk, tq), 0)
    qpos = lax.broadcasted_iota(jnp.int32, (tk, tq), 1)

    @pl.when((pl.program_id(0) == 0) & (qi == 0))
    def _():
        for t in range(3):
            bucket = _t5_bucket(t * tq + qpos - kpos if t < 2 else jnp.full((tk, tq), 2 * tq, jnp.int32))
            for h in range(nh):
                b = jnp.zeros((tk, tq), F32)
                for bk in range(N_BUCKETS):
                    b = jnp.where(bucket == bk, relb_ref[bk, h], b)
                bias_s[t, :, h * tq:(h + 1) * tq] = b * LOG2E

    lane = lax.broadcasted_iota(jnp.int32, (tq, LANES), 1)
    lo_half = jnp.where(lane < IDX_DIM, 1.0, 0.0)
    hi_half = 1.0 - lo_half
    for h in range(nh):
        qa_s[h * tq:(h + 1) * tq, :] = qabs_ref[:, h * KV_RANK:(h + 1) * KV_RANK]
        pair = qidx_ref[:, (h // 2) * LANES:(h // 2 + 1) * LANES].astype(F32)
        qi_s[h * tq:(h + 1) * tq, :] = (pair * (lo_half if h % 2 == 0 else hi_half)).astype(BF16)
    w_t = jnp.transpose(widx_ref[...])
    w_row = jnp.concatenate([w_t[h:h + 1, :] for h in range(nh)], axis=1)

    mn_s[...] = jnp.full(mn_s.shape, -NEG_BIG, F32)
    mx_s[...] = jnp.full(mx_s.shape, NEG_BIG, F32)

    def idx_dots(kj, r_ref):
        ks = pl.multiple_of(jnp.minimum(kj, n_kt - 1) * tk, tk)
        r_ref[...] = _dot_nt(kk_ref[pl.ds(ks, tk), :], qi_s[...])

    def idx_scores(kj, r_ref):
        r = jnp.maximum(r_ref[...], 0.0) * w_row
        s = r[:, :tq]
        for h in range(1, nh):
            s = s + r[:, h * tq:(h + 1) * tq]
        masked = (kpos > qpos) & (kj == qi)
        key_s[kj] = jnp.where(masked, NEG_BIG, s)
        mn_s[...] = jnp.minimum(mn_s[...], jnp.min(jnp.where(masked, -NEG_BIG, s).reshape(tk // 8, 8, tq), axis=0))
        mx_s[...] = jnp.maximum(mx_s[...], jnp.max(jnp.where(masked, NEG_BIG, s).reshape(tk // 8, 8, tq), axis=0))

    idx_dots(0, sa_s)

    def idx_pair(i, carry):
        idx_dots(2 * i + 1, sb_s)
        idx_scores(2 * i, sa_s)
        idx_dots(2 * i + 2, sa_s)
        idx_scores(2 * i + 1, sb_s)
        return carry

    lax.fori_loop(0, n_kt // 2, idx_pair, 0)

    @pl.when(n_kt % 2 == 1)
    def _():
        idx_scores(n_kt - 1, sa_s)

    lo = jnp.min(mn_s[...], axis=0, keepdims=True)
    hi = jnp.max(mx_s[...], axis=0, keepdims=True)
    scale = KEY_TOP / jnp.maximum(hi - lo, KEY_MIN_RANGE)

    def key_body(kj, carry):
        y = jnp.maximum((key_s[kj] - lo) * scale, -1.0)
        rem = y
        for p in (3, 2, 1, 0):
            d = jnp.floor(rem)
            dig_s[p, kj] = d.astype(BF16)
            rem = (rem - d) * 256.0
        key_s[kj] = y - rem * (2.0 ** -32)
        return carry

    lax.fori_loop(0, n_kt, key_body, 0)

    def count_active(pred, then=None):
        one, zero = jnp.ones((), BF16), jnp.zeros((), BF16)

        def body(kj, acc):
            a = act_s[kj]
            c = jnp.where(pred(a), one, zero).reshape(tk // 16, 16, tq)
            part = c[0]
            for j in range(1, tk // 16):
                part = part + c[j]
            if then is not None:
                then(kj, a)
            return acc + part.astype(F32)
        acc = lax.fori_loop(0, n_kt, body, jnp.zeros((16, tq), F32))
        return jnp.sum(acc, axis=0, keepdims=True)

    def to_digit(v):
        return v.astype(F32).astype(BF16)

    def digit_phase(p, prev_digit, target):
        if p == 3:
            def init_body(kj, carry):
                act_s[kj] = dig_s[3, kj]
                return carry
            lax.fori_loop(0, n_kt, init_body, 0)
        else:
            prev = to_digit(prev_digit)

            def narrow(kj, a):
                act_s[kj] = jnp.where(a == prev, dig_s[p, kj], jnp.full((), -1.0, BF16))
            target = target - count_active(lambda a: a > prev, then=narrow)
        n_bits = 7 if p == 3 else 8

        def bit_body(i, dgt):
            cand = dgt | jnp.left_shift(jnp.int32(1), n_bits - 1 - i)
            cand_d = to_digit(cand)
            cnt = count_active(lambda a: a >= cand_d)
            return jnp.where(cnt >= target, cand, dgt)
        return lax.fori_loop(0, n_bits, bit_body, jnp.zeros((1, tq), jnp.int32)), target

    target = jnp.full((1, tq), float(n_sel), F32)
    tau = jnp.zeros((1, tq), F32)
    digit = None
    for p in (3, 2, 1):
        digit, target = digit_phase(p, digit, target)
        tau = tau + digit.astype(F32) * (256.0 ** (p - 3))
    digit_d = to_digit(digit)
    excess = count_active(lambda a: a >= digit_d) - target

    def count_keys(pred):
        def body(kj, acc):
            c = jnp.where(pred(key_s[kj], kpos + kj * tk), 1.0, 0.0)
            return acc + jnp.sum(c.reshape(tk // 8, 8, tq), axis=0)
        acc = lax.fori_loop(0, n_kt, body, jnp.zeros((8, tq), F32))
        return jnp.sum(acc, axis=0, keepdims=True)

    first = qi == 0
    cs_s[...] = jnp.where(lax.broadcasted_iota(jnp.int32, cs_s.shape, 0) == 0, seq, 0)

    @pl.when(jnp.logical_not(first) & (jnp.max(excess) > 0.0))
    def _():
        digit0, target0 = digit_phase(0, digit, target)
        cs_s[1:2, :] = digit0
        tau0 = tau + digit0.astype(F32) * (256.0 ** -3)
        last = to_digit(digit0)
        n_gt = count_active(lambda a: a > last)
        n_eq = count_active(lambda a: a == last)
        need = target0 - n_gt

        @pl.when(jnp.max(n_eq - need) > 0.0)
        def _():
            nbits = seq.bit_length() - 1

            def tie_body(i, d):
                cand = d | jnp.left_shift(jnp.int32(1), nbits - 1 - i)
                cnt = count_keys(lambda k, pos: (k == tau0) & (pos < cand))
                return jnp.where(cnt < need, cand, d)

            cs_s[0:1, :] = lax.fori_loop(0, nbits, tie_body, jnp.zeros((1, tq), jnp.int32))

    tau = jnp.where(first, -0.5, tau + cs_s[1:2, :].astype(F32) * (256.0 ** -3))
    cstar = jnp.where(first, -1, cs_s[0:1, :])

    m_s[...] = jnp.full(m_s.shape, NEG_BIG, F32)
    acc_s[...] = jnp.zeros(acc_s.shape, F32)

    def scores(kj, s_ref):
        kc = jnp.minimum(kj, n_kt - 1)
        ks = pl.multiple_of(kc * tk, tk)
        k = key_s[kc]
        sel = ((k > tau) | ((k == tau) & (kpos + kc * tk <= cstar))) & (kj < n_kt)
        selb = jnp.where(sel, 0.0, NEG_BIG)
        s = _dot_nt(ckv_ref[pl.ds(ks, tk), :], qa_s[...])
        s_ref[...] = s + bias_s[jnp.minimum(qi - kc, 2)] + jnp.concatenate([selb] * nh, axis=1)

    def update(kj, s_ref):
        s = s_ref[...]
        m_old = m_s[0:1, :]
        m_new = jnp.maximum(m_old, jnp.max(s, axis=0, keepdims=True))
        alpha = jnp.exp2(m_old - m_new)
        p = jnp.exp2(s - m_new).astype(BF16)
        acc_s[...] = acc_s[...] * alpha + _dot(ckvt_ref[jnp.minimum(kj, n_kt - 1)], p)
        m_s[...] = jnp.broadcast_to(m_new, m_s.shape)

    scores(0, sa_s)

    def pair_body(i, carry):
        scores(2 * i + 1, sb_s)
        update(2 * i, sa_s)
        scores(2 * i + 2, sa_s)
        update(2 * i + 1, sb_s)
        return carry

    lax.fori_loop(0, n_kt // 2, pair_body, 0)

    @pl.when(n_kt % 2 == 1)
    def _():
        update(n_kt - 1, sa_s)

    acc = acc_s[...]
    o_t = acc[:KV_RANK] / acc[KV_RANK:]
    for h in range(nh):
        o_ref[:, h * KV_RANK:(h + 1) * KV_RANK] = jnp.transpose(o_t[:, h * tq:(h + 1) * tq]).astype(BF16)


def _dsa(rel_bias, qabs, qidx, widx, ckv, ckvt, kk, batch, seq):
    n = qabs.shape[0]
    tq = ATT_TILE
    nq = seq // tq
    n_sel = min(TOPK_MAX, seq // 4)
    assert n_sel == tq and seq % tq == 0 and seq & (seq - 1) == 0
    qrow = lambda w: pl.BlockSpec((tq, w), lambda b, q: (b * nq + q, 0))
    brow = lambda w: pl.BlockSpec((seq, w), lambda b, q: (b, 0))
    nhq = N_HEADS_A * tq
    return pl.pallas_call(
        functools.partial(_dsa_kernel, n_sel=n_sel, seq=seq),
        grid=(batch, nq),
        in_specs=[pl.BlockSpec(memory_space=pltpu.SMEM),
                  qrow(N_HEADS_A * KV_RANK), qrow(IDX_HEADS * IDX_DIM), qrow(LANES),
                  brow(KV_RANK), pl.BlockSpec((nq, 2 * KV_RANK, tq), lambda b, q: (b, 0, 0)), brow(LANES)],
        out_specs=qrow(N_HEADS_A * KV_RANK),
        out_shape=jax.ShapeDtypeStruct((n, N_HEADS_A * KV_RANK), BF16),
        scratch_shapes=[pltpu.VMEM((nhq, KV_RANK), BF16), pltpu.VMEM((nhq, LANES), BF16),
                        pltpu.VMEM((nq, tq, tq), F32), pltpu.VMEM((4, nq, tq, tq), BF16),
                        pltpu.VMEM((nq, tq, tq), BF16), pltpu.VMEM((8, tq), jnp.int32),
                        pltpu.VMEM((8, tq), F32), pltpu.VMEM((8, tq), F32),
                        pltpu.VMEM((3, tq, nhq), F32),
                        pltpu.VMEM((8, nhq), F32), pltpu.VMEM((2 * KV_RANK, nhq), F32),
                        pltpu.VMEM((tq, nhq), F32), pltpu.VMEM((tq, nhq), F32)],
        compiler_params=pltpu.CompilerParams(dimension_semantics=("arbitrary", "arbitrary"),
                                             vmem_limit_bytes=VMEM_LIMIT),
        name="dsa",
    )(rel_bias, qabs, qidx, widx, ckv, ckvt, kk)


def _stick_kernel(q_ref, k_ref, v_ref, o_ref):
    tq = tk = ATT_TILE
    qi = pl.program_id(1)
    npair = WIDTH_B // LANES
    lane = lax.broadcasted_iota(jnp.int32, (tq, LANES), 1)
    row = lax.broadcasted_iota(jnp.int32, (tq, tk), 0)
    col = lax.broadcasted_iota(jnp.int32, (tq, tk), 1)
    causal = col < row
    upper = jnp.where(row > col, 1.0, 0.0).astype(BF16)
    upper_ones = jnp.concatenate([upper, jnp.ones((tk, LANES), BF16)], axis=1)
    lo_half = jnp.where(lane < HEAD_DIM, 1.0, 0.0)
    causal2 = jnp.concatenate([causal, causal], axis=0)
    q_pairs = []
    for p in range(npair):
        q2 = q_ref[:, p * LANES:(p + 1) * LANES].astype(F32)
        q_pairs.append(jnp.concatenate([q2 * lo_half, q2 * (1.0 - lo_half)], axis=0).astype(BF16))

    def tile(kj, r_sums, outs, masked):
        ks = pl.multiple_of(kj * tk, tk)
        log_sig, log_1m = [], []
        for p in range(npair):
            z = _dot_nt(q_pairs[p], k_ref[pl.ds(ks, tk), p * LANES:(p + 1) * LANES])
            sp = jnp.maximum(z, 0.0) + jnp.log2(1.0 + jnp.exp2(-jnp.abs(z)))
            lm = jnp.where(causal2, -sp, 0.0) if masked else -sp
            log_sig.append(z - sp)
            log_1m.append(lm.astype(BF16))
        sums = _dot(jnp.concatenate(log_1m, axis=0), upper_ones)
        new_r, new_o = [], []
        for p in range(npair):
            sm = sums[p * 2 * tq:(p + 1) * 2 * tq]
            after = sm[:, :tk] + jnp.concatenate([r_sums[p]] * (tk // LANES), axis=1)
            a = jnp.exp2(log_sig[p] + after)
            if masked:
                a = jnp.where(causal2, a, 0.0)
            res = _dot(a.astype(BF16), v_ref[pl.ds(ks, tk), p * LANES:(p + 1) * LANES])
            new_o.append(outs[p] + jnp.where(lane < HEAD_DIM, res[:tq], res[tq:]))
            new_r.append(r_sums[p] + sm[:, tk:])
        return tuple(new_r), tuple(new_o)

    zeros = jnp.zeros((tq, LANES), F32)
    r_sums, outs = tile(qi, (jnp.zeros((2 * tq, LANES), F32),) * npair, (zeros,) * npair, True)

    def cond(c):
        it, r_sums, _ = c
        return (it < qi) & (jnp.max(functools.reduce(jnp.maximum, r_sums)) > EXP_UNDERFLOW)

    def body(c):
        it, r_sums, outs = c
        r_sums, outs = tile(qi - 1 - it, r_sums, outs, False)
        return it + 1, r_sums, outs

    _, _, outs = lax.while_loop(cond, body, (jnp.int32(0), r_sums, outs))
    for p in range(npair):
        o_ref[:, p * LANES:(p + 1) * LANES] = outs[p].astype(BF16)


def _stick(qb, kb, vb, batch, seq):
    n = qb.shape[0]
    tq = ATT_TILE
    nq = seq // tq
    qspec = pl.BlockSpec((tq, WIDTH_B), lambda b, q: (b * nq + q, 0))
    kspec = pl.BlockSpec((seq, WIDTH_B), lambda b, q: (b, 0))
    return pl.pallas_call(
        _stick_kernel,
        grid=(batch, nq),
        in_specs=[qspec, kspec, kspec],
        out_specs=qspec,
        out_shape=jax.ShapeDtypeStruct((n, WIDTH_B), BF16),
        compiler_params=pltpu.CompilerParams(dimension_semantics=("arbitrary",) * 2, vmem_limit_bytes=VMEM_LIMIT),
        name="stick",
    )(qb, kb, vb)


def _merge_kernel(x_ref, ol_ref, ob_ref, sa_ref, sb_ref, wuv_ref, wba_ref, wbb_ref, wout_ref, g_ref,
                  wr_ref, br_ref, x1_ref, h2_ref, comb_ref):
    half = N_HEADS_A * KV_RANK // 2
    oa = jnp.concatenate([_dot(ol_ref[:, :half], wuv_ref[:half, :WIDTH_A // 2]),
                          _dot(ol_ref[:, half:], wuv_ref[half:, WIDTH_A // 2:])], axis=1).astype(BF16)
    ya = _dot(oa, wba_ref[...])
    yb = _dot(ob_ref[...], wbb_ref[...])
    merged = sa_ref[...].astype(F32) * ya + sb_ref[...].astype(F32) * yb
    x1 = x_ref[...] + _dot(merged.astype(BF16), wout_ref[...])
    x1_ref[...] = x1
    h2 = _rms(x1, g_ref[...])
    h2_hi = h2.astype(BF16)
    h2_ref[...] = h2_hi
    h2_lo = (h2 - h2_hi.astype(F32)).astype(BF16)
    hi_part = _dot(h2_hi, wr_ref[...])
    logits = (hi_part[:, :LANES] + _dot(h2_lo, wr_ref[:, :LANES]) + hi_part[:, LANES:]) + br_ref[...]
    lane = lax.broadcasted_iota(jnp.int32, logits.shape, 1).astype(F32)
    ninf = -jnp.inf

    def first_max(v):
        m = jnp.max(v, axis=1, keepdims=True)
        return m, jnp.min(jnp.where(v == m, lane, 1e9), axis=1, keepdims=True)

    gmask = (lane >= N_EXPERTS) & (lane < N_EXPERTS + N_GROUPS)
    gl = jnp.where(gmask, logits, ninf)
    gmax, gidx = first_max(gl)
    p_g = 1.0 / jnp.sum(jnp.where(gmask, jnp.exp(gl - gmax), 0.0), axis=1, keepdims=True)
    e_lo = (gidx - N_EXPERTS) * EXPERTS_PER_GROUP
    el = jnp.where((lane >= e_lo) & (lane < e_lo + EXPERTS_PER_GROUP), logits, ninf)
    v1, i1 = first_max(el)
    el2 = jnp.where(lane == i1, ninf, el)
    v2, i2 = first_max(el2)
    e2 = jnp.exp(v2 - v1)
    w1 = 1.0 / (1.0 + e2)
    comb_ref[...] = jnp.where(lane == i1, w1 * p_g,
                              jnp.where(lane == i2, (e2 * w1) * p_g, jnp.where(lane == gidx, 1.0, 0.0)))


def _merge(x, olat, ob, sa, sb, wuv, wba, wbb, wout, g, wr, br):
    n, d = x.shape
    tm = ROW_TILE
    row = lambda w: pl.BlockSpec((tm, w), lambda i: (i, 0))
    ws = (wuv, wba, wbb, wout, g, wr, br)
    return pl.pallas_call(
        _merge_kernel,
        grid=(n // tm,),
        in_specs=[row(d), row(olat.shape[1]), row(ob.shape[1]), row(d), row(d)] + [_full_spec(w.shape) for w in ws],
        out_specs=[row(d), row(d), row(LANES)],
        out_shape=[jax.ShapeDtypeStruct((n, d), F32), jax.ShapeDtypeStruct((n, d), BF16),
                   jax.ShapeDtypeStruct((n, LANES), F32)],
        compiler_params=pltpu.CompilerParams(dimension_semantics=("arbitrary",), vmem_limit_bytes=VMEM_LIMIT),
        name="merge",
    )(x, olat, ob, sa, sb, *ws)


def _moe_kernel(h_ref, comb_ref, wgu_ref, wd_ref, o_ref, slotc_s, slotr_s, split_s, acc_s, rng_s):
    t, d = h_ref.shape
    nsub = MOE_SUBTILES
    ts = t // nsub
    c = MOE_CHUNK
    rb = MOE_RANK_BLOCK
    g = pl.program_id(1)

    @pl.when(g == 0)
    def _():
        lane_b = lax.broadcasted_iota(jnp.int32, (rb, LANES), 1)
        group_lanes = (lane_b >= N_EXPERTS) & (lane_b < N_EXPERTS + N_GROUPS)
        r_i = lax.broadcasted_iota(jnp.int32, (rb, rb), 0)
        c_i = lax.broadcasted_iota(jnp.int32, (rb, rb), 1)
        lower = jnp.where(c_i < r_i, 1.0, 0.0).astype(BF16)
        lane1 = lax.broadcasted_iota(jnp.int32, (1, LANES), 1)
        for sub in range(nsub):
            counts = jnp.zeros((1, LANES), F32)
            ranks = []
            for blk in range(ts // rb):
                rows = slice(sub * ts + blk * rb, sub * ts + (blk + 1) * rb)
                oh = jnp.where(group_lanes, comb_ref[rows, :], 0.0)
                before = _dot(lower, oh.astype(BF16)) + counts
                ranks.append(jnp.sum(before * oh, axis=1, keepdims=True))
                counts = counts + jnp.sum(oh, axis=0, keepdims=True)
            start = jnp.int32(0)
            seg = jnp.zeros((1, LANES), F32)
            for k in range(N_GROUPS):
                n_k = jnp.sum(jnp.where(lane1 == N_EXPERTS + k, counts, 0.0)).astype(jnp.int32)
                seg = jnp.where(lane1 == N_EXPERTS + k, (start * c).astype(F32), seg)
                rng_s[sub * 2 * N_GROUPS + k] = start
                for j in range(-(-ts // c)):
                    start = start + (n_k > j * c).astype(jnp.int32)
                rng_s[sub * 2 * N_GROUPS + N_GROUPS + k] = start
            for blk in range(ts // rb):
                rows = slice(sub * ts + blk * rb, sub * ts + (blk + 1) * rb)
                oh = jnp.where(group_lanes, comb_ref[rows, :], 0.0)
                slot = ranks[blk] + jnp.sum(oh * seg, axis=1, keepdims=True)
                slotc_s[rows, :] = jnp.broadcast_to(slot, (rb, LANES))
        slotr_s[...] = jnp.transpose(slotc_s[...])
        cb = comb_ref[...]
        hi = cb.astype(BF16)
        split_s[...] = jnp.concatenate([hi, (cb - hi.astype(F32)).astype(BF16)], axis=1)
        acc_s[...] = jnp.zeros(acc_s.shape, F32)

    lane = lax.broadcasted_iota(jnp.int32, (nsub * c, LANES), 1)
    first = [rng_s[sub * 2 * N_GROUPS + g] for sub in range(nsub)]
    n_chunks = [rng_s[sub * 2 * N_GROUPS + N_GROUPS + g] - first[sub] for sub in range(nsub)]

    def chunk_body(j, carry):
        bases = [((first[sub] + j) * c).astype(F32) for sub in range(nsub)]
        xs, cw = [], []
        for sub in range(nsub):
            rows = slice(sub * ts, (sub + 1) * ts)
            row_slot = lax.broadcasted_iota(jnp.int32, (c, ts), 0).astype(F32) + bases[sub]
            pc = jnp.where(slotr_s[0:1, rows] == row_slot, 1.0, 0.0).astype(BF16)
            xs.append(_dot(pc, h_ref[rows, :]).astype(BF16))
            cw2 = _dot(pc, split_s[rows, :])
            cw.append(cw2[:, :LANES] + cw2[:, LANES:])
        xs = jnp.concatenate(xs, axis=0)
        cw = jnp.concatenate(cw, axis=0)
        y = jnp.zeros((nsub * c, d), F32)
        for e in range(EXPERTS_PER_GROUP):
            gu = _dot(xs, wgu_ref[e])
            gate = gu[:, :D_EXPERT]
            hid = gate * jax.nn.sigmoid(gate) * gu[:, D_EXPERT:]
            ce = jnp.sum(jnp.where(lane == g * EXPERTS_PER_GROUP + e, cw, 0.0), axis=1, keepdims=True)
            y = y + _dot((hid * ce).astype(BF16), wd_ref[e])
        y = y.astype(BF16)
        for sub in range(nsub):
            rows = slice(sub * ts, (sub + 1) * ts)
            col_slot = lax.broadcasted_iota(jnp.int32, (ts, c), 1).astype(F32) + bases[sub]
            pct = jnp.where(jnp.broadcast_to(slotc_s[rows, 0:1], (ts, c)) == col_slot, 1.0, 0.0).astype(BF16)
            acc_s[rows, :] += _dot(pct, y[sub * c:(sub + 1) * c])
        return carry

    lax.fori_loop(0, functools.reduce(jnp.maximum, n_chunks), chunk_body, 0)

    @pl.when(g == N_GROUPS - 1)
    def _():
        o_ref[...] = acc_s[...].astype(BF16)


def _moe(h2, comb, wgu, wd):
    n, d = h2.shape
    tm = MOE_ROW_TILE
    row = lambda w: pl.BlockSpec((tm, w), lambda i, g: (i, 0))
    return pl.pallas_call(
        _moe_kernel,
        grid=(n // tm, N_GROUPS),
        in_specs=[row(d), row(LANES),
                  pl.BlockSpec((EXPERTS_PER_GROUP, d, 2 * D_EXPERT), lambda i, g: (g, 0, 0)),
                  pl.BlockSpec((EXPERTS_PER_GROUP, D_EXPERT, d), lambda i, g: (g, 0, 0))],
        out_specs=row(d),
        out_shape=jax.ShapeDtypeStruct((n, d), BF16),
        scratch_shapes=[pltpu.VMEM((tm, LANES), F32), pltpu.VMEM((LANES, tm), F32),
                        pltpu.VMEM((tm, 2 * LANES), BF16), pltpu.VMEM((tm, d), F32),
                        pltpu.SMEM((MOE_SUBTILES * 2 * N_GROUPS,), jnp.int32)],
        compiler_params=pltpu.CompilerParams(dimension_semantics=("arbitrary", "arbitrary"),
                                             vmem_limit_bytes=VMEM_LIMIT),
        name="moe",
    )(h2, comb, wgu, wd)


def _ple_kernel(x_ref, m_ref, p_ref, g_ref, wpg_ref, wple_ref, gf_ref, o_ref, *, last):
    x2 = x_ref[...] + m_ref[...].astype(F32)
    h3 = _rms(x2, g_ref[...]).astype(BF16)
    gate = jax.nn.sigmoid(_dot(h3, wpg_ref[...]))
    x3 = x2 + _dot(p_ref[...].astype(BF16), wple_ref[...]) * gate
    o_ref[...] = _rms(x3, gf_ref[...]) if last else x3


def _ple(x1, moe, p, g, wpg, wple, gf, last):
    n, d = x1.shape
    tm = ROW_TILE
    row = lambda w: pl.BlockSpec((tm, w), lambda i: (i, 0))
    ws = (g, wpg, wple, gf)
    return pl.pallas_call(
        functools.partial(_ple_kernel, last=last),
        grid=(n // tm,),
        in_specs=[row(d), row(d), row(p.shape[1])] + [_full_spec(w.shape) for w in ws],
        out_specs=row(d),
        out_shape=jax.ShapeDtypeStruct((n, d), F32),
        compiler_params=pltpu.CompilerParams(dimension_semantics=("arbitrary",), vmem_limit_bytes=VMEM_LIMIT),
        name="ple",
    )(x1, moe, p, *ws)


def _block_diag(blocks):
    h, r, c = blocks.shape
    eye = jnp.eye(h, dtype=blocks.dtype)
    return (eye[:, None, :, None] * blocks[:, :, None, :]).reshape(h * r, h * c)


def kernel(x, p, attn_norm, w_in, kv_norm, w_uk, w_uv, rel_bias, w_branch_a, w_branch_b, w_out, ffn_norm,
           w_r1, b_r1, w_r2, b_r2, w_gate, w_up, w_down, ple_norm, w_ple_gate, w_ple, final_norm):
    batch, seq, d = x.shape
    n = batch * seq
    depth = w_in.shape[0]
    xf = x.reshape(n, d).astype(F32)
    widths = [WIDTH_A, KV_RANK, IDX_HEADS * IDX_DIM, IDX_DIM, IDX_HEADS, 3 * WIDTH_B, d, d]
    starts = [sum(widths[:k]) for k in range(len(widths))]
    for i in range(depth):
        w_qa, w_ckv, w_qi, w_ki, w_wi, w_qkv, w_ga, w_gb = [
            w_in[i][:, s:s + w].astype(BF16) for s, w in zip(starts, widths)]
        w_kk = jnp.concatenate([w_ki, w_ki], axis=1)
        w_cw = jnp.concatenate([w_ckv, jnp.pad(w_wi, ((0, 0), (0, LANES - IDX_HEADS)))], axis=1)
        wuk_bd = _block_diag(jnp.swapaxes(w_uk[i], 1, 2)).astype(BF16)
        wuv_bd = _block_diag(w_uv[i]).astype(BF16)
        qabs, ckv, ckvt, qidx, kk, widx, qb, kb, vb, sa, sb = _proj(
            xf, attn_norm[i][None].astype(F32), w_qa, wuk_bd, w_cw, kv_norm[i][None].astype(F32),
            w_qi, w_kk, w_qkv, w_ga, w_gb)
        olat = _dsa(rel_bias.astype(F32), qabs, qidx, widx, ckv, ckvt, kk, batch, seq)
        ob = _stick(qb, kb, vb, batch, seq)
        w_r = jnp.concatenate([jnp.transpose(w_r2[i], (1, 0, 2)).reshape(d, N_EXPERTS), w_r1[i]], axis=1)
        w_r = jnp.pad(w_r.astype(F32), ((0, 0), (0, LANES - N_EXPERTS - N_GROUPS)))
        w_r_hi = w_r.astype(BF16)
        w_r_lo = (w_r - w_r_hi.astype(F32)).astype(BF16)
        b_r = jnp.pad(jnp.concatenate([b_r2[i].reshape(-1), b_r1[i]]).astype(F32),
                      (0, LANES - N_EXPERTS - N_GROUPS))[None]
        x1, h2, comb = _merge(xf, olat, ob, sa, sb, wuv_bd, w_branch_a[i].astype(BF16), w_branch_b[i].astype(BF16),
                              w_out[i].astype(BF16), ffn_norm[i][None].astype(F32),
                              jnp.concatenate([w_r_hi, w_r_lo], axis=1), b_r)
        wgu = jnp.concatenate([w_gate[i], w_up[i]], axis=2).astype(BF16)
        moe = _moe(h2, comb, wgu, w_down[i].astype(BF16))
        xf = _ple(x1, moe, p[i].reshape(n, -1).astype(F32), ple_norm[i][None].astype(F32), w_ple_gate[i].astype(BF16),
                  w_ple[i].astype(BF16), final_norm[None].astype(F32), last=(i == depth - 1))
    return xf.reshape(batch, seq, d).astype(x.dtype)
```

```python
import functools
import math

import jax
import jax.numpy as jnp
from jax import lax
from jax.experimental import pallas as pl
from jax.experimental.pallas import tpu as pltpu

D_MODEL = 1024
N_HEADS_A = 8
HEAD_DIM = 64
WIDTH_A = N_HEADS_A * HEAD_DIM
KV_RANK = 128
IDX_HEADS = 8
IDX_DIM = 64
TOPK_MAX = 256
N_HEADS_B = 8
WIDTH_B = N_HEADS_B * HEAD_DIM
N_BUCKETS = 32
MAX_EXACT = N_BUCKETS // 2
MAX_DISTANCE = 128
ATTN_SCALE = HEAD_DIM ** -0.5
IDX_SCALE = (IDX_HEADS ** -0.5) * (IDX_DIM ** -0.5)
N_GROUPS = 4
EXPERTS_PER_GROUP = 8
N_EXPERTS = N_GROUPS * EXPERTS_PER_GROUP
D_EXPERT = 256
PLE_DIM = 256
EPS = 1e-6
LOG2E = 1.4426950408889634

LANES = 128
ROW_TILE = 512
MOE_ROW_TILE = 1024
MOE_SUBTILES = 2
MOE_CHUNK = 160
MOE_RANK_BLOCK = 256
ATT_TILE = 256
VMEM_LIMIT = 56 * 1024 * 1024
NEG_BIG = -1e30
EXP_UNDERFLOW = -173.0
KEY_TOP = 127.0
KEY_MIN_RANGE = 1e-30

F32 = jnp.float32
BF16 = jnp.bfloat16


def _rms(x, g):
    return x * lax.rsqrt(jnp.mean(x * x, axis=-1, keepdims=True) + EPS) * g


def _dot(a, b):
    return jnp.dot(a, b, preferred_element_type=F32)


def _dot_nt(a, b):
    return lax.dot_general(a, b, (((1,), (1,)), ((), ())), preferred_element_type=F32)


def _full_spec(shape):
    nd = len(shape)
    return pl.BlockSpec(shape, lambda *_: (0,) * nd)


def _proj_kernel(x_ref, g_ref, wqa_ref, wuk_ref, wcw_ref, kvg_ref, wqi_ref, wkk_ref,
                 wqkv_ref, wga_ref, wgb_ref,
                 qabs_ref, ckv_ref, ckvt_ref, qidx_ref, kk_ref, widx_ref, qb_ref, kb_ref, vb_ref, sa_ref, sb_ref):
    h = _rms(x_ref[...], g_ref[...]).astype(BF16)
    qa = _dot(h, wqa_ref[...]).astype(BF16)
    half = WIDTH_A // 2
    qabs = jnp.concatenate([_dot(qa[:, :half], wuk_ref[:half, :N_HEADS_A * KV_RANK // 2]),
                            _dot(qa[:, half:], wuk_ref[half:, N_HEADS_A * KV_RANK // 2:])], axis=1)
    qabs_ref[...] = (qabs * (ATTN_SCALE * LOG2E)).astype(BF16)
    cw = _dot(h, wcw_ref[...])
    c = _rms(cw[:, :KV_RANK], kvg_ref[...])
    ckv_ref[...] = c.astype(BF16)
    c_ext = jnp.concatenate([c, jnp.ones_like(c)], axis=1)
    for j in range(ckvt_ref.shape[0]):
        ckvt_ref[j] = jnp.transpose(c_ext[j * ATT_TILE:(j + 1) * ATT_TILE, :]).astype(BF16)
    qidx_ref[...] = _dot(h, wqi_ref[...]).astype(BF16)
    kk_ref[...] = _dot(h, wkk_ref[...]).astype(BF16)
    widx_ref[...] = cw[:, KV_RANK:] * IDX_SCALE
    qkv = _dot(h, wqkv_ref[...])
    qb_ref[...] = (qkv[:, :WIDTH_B] * (ATTN_SCALE * LOG2E)).astype(BF16)
    kb_ref[...] = qkv[:, WIDTH_B:2 * WIDTH_B].astype(BF16)
    vb_ref[...] = qkv[:, 2 * WIDTH_B:].astype(BF16)
    sa_ref[...] = jax.nn.sigmoid(_dot(h, wga_ref[...])).astype(BF16)
    sb_ref[...] = jax.nn.sigmoid(_dot(h, wgb_ref[...])).astype(BF16)


def _proj(x, g, wqa, wuk, wcw, kvg, wqi, wkk, wqkv, wga, wgb):
    n, d = x.shape
    tm = ROW_TILE
    row = lambda w: pl.BlockSpec((tm, w), lambda i: (i, 0))
    ws = (g, wqa, wuk, wcw, kvg, wqi, wkk, wqkv, wga, wgb)
    outs = [(N_HEADS_A * KV_RANK, BF16), (KV_RANK, BF16), None, (IDX_HEADS * IDX_DIM, BF16), (LANES, BF16),
            (LANES, F32), (WIDTH_B, BF16), (WIDTH_B, BF16), (WIDTH_B, BF16), (d, BF16), (d, BF16)]
    t = ATT_TILE
    ckvt_spec = pl.BlockSpec((tm // t, 2 * KV_RANK, t), lambda i: (i, 0, 0))
    ckvt_shape = jax.ShapeDtypeStruct((n // t, 2 * KV_RANK, t), BF16)
    return pl.pallas_call(
        _proj_kernel,
        grid=(n // tm,),
        in_specs=[row(d)] + [_full_spec(w.shape) for w in ws],
        out_specs=[ckvt_spec if o is None else row(o[0]) for o in outs],
        out_shape=[ckvt_shape if o is None else jax.ShapeDtypeStruct((n, o[0]), o[1]) for o in outs],
        compiler_params=pltpu.CompilerParams(dimension_semantics=("arbitrary",), vmem_limit_bytes=VMEM_LIMIT),
        name="proj",
    )(x, *ws)


def _t5_bucket(dist):
    dist = jnp.maximum(dist, 0)
    d_f = jnp.maximum(dist, 1).astype(F32)
    large = MAX_EXACT + jnp.floor(jnp.log(d_f / MAX_EXACT) / math.log(MAX_DISTANCE / MAX_EXACT)
                                  * (N_BUCKETS - MAX_EXACT)).astype(jnp.int32)
    large = jnp.minimum(large, N_BUCKETS - 1)
    return jnp.where(dist < MAX_EXACT, dist, large)


def _dsa_kernel(relb_ref, qabs_ref, qidx_ref, widx_ref, ckv_ref, ckvt_ref, kk_ref, o_ref,
                qa_s, qi_s, key_s, dig_s, act_s, cs_s, mn_s, mx_s, bias_s, m_s, acc_s, sa_s, sb_s, *, n_sel, seq):
    tq = tk = ATT_TILE
    nh = N_HEADS_A
    nhq = nh * tq
    qi = pl.program_id(1)
    n_kt = qi + 1
    kpos = lax.broadcasted_iota(jnp.int32, (tk, tq), 0)
    qpos = lax.broadcasted_iota(jnp.int32, (tk, tq), 1)

    @pl.when((pl.program_id(0) == 0) & (qi == 0))
    def _():
        for t in range(3):
            bucket = _t5_bucket(t * tq + qpos - kpos if t < 2 else jnp.full((tk, tq), 2 * tq, jnp.int32))
            for h in range(nh):
                b = jnp.zeros((tk, tq), F32)
                for bk in range(N_BUCKETS):
                    b = jnp.where(bucket == bk, relb_ref[bk, h], b)
                bias_s[t, :, h * tq:(h + 1) * tq] = b * LOG2E

    lane = lax.broadcasted_iota(jnp.int32, (tq, LANES), 1)
    lo_half = jnp.where(lane < IDX_DIM, 1.0, 0.0)
    hi_half = 1.0 - lo_half
    for h in range(nh):
        qa_s[h * tq:(h + 1) * tq, :] = qabs_ref[:, h * KV_RANK:(h + 1) * KV_RANK]
        pair = qidx_ref[:, (h // 2) * LANES:(h // 2 + 1) * LANES].astype(F32)
        qi_s[h * tq:(h + 1) * tq, :] = (pair * (lo_half if h % 2 == 0 else hi_half)).astype(BF16)
    w_t = jnp.transpose(widx_ref[...])
    w_row = jnp.concatenate([w_t[h:h + 1, :] for h in range(nh)], axis=1)

    mn_s[...] = jnp.full(mn_s.shape, -NEG_BIG, F32)
    mx_s[...] = jnp.full(mx_s.shape, NEG_BIG, F32)

    def idx_dots(kj, r_ref):
        ks = pl.multiple_of(jnp.minimum(kj, n_kt - 1) * tk, tk)
        r_ref[...] = _dot_nt(kk_ref[pl.ds(ks, tk), :], qi_s[...])

    def idx_scores(kj, r_ref):
        r = jnp.maximum(r_ref[...], 0.0) * w_row
        s = r[:, :tq]
        for h in range(1, nh):
            s = s + r[:, h * tq:(h + 1) * tq]
        masked = (kpos > qpos) & (kj == qi)
        key_s[kj] = jnp.where(masked, NEG_BIG, s)
        mn_s[...] = jnp.minimum(mn_s[...], jnp.min(jnp.where(masked, -NEG_BIG, s).reshape(tk // 8, 8, tq), axis=0))
        mx_s[...] = jnp.maximum(mx_s[...], jnp.max(jnp.where(masked, NEG_BIG, s).reshape(tk // 8, 8, tq), axis=0))

    idx_dots(0, sa_s)

    def idx_pair(i, carry):
        idx_dots(2 * i + 1, sb_s)
        idx_scores(2 * i, sa_s)
        idx_dots(2 * i + 2, sa_s)
        idx_scores(2 * i + 1, sb_s)
        return carry

    lax.fori_loop(0, n_kt // 2, idx_pair, 0)

    @pl.when(n_kt % 2 == 1)
    def _():
        idx_scores(n_kt - 1, sa_s)

    lo = jnp.min(mn_s[...], axis=0, keepdims=True)
    hi = jnp.max(mx_s[...], axis=0, keepdims=True)
    scale = KEY_TOP / jnp.maximum(hi - lo, KEY_MIN_RANGE)

    def key_body(kj, carry):
        y = jnp.maximum((key_s[kj] - lo) * scale, -1.0)
        rem = y
        for p in (3, 2, 1, 0):
            d = jnp.floor(rem)
            dig_s[p, kj] = d.astype(BF16)
            rem = (rem - d) * 256.0
        key_s[kj] = y - rem * (2.0 ** -32)
        return carry

    lax.fori_loop(0, n_kt, key_body, 0)

    def count_active(pred, then=None):
        one, zero = jnp.ones((), BF16), jnp.zeros((), BF16)

        def body(kj, acc):
            a = act_s[kj]
            c = jnp.where(pred(a), one, zero).reshape(tk // 16, 16, tq)
            part = c[0]
            for j in range(1, tk // 16):
                part = part + c[j]
            if then is not None:
                then(kj, a)
            return acc + part.astype(F32)
        acc = lax.fori_loop(0, n_kt, body, jnp.zeros((16, tq), F32))
        return jnp.sum(acc, axis=0, keepdims=True)

    def to_digit(v):
        return v.astype(F32).astype(BF16)

    def digit_phase(p, prev_digit, target):
        if p == 3:
            def init_body(kj, carry):
                act_s[kj] = dig_s[3, kj]
                return carry
            lax.fori_loop(0, n_kt, init_body, 0)
        else:
            prev = to_digit(prev_digit)

            def narrow(kj, a):
                act_s[kj] = jnp.where(a == prev, dig_s[p, kj], jnp.full((), -1.0, BF16))
            target = target - count_active(lambda a: a > prev, then=narrow)
        n_bits = 7 if p == 3 else 8

        def bit_body(i, dgt):
            cand = dgt | jnp.left_shift(jnp.int32(1), n_bits - 1 - i)
            cand_d = to_digit(cand)
            cnt = count_active(lambda a: a >= cand_d)
            return jnp.where(cnt >= target, cand, dgt)
        return lax.fori_loop(0, n_bits, bit_body, jnp.zeros((1, tq), jnp.int32)), target

    target = jnp.full((1, tq), float(n_sel), F32)
    tau = jnp.zeros((1, tq), F32)
    digit = None
    for p in (3, 2, 1):
        digit, target = digit_phase(p, digit, target)
        tau = tau + digit.astype(F32) * (256.0 ** (p - 3))
    digit_d = to_digit(digit)
    excess = count_active(lambda a: a >= digit_d) - target

    def count_keys(pred):
        def body(kj, acc):
            c = jnp.where(pred(key_s[kj], kpos + kj * tk), 1.0, 0.0)
            return acc + jnp.sum(c.reshape(tk // 8, 8, tq), axis=0)
        acc = lax.fori_loop(0, n_kt, body, jnp.zeros((8, tq), F32))
        return jnp.sum(acc, axis=0, keepdims=True)

    first = qi == 0
    cs_s[...] = jnp.where(lax.broadcasted_iota(jnp.int32, cs_s.shape, 0) == 0, seq, 0)

    @pl.when(jnp.logical_not(first) & (jnp.max(excess) > 0.0))
    def _():
        digit0, target0 = digit_phase(0, digit, target)
        cs_s[1:2, :] = digit0
        tau0 = tau + digit0.astype(F32) * (256.0 ** -3)
        last = to_digit(digit0)
        n_gt = count_active(lambda a: a > last)
        n_eq = count_active(lambda a: a == last)
        need = target0 - n_gt

        @pl.when(jnp.max(n_eq - need) > 0.0)
        def _():
            nbits = seq.bit_length() - 1

            def tie_body(i, d):
                cand = d | jnp.left_shift(jnp.int32(1), nbits - 1 - i)
                cnt = count_keys(lambda k, pos: (k == tau0) & (pos < cand))
                return jnp.where(cnt < need, cand, d)

            cs_s[0:1, :] = lax.fori_loop(0, nbits, tie_body, jnp.zeros((1, tq), jnp.int32))

    tau = jnp.where(first, -0.5, tau + cs_s[1:2, :].astype(F32) * (256.0 ** -3))
    cstar = jnp.where(first, -1, cs_s[0:1, :])

    m_s[...] = jnp.full(m_s.shape, NEG_BIG, F32)
    acc_s[...] = jnp.zeros(acc_s.shape, F32)

    def scores(kj, s_ref):
        kc = jnp.minimum(kj, n_kt - 1)
        ks = pl.multiple_of(kc * tk, tk)
        k = key_s[kc]
        sel = ((k > tau) | ((k == tau) & (kpos + kc * tk <= cstar))) & (kj < n_kt)
        selb = jnp.where(sel, 0.0, NEG_BIG)
        s = _dot_nt(ckv_ref[pl.ds(ks, tk), :], qa_s[...])
        s_ref[...] = s + bias_s[jnp.minimum(qi - kc, 2)] + jnp.concatenate([selb] * nh, axis=1)

    def update(kj, s_ref):
        s = s_ref[...]
        m_old = m_s[0:1, :]
        m_new = jnp.maximum(m_old, jnp.max(s, axis=0, keepdims=True))
        alpha = jnp.exp2(m_old - m_new)
        p = jnp.exp2(s - m_new).astype(BF16)
        acc_s[...] = acc_s[...] * alpha + _dot(ckvt_ref[jnp.minimum(kj, n_kt - 1)], p)
        m_s[...] = jnp.broadcast_to(m_new, m_s.shape)

    scores(0, sa_s)

    def pair_body(i, carry):
        scores(2 * i + 1, sb_s)
        update(2 * i, sa_s)
        scores(2 * i + 2, sa_s)
        update(2 * i + 1, sb_s)
        return carry

    lax.fori_loop(0, n_kt // 2, pair_body, 0)

    @pl.when(n_kt % 2 == 1)
    def _():
        update(n_kt - 1, sa_s)

    acc = acc_s[...]
    o_t = acc[:KV_RANK] / acc[KV_RANK:]
    for h in range(nh):
        o_ref[:, h * KV_RANK:(h + 1) * KV_RANK] = jnp.transpose(o_t[:, h * tq:(h + 1) * tq]).astype(BF16)


def _dsa(rel_bias, qabs, qidx, widx, ckv, ckvt, kk, batch, seq):
    n = qabs.shape[0]
    tq = ATT_TILE
    nq = seq // tq
    n_sel = min(TOPK_MAX, seq // 4)
    assert n_sel == tq and seq % tq == 0 and seq & (seq - 1) == 0
    qrow = lambda w: pl.BlockSpec((tq, w), lambda b, q: (b * nq + q, 0))
    brow = lambda w: pl.BlockSpec((seq, w), lambda b, q: (b, 0))
    nhq = N_HEADS_A * tq
    return pl.pallas_call(
        functools.partial(_dsa_kernel, n_sel=n_sel, seq=seq),
        grid=(batch, nq),
        in_specs=[pl.BlockSpec(memory_space=pltpu.SMEM),
                  qrow(N_HEADS_A * KV_RANK), qrow(IDX_HEADS * IDX_DIM), qrow(LANES),
                  brow(KV_RANK), pl.BlockSpec((nq, 2 * KV_RANK, tq), lambda b, q: (b, 0, 0)), brow(LANES)],
        out_specs=qrow(N_HEADS_A * KV_RANK),
        out_shape=jax.ShapeDtypeStruct((n, N_HEADS_A * KV_RANK), BF16),
        scratch_shapes=[pltpu.VMEM((nhq, KV_RANK), BF16), pltpu.VMEM((nhq, LANES), BF16),
                        pltpu.VMEM((nq, tq, tq), F32), pltpu.VMEM((4, nq, tq, tq), BF16),
                        pltpu.VMEM((nq, tq, tq), BF16), pltpu.VMEM((8, tq), jnp.int32),
                        pltpu.VMEM((8, tq), F32), pltpu.VMEM((8, tq), F32),
                        pltpu.VMEM((3, tq, nhq), F32),
                        pltpu.VMEM((8, nhq), F32), pltpu.VMEM((2 * KV_RANK, nhq), F32),
                        pltpu.VMEM((tq, nhq), F32), pltpu.VMEM((tq, nhq), F32)],
        compiler_params=pltpu.CompilerParams(dimension_semantics=("arbitrary", "arbitrary"),
                                             vmem_limit_bytes=VMEM_LIMIT),
        name="dsa",
    )(rel_bias, qabs, qidx, widx, ckv, ckvt, kk)


def _stick_kernel(q_ref, k_ref, v_ref, o_ref):
    tq = tk = ATT_TILE
    qi = pl.program_id(1)
    npair = WIDTH_B // LANES
    lane = lax.broadcasted_iota(jnp.int32, (tq, LANES), 1)
    row = lax.broadcasted_iota(jnp.int32, (tq, tk), 0)
    col = lax.broadcasted_iota(jnp.int32, (tq, tk), 1)
    causal = col < row
    upper = jnp.where(row > col, 1.0, 0.0).astype(BF16)
    upper_ones = jnp.concatenate([upper, jnp.ones((tk, LANES), BF16)], axis=1)
    lo_half = jnp.where(lane < HEAD_DIM, 1.0, 0.0)
    causal2 = jnp.concatenate([causal, causal], axis=0)
    q_pairs = []
    for p in range(npair):
        q2 = q_ref[:, p * LANES:(p + 1) * LANES].astype(F32)
        q_pairs.append(jnp.concatenate([q2 * lo_half, q2 * (1.0 - lo_half)], axis=0).astype(BF16))

    def tile(kj, r_sums, outs, masked):
        ks = pl.multiple_of(kj * tk, tk)
        log_sig, log_1m = [], []
        for p in range(npair):
            z = _dot_nt(q_pairs[p], k_ref[pl.ds(ks, tk), p * LANES:(p + 1) * LANES])
            sp = jnp.maximum(z, 0.0) + jnp.log2(1.0 + jnp.exp2(-jnp.abs(z)))
            lm = jnp.where(causal2, -sp, 0.0) if masked else -sp
            log_sig.append(z - sp)
            log_1m.append(lm.astype(BF16))
        sums = _dot(jnp.concatenate(log_1m, axis=0), upper_ones)
        new_r, new_o = [], []
        for p in range(npair):
            sm = sums[p * 2 * tq:(p + 1) * 2 * tq]
            after = sm[:, :tk] + jnp.concatenate([r_sums[p]] * (tk // LANES), axis=1)
            a = jnp.exp2(log_sig[p] + after)
            if masked:
                a = jnp.where(causal2, a, 0.0)
            res = _dot(a.astype(BF16), v_ref[pl.ds(ks, tk), p * LANES:(p + 1) * LANES])
            new_o.append(outs[p] + jnp.where(lane < HEAD_DIM, res[:tq], res[tq:]))
            new_r.append(r_sums[p] + sm[:, tk:])
        return tuple(new_r), tuple(new_o)

    zeros = jnp.zeros((tq, LANES), F32)
    r_sums, outs = tile(qi, (jnp.zeros((2 * tq, LANES), F32),) * npair, (zeros,) * npair, True)

    def cond(c):
        it, r_sums, _ = c
        return (it < qi) & (jnp.max(functools.reduce(jnp.maximum, r_sums)) > EXP_UNDERFLOW)

    def body(c):
        it, r_sums, outs = c
        r_sums, outs = tile(qi - 1 - it, r_sums, outs, False)
        return it + 1, r_sums, outs

    _, _, outs = lax.while_loop(cond, body, (jnp.int32(0), r_sums, outs))
    for p in range(npair):
        o_ref[:, p * LANES:(p + 1) * LANES] = outs[p].astype(BF16)


def _stick(qb, kb, vb, batch, seq):
    n = qb.shape[0]
    tq = ATT_TILE
    nq = seq // tq
    qspec = pl.BlockSpec((tq, WIDTH_B), lambda b, q: (b * nq + q, 0))
    kspec = pl.BlockSpec((seq, WIDTH_B), lambda b, q: (b, 0))
    return pl.pallas_call(
        _stick_kernel,
        grid=(batch, nq),
        in_specs=[qspec, kspec, kspec],
        out_specs=qspec,
        out_shape=jax.ShapeDtypeStruct((n, WIDTH_B), BF16),
        compiler_params=pltpu.CompilerParams(dimension_semantics=("arbitrary",) * 2, vmem_limit_bytes=VMEM_LIMIT),
        name="stick",
    )(qb, kb, vb)


def _merge_kernel(x_ref, ol_ref, ob_ref, sa_ref, sb_ref, wuv_ref, wba_ref, wbb_ref, wout_ref, g_ref,
                  wr_ref, br_ref, x1_ref, h2_ref, comb_ref):
    half = N_HEADS_A * KV_RANK // 2
    oa = jnp.concatenate([_dot(ol_ref[:, :half], wuv_ref[:half, :WIDTH_A // 2]),
                          _dot(ol_ref[:, half:], wuv_ref[half:, WIDTH_A // 2:])], axis=1).astype(BF16)
    ya = _dot(oa, wba_ref[...])
    yb = _dot(ob_ref[...], wbb_ref[...])
    merged = sa_ref[...].astype(F32) * ya + sb_ref[...].astype(F32) * yb
    x1 = x_ref[...] + _dot(merged.astype(BF16), wout_ref[...])
    x1_ref[...] = x1
    h2 = _rms(x1, g_ref[...])
    h2_hi = h2.astype(BF16)
    h2_ref[...] = h2_hi
    h2_lo = (h2 - h2_hi.astype(F32)).astype(BF16)
    hi_part = _dot(h2_hi, wr_ref[...])
    logits = (hi_part[:, :LANES] + _dot(h2_lo, wr_ref[:, :LANES]) + hi_part[:, LANES:]) + br_ref[...]
    lane = lax.broadcasted_iota(jnp.int32, logits.shape, 1).astype(F32)
    ninf = -jnp.inf

    def first_max(v):
        m = jnp.max(v, axis=1, keepdims=True)
        return m, jnp.min(jnp.where(v == m, lane, 1e9), axis=1, keepdims=True)

    gmask = (lane >= N_EXPERTS) & (lane < N_EXPERTS + N_GROUPS)
    gl = jnp.where(gmask, logits, ninf)
    gmax, gidx = first_max(gl)
    p_g = 1.0 / jnp.sum(jnp.where(gmask, jnp.exp(gl - gmax), 0.0), axis=1, keepdims=True)
    e_lo = (gidx - N_EXPERTS) * EXPERTS_PER_GROUP
    el = jnp.where((lane >= e_lo) & (lane < e_lo + EXPERTS_PER_GROUP), logits, ninf)
    v1, i1 = first_max(el)
    el2 = jnp.where(lane == i1, ninf, el)
    v2, i2 = first_max(el2)
    e2 = jnp.exp(v2 - v1)
    w1 = 1.0 / (1.0 + e2)
    comb_ref[...] = jnp.where(lane == i1, w1 * p_g,
                              jnp.where(lane == i2, (e2 * w1) * p_g, jnp.where(lane == gidx, 1.0, 0.0)))


def _merge(x, olat, ob, sa, sb, wuv, wba, wbb, wout, g, wr, br):
    n, d = x.shape
    tm = ROW_TILE
    row = lambda w: pl.BlockSpec((tm, w), lambda i: (i, 0))
    ws = (wuv, wba, wbb, wout, g, wr, br)
    return pl.pallas_call(
        _merge_kernel,
        grid=(n // tm,),
        in_specs=[row(d), row(olat.shape[1]), row(ob.shape[1]), row(d), row(d)] + [_full_spec(w.shape) for w in ws],
        out_specs=[row(d), row(d), row(LANES)],
        out_shape=[jax.ShapeDtypeStruct((n, d), F32), jax.ShapeDtypeStruct((n, d), BF16),
                   jax.ShapeDtypeStruct((n, LANES), F32)],
        compiler_params=pltpu.CompilerParams(dimension_semantics=("arbitrary",), vmem_limit_bytes=VMEM_LIMIT),
        name="merge",
    )(x, olat, ob, sa, sb, *ws)


def _moe_kernel(h_ref, comb_ref, wg_ref, wu_ref, wd_ref, o_ref, slotc_s, slotr_s, split_s, acc_s, rng_s):
    t, d = h_ref.shape
    nsub = MOE_SUBTILES
    ts = t // nsub
    c = MOE_CHUNK
    rb = MOE_RANK_BLOCK
    g = pl.program_id(1)

    @pl.when(g == 0)
    def _():
        lane_b = lax.broadcasted_iota(jnp.int32, (rb, LANES), 1)
        group_lanes = (lane_b >= N_EXPERTS) & (lane_b < N_EXPERTS + N_GROUPS)
        r_i = lax.broadcasted_iota(jnp.int32, (rb, rb), 0)
        c_i = lax.broadcasted_iota(jnp.int32, (rb, rb), 1)
        lower = jnp.where(c_i < r_i, 1.0, 0.0).astype(BF16)
        lane1 = lax.broadcasted_iota(jnp.int32, (1, LANES), 1)
        for sub in range(nsub):
            counts = jnp.zeros((1, LANES), F32)
            ranks = []
            for blk in range(ts // rb):
                rows = slice(sub * ts + blk * rb, sub * ts + (blk + 1) * rb)
                oh = jnp.where(group_lanes, comb_ref[rows, :], 0.0)
                before = _dot(lower, oh.astype(BF16)) + counts
                ranks.append(jnp.sum(before * oh, axis=1, keepdims=True))
                counts = counts + jnp.sum(oh, axis=0, keepdims=True)
            start = jnp.int32(0)
            seg = jnp.zeros((1, LANES), F32)
            for k in range(N_GROUPS):
                n_k = jnp.sum(jnp.where(lane1 == N_EXPERTS + k, counts, 0.0)).astype(jnp.int32)
                seg = jnp.where(lane1 == N_EXPERTS + k, (start * c).astype(F32), seg)
                rng_s[sub * 2 * N_GROUPS + k] = start
                for j in range(-(-ts // c)):
                    start = start + (n_k > j * c).astype(jnp.int32)
                rng_s[sub * 2 * N_GROUPS + N_GROUPS + k] = start
            for blk in range(ts // rb):
                rows = slice(sub * ts + blk * rb, sub * ts + (blk + 1) * rb)
                oh = jnp.where(group_lanes, comb_ref[rows, :], 0.0)
                slot = ranks[blk] + jnp.sum(oh * seg, axis=1, keepdims=True)
                slotc_s[rows, :] = jnp.broadcast_to(slot, (rb, LANES))
        slotr_s[...] = jnp.transpose(slotc_s[...])
        cb = comb_ref[...]
        hi = cb.astype(BF16)
        split_s[...] = jnp.concatenate([hi, (cb - hi.astype(F32)).astype(BF16)], axis=1)
        acc_s[...] = jnp.zeros(acc_s.shape, F32)

    lane = lax.broadcasted_iota(jnp.int32, (nsub * c, LANES), 1)
    first = [rng_s[sub * 2 * N_GROUPS + g] for sub in range(nsub)]
    n_chunks = [rng_s[sub * 2 * N_GROUPS + N_GROUPS + g] - first[sub] for sub in range(nsub)]

    def chunk_body(j, carry):
        bases = [((first[sub] + j) * c).astype(F32) for sub in range(nsub)]
        xs, cw = [], []
        for sub in range(nsub):
            rows = slice(sub * ts, (sub + 1) * ts)
            row_slot = lax.broadcasted_iota(jnp.int32, (c, ts), 0).astype(F32) + bases[sub]
            pc = jnp.where(slotr_s[0:1, rows] == row_slot, 1.0, 0.0).astype(BF16)
            xs.append(_dot(pc, h_ref[rows, :]).astype(BF16))
            cw2 = _dot(pc, split_s[rows, :])
            cw.append(cw2[:, :LANES] + cw2[:, LANES:])
        xs = jnp.concatenate(xs, axis=0)
        cw = jnp.concatenate(cw, axis=0)
        y = jnp.zeros((nsub * c, d), F32)
        for e in range(EXPERTS_PER_GROUP):
            gate = _dot(xs, wg_ref[e])
            hid = gate * jax.nn.sigmoid(gate) * _dot(xs, wu_ref[e])
            ce = jnp.sum(jnp.where(lane == g * EXPERTS_PER_GROUP + e, cw, 0.0), axis=1, keepdims=True)
            y = y + _dot((hid * ce).astype(BF16), wd_ref[e])
        y = y.astype(BF16)
        for sub in range(nsub):
            rows = slice(sub * ts, (sub + 1) * ts)
            col_slot = lax.broadcasted_iota(jnp.int32, (ts, c), 1).astype(F32) + bases[sub]
            pct = jnp.where(jnp.broadcast_to(slotc_s[rows, 0:1], (ts, c)) == col_slot, 1.0, 0.0).astype(BF16)
            acc_s[rows, :] += _dot(pct, y[sub * c:(sub + 1) * c])
        return carry

    lax.fori_loop(0, functools.reduce(jnp.maximum, n_chunks), chunk_body, 0)

    @pl.when(g == N_GROUPS - 1)
    def _():
        o_ref[...] = acc_s[...].astype(BF16)


def _moe(h2, comb, wg, wu, wd):
    n, d = h2.shape
    tm = MOE_ROW_TILE
    row = lambda w: pl.BlockSpec((tm, w), lambda i, g: (i, 0))
    return pl.pallas_call(
        _moe_kernel,
        grid=(n // tm, N_GROUPS),
        in_specs=[row(d), row(LANES),
                  pl.BlockSpec((EXPERTS_PER_GROUP, d, D_EXPERT), lambda i, g: (g, 0, 0)),
                  pl.BlockSpec((EXPERTS_PER_GROUP, d, D_EXPERT), lambda i, g: (g, 0, 0)),
                  pl.BlockSpec((EXPERTS_PER_GROUP, D_EXPERT, d), lambda i, g: (g, 0, 0))],
        out_specs=row(d),
        out_shape=jax.ShapeDtypeStruct((n, d), BF16),
        scratch_shapes=[pltpu.VMEM((tm, LANES), F32), pltpu.VMEM((LANES, tm), F32),
                        pltpu.VMEM((tm, 2 * LANES), BF16), pltpu.VMEM((tm, d), F32),
                        pltpu.SMEM((MOE_SUBTILES * 2 * N_GROUPS,), jnp.int32)],
        compiler_params=pltpu.CompilerParams(dimension_semantics=("arbitrary", "arbitrary"),
                                             vmem_limit_bytes=VMEM_LIMIT),
        name="moe",
    )(h2, comb, wg, wu, wd)


def _ple_kernel(x_ref, m_ref, p_ref, g_ref, wpg_ref, wple_ref, gf_ref, o_ref, *, last):
    x2 = x_ref[...] + m_ref[...].astype(F32)
    h3 = _rms(x2, g_ref[...]).astype(BF16)
    gate = jax.nn.sigmoid(_dot(h3, wpg_ref[...]))
    x3 = x2 + _dot(p_ref[...].astype(BF16), wple_ref[...]) * gate
    o_ref[...] = _rms(x3, gf_ref[...]) if last else x3


def _ple(x1, moe, p, g, wpg, wple, gf, last):
    n, d = x1.shape
    tm = ROW_TILE
    row = lambda w: pl.BlockSpec((tm, w), lambda i: (i, 0))
    ws = (g, wpg, wple, gf)
    return pl.pallas_call(
        functools.partial(_ple_kernel, last=last),
        grid=(n // tm,),
        in_specs=[row(d), row(d), row(p.shape[1])] + [_full_spec(w.shape) for w in ws],
        out_specs=row(d),
        out_shape=jax.ShapeDtypeStruct((n, d), F32),
        compiler_params=pltpu.CompilerParams(dimension_semantics=("arbitrary",), vmem_limit_bytes=VMEM_LIMIT),
        name="ple",
    )(x1, moe, p, *ws)


def _block_diag(blocks):
    h, r, c = blocks.shape
    eye = jnp.eye(h, dtype=blocks.dtype)
    return (eye[:, None, :, None] * blocks[:, :, None, :]).reshape(h * r, h * c)


def kernel(x, p, attn_norm, w_in, kv_norm, w_uk, w_uv, rel_bias, w_branch_a, w_branch_b, w_out, ffn_norm,
           w_r1, b_r1, w_r2, b_r2, w_gate, w_up, w_down, ple_norm, w_ple_gate, w_ple, final_norm):
    batch, seq, d = x.shape
    n = batch * seq
    depth = w_in.shape[0]
    xf = x.reshape(n, d).astype(F32)
    widths = [WIDTH_A, KV_RANK, IDX_HEADS * IDX_DIM, IDX_DIM, IDX_HEADS, 3 * WIDTH_B, d, d]
    starts = [sum(widths[:k]) for k in range(len(widths))]
    for i in range(depth):
        w_qa, w_ckv, w_qi, w_ki, w_wi, w_qkv, w_ga, w_gb = [
            w_in[i][:, s:s + w].astype(BF16) for s, w in zip(starts, widths)]
        w_kk = jnp.concatenate([w_ki, w_ki], axis=1)
        w_cw = jnp.concatenate([w_ckv, jnp.pad(w_wi, ((0, 0), (0, LANES - IDX_HEADS)))], axis=1)
        wuk_bd = _block_diag(jnp.swapaxes(w_uk[i], 1, 2)).astype(BF16)
        wuv_bd = _block_diag(w_uv[i]).astype(BF16)
        qabs, ckv, ckvt, qidx, kk, widx, qb, kb, vb, sa, sb = _proj(
            xf, attn_norm[i][None].astype(F32), w_qa, wuk_bd, w_cw, kv_norm[i][None].astype(F32),
            w_qi, w_kk, w_qkv, w_ga, w_gb)
        olat = _dsa(rel_bias.astype(F32), qabs, qidx, widx, ckv, ckvt, kk, batch, seq)
        ob = _stick(qb, kb, vb, batch, seq)
        w_r = jnp.concatenate([jnp.transpose(w_r2[i], (1, 0, 2)).reshape(d, N_EXPERTS), w_r1[i]], axis=1)
        w_r = jnp.pad(w_r.astype(F32), ((0, 0), (0, LANES - N_EXPERTS - N_GROUPS)))
        w_r_hi = w_r.astype(BF16)
        w_r_lo = (w_r - w_r_hi.astype(F32)).astype(BF16)
        b_r = jnp.pad(jnp.concatenate([b_r2[i].reshape(-1), b_r1[i]]).astype(F32),
                      (0, LANES - N_EXPERTS - N_GROUPS))[None]
        x1, h2, comb = _merge(xf, olat, ob, sa, sb, wuv_bd, w_branch_a[i].astype(BF16), w_branch_b[i].astype(BF16),
                              w_out[i].astype(BF16), ffn_norm[i][None].astype(F32),
                              jnp.concatenate([w_r_hi, w_r_lo], axis=1), b_r)
        moe = _moe(h2, comb, w_gate[i].astype(BF16), w_up[i].astype(BF16), w_down[i].astype(BF16))
        xf = _ple(x1, moe, p[i].reshape(n, -1).astype(F32), ple_norm[i][None].astype(F32), w_ple_gate[i].astype(BF16),
                  w_ple[i].astype(BF16), final_norm[None].astype(F32), last=(i == depth - 1))
    return xf.reshape(batch, seq, d).astype(x.dtype)
```

```python
import functools
import math

import jax
import jax.numpy as jnp
from jax import lax
from jax.experimental import pallas as pl
from jax.experimental.pallas import tpu as pltpu

D_MODEL = 1024
N_HEADS_A = 8
HEAD_DIM = 64
WIDTH_A = N_HEADS_A * HEAD_DIM
KV_RANK = 128
IDX_HEADS = 8
IDX_DIM = 64
TOPK_MAX = 256
N_HEADS_B = 8
WIDTH_B = N_HEADS_B * HEAD_DIM
N_BUCKETS = 32
MAX_EXACT = N_BUCKETS // 2
MAX_DISTANCE = 128
ATTN_SCALE = HEAD_DIM ** -0.5
IDX_SCALE = (IDX_HEADS ** -0.5) * (IDX_DIM ** -0.5)
N_GROUPS = 4
EXPERTS_PER_GROUP = 8
N_EXPERTS = N_GROUPS * EXPERTS_PER_GROUP
D_EXPERT = 256
PLE_DIM = 256
EPS = 1e-6
LOG2E = 1.4426950408889634

LANES = 128
ROW_TILE = 512
MOE_ROW_TILE = 1024
MOE_SUBTILES = 2
MOE_CHUNK = 160
MOE_RANK_BLOCK = 256
ATT_TILE = 256
VMEM_LIMIT = 56 * 1024 * 1024
NEG_BIG = -1e30
EXP_UNDERFLOW = -173.0
KEY_TOP = 127.0
KEY_MIN_RANGE = 1e-30

F32 = jnp.float32
BF16 = jnp.bfloat16


def _rms(x, g):
    return x * lax.rsqrt(jnp.mean(x * x, axis=-1, keepdims=True) + EPS) * g


def _dot(a, b):
    return jnp.dot(a, b, preferred_element_type=F32)


def _dot_nt(a, b):
    return lax.dot_general(a, b, (((1,), (1,)), ((), ())), preferred_element_type=F32)


def _full_spec(shape):
    nd = len(shape)
    return pl.BlockSpec(shape, lambda *_: (0,) * nd)


def _proj_kernel(x_ref, g_ref, wqa_ref, wuk_ref, wcw_ref, kvg_ref, wqi_ref, wkk_ref,
                 wqkv_ref, wga_ref, wgb_ref,
                 qabs_ref, ckv_ref, ckvt_ref, qidx_ref, kk_ref, widx_ref, qb_ref, kb_ref, vb_ref, sa_ref, sb_ref):
    h = _rms(x_ref[...], g_ref[...]).astype(BF16)
    qa = _dot(h, wqa_ref[...]).astype(BF16)
    half = WIDTH_A // 2
    qabs = jnp.concatenate([_dot(qa[:, :half], wuk_ref[:half, :N_HEADS_A * KV_RANK // 2]),
                            _dot(qa[:, half:], wuk_ref[half:, N_HEADS_A * KV_RANK // 2:])], axis=1)
    qabs_ref[...] = (qabs * (ATTN_SCALE * LOG2E)).astype(BF16)
    cw = _dot(h, wcw_ref[...])
    c = _rms(cw[:, :KV_RANK], kvg_ref[...])
    ckv_ref[...] = c.astype(BF16)
    c_ext = jnp.concatenate([c, jnp.ones_like(c)], axis=1)
    for j in range(ckvt_ref.shape[0]):
        ckvt_ref[j] = jnp.transpose(c_ext[j * ATT_TILE:(j + 1) * ATT_TILE, :]).astype(BF16)
    qidx_ref[...] = _dot(h, wqi_ref[...]).astype(BF16)
    kk_ref[...] = _dot(h, wkk_ref[...]).astype(BF16)
    widx_ref[...] = cw[:, KV_RANK:] * IDX_SCALE
    qkv = _dot(h, wqkv_ref[...])
    qb_ref[...] = (qkv[:, :WIDTH_B] * (ATTN_SCALE * LOG2E)).astype(BF16)
    kb_ref[...] = qkv[:, WIDTH_B:2 * WIDTH_B].astype(BF16)
    vb_ref[...] = qkv[:, 2 * WIDTH_B:].astype(BF16)
    sa_ref[...] = jax.nn.sigmoid(_dot(h, wga_ref[...])).astype(BF16)
    sb_ref[...] = jax.nn.sigmoid(_dot(h, wgb_ref[...])).astype(BF16)


def _proj(x, g, wqa, wuk, wcw, kvg, wqi, wkk, wqkv, wga, wgb):
    n, d = x.shape
    tm = ROW_TILE
    row = lambda w: pl.BlockSpec((tm, w), lambda i: (i, 0))
    ws = (g, wqa, wuk, wcw, kvg, wqi, wkk, wqkv, wga, wgb)
    outs = [(N_HEADS_A * KV_RANK, BF16), (KV_RANK, BF16), None, (IDX_HEADS * IDX_DIM, BF16), (LANES, BF16),
            (LANES, F32), (WIDTH_B, BF16), (WIDTH_B, BF16), (WIDTH_B, BF16), (d, BF16), (d, BF16)]
    t = ATT_TILE
    ckvt_spec = pl.BlockSpec((tm // t, 2 * KV_RANK, t), lambda i: (i, 0, 0))
    ckvt_shape = jax.ShapeDtypeStruct((n // t, 2 * KV_RANK, t), BF16)
    return pl.pallas_call(
        _proj_kernel,
        grid=(n // tm,),
        in_specs=[row(d)] + [_full_spec(w.shape) for w in ws],
        out_specs=[ckvt_spec if o is None else row(o[0]) for o in outs],
        out_shape=[ckvt_shape if o is None else jax.ShapeDtypeStruct((n, o[0]), o[1]) for o in outs],
        compiler_params=pltpu.CompilerParams(dimension_semantics=("arbitrary",), vmem_limit_bytes=VMEM_LIMIT),
        name="proj",
    )(x, *ws)


def _t5_bucket(dist):
    dist = jnp.maximum(dist, 0)
    d_f = jnp.maximum(dist, 1).astype(F32)
    large = MAX_EXACT + jnp.floor(jnp.log(d_f / MAX_EXACT) / math.log(MAX_DISTANCE / MAX_EXACT)
                                  * (N_BUCKETS - MAX_EXACT)).astype(jnp.int32)
    large = jnp.minimum(large, N_BUCKETS - 1)
    return jnp.where(dist < MAX_EXACT, dist, large)


def _dsa_kernel(relb_ref, qabs_ref, qidx_ref, widx_ref, ckv_ref, ckvt_ref, kk_ref, o_ref,
                qa_s, qi_s, key_s, dig_s, act_s, cs_s, mn_s, mx_s, bias_s, m_s, acc_s, sa_s, sb_s, ma_s, mb_s, *, n_sel, seq):
    tq = tk = ATT_TILE
    nh = N_HEADS_A
    nhq = nh * tq
    qi = pl.program_id(1)
    n_kt = qi + 1
    kpos = lax.broadcasted_iota(jnp.int32, (tk, tq), 0)
    qpos = lax.broadcasted_iota(jnp.int32, (tk, tq), 1)

    @pl.when((pl.program_id(0) == 0) & (qi == 0))
    def _():
        for t in range(3):
            bucket = _t5_bucket(t * tq + qpos - kpos if t < 2 else jnp.full((tk, tq), 2 * tq, jnp.int32))
            for h in range(nh):
                b = jnp.zeros((tk, tq), F32)
                for bk in range(N_BUCKETS):
                    b = jnp.where(bucket == bk, relb_ref[bk, h], b)
                bias_s[t, :, h * tq:(h + 1) * tq] = b * LOG2E

    lane = lax.broadcasted_iota(jnp.int32, (tq, LANES), 1)
    lo_half = jnp.where(lane < IDX_DIM, 1.0, 0.0)
    hi_half = 1.0 - lo_half
    for h in range(nh):
        qa_s[h * tq:(h + 1) * tq, :] = qabs_ref[:, h * KV_RANK:(h + 1) * KV_RANK]
        pair = qidx_ref[:, (h // 2) * LANES:(h // 2 + 1) * LANES].astype(F32)
        qi_s[h * tq:(h + 1) * tq, :] = (pair * (lo_half if h % 2 == 0 else hi_half)).astype(BF16)
    w_t = jnp.transpose(widx_ref[...])
    w_row = jnp.concatenate([w_t[h:h + 1, :] for h in range(nh)], axis=1)

    mn_s[...] = jnp.full(mn_s.shape, -NEG_BIG, F32)
    mx_s[...] = jnp.full(mx_s.shape, NEG_BIG, F32)

    def idx_dots(kj, r_ref):
        ks = pl.multiple_of(jnp.minimum(kj, n_kt - 1) * tk, tk)
        r_ref[...] = _dot_nt(kk_ref[pl.ds(ks, tk), :], qi_s[...])

    def idx_scores(kj, r_ref):
        r = jnp.maximum(r_ref[...], 0.0) * w_row
        s = r[:, :tq]
        for h in range(1, nh):
            s = s + r[:, h * tq:(h + 1) * tq]
        masked = (kpos > qpos) & (kj == qi)
        key_s[kj] = jnp.where(masked, NEG_BIG, s)
        mn_s[...] = jnp.minimum(mn_s[...], jnp.min(jnp.where(masked, -NEG_BIG, s).reshape(tk // 8, 8, tq), axis=0))
        mx_s[...] = jnp.maximum(mx_s[...], jnp.max(jnp.where(masked, NEG_BIG, s).reshape(tk // 8, 8, tq), axis=0))

    idx_dots(0, sa_s)

    def idx_pair(i, carry):
        idx_dots(2 * i + 1, sb_s)
        idx_scores(2 * i, sa_s)
        idx_dots(2 * i + 2, sa_s)
        idx_scores(2 * i + 1, sb_s)
        return carry

    lax.fori_loop(0, n_kt // 2, idx_pair, 0)

    @pl.when(n_kt % 2 == 1)
    def _():
        idx_scores(n_kt - 1, sa_s)

    lo = jnp.min(mn_s[...], axis=0, keepdims=True)
    hi = jnp.max(mx_s[...], axis=0, keepdims=True)
    scale = KEY_TOP / jnp.maximum(hi - lo, KEY_MIN_RANGE)

    def key_body(kj, carry):
        y = jnp.maximum((key_s[kj] - lo) * scale, -1.0)
        rem = y
        for p in (3, 2, 1, 0):
            d = jnp.floor(rem)
            dig_s[p, kj] = d.astype(BF16)
            rem = (rem - d) * 256.0
        key_s[kj] = y - rem * (2.0 ** -32)
        return carry

    lax.fori_loop(0, n_kt, key_body, 0)

    def count_active(pred, then=None):
        one, zero = jnp.ones((), BF16), jnp.zeros((), BF16)

        def body(kj, acc):
            a = act_s[kj]
            c = jnp.where(pred(a), one, zero).reshape(tk // 16, 16, tq)
            part = c[0]
            for j in range(1, tk // 16):
                part = part + c[j]
            if then is not None:
                then(kj, a)
            return acc + part.astype(F32)
        acc = lax.fori_loop(0, n_kt, body, jnp.zeros((16, tq), F32))
        return jnp.sum(acc, axis=0, keepdims=True)

    def to_digit(v):
        return v.astype(F32).astype(BF16)

    def digit_phase(p, prev_digit, target):
        if p == 3:
            def init_body(kj, carry):
                act_s[kj] = dig_s[3, kj]
                return carry
            lax.fori_loop(0, n_kt, init_body, 0)
        else:
            prev = to_digit(prev_digit)

            def narrow(kj, a):
                act_s[kj] = jnp.where(a == prev, dig_s[p, kj], jnp.full((), -1.0, BF16))
            target = target - count_active(lambda a: a > prev, then=narrow)
        n_bits = 7 if p == 3 else 8

        def bit_body(i, dgt):
            cand = dgt | jnp.left_shift(jnp.int32(1), n_bits - 1 - i)
            cand_d = to_digit(cand)
            cnt = count_active(lambda a: a >= cand_d)
            return jnp.where(cnt >= target, cand, dgt)
        return lax.fori_loop(0, n_bits, bit_body, jnp.zeros((1, tq), jnp.int32)), target

    target = jnp.full((1, tq), float(n_sel), F32)
    tau = jnp.zeros((1, tq), F32)
    digit = None
    for p in (3, 2, 1):
        digit, target = digit_phase(p, digit, target)
        tau = tau + digit.astype(F32) * (256.0 ** (p - 3))
    digit_d = to_digit(digit)
    excess = count_active(lambda a: a >= digit_d) - target

    def count_keys(pred):
        def body(kj, acc):
            c = jnp.where(pred(key_s[kj], kpos + kj * tk), 1.0, 0.0)
            return acc + jnp.sum(c.reshape(tk // 8, 8, tq), axis=0)
        acc = lax.fori_loop(0, n_kt, body, jnp.zeros((8, tq), F32))
        return jnp.sum(acc, axis=0, keepdims=True)

    first = qi == 0
    cs_s[...] = jnp.where(lax.broadcasted_iota(jnp.int32, cs_s.shape, 0) == 0, seq, 0)

    @pl.when(jnp.logical_not(first) & (jnp.max(excess) > 0.0))
    def _():
        digit0, target0 = digit_phase(0, digit, target)
        cs_s[1:2, :] = digit0
        tau0 = tau + digit0.astype(F32) * (256.0 ** -3)
        last = to_digit(digit0)
        n_gt = count_active(lambda a: a > last)
        n_eq = count_active(lambda a: a == last)
        need = target0 - n_gt

        @pl.when(jnp.max(n_eq - need) > 0.0)
        def _():
            nbits = seq.bit_length() - 1

            def tie_body(i, d):
                cand = d | jnp.left_shift(jnp.int32(1), nbits - 1 - i)
                cnt = count_keys(lambda k, pos: (k == tau0) & (pos < cand))
                return jnp.where(cnt < need, cand, d)

            cs_s[0:1, :] = lax.fori_loop(0, nbits, tie_body, jnp.zeros((1, tq), jnp.int32))

    tau = jnp.where(first, -0.5, tau + cs_s[1:2, :].astype(F32) * (256.0 ** -3))
    cstar = jnp.where(first, -1, cs_s[0:1, :])

    m_s[...] = jnp.full(m_s.shape, NEG_BIG, F32)
    acc_s[...] = jnp.zeros(acc_s.shape, F32)

    def scores(kj, s_ref, mx_ref):
        kc = jnp.minimum(kj, n_kt - 1)
        ks = pl.multiple_of(kc * tk, tk)
        k = key_s[kc]
        sel = ((k > tau) | ((k == tau) & (kpos + kc * tk <= cstar))) & (kj < n_kt)
        selb = jnp.where(sel, 0.0, NEG_BIG)
        s = _dot_nt(ckv_ref[pl.ds(ks, tk), :], qa_s[...])
        s = s + bias_s[jnp.minimum(qi - kc, 2)] + jnp.concatenate([selb] * nh, axis=1)
        s_ref[...] = s
        mx_ref[...] = jnp.max(s.reshape(tk // 8, 8, nhq), axis=0)

    def update(kj, s_ref, mx_ref):
        m_old = m_s[0:1, :]
        m_new = jnp.maximum(m_old, jnp.max(mx_ref[...], axis=0, keepdims=True))
        alpha = jnp.exp2(m_old - m_new)
        p = jnp.exp2(s_ref[...] - m_new).astype(BF16)
        acc_s[...] = acc_s[...] * alpha + _dot(ckvt_ref[jnp.minimum(kj, n_kt - 1)], p)
        m_s[...] = jnp.broadcast_to(m_new, m_s.shape)

    scores(0, sa_s, ma_s)

    def pair_body(i, carry):
        scores(2 * i + 1, sb_s, mb_s)
        update(2 * i, sa_s, ma_s)
        scores(2 * i + 2, sa_s, ma_s)
        update(2 * i + 1, sb_s, mb_s)
        return carry

    lax.fori_loop(0, n_kt // 2, pair_body, 0)

    @pl.when(n_kt % 2 == 1)
    def _():
        update(n_kt - 1, sa_s, ma_s)

    acc = acc_s[...]
    o_t = acc[:KV_RANK] / acc[KV_RANK:]
    for h in range(nh):
        o_ref[:, h * KV_RANK:(h + 1) * KV_RANK] = jnp.transpose(o_t[:, h * tq:(h + 1) * tq]).astype(BF16)


def _dsa(rel_bias, qabs, qidx, widx, ckv, ckvt, kk, batch, seq):
    n = qabs.shape[0]
    tq = ATT_TILE
    nq = seq // tq
    n_sel = min(TOPK_MAX, seq // 4)
    assert n_sel == tq and seq % tq == 0 and seq & (seq - 1) == 0
    qrow = lambda w: pl.BlockSpec((tq, w), lambda b, q: (b * nq + q, 0))
    brow = lambda w: pl.BlockSpec((seq, w), lambda b, q: (b, 0))
    nhq = N_HEADS_A * tq
    return pl.pallas_call(
        functools.partial(_dsa_kernel, n_sel=n_sel, seq=seq),
        grid=(batch, nq),
        in_specs=[pl.BlockSpec(memory_space=pltpu.SMEM),
                  qrow(N_HEADS_A * KV_RANK), qrow(IDX_HEADS * IDX_DIM), qrow(LANES),
                  brow(KV_RANK), pl.BlockSpec((nq, 2 * KV_RANK, tq), lambda b, q: (b, 0, 0)), brow(LANES)],
        out_specs=qrow(N_HEADS_A * KV_RANK),
        out_shape=jax.ShapeDtypeStruct((n, N_HEADS_A * KV_RANK), BF16),
        scratch_shapes=[pltpu.VMEM((nhq, KV_RANK), BF16), pltpu.VMEM((nhq, LANES), BF16),
                        pltpu.VMEM((nq, tq, tq), F32), pltpu.VMEM((4, nq, tq, tq), BF16),
                        pltpu.VMEM((nq, tq, tq), BF16), pltpu.VMEM((8, tq), jnp.int32),
                        pltpu.VMEM((8, tq), F32), pltpu.VMEM((8, tq), F32),
                        pltpu.VMEM((3, tq, nhq), F32),
                        pltpu.VMEM((8, nhq), F32), pltpu.VMEM((2 * KV_RANK, nhq), F32),
                        pltpu.VMEM((tq, nhq), F32), pltpu.VMEM((tq, nhq), F32),
                        pltpu.VMEM((8, nhq), F32), pltpu.VMEM((8, nhq), F32)],
        compiler_params=pltpu.CompilerParams(dimension_semantics=("arbitrary", "arbitrary"),
                                             vmem_limit_bytes=VMEM_LIMIT),
        name="dsa",
    )(rel_bias, qabs, qidx, widx, ckv, ckvt, kk)


def _stick_kernel(q_ref, k_ref, v_ref, o_ref):
    tq = tk = ATT_TILE
    qi = pl.program_id(1)
    npair = WIDTH_B // LANES
    lane = lax.broadcasted_iota(jnp.int32, (tq, LANES), 1)
    row = lax.broadcasted_iota(jnp.int32, (tq, tk), 0)
    col = lax.broadcasted_iota(jnp.int32, (tq, tk), 1)
    causal = col < row
    upper = jnp.where(row > col, 1.0, 0.0).astype(BF16)
    upper_ones = jnp.concatenate([upper, jnp.ones((tk, LANES), BF16)], axis=1)
    lo_half = jnp.where(lane < HEAD_DIM, 1.0, 0.0)
    causal2 = jnp.concatenate([causal, causal], axis=0)
    q_pairs = []
    for p in range(npair):
        q2 = q_ref[:, p * LANES:(p + 1) * LANES].astype(F32)
        q_pairs.append(jnp.concatenate([q2 * lo_half, q2 * (1.0 - lo_half)], axis=0).astype(BF16))

    def tile(kj, r_sums, outs, masked):
        ks = pl.multiple_of(kj * tk, tk)
        log_sig, log_1m = [], []
        for p in range(npair):
            z = _dot_nt(q_pairs[p], k_ref[pl.ds(ks, tk), p * LANES:(p + 1) * LANES])
            sp = jnp.maximum(z, 0.0) + jnp.log2(1.0 + jnp.exp2(-jnp.abs(z)))
            lm = jnp.where(causal2, -sp, 0.0) if masked else -sp
            log_sig.append(z - sp)
            log_1m.append(lm.astype(BF16))
        sums = _dot(jnp.concatenate(log_1m, axis=0), upper_ones)
        new_r, new_o = [], []
        for p in range(npair):
            sm = sums[p * 2 * tq:(p + 1) * 2 * tq]
            after = sm[:, :tk] + jnp.concatenate([r_sums[p]] * (tk // LANES), axis=1)
            a = jnp.exp2(log_sig[p] + after)
            if masked:
                a = jnp.where(causal2, a, 0.0)
            res = _dot(a.astype(BF16), v_ref[pl.ds(ks, tk), p * LANES:(p + 1) * LANES])
            new_o.append(outs[p] + jnp.where(lane < HEAD_DIM, res[:tq], res[tq:]))
            new_r.append(r_sums[p] + sm[:, tk:])
        return tuple(new_r), tuple(new_o)

    zeros = jnp.zeros((tq, LANES), F32)
    r_sums, outs = tile(qi, (jnp.zeros((2 * tq, LANES), F32),) * npair, (zeros,) * npair, True)

    def cond(c):
        it, r_sums, _ = c
        return (it < qi) & (jnp.max(functools.reduce(jnp.maximum, r_sums)) > EXP_UNDERFLOW)

    def body(c):
        it, r_sums, outs = c
        r_sums, outs = tile(qi - 1 - it, r_sums, outs, False)
        return it + 1, r_sums, outs

    _, _, outs = lax.while_loop(cond, body, (jnp.int32(0), r_sums, outs))
    for p in range(npair):
        o_ref[:, p * LANES:(p + 1) * LANES] = outs[p].astype(BF16)


def _stick(qb, kb, vb, batch, seq):
    n = qb.shape[0]
    tq = ATT_TILE
    nq = seq // tq
    qspec = pl.BlockSpec((tq, WIDTH_B), lambda b, q: (b * nq + q, 0))
    kspec = pl.BlockSpec((seq, WIDTH_B), lambda b, q: (b, 0))
    return pl.pallas_call(
        _stick_kernel,
        grid=(batch, nq),
        in_specs=[qspec, kspec, kspec],
        out_specs=qspec,
        out_shape=jax.ShapeDtypeStruct((n, WIDTH_B), BF16),
        compiler_params=pltpu.CompilerParams(dimension_semantics=("arbitrary",) * 2, vmem_limit_bytes=VMEM_LIMIT),
        name="stick",
    )(qb, kb, vb)


def _merge_kernel(x_ref, ol_ref, ob_ref, sa_ref, sb_ref, wuv_ref, wba_ref, wbb_ref, wout_ref, g_ref,
                  wr_ref, br_ref, x1_ref, h2_ref, comb_ref):
    half = N_HEADS_A * KV_RANK // 2
    oa = jnp.concatenate([_dot(ol_ref[:, :half], wuv_ref[:half, :WIDTH_A // 2]),
                          _dot(ol_ref[:, half:], wuv_ref[half:, WIDTH_A // 2:])], axis=1).astype(BF16)
    ya = _dot(oa, wba_ref[...])
    yb = _dot(ob_ref[...], wbb_ref[...])
    merged = sa_ref[...].astype(F32) * ya + sb_ref[...].astype(F32) * yb
    x1 = x_ref[...] + _dot(merged.astype(BF16), wout_ref[...])
    x1_ref[...] = x1
    h2 = _rms(x1, g_ref[...])
    h2_hi = h2.astype(BF16)
    h2_ref[...] = h2_hi
    h2_lo = (h2 - h2_hi.astype(F32)).astype(BF16)
    hi_part = _dot(h2_hi, wr_ref[...])
    logits = (hi_part[:, :LANES] + _dot(h2_lo, wr_ref[:, :LANES]) + hi_part[:, LANES:]) + br_ref[...]
    lane = lax.broadcasted_iota(jnp.int32, logits.shape, 1).astype(F32)
    ninf = -jnp.inf

    def first_max(v):
        m = jnp.max(v, axis=1, keepdims=True)
        return m, jnp.min(jnp.where(v == m, lane, 1e9), axis=1, keepdims=True)

    gmask = (lane >= N_EXPERTS) & (lane < N_EXPERTS + N_GROUPS)
    gl = jnp.where(gmask, logits, ninf)
    gmax, gidx = first_max(gl)
    p_g = 1.0 / jnp.sum(jnp.where(gmask, jnp.exp(gl - gmax), 0.0), axis=1, keepdims=True)
    e_lo = (gidx - N_EXPERTS) * EXPERTS_PER_GROUP
    el = jnp.where((lane >= e_lo) & (lane < e_lo + EXPERTS_PER_GROUP), logits, ninf)
    v1, i1 = first_max(el)
    el2 = jnp.where(lane == i1, ninf, el)
    v2, i2 = first_max(el2)
    e2 = jnp.exp(v2 - v1)
    w1 = 1.0 / (1.0 + e2)
    comb_ref[...] = jnp.where(lane == i1, w1 * p_g,
                              jnp.where(lane == i2, (e2 * w1) * p_g, jnp.where(lane == gidx, 1.0, 0.0)))


def _merge(x, olat, ob, sa, sb, wuv, wba, wbb, wout, g, wr, br):
    n, d = x.shape
    tm = ROW_TILE
    row = lambda w: pl.BlockSpec((tm, w), lambda i: (i, 0))
    ws = (wuv, wba, wbb, wout, g, wr, br)
    return pl.pallas_call(
        _merge_kernel,
        grid=(n // tm,),
        in_specs=[row(d), row(olat.shape[1]), row(ob.shape[1]), row(d), row(d)] + [_full_spec(w.shape) for w in ws],
        out_specs=[row(d), row(d), row(LANES)],
        out_shape=[jax.ShapeDtypeStruct((n, d), F32), jax.ShapeDtypeStruct((n, d), BF16),
                   jax.ShapeDtypeStruct((n, LANES), F32)],
        compiler_params=pltpu.CompilerParams(dimension_semantics=("arbitrary",), vmem_limit_bytes=VMEM_LIMIT),
        name="merge",
    )(x, olat, ob, sa, sb, *ws)


def _moe_kernel(h_ref, comb_ref, wg_ref, wu_ref, wd_ref, o_ref, slotc_s, slotr_s, split_s, acc_s, rng_s):
    t, d = h_ref.shape
    nsub = MOE_SUBTILES
    ts = t // nsub
    c = MOE_CHUNK
    rb = MOE_RANK_BLOCK
    g = pl.program_id(1)

    @pl.when(g == 0)
    def _():
        lane_b = lax.broadcasted_iota(jnp.int32, (rb, LANES), 1)
        group_lanes = (lane_b >= N_EXPERTS) & (lane_b < N_EXPERTS + N_GROUPS)
        r_i = lax.broadcasted_iota(jnp.int32, (rb, rb), 0)
        c_i = lax.broadcasted_iota(jnp.int32, (rb, rb), 1)
        lower = jnp.where(c_i < r_i, 1.0, 0.0).astype(BF16)
        lane1 = lax.broadcasted_iota(jnp.int32, (1, LANES), 1)
        for sub in range(nsub):
            counts = jnp.zeros((1, LANES), F32)
            ranks = []
            for blk in range(ts // rb):
                rows = slice(sub * ts + blk * rb, sub * ts + (blk + 1) * rb)
                oh = jnp.where(group_lanes, comb_ref[rows, :], 0.0)
                before = _dot(lower, oh.astype(BF16)) + counts
                ranks.append(jnp.sum(before * oh, axis=1, keepdims=True))
                counts = counts + jnp.sum(oh, axis=0, keepdims=True)
            start = jnp.int32(0)
            seg = jnp.zeros((1, LANES), F32)
            for k in range(N_GROUPS):
                n_k = jnp.sum(jnp.where(lane1 == N_EXPERTS + k, counts, 0.0)).astype(jnp.int32)
                seg = jnp.where(lane1 == N_EXPERTS + k, (start * c).astype(F32), seg)
                rng_s[sub * 2 * N_GROUPS + k] = start
                for j in range(-(-ts // c)):
                    start = start + (n_k > j * c).astype(jnp.int32)
                rng_s[sub * 2 * N_GROUPS + N_GROUPS + k] = start
            for blk in range(ts // rb):
                rows = slice(sub * ts + blk * rb, sub * ts + (blk + 1) * rb)
                oh = jnp.where(group_lanes, comb_ref[rows, :], 0.0)
                slot = ranks[blk] + jnp.sum(oh * seg, axis=1, keepdims=True)
                slotc_s[rows, :] = jnp.broadcast_to(slot, (rb, LANES))
        slotr_s[...] = jnp.transpose(slotc_s[...])
        cb = comb_ref[...]
        hi = cb.astype(BF16)
        split_s[...] = jnp.concatenate([hi, (cb - hi.astype(F32)).astype(BF16)], axis=1)
        acc_s[...] = jnp.zeros(acc_s.shape, F32)

    lane = lax.broadcasted_iota(jnp.int32, (nsub * c, LANES), 1)
    first = [rng_s[sub * 2 * N_GROUPS + g] for sub in range(nsub)]
    n_chunks = [rng_s[sub * 2 * N_GROUPS + N_GROUPS + g] - first[sub] for sub in range(nsub)]

    def chunk_body(j, carry):
        bases = [((first[sub] + j) * c).astype(F32) for sub in range(nsub)]
        xs, cw = [], []
        for sub in range(nsub):
            rows = slice(sub * ts, (sub + 1) * ts)
            row_slot = lax.broadcasted_iota(jnp.int32, (c, ts), 0).astype(F32) + bases[sub]
            pc = jnp.where(slotr_s[0:1, rows] == row_slot, 1.0, 0.0).astype(BF16)
            xs.append(_dot(pc, h_ref[rows, :]).astype(BF16))
            cw2 = _dot(pc, split_s[rows, :])
            cw.append(cw2[:, :LANES] + cw2[:, LANES:])
        xs = jnp.concatenate(xs, axis=0)
        cw = jnp.concatenate(cw, axis=0)
        y = jnp.zeros((nsub * c, d), F32)
        for e in range(EXPERTS_PER_GROUP):
            gate = _dot(xs, wg_ref[e])
            hid = gate * jax.nn.sigmoid(gate) * _dot(xs, wu_ref[e])
            ce = jnp.sum(jnp.where(lane == g * EXPERTS_PER_GROUP + e, cw, 0.0), axis=1, keepdims=True)
            y = y + _dot((hid * ce).astype(BF16), wd_ref[e])
        y = y.astype(BF16)
        for sub in range(nsub):
            rows = slice(sub * ts, (sub + 1) * ts)
            col_slot = lax.broadcasted_iota(jnp.int32, (ts, c), 1).astype(F32) + bases[sub]
            pct = jnp.where(jnp.broadcast_to(slotc_s[rows, 0:1], (ts, c)) == col_slot, 1.0, 0.0).astype(BF16)
            acc_s[rows, :] += _dot(pct, y[sub * c:(sub + 1) * c])
        return carry

    lax.fori_loop(0, functools.reduce(jnp.maximum, n_chunks), chunk_body, 0)

    @pl.when(g == N_GROUPS - 1)
    def _():
        o_ref[...] = acc_s[...].astype(BF16)


def _moe(h2, comb, wg, wu, wd):
    n, d = h2.shape
    tm = MOE_ROW_TILE
    row = lambda w: pl.BlockSpec((tm, w), lambda i, g: (i, 0))
    return pl.pallas_call(
        _moe_kernel,
        grid=(n // tm, N_GROUPS),
        in_specs=[row(d), row(LANES),
                  pl.BlockSpec((EXPERTS_PER_GROUP, d, D_EXPERT), lambda i, g: (g, 0, 0)),
                  pl.BlockSpec((EXPERTS_PER_GROUP, d, D_EXPERT), lambda i, g: (g, 0, 0)),
                  pl.BlockSpec((EXPERTS_PER_GROUP, D_EXPERT, d), lambda i, g: (g, 0, 0))],
        out_specs=row(d),
        out_shape=jax.ShapeDtypeStruct((n, d), BF16),
        scratch_shapes=[pltpu.VMEM((tm, LANES), F32), pltpu.VMEM((LANES, tm), F32),
                        pltpu.VMEM((tm, 2 * LANES), BF16), pltpu.VMEM((tm, d), F32),
                        pltpu.SMEM((MOE_SUBTILES * 2 * N_GROUPS,), jnp.int32)],
        compiler_params=pltpu.CompilerParams(dimension_semantics=("arbitrary", "arbitrary"),
                                             vmem_limit_bytes=VMEM_LIMIT),
        name="moe",
    )(h2, comb, wg, wu, wd)


def _ple_kernel(x_ref, m_ref, p_ref, g_ref, wpg_ref, wple_ref, gf_ref, o_ref, *, last):
    x2 = x_ref[...] + m_ref[...].astype(F32)
    h3 = _rms(x2, g_ref[...]).astype(BF16)
    gate = jax.nn.sigmoid(_dot(h3, wpg_ref[...]))
    x3 = x2 + _dot(p_ref[...].astype(BF16), wple_ref[...]) * gate
    o_ref[...] = _rms(x3, gf_ref[...]) if last else x3


def _ple(x1, moe, p, g, wpg, wple, gf, last):
    n, d = x1.shape
    tm = ROW_TILE
    row = lambda w: pl.BlockSpec((tm, w), lambda i: (i, 0))
    ws = (g, wpg, wple, gf)
    return pl.pallas_call(
        functools.partial(_ple_kernel, last=last),
        grid=(n // tm,),
        in_specs=[row(d), row(d), row(p.shape[1])] + [_full_spec(w.shape) for w in ws],
        out_specs=row(d),
        out_shape=jax.ShapeDtypeStruct((n, d), F32),
        compiler_params=pltpu.CompilerParams(dimension_semantics=("arbitrary",), vmem_limit_bytes=VMEM_LIMIT),
        name="ple",
    )(x1, moe, p, *ws)


def _block_diag(blocks):
    h, r, c = blocks.shape
    eye = jnp.eye(h, dtype=blocks.dtype)
    return (eye[:, None, :, None] * blocks[:, :, None, :]).reshape(h * r, h * c)


def kernel(x, p, attn_norm, w_in, kv_norm, w_uk, w_uv, rel_bias, w_branch_a, w_branch_b, w_out, ffn_norm,
           w_r1, b_r1, w_r2, b_r2, w_gate, w_up, w_down, ple_norm, w_ple_gate, w_ple, final_norm):
    batch, seq, d = x.shape
    n = batch * seq
    depth = w_in.shape[0]
    xf = x.reshape(n, d).astype(F32)
    widths = [WIDTH_A, KV_RANK, IDX_HEADS * IDX_DIM, IDX_DIM, IDX_HEADS, 3 * WIDTH_B, d, d]
    starts = [sum(widths[:k]) for k in range(len(widths))]
    for i in range(depth):
        w_qa, w_ckv, w_qi, w_ki, w_wi, w_qkv, w_ga, w_gb = [
            w_in[i][:, s:s + w].astype(BF16) for s, w in zip(starts, widths)]
        w_kk = jnp.concatenate([w_ki, w_ki], axis=1)
        w_cw = jnp.concatenate([w_ckv, jnp.pad(w_wi, ((0, 0), (0, LANES - IDX_HEADS)))], axis=1)
        wuk_bd = _block_diag(jnp.swapaxes(w_uk[i], 1, 2)).astype(BF16)
        wuv_bd = _block_diag(w_uv[i]).astype(BF16)
        qabs, ckv, ckvt, qidx, kk, widx, qb, kb, vb, sa, sb = _proj(
            xf, attn_norm[i][None].astype(F32), w_qa, wuk_bd, w_cw, kv_norm[i][None].astype(F32),
            w_qi, w_kk, w_qkv, w_ga, w_gb)
        olat = _dsa(rel_bias.astype(F32), qabs, qidx, widx, ckv, ckvt, kk, batch, seq)
        ob = _stick(qb, kb, vb, batch, seq)
        w_r = jnp.concatenate([jnp.transpose(w_r2[i], (1, 0, 2)).reshape(d, N_EXPERTS), w_r1[i]], axis=1)
        w_r = jnp.pad(w_r.astype(F32), ((0, 0), (0, LANES - N_EXPERTS - N_GROUPS)))
        w_r_hi = w_r.astype(BF16)
        w_r_lo = (w_r - w_r_hi.astype(F32)).astype(BF16)
        b_r = jnp.pad(jnp.concatenate([b_r2[i].reshape(-1), b_r1[i]]).astype(F32),
                      (0, LANES - N_EXPERTS - N_GROUPS))[None]
        x1, h2, comb = _merge(xf, olat, ob, sa, sb, wuv_bd, w_branch_a[i].astype(BF16), w_branch_b[i].astype(BF16),
                              w_out[i].astype(BF16), ffn_norm[i][None].astype(F32),
                              jnp.concatenate([w_r_hi, w_r_lo], axis=1), b_r)
        moe = _moe(h2, comb, w_gate[i].astype(BF16), w_up[i].astype(BF16), w_down[i].astype(BF16))
        xf = _ple(x1, moe, p[i].reshape(n, -1).astype(F32), ple_norm[i][None].astype(F32), w_ple_gate[i].astype(BF16),
                  w_ple[i].astype(BF16), final_norm[None].astype(F32), last=(i == depth - 1))
    return xf.reshape(batch, seq, d).astype(x.dtype)
```

```python
import functools
import math

import jax
import jax.numpy as jnp
from jax import lax
from jax.experimental import pallas as pl
from jax.experimental.pallas import tpu as pltpu

D_MODEL = 1024
N_HEADS_A = 8
HEAD_DIM = 64
WIDTH_A = N_HEADS_A * HEAD_DIM
KV_RANK = 128
IDX_HEADS = 8
IDX_DIM = 64
TOPK_MAX = 256
N_HEADS_B = 8
WIDTH_B = N_HEADS_B * HEAD_DIM
N_BUCKETS = 32
MAX_EXACT = N_BUCKETS // 2
MAX_DISTANCE = 128
ATTN_SCALE = HEAD_DIM ** -0.5
IDX_SCALE = (IDX_HEADS ** -0.5) * (IDX_DIM ** -0.5)
N_GROUPS = 4
EXPERTS_PER_GROUP = 8
N_EXPERTS = N_GROUPS * EXPERTS_PER_GROUP
D_EXPERT = 256
PLE_DIM = 256
EPS = 1e-6
LOG2E = 1.4426950408889634

LANES = 128
ROW_TILE = 512
MOE_ROW_TILE = 1024
PLE_ROW_TILE = 1024
MOE_SUBTILES = 2
MOE_CHUNK = 160
MOE_RANK_BLOCK = 256
ATT_TILE = 256
VMEM_LIMIT = 56 * 1024 * 1024
NEG_BIG = -1e30
EXP_UNDERFLOW = -173.0
KEY_TOP = 127.0
KEY_MIN_RANGE = 1e-30

F32 = jnp.float32
BF16 = jnp.bfloat16


def _rms(x, g):
    return x * lax.rsqrt(jnp.mean(x * x, axis=-1, keepdims=True) + EPS) * g


def _dot(a, b):
    return jnp.dot(a, b, preferred_element_type=F32)


def _dot_nt(a, b):
    return lax.dot_general(a, b, (((1,), (1,)), ((), ())), preferred_element_type=F32)


def _full_spec(shape):
    nd = len(shape)
    return pl.BlockSpec(shape, lambda *_: (0,) * nd)


def _proj_kernel(x_ref, g_ref, wqa_ref, wuk_ref, wcw_ref, kvg_ref, wqi_ref, wkk_ref,
                 wqkv_ref, wga_ref, wgb_ref,
                 qabs_ref, ckv_ref, ckvt_ref, qidx_ref, kk_ref, widx_ref, qb_ref, kb_ref, vb_ref, sa_ref, sb_ref):
    h = _rms(x_ref[...], g_ref[...]).astype(BF16)
    qa = _dot(h, wqa_ref[...]).astype(BF16)
    half = WIDTH_A // 2
    qabs = jnp.concatenate([_dot(qa[:, :half], wuk_ref[:half, :N_HEADS_A * KV_RANK // 2]),
                            _dot(qa[:, half:], wuk_ref[half:, N_HEADS_A * KV_RANK // 2:])], axis=1)
    qabs_ref[...] = (qabs * (ATTN_SCALE * LOG2E)).astype(BF16)
    cw = _dot(h, wcw_ref[...])
    c = _rms(cw[:, :KV_RANK], kvg_ref[...])
    ckv_ref[...] = c.astype(BF16)
    c_ext = jnp.concatenate([c, jnp.ones_like(c)], axis=1)
    for j in range(ckvt_ref.shape[0]):
        ckvt_ref[j] = jnp.transpose(c_ext[j * ATT_TILE:(j + 1) * ATT_TILE, :]).astype(BF16)
    qidx_ref[...] = _dot(h, wqi_ref[...]).astype(BF16)
    kk_ref[...] = _dot(h, wkk_ref[...]).astype(BF16)
    widx_ref[...] = cw[:, KV_RANK:] * IDX_SCALE
    qkv = _dot(h, wqkv_ref[...])
    qb_ref[...] = (qkv[:, :WIDTH_B] * (ATTN_SCALE * LOG2E)).astype(BF16)
    kb_ref[...] = qkv[:, WIDTH_B:2 * WIDTH_B].astype(BF16)
    vb_ref[...] = qkv[:, 2 * WIDTH_B:].astype(BF16)
    sa_ref[...] = jax.nn.sigmoid(_dot(h, wga_ref[...])).astype(BF16)
    sb_ref[...] = jax.nn.sigmoid(_dot(h, wgb_ref[...])).astype(BF16)


def _proj(x, g, wqa, wuk, wcw, kvg, wqi, wkk, wqkv, wga, wgb):
    n, d = x.shape
    tm = ROW_TILE
    row = lambda w: pl.BlockSpec((tm, w), lambda i: (i, 0))
    ws = (g, wqa, wuk, wcw, kvg, wqi, wkk, wqkv, wga, wgb)
    outs = [(N_HEADS_A * KV_RANK, BF16), (KV_RANK, BF16), None, (IDX_HEADS * IDX_DIM, BF16), (LANES, BF16),
            (LANES, F32), (WIDTH_B, BF16), (WIDTH_B, BF16), (WIDTH_B, BF16), (d, BF16), (d, BF16)]
    t = ATT_TILE
    ckvt_spec = pl.BlockSpec((tm // t, 2 * KV_RANK, t), lambda i: (i, 0, 0))
    ckvt_shape = jax.ShapeDtypeStruct((n // t, 2 * KV_RANK, t), BF16)
    return pl.pallas_call(
        _proj_kernel,
        grid=(n // tm,),
        in_specs=[row(d)] + [_full_spec(w.shape) for w in ws],
        out_specs=[ckvt_spec if o is None else row(o[0]) for o in outs],
        out_shape=[ckvt_shape if o is None else jax.ShapeDtypeStruct((n, o[0]), o[1]) for o in outs],
        compiler_params=pltpu.CompilerParams(dimension_semantics=("arbitrary",), vmem_limit_bytes=VMEM_LIMIT),
        name="proj",
    )(x, *ws)


def _t5_bucket(dist):
    dist = jnp.maximum(dist, 0)
    d_f = jnp.maximum(dist, 1).astype(F32)
    large = MAX_EXACT + jnp.floor(jnp.log(d_f / MAX_EXACT) / math.log(MAX_DISTANCE / MAX_EXACT)
                                  * (N_BUCKETS - MAX_EXACT)).astype(jnp.int32)
    large = jnp.minimum(large, N_BUCKETS - 1)
    return jnp.where(dist < MAX_EXACT, dist, large)


def _dsa_kernel(relb_ref, qabs_ref, qidx_ref, widx_ref, ckv_ref, ckvt_ref, kk_ref, o_ref,
                qa_s, qi_s, key_s, dig_s, act_s, cs_s, mn_s, mx_s, bias_s, m_s, acc_s, sa_s, sb_s, ma_s, mb_s, *, n_sel, seq):
    tq = tk = ATT_TILE
    nh = N_HEADS_A
    nhq = nh * tq
    qi = pl.program_id(1)
    n_kt = qi + 1
    kpos = lax.broadcasted_iota(jnp.int32, (tk, tq), 0)
    qpos = lax.broadcasted_iota(jnp.int32, (tk, tq), 1)

    @pl.when((pl.program_id(0) == 0) & (qi == 0))
    def _():
        for t in range(3):
            bucket = _t5_bucket(t * tq + qpos - kpos if t < 2 else jnp.full((tk, tq), 2 * tq, jnp.int32))
            for h in range(nh):
                b = jnp.zeros((tk, tq), F32)
                for bk in range(N_BUCKETS):
                    b = jnp.where(bucket == bk, relb_ref[bk, h], b)
                bias_s[t, :, h * tq:(h + 1) * tq] = b * LOG2E

    lane = lax.broadcasted_iota(jnp.int32, (tq, LANES), 1)
    lo_half = jnp.where(lane < IDX_DIM, 1.0, 0.0)
    hi_half = 1.0 - lo_half
    for h in range(nh):
        qa_s[h * tq:(h + 1) * tq, :] = qabs_ref[:, h * KV_RANK:(h + 1) * KV_RANK]
        pair = qidx_ref[:, (h // 2) * LANES:(h // 2 + 1) * LANES].astype(F32)
        qi_s[h * tq:(h + 1) * tq, :] = (pair * (lo_half if h % 2 == 0 else hi_half)).astype(BF16)
    w_t = jnp.transpose(widx_ref[...])
    w_row = jnp.concatenate([w_t[h:h + 1, :] for h in range(nh)], axis=1)

    mn_s[...] = jnp.full(mn_s.shape, -NEG_BIG, F32)
    mx_s[...] = jnp.full(mx_s.shape, NEG_BIG, F32)

    def idx_dots(kj, r_ref):
        ks = pl.multiple_of(jnp.minimum(kj, n_kt - 1) * tk, tk)
        r_ref[...] = _dot_nt(kk_ref[pl.ds(ks, tk), :], qi_s[...])

    def idx_scores(kj, r_ref):
        r = jnp.maximum(r_ref[...], 0.0) * w_row
        s = r[:, :tq]
        for h in range(1, nh):
            s = s + r[:, h * tq:(h + 1) * tq]
        masked = (kpos > qpos) & (kj == qi)
        key_s[kj] = jnp.where(masked, NEG_BIG, s)
        mn_s[...] = jnp.minimum(mn_s[...], jnp.min(jnp.where(masked, -NEG_BIG, s).reshape(tk // 8, 8, tq), axis=0))
        mx_s[...] = jnp.maximum(mx_s[...], jnp.max(jnp.where(masked, NEG_BIG, s).reshape(tk // 8, 8, tq), axis=0))

    idx_dots(0, sa_s)

    def idx_pair(i, carry):
        idx_dots(2 * i + 1, sb_s)
        idx_scores(2 * i, sa_s)
        idx_dots(2 * i + 2, sa_s)
        idx_scores(2 * i + 1, sb_s)
        return carry

    lax.fori_loop(0, n_kt // 2, idx_pair, 0)

    @pl.when(n_kt % 2 == 1)
    def _():
        idx_scores(n_kt - 1, sa_s)

    lo = jnp.min(mn_s[...], axis=0, keepdims=True)
    hi = jnp.max(mx_s[...], axis=0, keepdims=True)
    scale = KEY_TOP / jnp.maximum(hi - lo, KEY_MIN_RANGE)

    def key_body(kj, carry):
        y = jnp.maximum((key_s[kj] - lo) * scale, -1.0)
        rem = y
        for p in (3, 2, 1, 0):
            d = jnp.floor(rem)
            dig_s[p, kj] = d.astype(BF16)
            rem = (rem - d) * 256.0
        key_s[kj] = y - rem * (2.0 ** -32)
        return carry

    lax.fori_loop(0, n_kt, key_body, 0)

    n_search = jnp.where(qi == 0, 0, n_kt)

    def count_active(pred, then=None):
        one, zero = jnp.ones((), BF16), jnp.zeros((), BF16)

        def body(kj, acc):
            a = act_s[kj]
            c = jnp.where(pred(a), one, zero).reshape(tk // 16, 16, tq)
            part = c[0]
            for j in range(1, tk // 16):
                part = part + c[j]
            if then is not None:
                then(kj, a)
            return acc + part.astype(F32)
        acc = lax.fori_loop(0, n_search, body, jnp.zeros((16, tq), F32))
        return jnp.sum(acc, axis=0, keepdims=True)

    def to_digit(v):
        return v.astype(F32).astype(BF16)

    def digit_phase(p, prev_digit, target):
        if p == 3:
            def init_body(kj, carry):
                act_s[kj] = dig_s[3, kj]
                return carry
            lax.fori_loop(0, n_search, init_body, 0)
        else:
            prev = to_digit(prev_digit)

            def narrow(kj, a):
                act_s[kj] = jnp.where(a == prev, dig_s[p, kj], jnp.full((), -1.0, BF16))
            target = target - count_active(lambda a: a > prev, then=narrow)
        n_bits = 7 if p == 3 else 8

        def bit_body(i, dgt):
            cand = dgt | jnp.left_shift(jnp.int32(1), n_bits - 1 - i)
            cand_d = to_digit(cand)
            cnt = count_active(lambda a: a >= cand_d)
            return jnp.where(cnt >= target, cand, dgt)
        return lax.fori_loop(0, n_bits, bit_body, jnp.zeros((1, tq), jnp.int32)), target

    target = jnp.full((1, tq), float(n_sel), F32)
    tau = jnp.zeros((1, tq), F32)
    digit = None
    for p in (3, 2, 1):
        digit, target = digit_phase(p, digit, target)
        tau = tau + digit.astype(F32) * (256.0 ** (p - 3))
    digit_d = to_digit(digit)
    excess = count_active(lambda a: a >= digit_d) - target

    def count_keys(pred):
        def body(kj, acc):
            c = jnp.where(pred(key_s[kj], kpos + kj * tk), 1.0, 0.0)
            return acc + jnp.sum(c.reshape(tk // 8, 8, tq), axis=0)
        acc = lax.fori_loop(0, n_kt, body, jnp.zeros((8, tq), F32))
        return jnp.sum(acc, axis=0, keepdims=True)

    first = qi == 0
    cs_s[...] = jnp.where(lax.broadcasted_iota(jnp.int32, cs_s.shape, 0) == 0, seq, 0)

    @pl.when(jnp.logical_not(first) & (jnp.max(excess) > 0.0))
    def _():
        digit0, target0 = digit_phase(0, digit, target)
        cs_s[1:2, :] = digit0
        tau0 = tau + digit0.astype(F32) * (256.0 ** -3)
        last = to_digit(digit0)
        n_gt = count_active(lambda a: a > last)
        n_eq = count_active(lambda a: a == last)
        need = target0 - n_gt

        @pl.when(jnp.max(n_eq - need) > 0.0)
        def _():
            nbits = seq.bit_length() - 1

            def tie_body(i, d):
                cand = d | jnp.left_shift(jnp.int32(1), nbits - 1 - i)
                cnt = count_keys(lambda k, pos: (k == tau0) & (pos < cand))
                return jnp.where(cnt < need, cand, d)

            cs_s[0:1, :] = lax.fori_loop(0, nbits, tie_body, jnp.zeros((1, tq), jnp.int32))

    tau = jnp.where(first, -0.5, tau + cs_s[1:2, :].astype(F32) * (256.0 ** -3))
    cstar = jnp.where(first, -1, cs_s[0:1, :])

    m_s[...] = jnp.full(m_s.shape, NEG_BIG, F32)
    acc_s[...] = jnp.zeros(acc_s.shape, F32)

    def scores(kj, s_ref, mx_ref):
        kc = jnp.minimum(kj, n_kt - 1)
        ks = pl.multiple_of(kc * tk, tk)
        k = key_s[kc]
        sel = ((k > tau) | ((k == tau) & (kpos + kc * tk <= cstar))) & (kj < n_kt)
        selb = jnp.where(sel, 0.0, NEG_BIG)
        s = _dot_nt(ckv_ref[pl.ds(ks, tk), :], qa_s[...])
        s = s + bias_s[jnp.minimum(qi - kc, 2)] + jnp.concatenate([selb] * nh, axis=1)
        s_ref[...] = s
        mx_ref[...] = jnp.max(s.reshape(tk // 8, 8, nhq), axis=0)

    def update(kj, s_ref, mx_ref):
        m_old = m_s[0:1, :]
        m_new = jnp.maximum(m_old, jnp.max(mx_ref[...], axis=0, keepdims=True))
        alpha = jnp.exp2(m_old - m_new)
        p = jnp.exp2(s_ref[...] - m_new).astype(BF16)
        acc_s[...] = acc_s[...] * alpha + _dot(ckvt_ref[jnp.minimum(kj, n_kt - 1)], p)
        m_s[...] = jnp.broadcast_to(m_new, m_s.shape)

    scores(0, sa_s, ma_s)

    def pair_body(i, carry):
        scores(2 * i + 1, sb_s, mb_s)
        update(2 * i, sa_s, ma_s)
        scores(2 * i + 2, sa_s, ma_s)
        update(2 * i + 1, sb_s, mb_s)
        return carry

    lax.fori_loop(0, n_kt // 2, pair_body, 0)

    @pl.when(n_kt % 2 == 1)
    def _():
        update(n_kt - 1, sa_s, ma_s)

    acc = acc_s[...]
    o_t = acc[:KV_RANK] / acc[KV_RANK:]
    for h in range(nh):
        o_ref[:, h * KV_RANK:(h + 1) * KV_RANK] = jnp.transpose(o_t[:, h * tq:(h + 1) * tq]).astype(BF16)


def _dsa(rel_bias, qabs, qidx, widx, ckv, ckvt, kk, batch, seq):
    n = qabs.shape[0]
    tq = ATT_TILE
    nq = seq // tq
    n_sel = min(TOPK_MAX, seq // 4)
    assert n_sel == tq and seq % tq == 0 and seq & (seq - 1) == 0
    qrow = lambda w: pl.BlockSpec((tq, w), lambda b, q: (b * nq + q, 0))
    brow = lambda w: pl.BlockSpec((seq, w), lambda b, q: (b, 0))
    nhq = N_HEADS_A * tq
    return pl.pallas_call(
        functools.partial(_dsa_kernel, n_sel=n_sel, seq=seq),
        grid=(batch, nq),
        in_specs=[pl.BlockSpec(memory_space=pltpu.SMEM),
                  qrow(N_HEADS_A * KV_RANK), qrow(IDX_HEADS * IDX_DIM), qrow(LANES),
                  brow(KV_RANK), pl.BlockSpec((nq, 2 * KV_RANK, tq), lambda b, q: (b, 0, 0)), brow(LANES)],
        out_specs=qrow(N_HEADS_A * KV_RANK),
        out_shape=jax.ShapeDtypeStruct((n, N_HEADS_A * KV_RANK), BF16),
        scratch_shapes=[pltpu.VMEM((nhq, KV_RANK), BF16), pltpu.VMEM((nhq, LANES), BF16),
                        pltpu.VMEM((nq, tq, tq), F32), pltpu.VMEM((4, nq, tq, tq), BF16),
                        pltpu.VMEM((nq, tq, tq), BF16), pltpu.VMEM((8, tq), jnp.int32),
                        pltpu.VMEM((8, tq), F32), pltpu.VMEM((8, tq), F32),
                        pltpu.VMEM((3, tq, nhq), F32),
                        pltpu.VMEM((8, nhq), F32), pltpu.VMEM((2 * KV_RANK, nhq), F32),
                        pltpu.VMEM((tq, nhq), F32), pltpu.VMEM((tq, nhq), F32),
                        pltpu.VMEM((8, nhq), F32), pltpu.VMEM((8, nhq), F32)],
        compiler_params=pltpu.CompilerParams(dimension_semantics=("arbitrary", "arbitrary"),
                                             vmem_limit_bytes=VMEM_LIMIT),
        name="dsa",
    )(rel_bias, qabs, qidx, widx, ckv, ckvt, kk)


def _stick_kernel(q_ref, k_ref, v_ref, o_ref):
    tq = tk = ATT_TILE
    qi = pl.program_id(1)
    npair = WIDTH_B // LANES
    lane = lax.broadcasted_iota(jnp.int32, (tq, LANES), 1)
    row = lax.broadcasted_iota(jnp.int32, (tq, tk), 0)
    col = lax.broadcasted_iota(jnp.int32, (tq, tk), 1)
    causal = col < row
    upper = jnp.where(row > col, 1.0, 0.0).astype(BF16)
    upper_ones = jnp.concatenate([upper, jnp.ones((tk, LANES), BF16)], axis=1)
    lo_half = jnp.where(lane < HEAD_DIM, 1.0, 0.0)
    causal2 = jnp.concatenate([causal, causal], axis=0)
    q_pairs = []
    for p in range(npair):
        q2 = q_ref[:, p * LANES:(p + 1) * LANES].astype(F32)
        q_pairs.append(jnp.concatenate([q2 * lo_half, q2 * (1.0 - lo_half)], axis=0).astype(BF16))

    def tile(kj, r_sums, outs, masked):
        ks = pl.multiple_of(kj * tk, tk)
        log_sig, log_1m = [], []
        for p in range(npair):
            z = _dot_nt(q_pairs[p], k_ref[pl.ds(ks, tk), p * LANES:(p + 1) * LANES])
            sp = jnp.maximum(z, 0.0) + jnp.log2(1.0 + jnp.exp2(-jnp.abs(z)))
            lm = jnp.where(causal2, -sp, 0.0) if masked else -sp
            log_sig.append(z - sp)
            log_1m.append(lm.astype(BF16))
        sums = _dot(jnp.concatenate(log_1m, axis=0), upper_ones)
        new_r, new_o = [], []
        for p in range(npair):
            sm = sums[p * 2 * tq:(p + 1) * 2 * tq]
            after = sm[:, :tk] + jnp.concatenate([r_sums[p]] * (tk // LANES), axis=1)
            a = jnp.exp2(log_sig[p] + after)
            if masked:
                a = jnp.where(causal2, a, 0.0)
            res = _dot(a.astype(BF16), v_ref[pl.ds(ks, tk), p * LANES:(p + 1) * LANES])
            new_o.append(outs[p] + jnp.where(lane < HEAD_DIM, res[:tq], res[tq:]))
            new_r.append(r_sums[p] + sm[:, tk:])
        return tuple(new_r), tuple(new_o)

    zeros = jnp.zeros((tq, LANES), F32)
    r_sums, outs = tile(qi, (jnp.zeros((2 * tq, LANES), F32),) * npair, (zeros,) * npair, True)

    def cond(c):
        it, r_sums, _ = c
        return (it < qi) & (jnp.max(functools.reduce(jnp.maximum, r_sums)) > EXP_UNDERFLOW)

    def body(c):
        it, r_sums, outs = c
        r_sums, outs = tile(qi - 1 - it, r_sums, outs, False)
        return it + 1, r_sums, outs

    _, _, outs = lax.while_loop(cond, body, (jnp.int32(0), r_sums, outs))
    for p in range(npair):
        o_ref[:, p * LANES:(p + 1) * LANES] = outs[p].astype(BF16)


def _stick(qb, kb, vb, batch, seq):
    n = qb.shape[0]
    tq = ATT_TILE
    nq = seq // tq
    qspec = pl.BlockSpec((tq, WIDTH_B), lambda b, q: (b * nq + q, 0))
    kspec = pl.BlockSpec((seq, WIDTH_B), lambda b, q: (b, 0))
    return pl.pallas_call(
        _stick_kernel,
        grid=(batch, nq),
        in_specs=[qspec, kspec, kspec],
        out_specs=qspec,
        out_shape=jax.ShapeDtypeStruct((n, WIDTH_B), BF16),
        compiler_params=pltpu.CompilerParams(dimension_semantics=("arbitrary",) * 2, vmem_limit_bytes=VMEM_LIMIT),
        name="stick",
    )(qb, kb, vb)


def _merge_kernel(x_ref, ol_ref, ob_ref, sa_ref, sb_ref, wuv_ref, wba_ref, wbb_ref, wout_ref, g_ref,
                  wr_ref, br_ref, x1_ref, h2_ref, comb_ref):
    half = N_HEADS_A * KV_RANK // 2
    oa = jnp.concatenate([_dot(ol_ref[:, :half], wuv_ref[:half, :WIDTH_A // 2]),
                          _dot(ol_ref[:, half:], wuv_ref[half:, WIDTH_A // 2:])], axis=1).astype(BF16)
    ya = _dot(oa, wba_ref[...])
    yb = _dot(ob_ref[...], wbb_ref[...])
    merged = sa_ref[...].astype(F32) * ya + sb_ref[...].astype(F32) * yb
    x1 = x_ref[...] + _dot(merged.astype(BF16), wout_ref[...])
    x1_ref[...] = x1
    h2 = _rms(x1, g_ref[...])
    h2_hi = h2.astype(BF16)
    h2_ref[...] = h2_hi
    h2_lo = (h2 - h2_hi.astype(F32)).astype(BF16)
    hi_part = _dot(h2_hi, wr_ref[...])
    logits = (hi_part[:, :LANES] + _dot(h2_lo, wr_ref[:, :LANES]) + hi_part[:, LANES:]) + br_ref[...]
    lane = lax.broadcasted_iota(jnp.int32, logits.shape, 1).astype(F32)
    ninf = -jnp.inf

    def first_max(v):
        m = jnp.max(v, axis=1, keepdims=True)
        return m, jnp.min(jnp.where(v == m, lane, 1e9), axis=1, keepdims=True)

    gmask = (lane >= N_EXPERTS) & (lane < N_EXPERTS + N_GROUPS)
    gl = jnp.where(gmask, logits, ninf)
    gmax, gidx = first_max(gl)
    p_g = 1.0 / jnp.sum(jnp.where(gmask, jnp.exp(gl - gmax), 0.0), axis=1, keepdims=True)
    e_lo = (gidx - N_EXPERTS) * EXPERTS_PER_GROUP
    el = jnp.where((lane >= e_lo) & (lane < e_lo + EXPERTS_PER_GROUP), logits, ninf)
    v1, i1 = first_max(el)
    el2 = jnp.where(lane == i1, ninf, el)
    v2, i2 = first_max(el2)
    e2 = jnp.exp(v2 - v1)
    w1 = 1.0 / (1.0 + e2)
    comb_ref[...] = jnp.where(lane == i1, w1 * p_g,
                              jnp.where(lane == i2, (e2 * w1) * p_g, jnp.where(lane == gidx, 1.0, 0.0)))


def _merge(x, olat, ob, sa, sb, wuv, wba, wbb, wout, g, wr, br):
    n, d = x.shape
    tm = ROW_TILE
    row = lambda w: pl.BlockSpec((tm, w), lambda i: (i, 0))
    ws = (wuv, wba, wbb, wout, g, wr, br)
    return pl.pallas_call(
        _merge_kernel,
        grid=(n // tm,),
        in_specs=[row(d), row(olat.shape[1]), row(ob.shape[1]), row(d), row(d)] + [_full_spec(w.shape) for w in ws],
        out_specs=[row(d), row(d), row(LANES)],
        out_shape=[jax.ShapeDtypeStruct((n, d), F32), jax.ShapeDtypeStruct((n, d), BF16),
                   jax.ShapeDtypeStruct((n, LANES), F32)],
        compiler_params=pltpu.CompilerParams(dimension_semantics=("arbitrary",), vmem_limit_bytes=VMEM_LIMIT),
        name="merge",
    )(x, olat, ob, sa, sb, *ws)


def _moe_kernel(h_ref, comb_ref, wg_ref, wu_ref, wd_ref, o_ref, slotc_s, slotr_s, split_s, acc_s, rng_s):
    t, d = h_ref.shape
    nsub = MOE_SUBTILES
    ts = t // nsub
    c = MOE_CHUNK
    rb = MOE_RANK_BLOCK
    g = pl.program_id(1)

    @pl.when(g == 0)
    def _():
        lane_b = lax.broadcasted_iota(jnp.int32, (rb, LANES), 1)
        group_lanes = (lane_b >= N_EXPERTS) & (lane_b < N_EXPERTS + N_GROUPS)
        r_i = lax.broadcasted_iota(jnp.int32, (rb, rb), 0)
        c_i = lax.broadcasted_iota(jnp.int32, (rb, rb), 1)
        lower = jnp.where(c_i < r_i, 1.0, 0.0).astype(BF16)
        lane1 = lax.broadcasted_iota(jnp.int32, (1, LANES), 1)
        for sub in range(nsub):
            counts = jnp.zeros((1, LANES), F32)
            ranks = []
            for blk in range(ts // rb):
                rows = slice(sub * ts + blk * rb, sub * ts + (blk + 1) * rb)
                oh = jnp.where(group_lanes, comb_ref[rows, :], 0.0)
                before = _dot(lower, oh.astype(BF16)) + counts
                ranks.append(jnp.sum(before * oh, axis=1, keepdims=True))
                counts = counts + jnp.sum(oh, axis=0, keepdims=True)
            start = jnp.int32(0)
            seg = jnp.zeros((1, LANES), F32)
            for k in range(N_GROUPS):
                n_k = jnp.sum(jnp.where(lane1 == N_EXPERTS + k, counts, 0.0)).astype(jnp.int32)
                seg = jnp.where(lane1 == N_EXPERTS + k, (start * c).astype(F32), seg)
                rng_s[sub * 2 * N_GROUPS + k] = start
                for j in range(-(-ts // c)):
                    start = start + (n_k > j * c).astype(jnp.int32)
                rng_s[sub * 2 * N_GROUPS + N_GROUPS + k] = start
            for blk in range(ts // rb):
                rows = slice(sub * ts + blk * rb, sub * ts + (blk + 1) * rb)
                oh = jnp.where(group_lanes, comb_ref[rows, :], 0.0)
                slot = ranks[blk] + jnp.sum(oh * seg, axis=1, keepdims=True)
                slotc_s[rows, :] = jnp.broadcast_to(slot, (rb, LANES))
        slotr_s[...] = jnp.transpose(slotc_s[...])
        cb = comb_ref[...]
        hi = cb.astype(BF16)
        split_s[...] = jnp.concatenate([hi, (cb - hi.astype(F32)).astype(BF16)], axis=1)
        acc_s[...] = jnp.zeros(acc_s.shape, F32)

    lane = lax.broadcasted_iota(jnp.int32, (nsub * c, LANES), 1)
    first = [rng_s[sub * 2 * N_GROUPS + g] for sub in range(nsub)]
    n_chunks = [rng_s[sub * 2 * N_GROUPS + N_GROUPS + g] - first[sub] for sub in range(nsub)]

    def chunk_body(j, carry):
        bases = [((first[sub] + j) * c).astype(F32) for sub in range(nsub)]
        xs, cw = [], []
        for sub in range(nsub):
            rows = slice(sub * ts, (sub + 1) * ts)
            row_slot = lax.broadcasted_iota(jnp.int32, (c, ts), 0).astype(F32) + bases[sub]
            pc = jnp.where(slotr_s[0:1, rows] == row_slot, 1.0, 0.0).astype(BF16)
            xs.append(_dot(pc, h_ref[rows, :]).astype(BF16))
            cw2 = _dot(pc, split_s[rows, :])
            cw.append(cw2[:, :LANES] + cw2[:, LANES:])
        xs = jnp.concatenate(xs, axis=0)
        cw = jnp.concatenate(cw, axis=0)
        y = jnp.zeros((nsub * c, d), F32)
        for e in range(EXPERTS_PER_GROUP):
            gate = _dot(xs, wg_ref[e])
            hid = gate * jax.nn.sigmoid(gate) * _dot(xs, wu_ref[e])
            ce = jnp.sum(jnp.where(lane == g * EXPERTS_PER_GROUP + e, cw, 0.0), axis=1, keepdims=True)
            y = y + _dot((hid * ce).astype(BF16), wd_ref[e])
        y = y.astype(BF16)
        for sub in range(nsub):
            rows = slice(sub * ts, (sub + 1) * ts)
            col_slot = lax.broadcasted_iota(jnp.int32, (ts, c), 1).astype(F32) + bases[sub]
            pct = jnp.where(jnp.broadcast_to(slotc_s[rows, 0:1], (ts, c)) == col_slot, 1.0, 0.0).astype(BF16)
            acc_s[rows, :] += _dot(pct, y[sub * c:(sub + 1) * c])
        return carry

    lax.fori_loop(0, functools.reduce(jnp.maximum, n_chunks), chunk_body, 0)

    @pl.when(g == N_GROUPS - 1)
    def _():
        o_ref[...] = acc_s[...].astype(BF16)


def _moe(h2, comb, wg, wu, wd):
    n, d = h2.shape
    tm = MOE_ROW_TILE
    row = lambda w: pl.BlockSpec((tm, w), lambda i, g: (i, 0))
    return pl.pallas_call(
        _moe_kernel,
        grid=(n // tm, N_GROUPS),
        in_specs=[row(d), row(LANES),
                  pl.BlockSpec((EXPERTS_PER_GROUP, d, D_EXPERT), lambda i, g: (g, 0, 0)),
                  pl.BlockSpec((EXPERTS_PER_GROUP, d, D_EXPERT), lambda i, g: (g, 0, 0)),
                  pl.BlockSpec((EXPERTS_PER_GROUP, D_EXPERT, d), lambda i, g: (g, 0, 0))],
        out_specs=row(d),
        out_shape=jax.ShapeDtypeStruct((n, d), BF16),
        scratch_shapes=[pltpu.VMEM((tm, LANES), F32), pltpu.VMEM((LANES, tm), F32),
                        pltpu.VMEM((tm, 2 * LANES), BF16), pltpu.VMEM((tm, d), F32),
                        pltpu.SMEM((MOE_SUBTILES * 2 * N_GROUPS,), jnp.int32)],
        compiler_params=pltpu.CompilerParams(dimension_semantics=("arbitrary", "arbitrary"),
                                             vmem_limit_bytes=VMEM_LIMIT),
        name="moe",
    )(h2, comb, wg, wu, wd)


def _ple_kernel(x_ref, m_ref, p_ref, g_ref, wpg_ref, wple_ref, gf_ref, o_ref, *, last):
    x2 = x_ref[...] + m_ref[...].astype(F32)
    h3 = _rms(x2, g_ref[...]).astype(BF16)
    gate = jax.nn.sigmoid(_dot(h3, wpg_ref[...]))
    x3 = x2 + _dot(p_ref[...].astype(BF16), wple_ref[...]) * gate
    o_ref[...] = _rms(x3, gf_ref[...]) if last else x3


def _ple(x1, moe, p, g, wpg, wple, gf, last):
    n, d = x1.shape
    tm = PLE_ROW_TILE
    row = lambda w: pl.BlockSpec((tm, w), lambda i: (i, 0))
    ws = (g, wpg, wple, gf)
    return pl.pallas_call(
        functools.partial(_ple_kernel, last=last),
        grid=(n // tm,),
        in_specs=[row(d), row(d), row(p.shape[1])] + [_full_spec(w.shape) for w in ws],
        out_specs=row(d),
        out_shape=jax.ShapeDtypeStruct((n, d), F32),
        compiler_params=pltpu.CompilerParams(dimension_semantics=("arbitrary",), vmem_limit_bytes=VMEM_LIMIT),
        name="ple",
    )(x1, moe, p, *ws)


def _block_diag(blocks):
    h, r, c = blocks.shape
    eye = jnp.eye(h, dtype=blocks.dtype)
    return (eye[:, None, :, None] * blocks[:, :, None, :]).reshape(h * r, h * c)


def kernel(x, p, attn_norm, w_in, kv_norm, w_uk, w_uv, rel_bias, w_branch_a, w_branch_b, w_out, ffn_norm,
           w_r1, b_r1, w_r2, b_r2, w_gate, w_up, w_down, ple_norm, w_ple_gate, w_ple, final_norm):
    batch, seq, d = x.shape
    n = batch * seq
    depth = w_in.shape[0]
    xf = x.reshape(n, d).astype(F32)
    widths = [WIDTH_A, KV_RANK, IDX_HEADS * IDX_DIM, IDX_DIM, IDX_HEADS, 3 * WIDTH_B, d, d]
    starts = [sum(widths[:k]) for k in range(len(widths))]
    for i in range(depth):
        w_qa, w_ckv, w_qi, w_ki, w_wi, w_qkv, w_ga, w_gb = [
            w_in[i][:, s:s + w].astype(BF16) for s, w in zip(starts, widths)]
        w_kk = jnp.concatenate([w_ki, w_ki], axis=1)
        w_cw = jnp.concatenate([w_ckv, jnp.pad(w_wi, ((0, 0), (0, LANES - IDX_HEADS)))], axis=1)
        wuk_bd = _block_diag(jnp.swapaxes(w_uk[i], 1, 2)).astype(BF16)
        wuv_bd = _block_diag(w_uv[i]).astype(BF16)
        qabs, ckv, ckvt, qidx, kk, widx, qb, kb, vb, sa, sb = _proj(
            xf, attn_norm[i][None].astype(F32), w_qa, wuk_bd, w_cw, kv_norm[i][None].astype(F32),
            w_qi, w_kk, w_qkv, w_ga, w_gb)
        olat = _dsa(rel_bias.astype(F32), qabs, qidx, widx, ckv, ckvt, kk, batch, seq)
        ob = _stick(qb, kb, vb, batch, seq)
        w_r = jnp.concatenate([jnp.transpose(w_r2[i], (1, 0, 2)).reshape(d, N_EXPERTS), w_r1[i]], axis=1)
        w_r = jnp.pad(w_r.astype(F32), ((0, 0), (0, LANES - N_EXPERTS - N_GROUPS)))
        w_r_hi = w_r.astype(BF16)
        w_r_lo = (w_r - w_r_hi.astype(F32)).astype(BF16)
        b_r = jnp.pad(jnp.concatenate([b_r2[i].reshape(-1), b_r1[i]]).astype(F32),
                      (0, LANES - N_EXPERTS - N_GROUPS))[None]
        x1, h2, comb = _merge(xf, olat, ob, sa, sb, wuv_bd, w_branch_a[i].astype(BF16), w_branch_b[i].astype(BF16),
                              w_out[i].astype(BF16), ffn_norm[i][None].astype(F32),
                              jnp.concatenate([w_r_hi, w_r_lo], axis=1), b_r)
        moe = _moe(h2, comb, w_gate[i].astype(BF16), w_up[i].astype(BF16), w_down[i].astype(BF16))
        xf = _ple(x1, moe, p[i].reshape(n, -1).astype(F32), ple_norm[i][None].astype(F32), w_ple_gate[i].astype(BF16),
                  w_ple[i].astype(BF16), final_norm[None].astype(F32), last=(i == depth - 1))
    return xf.reshape(batch, seq, d).astype(x.dtype)
```

```python
import functools
import math

import jax
import jax.numpy as jnp
from jax import lax
from jax.experimental import pallas as pl
from jax.experimental.pallas import tpu as pltpu

D_MODEL = 1024
N_HEADS_A = 8
HEAD_DIM = 64
WIDTH_A = N_HEADS_A * HEAD_DIM
KV_RANK = 128
IDX_HEADS = 8
IDX_DIM = 64
TOPK_MAX = 256
N_HEADS_B = 8
WIDTH_B = N_HEADS_B * HEAD_DIM
N_BUCKETS = 32
MAX_EXACT = N_BUCKETS // 2
MAX_DISTANCE = 128
ATTN_SCALE = HEAD_DIM ** -0.5
IDX_SCALE = (IDX_HEADS ** -0.5) * (IDX_DIM ** -0.5)
N_GROUPS = 4
EXPERTS_PER_GROUP = 8
N_EXPERTS = N_GROUPS * EXPERTS_PER_GROUP
D_EXPERT = 256
PLE_DIM = 256
EPS = 1e-6
LOG2E = 1.4426950408889634

LANES = 128
ROW_TILE = 512
MOE_ROW_TILE = 2048
PLE_ROW_TILE = 1024
MOE_SUBTILES = 4
MOE_CHUNK = 160
MOE_RANK_BLOCK = 256
ATT_TILE = 256
VMEM_LIMIT = 56 * 1024 * 1024
MOE_VMEM_LIMIT = 60 * 1024 * 1024
NEG_BIG = -1e30
EXP_UNDERFLOW = -173.0
KEY_TOP = 127.0
KEY_MIN_RANGE = 1e-30

F32 = jnp.float32
BF16 = jnp.bfloat16


def _rms(x, g):
    return x * lax.rsqrt(jnp.mean(x * x, axis=-1, keepdims=True) + EPS) * g


def _dot(a, b):
    return jnp.dot(a, b, preferred_element_type=F32)


def _dot_nt(a, b):
    return lax.dot_general(a, b, (((1,), (1,)), ((), ())), preferred_element_type=F32)


def _full_spec(shape):
    nd = len(shape)
    return pl.BlockSpec(shape, lambda *_: (0,) * nd)


def _proj_kernel(x_ref, g_ref, wqa_ref, wuk_ref, wcw_ref, kvg_ref, wqi_ref, wkk_ref,
                 wqkv_ref, wga_ref, wgb_ref,
                 qabs_ref, ckv_ref, ckvt_ref, qidx_ref, kk_ref, widx_ref, qb_ref, kb_ref, vb_ref, sa_ref, sb_ref):
    h = _rms(x_ref[...], g_ref[...]).astype(BF16)
    qa = _dot(h, wqa_ref[...]).astype(BF16)
    half = WIDTH_A // 2
    qabs = jnp.concatenate([_dot(qa[:, :half], wuk_ref[:half, :N_HEADS_A * KV_RANK // 2]),
                            _dot(qa[:, half:], wuk_ref[half:, N_HEADS_A * KV_RANK // 2:])], axis=1)
    qabs_ref[...] = (qabs * (ATTN_SCALE * LOG2E)).astype(BF16)
    cw = _dot(h, wcw_ref[...])
    c = _rms(cw[:, :KV_RANK], kvg_ref[...])
    ckv_ref[...] = c.astype(BF16)
    c_ext = jnp.concatenate([c, jnp.ones_like(c)], axis=1)
    for j in range(ckvt_ref.shape[0]):
        ckvt_ref[j] = jnp.transpose(c_ext[j * ATT_TILE:(j + 1) * ATT_TILE, :]).astype(BF16)
    qidx_ref[...] = _dot(h, wqi_ref[...]).astype(BF16)
    kk_ref[...] = _dot(h, wkk_ref[...]).astype(BF16)
    widx_ref[...] = cw[:, KV_RANK:] * IDX_SCALE
    qkv = _dot(h, wqkv_ref[...])
    qb_ref[...] = (qkv[:, :WIDTH_B] * (ATTN_SCALE * LOG2E)).astype(BF16)
    kb_ref[...] = qkv[:, WIDTH_B:2 * WIDTH_B].astype(BF16)
    vb_ref[...] = qkv[:, 2 * WIDTH_B:].astype(BF16)
    sa_ref[...] = jax.nn.sigmoid(_dot(h, wga_ref[...])).astype(BF16)
    sb_ref[...] = jax.nn.sigmoid(_dot(h, wgb_ref[...])).astype(BF16)


def _proj(x, g, wqa, wuk, wcw, kvg, wqi, wkk, wqkv, wga, wgb):
    n, d = x.shape
    tm = ROW_TILE
    row = lambda w: pl.BlockSpec((tm, w), lambda i: (i, 0))
    ws = (g, wqa, wuk, wcw, kvg, wqi, wkk, wqkv, wga, wgb)
    outs = [(N_HEADS_A * KV_RANK, BF16), (KV_RANK, BF16), None, (IDX_HEADS * IDX_DIM, BF16), (LANES, BF16),
            (LANES, F32), (WIDTH_B, BF16), (WIDTH_B, BF16), (WIDTH_B, BF16), (d, BF16), (d, BF16)]
    t = ATT_TILE
    ckvt_spec = pl.BlockSpec((tm // t, 2 * KV_RANK, t), lambda i: (i, 0, 0))
    ckvt_shape = jax.ShapeDtypeStruct((n // t, 2 * KV_RANK, t), BF16)
    return pl.pallas_call(
        _proj_kernel,
        grid=(n // tm,),
        in_specs=[row(d)] + [_full_spec(w.shape) for w in ws],
        out_specs=[ckvt_spec if o is None else row(o[0]) for o in outs],
        out_shape=[ckvt_shape if o is None else jax.ShapeDtypeStruct((n, o[0]), o[1]) for o in outs],
        compiler_params=pltpu.CompilerParams(dimension_semantics=("arbitrary",), vmem_limit_bytes=VMEM_LIMIT),
        name="proj",
    )(x, *ws)


def _t5_bucket(dist):
    dist = jnp.maximum(dist, 0)
    d_f = jnp.maximum(dist, 1).astype(F32)
    large = MAX_EXACT + jnp.floor(jnp.log(d_f / MAX_EXACT) / math.log(MAX_DISTANCE / MAX_EXACT)
                                  * (N_BUCKETS - MAX_EXACT)).astype(jnp.int32)
    large = jnp.minimum(large, N_BUCKETS - 1)
    return jnp.where(dist < MAX_EXACT, dist, large)


def _dsa_kernel(relb_ref, qabs_ref, qidx_ref, widx_ref, ckv_ref, ckvt_ref, kk_ref, o_ref,
                qa_s, qi_s, key_s, dig_s, act_s, cs_s, mn_s, mx_s, bias_s, m_s, acc_s, sa_s, sb_s, ma_s, mb_s, *, n_sel, seq):
    tq = tk = ATT_TILE
    nh = N_HEADS_A
    nhq = nh * tq
    qi = pl.program_id(1)
    n_kt = qi + 1
    kpos = lax.broadcasted_iota(jnp.int32, (tk, tq), 0)
    qpos = lax.broadcasted_iota(jnp.int32, (tk, tq), 1)

    @pl.when((pl.program_id(0) == 0) & (qi == 0))
    def _():
        for t in range(3):
            bucket = _t5_bucket(t * tq + qpos - kpos if t < 2 else jnp.full((tk, tq), 2 * tq, jnp.int32))
            for h in range(nh):
                b = jnp.zeros((tk, tq), F32)
                for bk in range(N_BUCKETS):
                    b = jnp.where(bucket == bk, relb_ref[bk, h], b)
                bias_s[t, :, h * tq:(h + 1) * tq] = b * LOG2E

    lane = lax.broadcasted_iota(jnp.int32, (tq, LANES), 1)
    lo_half = jnp.where(lane < IDX_DIM, 1.0, 0.0)
    hi_half = 1.0 - lo_half
    for h in range(nh):
        qa_s[h * tq:(h + 1) * tq, :] = qabs_ref[:, h * KV_RANK:(h + 1) * KV_RANK]
        pair = qidx_ref[:, (h // 2) * LANES:(h // 2 + 1) * LANES].astype(F32)
        qi_s[h * tq:(h + 1) * tq, :] = (pair * (lo_half if h % 2 == 0 else hi_half)).astype(BF16)
    w_t = jnp.transpose(widx_ref[...])
    w_row = jnp.concatenate([w_t[h:h + 1, :] for h in range(nh)], axis=1)

    mn_s[...] = jnp.full(mn_s.shape, -NEG_BIG, F32)
    mx_s[...] = jnp.full(mx_s.shape, NEG_BIG, F32)

    def idx_dots(kj, r_ref):
        ks = pl.multiple_of(jnp.minimum(kj, n_kt - 1) * tk, tk)
        r_ref[...] = _dot_nt(kk_ref[pl.ds(ks, tk), :], qi_s[...])

    def idx_scores(kj, r_ref):
        r = jnp.maximum(r_ref[...], 0.0) * w_row
        s = r[:, :tq]
        for h in range(1, nh):
            s = s + r[:, h * tq:(h + 1) * tq]
        masked = (kpos > qpos) & (kj == qi)
        key_s[kj] = jnp.where(masked, NEG_BIG, s)
        mn_s[...] = jnp.minimum(mn_s[...], jnp.min(jnp.where(masked, -NEG_BIG, s).reshape(tk // 8, 8, tq), axis=0))
        mx_s[...] = jnp.maximum(mx_s[...], jnp.max(jnp.where(masked, NEG_BIG, s).reshape(tk // 8, 8, tq), axis=0))

    idx_dots(0, sa_s)

    def idx_pair(i, carry):
        idx_dots(2 * i + 1, sb_s)
        idx_scores(2 * i, sa_s)
        idx_dots(2 * i + 2, sa_s)
        idx_scores(2 * i + 1, sb_s)
        return carry

    lax.fori_loop(0, n_kt // 2, idx_pair, 0)

    @pl.when(n_kt % 2 == 1)
    def _():
        idx_scores(n_kt - 1, sa_s)

    lo = jnp.min(mn_s[...], axis=0, keepdims=True)
    hi = jnp.max(mx_s[...], axis=0, keepdims=True)
    scale = KEY_TOP / jnp.maximum(hi - lo, KEY_MIN_RANGE)

    def key_body(kj, carry):
        y = jnp.maximum((key_s[kj] - lo) * scale, -1.0)
        rem = y
        for p in (3, 2, 1, 0):
            d = jnp.floor(rem)
            dig_s[p, kj] = d.astype(BF16)
            rem = (rem - d) * 256.0
        key_s[kj] = y - rem * (2.0 ** -32)
        return carry

    lax.fori_loop(0, n_kt, key_body, 0)

    n_search = jnp.where(qi == 0, 0, n_kt)

    def count_active(pred, then=None):
        one, zero = jnp.ones((), BF16), jnp.zeros((), BF16)

        def body(kj, acc):
            a = act_s[kj]
            c = jnp.where(pred(a), one, zero).reshape(tk // 16, 16, tq)
            part = c[0]
            for j in range(1, tk // 16):
                part = part + c[j]
            if then is not None:
                then(kj, a)
            return acc + part.astype(F32)
        acc = lax.fori_loop(0, n_search, body, jnp.zeros((16, tq), F32))
        return jnp.sum(acc, axis=0, keepdims=True)

    def to_digit(v):
        return v.astype(F32).astype(BF16)

    def digit_phase(p, prev_digit, target):
        if p == 3:
            def init_body(kj, carry):
                act_s[kj] = dig_s[3, kj]
                return carry
            lax.fori_loop(0, n_search, init_body, 0)
        else:
            prev = to_digit(prev_digit)

            def narrow(kj, a):
                act_s[kj] = jnp.where(a == prev, dig_s[p, kj], jnp.full((), -1.0, BF16))
            target = target - count_active(lambda a: a > prev, then=narrow)
        n_bits = 7 if p == 3 else 8

        def bit_body(i, dgt):
            cand = dgt | jnp.left_shift(jnp.int32(1), n_bits - 1 - i)
            cand_d = to_digit(cand)
            cnt = count_active(lambda a: a >= cand_d)
            return jnp.where(cnt >= target, cand, dgt)
        return lax.fori_loop(0, n_bits, bit_body, jnp.zeros((1, tq), jnp.int32)), target

    target = jnp.full((1, tq), float(n_sel), F32)
    tau = jnp.zeros((1, tq), F32)
    digit = None
    for p in (3, 2, 1):
        digit, target = digit_phase(p, digit, target)
        tau = tau + digit.astype(F32) * (256.0 ** (p - 3))
    digit_d = to_digit(digit)
    excess = count_active(lambda a: a >= digit_d) - target

    def count_keys(pred):
        def body(kj, acc):
            c = jnp.where(pred(key_s[kj], kpos + kj * tk), 1.0, 0.0)
            return acc + jnp.sum(c.reshape(tk // 8, 8, tq), axis=0)
        acc = lax.fori_loop(0, n_kt, body, jnp.zeros((8, tq), F32))
        return jnp.sum(acc, axis=0, keepdims=True)

    first = qi == 0
    cs_s[...] = jnp.where(lax.broadcasted_iota(jnp.int32, cs_s.shape, 0) == 0, seq, 0)

    @pl.when(jnp.logical_not(first) & (jnp.max(excess) > 0.0))
    def _():
        digit0, target0 = digit_phase(0, digit, target)
        cs_s[1:2, :] = digit0
        tau0 = tau + digit0.astype(F32) * (256.0 ** -3)
        last = to_digit(digit0)
        n_gt = count_active(lambda a: a > last)
        n_eq = count_active(lambda a: a == last)
        need = target0 - n_gt

        @pl.when(jnp.max(n_eq - need) > 0.0)
        def _():
            nbits = seq.bit_length() - 1

            def tie_body(i, d):
                cand = d | jnp.left_shift(jnp.int32(1), nbits - 1 - i)
                cnt = count_keys(lambda k, pos: (k == tau0) & (pos < cand))
                return jnp.where(cnt < need, cand, d)

            cs_s[0:1, :] = lax.fori_loop(0, nbits, tie_body, jnp.zeros((1, tq), jnp.int32))

    tau = jnp.where(first, -0.5, tau + cs_s[1:2, :].astype(F32) * (256.0 ** -3))
    cstar = jnp.where(first, -1, cs_s[0:1, :])

    m_s[...] = jnp.full(m_s.shape, NEG_BIG, F32)
    acc_s[...] = jnp.zeros(acc_s.shape, F32)

    def scores(kj, s_ref, mx_ref):
        kc = jnp.minimum(kj, n_kt - 1)
        ks = pl.multiple_of(kc * tk, tk)
        k = key_s[kc]
        sel = ((k > tau) | ((k == tau) & (kpos + kc * tk <= cstar))) & (kj < n_kt)
        selb = jnp.where(sel, 0.0, NEG_BIG)
        s = _dot_nt(ckv_ref[pl.ds(ks, tk), :], qa_s[...])
        s = s + bias_s[jnp.minimum(qi - kc, 2)] + jnp.concatenate([selb] * nh, axis=1)
        s_ref[...] = s
        mx_ref[...] = jnp.max(s.reshape(tk // 8, 8, nhq), axis=0)

    def update(kj, s_ref, mx_ref):
        m_old = m_s[0:1, :]
        m_new = jnp.maximum(m_old, jnp.max(mx_ref[...], axis=0, keepdims=True))
        alpha = jnp.exp2(m_old - m_new)
        p = jnp.exp2(s_ref[...] - m_new).astype(BF16)
        acc_s[...] = acc_s[...] * alpha + _dot(ckvt_ref[jnp.minimum(kj, n_kt - 1)], p)
        m_s[...] = jnp.broadcast_to(m_new, m_s.shape)

    scores(0, sa_s, ma_s)

    def pair_body(i, carry):
        scores(2 * i + 1, sb_s, mb_s)
        update(2 * i, sa_s, ma_s)
        scores(2 * i + 2, sa_s, ma_s)
        update(2 * i + 1, sb_s, mb_s)
        return carry

    lax.fori_loop(0, n_kt // 2, pair_body, 0)

    @pl.when(n_kt % 2 == 1)
    def _():
        update(n_kt - 1, sa_s, ma_s)

    acc = acc_s[...]
    o_t = acc[:KV_RANK] / acc[KV_RANK:]
    for h in range(nh):
        o_ref[:, h * KV_RANK:(h + 1) * KV_RANK] = jnp.transpose(o_t[:, h * tq:(h + 1) * tq]).astype(BF16)


def _dsa(rel_bias, qabs, qidx, widx, ckv, ckvt, kk, batch, seq):
    n = qabs.shape[0]
    tq = ATT_TILE
    nq = seq // tq
    n_sel = min(TOPK_MAX, seq // 4)
    assert n_sel == tq and seq % tq == 0 and seq & (seq - 1) == 0
    qrow = lambda w: pl.BlockSpec((tq, w), lambda b, q: (b * nq + q, 0))
    brow = lambda w: pl.BlockSpec((seq, w), lambda b, q: (b, 0))
    nhq = N_HEADS_A * tq
    return pl.pallas_call(
        functools.partial(_dsa_kernel, n_sel=n_sel, seq=seq),
        grid=(batch, nq),
        in_specs=[pl.BlockSpec(memory_space=pltpu.SMEM),
                  qrow(N_HEADS_A * KV_RANK), qrow(IDX_HEADS * IDX_DIM), qrow(LANES),
                  brow(KV_RANK), pl.BlockSpec((nq, 2 * KV_RANK, tq), lambda b, q: (b, 0, 0)), brow(LANES)],
        out_specs=qrow(N_HEADS_A * KV_RANK),
        out_shape=jax.ShapeDtypeStruct((n, N_HEADS_A * KV_RANK), BF16),
        scratch_shapes=[pltpu.VMEM((nhq, KV_RANK), BF16), pltpu.VMEM((nhq, LANES), BF16),
                        pltpu.VMEM((nq, tq, tq), F32), pltpu.VMEM((4, nq, tq, tq), BF16),
                        pltpu.VMEM((nq, tq, tq), BF16), pltpu.VMEM((8, tq), jnp.int32),
                        pltpu.VMEM((8, tq), F32), pltpu.VMEM((8, tq), F32),
                        pltpu.VMEM((3, tq, nhq), F32),
                        pltpu.VMEM((8, nhq), F32), pltpu.VMEM((2 * KV_RANK, nhq), F32),
                        pltpu.VMEM((tq, nhq), F32), pltpu.VMEM((tq, nhq), F32),
                        pltpu.VMEM((8, nhq), F32), pltpu.VMEM((8, nhq), F32)],
        compiler_params=pltpu.CompilerParams(dimension_semantics=("arbitrary", "arbitrary"),
                                             vmem_limit_bytes=VMEM_LIMIT),
        name="dsa",
    )(rel_bias, qabs, qidx, widx, ckv, ckvt, kk)


def _stick_kernel(q_ref, k_ref, v_ref, o_ref):
    tq = tk = ATT_TILE
    qi = pl.program_id(1)
    npair = WIDTH_B // LANES
    lane = lax.broadcasted_iota(jnp.int32, (tq, LANES), 1)
    row = lax.broadcasted_iota(jnp.int32, (tq, tk), 0)
    col = lax.broadcasted_iota(jnp.int32, (tq, tk), 1)
    causal = col < row
    upper = jnp.where(row > col, 1.0, 0.0).astype(BF16)
    upper_ones = jnp.concatenate([upper, jnp.ones((tk, LANES), BF16)], axis=1)
    lo_half = jnp.where(lane < HEAD_DIM, 1.0, 0.0)
    causal2 = jnp.concatenate([causal, causal], axis=0)
    q_pairs = []
    for p in range(npair):
        q2 = q_ref[:, p * LANES:(p + 1) * LANES].astype(F32)
        q_pairs.append(jnp.concatenate([q2 * lo_half, q2 * (1.0 - lo_half)], axis=0).astype(BF16))

    def tile(kj, r_sums, outs, masked):
        ks = pl.multiple_of(kj * tk, tk)
        log_sig, log_1m = [], []
        for p in range(npair):
            z = _dot_nt(q_pairs[p], k_ref[pl.ds(ks, tk), p * LANES:(p + 1) * LANES])
            sp = jnp.maximum(z, 0.0) + jnp.log2(1.0 + jnp.exp2(-jnp.abs(z)))
            lm = jnp.where(causal2, -sp, 0.0) if masked else -sp
            log_sig.append(z - sp)
            log_1m.append(lm.astype(BF16))
        sums = _dot(jnp.concatenate(log_1m, axis=0), upper_ones)
        new_r, new_o = [], []
        for p in range(npair):
            sm = sums[p * 2 * tq:(p + 1) * 2 * tq]
            after = sm[:, :tk] + jnp.concatenate([r_sums[p]] * (tk // LANES), axis=1)
            a = jnp.exp2(log_sig[p] + after)
            if masked:
                a = jnp.where(causal2, a, 0.0)
            res = _dot(a.astype(BF16), v_ref[pl.ds(ks, tk), p * LANES:(p + 1) * LANES])
            new_o.append(outs[p] + jnp.where(lane < HEAD_DIM, res[:tq], res[tq:]))
            new_r.append(r_sums[p] + sm[:, tk:])
        return tuple(new_r), tuple(new_o)

    zeros = jnp.zeros((tq, LANES), F32)
    r_sums, outs = tile(qi, (jnp.zeros((2 * tq, LANES), F32),) * npair, (zeros,) * npair, True)

    def cond(c):
        it, r_sums, _ = c
        return (it < qi) & (jnp.max(functools.reduce(jnp.maximum, r_sums)) > EXP_UNDERFLOW)

    def body(c):
        it, r_sums, outs = c
        r_sums, outs = tile(qi - 1 - it, r_sums, outs, False)
        return it + 1, r_sums, outs

    _, _, outs = lax.while_loop(cond, body, (jnp.int32(0), r_sums, outs))
    for p in range(npair):
        o_ref[:, p * LANES:(p + 1) * LANES] = outs[p].astype(BF16)


def _stick(qb, kb, vb, batch, seq):
    n = qb.shape[0]
    tq = ATT_TILE
    nq = seq // tq
    qspec = pl.BlockSpec((tq, WIDTH_B), lambda b, q: (b * nq + q, 0))
    kspec = pl.BlockSpec((seq, WIDTH_B), lambda b, q: (b, 0))
    return pl.pallas_call(
        _stick_kernel,
        grid=(batch, nq),
        in_specs=[qspec, kspec, kspec],
        out_specs=qspec,
        out_shape=jax.ShapeDtypeStruct((n, WIDTH_B), BF16),
        compiler_params=pltpu.CompilerParams(dimension_semantics=("arbitrary",) * 2, vmem_limit_bytes=VMEM_LIMIT),
        name="stick",
    )(qb, kb, vb)


def _merge_kernel(x_ref, ol_ref, ob_ref, sa_ref, sb_ref, wuv_ref, wba_ref, wbb_ref, wout_ref, g_ref,
                  wr_ref, br_ref, x1_ref, h2_ref, comb_ref):
    half = N_HEADS_A * KV_RANK // 2
    oa = jnp.concatenate([_dot(ol_ref[:, :half], wuv_ref[:half, :WIDTH_A // 2]),
                          _dot(ol_ref[:, half:], wuv_ref[half:, WIDTH_A // 2:])], axis=1).astype(BF16)
    ya = _dot(oa, wba_ref[...])
    yb = _dot(ob_ref[...], wbb_ref[...])
    merged = sa_ref[...].astype(F32) * ya + sb_ref[...].astype(F32) * yb
    x1 = x_ref[...] + _dot(merged.astype(BF16), wout_ref[...])
    x1_ref[...] = x1
    h2 = _rms(x1, g_ref[...])
    h2_hi = h2.astype(BF16)
    h2_ref[...] = h2_hi
    h2_lo = (h2 - h2_hi.astype(F32)).astype(BF16)
    hi_part = _dot(h2_hi, wr_ref[...])
    logits = (hi_part[:, :LANES] + _dot(h2_lo, wr_ref[:, :LANES]) + hi_part[:, LANES:]) + br_ref[...]
    lane = lax.broadcasted_iota(jnp.int32, logits.shape, 1).astype(F32)
    ninf = -jnp.inf

    def first_max(v):
        m = jnp.max(v, axis=1, keepdims=True)
        return m, jnp.min(jnp.where(v == m, lane, 1e9), axis=1, keepdims=True)

    gmask = (lane >= N_EXPERTS) & (lane < N_EXPERTS + N_GROUPS)
    gl = jnp.where(gmask, logits, ninf)
    gmax, gidx = first_max(gl)
    p_g = 1.0 / jnp.sum(jnp.where(gmask, jnp.exp(gl - gmax), 0.0), axis=1, keepdims=True)
    e_lo = (gidx - N_EXPERTS) * EXPERTS_PER_GROUP
    el = jnp.where((lane >= e_lo) & (lane < e_lo + EXPERTS_PER_GROUP), logits, ninf)
    v1, i1 = first_max(el)
    el2 = jnp.where(lane == i1, ninf, el)
    v2, i2 = first_max(el2)
    e2 = jnp.exp(v2 - v1)
    w1 = 1.0 / (1.0 + e2)
    comb_ref[...] = jnp.where(lane == i1, w1 * p_g,
                              jnp.where(lane == i2, (e2 * w1) * p_g, jnp.where(lane == gidx, 1.0, 0.0)))


def _merge(x, olat, ob, sa, sb, wuv, wba, wbb, wout, g, wr, br):
    n, d = x.shape
    tm = ROW_TILE
    row = lambda w: pl.BlockSpec((tm, w), lambda i: (i, 0))
    ws = (wuv, wba, wbb, wout, g, wr, br)
    return pl.pallas_call(
        _merge_kernel,
        grid=(n // tm,),
        in_specs=[row(d), row(olat.shape[1]), row(ob.shape[1]), row(d), row(d)] + [_full_spec(w.shape) for w in ws],
        out_specs=[row(d), row(d), row(LANES)],
        out_shape=[jax.ShapeDtypeStruct((n, d), F32), jax.ShapeDtypeStruct((n, d), BF16),
                   jax.ShapeDtypeStruct((n, LANES), F32)],
        compiler_params=pltpu.CompilerParams(dimension_semantics=("arbitrary",), vmem_limit_bytes=VMEM_LIMIT),
        name="merge",
    )(x, olat, ob, sa, sb, *ws)


def _moe_kernel(h_ref, comb_ref, wg_ref, wu_ref, wd_ref, o_ref, slotc_s, slotr_s, split_s, acc_s, rng_s):
    t, d = h_ref.shape
    nsub = MOE_SUBTILES
    ts = t // nsub
    c = MOE_CHUNK
    rb = MOE_RANK_BLOCK
    g = pl.program_id(1)

    @pl.when(g == 0)
    def _():
        lane_b = lax.broadcasted_iota(jnp.int32, (rb, LANES), 1)
        group_lanes = (lane_b >= N_EXPERTS) & (lane_b < N_EXPERTS + N_GROUPS)
        r_i = lax.broadcasted_iota(jnp.int32, (rb, rb), 0)
        c_i = lax.broadcasted_iota(jnp.int32, (rb, rb), 1)
        lower = jnp.where(c_i < r_i, 1.0, 0.0).astype(BF16)
        lane1 = lax.broadcasted_iota(jnp.int32, (1, LANES), 1)
        for sub in range(nsub):
            counts = jnp.zeros((1, LANES), F32)
            ranks = []
            for blk in range(ts // rb):
                rows = slice(sub * ts + blk * rb, sub * ts + (blk + 1) * rb)
                oh = jnp.where(group_lanes, comb_ref[rows, :], 0.0)
                before = _dot(lower, oh.astype(BF16)) + counts
                ranks.append(jnp.sum(before * oh, axis=1, keepdims=True))
                counts = counts + jnp.sum(oh, axis=0, keepdims=True)
            start = jnp.int32(0)
            seg = jnp.zeros((1, LANES), F32)
            for k in range(N_GROUPS):
                n_k = jnp.sum(jnp.where(lane1 == N_EXPERTS + k, counts, 0.0)).astype(jnp.int32)
                seg = jnp.where(lane1 == N_EXPERTS + k, (start * c).astype(F32), seg)
                rng_s[sub * 2 * N_GROUPS + k] = start
                for j in range(-(-ts // c)):
                    start = start + (n_k > j * c).astype(jnp.int32)
                rng_s[sub * 2 * N_GROUPS + N_GROUPS + k] = start
            for blk in range(ts // rb):
                rows = slice(sub * ts + blk * rb, sub * ts + (blk + 1) * rb)
                oh = jnp.where(group_lanes, comb_ref[rows, :], 0.0)
                slot = ranks[blk] + jnp.sum(oh * seg, axis=1, keepdims=True)
                slotc_s[rows, :] = jnp.broadcast_to(slot, (rb, LANES))
        slotr_s[...] = jnp.transpose(slotc_s[...])
        cb = comb_ref[...]
        hi = cb.astype(BF16)
        split_s[...] = jnp.concatenate([hi, (cb - hi.astype(F32)).astype(BF16)], axis=1)
        acc_s[...] = jnp.zeros(acc_s.shape, F32)

    lane = lax.broadcasted_iota(jnp.int32, (nsub * c, LANES), 1)
    first = [rng_s[sub * 2 * N_GROUPS + g] for sub in range(nsub)]
    n_chunks = [rng_s[sub * 2 * N_GROUPS + N_GROUPS + g] - first[sub] for sub in range(nsub)]

    def chunk_body(j, carry):
        bases = [((first[sub] + j) * c).astype(F32) for sub in range(nsub)]
        xs, cw = [], []
        for sub in range(nsub):
            rows = slice(sub * ts, (sub + 1) * ts)
            row_slot = lax.broadcasted_iota(jnp.int32, (c, ts), 0).astype(F32) + bases[sub]
            pc = jnp.where(slotr_s[0:1, rows] == row_slot, 1.0, 0.0).astype(BF16)
            xs.append(_dot(pc, h_ref[rows, :]).astype(BF16))
            cw2 = _dot(pc, split_s[rows, :])
            cw.append(cw2[:, :LANES] + cw2[:, LANES:])
        xs = jnp.concatenate(xs, axis=0)
        cw = jnp.concatenate(cw, axis=0)
        y = jnp.zeros((nsub * c, d), F32)
        for e in range(EXPERTS_PER_GROUP):
            gate = _dot(xs, wg_ref[e])
            hid = gate * jax.nn.sigmoid(gate) * _dot(xs, wu_ref[e])
            ce = jnp.sum(jnp.where(lane == g * EXPERTS_PER_GROUP + e, cw, 0.0), axis=1, keepdims=True)
            y = y + _dot((hid * ce).astype(BF16), wd_ref[e])
        y = y.astype(BF16)
        for sub in range(nsub):
            rows = slice(sub * ts, (sub + 1) * ts)
            col_slot = lax.broadcasted_iota(jnp.int32, (ts, c), 1).astype(F32) + bases[sub]
            pct = jnp.where(jnp.broadcast_to(slotc_s[rows, 0:1], (ts, c)) == col_slot, 1.0, 0.0).astype(BF16)
            acc_s[rows, :] += _dot(pct, y[sub * c:(sub + 1) * c])
        return carry

    lax.fori_loop(0, functools.reduce(jnp.maximum, n_chunks), chunk_body, 0)

    @pl.when(g == N_GROUPS - 1)
    def _():
        o_ref[...] = acc_s[...].astype(BF16)


def _moe(h2, comb, wg, wu, wd):
    n, d = h2.shape
    tm = MOE_ROW_TILE
    row = lambda w: pl.BlockSpec((tm, w), lambda i, g: (i, 0))
    return pl.pallas_call(
        _moe_kernel,
        grid=(n // tm, N_GROUPS),
        in_specs=[row(d), row(LANES),
                  pl.BlockSpec((EXPERTS_PER_GROUP, d, D_EXPERT), lambda i, g: (g, 0, 0)),
                  pl.BlockSpec((EXPERTS_PER_GROUP, d, D_EXPERT), lambda i, g: (g, 0, 0)),
                  pl.BlockSpec((EXPERTS_PER_GROUP, D_EXPERT, d), lambda i, g: (g, 0, 0))],
        out_specs=row(d),
        out_shape=jax.ShapeDtypeStruct((n, d), BF16),
        scratch_shapes=[pltpu.VMEM((tm, LANES), F32), pltpu.VMEM((LANES, tm), F32),
                        pltpu.VMEM((tm, 2 * LANES), BF16), pltpu.VMEM((tm, d), F32),
                        pltpu.SMEM((MOE_SUBTILES * 2 * N_GROUPS,), jnp.int32)],
        compiler_params=pltpu.CompilerParams(dimension_semantics=("arbitrary", "arbitrary"),
                                             vmem_limit_bytes=MOE_VMEM_LIMIT),
        name="moe",
    )(h2, comb, wg, wu, wd)


def _ple_kernel(x_ref, m_ref, p_ref, g_ref, wpg_ref, wple_ref, gf_ref, o_ref, *, last):
    x2 = x_ref[...] + m_ref[...].astype(F32)
    h3 = _rms(x2, g_ref[...]).astype(BF16)
    gate = jax.nn.sigmoid(_dot(h3, wpg_ref[...]))
    x3 = x2 + _dot(p_ref[...].astype(BF16), wple_ref[...]) * gate
    o_ref[...] = _rms(x3, gf_ref[...]) if last else x3


def _ple(x1, moe, p, g, wpg, wple, gf, last):
    n, d = x1.shape
    tm = PLE_ROW_TILE
    row = lambda w: pl.BlockSpec((tm, w), lambda i: (i, 0))
    ws = (g, wpg, wple, gf)
    return pl.pallas_call(
        functools.partial(_ple_kernel, last=last),
        grid=(n // tm,),
        in_specs=[row(d), row(d), row(p.shape[1])] + [_full_spec(w.shape) for w in ws],
        out_specs=row(d),
        out_shape=jax.ShapeDtypeStruct((n, d), F32),
        compiler_params=pltpu.CompilerParams(dimension_semantics=("arbitrary",), vmem_limit_bytes=VMEM_LIMIT),
        name="ple",
    )(x1, moe, p, *ws)


def _block_diag(blocks):
    h, r, c = blocks.shape
    eye = jnp.eye(h, dtype=blocks.dtype)
    return (eye[:, None, :, None] * blocks[:, :, None, :]).reshape(h * r, h * c)


def kernel(x, p, attn_norm, w_in, kv_norm, w_uk, w_uv, rel_bias, w_branch_a, w_branch_b, w_out, ffn_norm,
           w_r1, b_r1, w_r2, b_r2, w_gate, w_up, w_down, ple_norm, w_ple_gate, w_ple, final_norm):
    batch, seq, d = x.shape
    n = batch * seq
    depth = w_in.shape[0]
    xf = x.reshape(n, d).astype(F32)
    widths = [WIDTH_A, KV_RANK, IDX_HEADS * IDX_DIM, IDX_DIM, IDX_HEADS, 3 * WIDTH_B, d, d]
    starts = [sum(widths[:k]) for k in range(len(widths))]
    for i in range(depth):
        w_qa, w_ckv, w_qi, w_ki, w_wi, w_qkv, w_ga, w_gb = [
            w_in[i][:, s:s + w].astype(BF16) for s, w in zip(starts, widths)]
        w_kk = jnp.concatenate([w_ki, w_ki], axis=1)
        w_cw = jnp.concatenate([w_ckv, jnp.pad(w_wi, ((0, 0), (0, LANES - IDX_HEADS)))], axis=1)
        wuk_bd = _block_diag(jnp.swapaxes(w_uk[i], 1, 2)).astype(BF16)
        wuv_bd = _block_diag(w_uv[i]).astype(BF16)
        qabs, ckv, ckvt, qidx, kk, widx, qb, kb, vb, sa, sb = _proj(
            xf, attn_norm[i][None].astype(F32), w_qa, wuk_bd, w_cw, kv_norm[i][None].astype(F32),
            w_qi, w_kk, w_qkv, w_ga, w_gb)
        olat = _dsa(rel_bias.astype(F32), qabs, qidx, widx, ckv, ckvt, kk, batch, seq)
        ob = _stick(qb, kb, vb, batch, seq)
        w_r = jnp.concatenate([jnp.transpose(w_r2[i], (1, 0, 2)).reshape(d, N_EXPERTS), w_r1[i]], axis=1)
        w_r = jnp.pad(w_r.astype(F32), ((0, 0), (0, LANES - N_EXPERTS - N_GROUPS)))
        w_r_hi = w_r.astype(BF16)
        w_r_lo = (w_r - w_r_hi.astype(F32)).astype(BF16)
        b_r = jnp.pad(jnp.concatenate([b_r2[i].reshape(-1), b_r1[i]]).astype(F32),
                      (0, LANES - N_EXPERTS - N_GROUPS))[None]
        x1, h2, comb = _merge(xf, olat, ob, sa, sb, wuv_bd, w_branch_a[i].astype(BF16), w_branch_b[i].astype(BF16),
                              w_out[i].astype(BF16), ffn_norm[i][None].astype(F32),
                              jnp.concatenate([w_r_hi, w_r_lo], axis=1), b_r)
        moe = _moe(h2, comb, w_gate[i].astype(BF16), w_up[i].astype(BF16), w_down[i].astype(BF16))
        xf = _ple(x1, moe, p[i].reshape(n, -1).astype(F32), ple_norm[i][None].astype(F32), w_ple_gate[i].astype(BF16),
                  w_ple[i].astype(BF16), final_norm[None].astype(F32), last=(i == depth - 1))
    return xf.reshape(batch, seq, d).astype(x.dtype)
```

```python
import functools
import math

import jax
import jax.numpy as jnp
from jax import lax
from jax.experimental import pallas as pl
from jax.experimental.pallas import tpu as pltpu

D_MODEL = 1024
N_HEADS_A = 8
HEAD_DIM = 64
WIDTH_A = N_HEADS_A * HEAD_DIM
KV_RANK = 128
IDX_HEADS = 8
IDX_DIM = 64
TOPK_MAX = 256
N_HEADS_B = 8
WIDTH_B = N_HEADS_B * HEAD_DIM
N_BUCKETS = 32
MAX_EXACT = N_BUCKETS // 2
MAX_DISTANCE = 128
ATTN_SCALE = HEAD_DIM ** -0.5
IDX_SCALE = (IDX_HEADS ** -0.5) * (IDX_DIM ** -0.5)
N_GROUPS = 4
EXPERTS_PER_GROUP = 8
N_EXPERTS = N_GROUPS * EXPERTS_PER_GROUP
D_EXPERT = 256
PLE_DIM = 256
EPS = 1e-6
LOG2E = 1.4426950408889634

LANES = 128
ROW_TILE = 1024
MOE_ROW_TILE = 2048
MOE_SUBTILES = 4
MOE_CHUNK = 160
MOE_RANK_BLOCK = 256
ATT_TILE = 256
STICK_Q_TILES = 2
VMEM_LIMIT = 56 * 1024 * 1024
MOE_VMEM_LIMIT = 60 * 1024 * 1024
NEG_BIG = -1e30
EXP_UNDERFLOW = -173.0
KEY_TOP = 127.0
KEY_MIN_RANGE = 1e-30

F32 = jnp.float32
BF16 = jnp.bfloat16


def _rms(x, g):
    return x * lax.rsqrt(jnp.mean(x * x, axis=-1, keepdims=True) + EPS) * g


def _dot(a, b):
    return jnp.dot(a, b, preferred_element_type=F32)


def _dot_nt(a, b):
    return lax.dot_general(a, b, (((1,), (1,)), ((), ())), preferred_element_type=F32)


def _full_spec(shape):
    nd = len(shape)
    return pl.BlockSpec(shape, lambda *_: (0,) * nd)


def _proj_kernel(x_ref, g_ref, wqa_ref, wuk_ref, wcw_ref, kvg_ref, wqi_ref, wkk_ref,
                 wqkv_ref, wga_ref, wgb_ref,
                 qabs_ref, ckv_ref, ckvt_ref, qidx_ref, kk_ref, widx_ref, qb_ref, kb_ref, vb_ref, sa_ref, sb_ref):
    h = _rms(x_ref[...], g_ref[...]).astype(BF16)
    qa = _dot(h, wqa_ref[...]).astype(BF16)
    half = WIDTH_A // 2
    qabs = jnp.concatenate([_dot(qa[:, :half], wuk_ref[:half, :N_HEADS_A * KV_RANK // 2]),
                            _dot(qa[:, half:], wuk_ref[half:, N_HEADS_A * KV_RANK // 2:])], axis=1)
    qabs_ref[...] = (qabs * (ATTN_SCALE * LOG2E)).astype(BF16)
    cw = _dot(h, wcw_ref[...])
    c = _rms(cw[:, :KV_RANK], kvg_ref[...])
    ckv_ref[...] = c.astype(BF16)
    c_ext = jnp.concatenate([c, jnp.ones_like(c)], axis=1)
    for j in range(ckvt_ref.shape[0]):
        ckvt_ref[j] = jnp.transpose(c_ext[j * ATT_TILE:(j + 1) * ATT_TILE, :]).astype(BF16)
    qidx_ref[...] = _dot(h, wqi_ref[...]).astype(BF16)
    kk_ref[...] = _dot(h, wkk_ref[...]).astype(BF16)
    widx_ref[...] = cw[:, KV_RANK:] * IDX_SCALE
    qkv = _dot(h, wqkv_ref[...])
    qb_ref[...] = (qkv[:, :WIDTH_B] * (ATTN_SCALE * LOG2E)).astype(BF16)
    kb_ref[...] = qkv[:, WIDTH_B:2 * WIDTH_B].astype(BF16)
    vb_ref[...] = qkv[:, 2 * WIDTH_B:].astype(BF16)
    sa_ref[...] = jax.nn.sigmoid(_dot(h, wga_ref[...])).astype(BF16)
    sb_ref[...] = jax.nn.sigmoid(_dot(h, wgb_ref[...])).astype(BF16)


def _proj(x, g, wqa, wuk, wcw, kvg, wqi, wkk, wqkv, wga, wgb):
    n, d = x.shape
    tm = ROW_TILE
    row = lambda w: pl.BlockSpec((tm, w), lambda i: (i, 0))
    ws = (g, wqa, wuk, wcw, kvg, wqi, wkk, wqkv, wga, wgb)
    outs = [(N_HEADS_A * KV_RANK, BF16), (KV_RANK, BF16), None, (IDX_HEADS * IDX_DIM, BF16), (LANES, BF16),
            (LANES, F32), (WIDTH_B, BF16), (WIDTH_B, BF16), (WIDTH_B, BF16), (d, BF16), (d, BF16)]
    t = ATT_TILE
    ckvt_spec = pl.BlockSpec((tm // t, 2 * KV_RANK, t), lambda i: (i, 0, 0))
    ckvt_shape = jax.ShapeDtypeStruct((n // t, 2 * KV_RANK, t), BF16)
    return pl.pallas_call(
        _proj_kernel,
        grid=(n // tm,),
        in_specs=[row(d)] + [_full_spec(w.shape) for w in ws],
        out_specs=[ckvt_spec if o is None else row(o[0]) for o in outs],
        out_shape=[ckvt_shape if o is None else jax.ShapeDtypeStruct((n, o[0]), o[1]) for o in outs],
        compiler_params=pltpu.CompilerParams(dimension_semantics=("arbitrary",), vmem_limit_bytes=VMEM_LIMIT),
        name="proj",
    )(x, *ws)


def _t5_bucket(dist):
    dist = jnp.maximum(dist, 0)
    d_f = jnp.maximum(dist, 1).astype(F32)
    large = MAX_EXACT + jnp.floor(jnp.log(d_f / MAX_EXACT) / math.log(MAX_DISTANCE / MAX_EXACT)
                                  * (N_BUCKETS - MAX_EXACT)).astype(jnp.int32)
    large = jnp.minimum(large, N_BUCKETS - 1)
    return jnp.where(dist < MAX_EXACT, dist, large)


def _dsa_kernel(relb_ref, qabs_ref, qidx_ref, widx_ref, ckv_ref, ckvt_ref, kk_ref, o_ref,
                qa_s, qi_s, key_s, dig_s, act_s, cs_s, mn_s, mx_s, bias_s, m_s, acc_s, sa_s, sb_s, ma_s, mb_s, *, n_sel, seq):
    tq = tk = ATT_TILE
    nh = N_HEADS_A
    nhq = nh * tq
    qi = pl.program_id(1)
    n_kt = qi + 1
    kpos = lax.broadcasted_iota(jnp.int32, (tk, tq), 0)
    qpos = lax.broadcasted_iota(jnp.int32, (tk, tq), 1)

    @pl.when((pl.program_id(0) == 0) & (qi == 0))
    def _():
        for t in range(3):
            bucket = _t5_bucket(t * tq + qpos - kpos if t < 2 else jnp.full((tk, tq), 2 * tq, jnp.int32))
            for h in range(nh):
                b = jnp.zeros((tk, tq), F32)
                for bk in range(N_BUCKETS):
                    b = jnp.where(bucket == bk, relb_ref[bk, h], b)
                bias_s[t, :, h * tq:(h + 1) * tq] = b * LOG2E

    lane = lax.broadcasted_iota(jnp.int32, (tq, LANES), 1)
    lo_half = jnp.where(lane < IDX_DIM, 1.0, 0.0)
    hi_half = 1.0 - lo_half
    for h in range(nh):
        qa_s[h * tq:(h + 1) * tq, :] = qabs_ref[:, h * KV_RANK:(h + 1) * KV_RANK]
        pair = qidx_ref[:, (h // 2) * LANES:(h // 2 + 1) * LANES].astype(F32)
        qi_s[h * tq:(h + 1) * tq, :] = (pair * (lo_half if h % 2 == 0 else hi_half)).astype(BF16)
    w_t = jnp.transpose(widx_ref[...])
    w_row = jnp.concatenate([w_t[h:h + 1, :] for h in range(nh)], axis=1)

    mn_s[...] = jnp.full(mn_s.shape, -NEG_BIG, F32)
    mx_s[...] = jnp.full(mx_s.shape, NEG_BIG, F32)

    def idx_dots(kj, r_ref):
        ks = pl.multiple_of(jnp.minimum(kj, n_kt - 1) * tk, tk)
        r_ref[...] = _dot_nt(kk_ref[pl.ds(ks, tk), :], qi_s[...])

    def idx_scores(kj, r_ref):
        r = jnp.maximum(r_ref[...], 0.0) * w_row
        s = r[:, :tq]
        for h in range(1, nh):
            s = s + r[:, h * tq:(h + 1) * tq]
        masked = (kpos > qpos) & (kj == qi)
        key_s[kj] = jnp.where(masked, NEG_BIG, s)
        mn_s[...] = jnp.minimum(mn_s[...], jnp.min(jnp.where(masked, -NEG_BIG, s).reshape(tk // 8, 8, tq), axis=0))
        mx_s[...] = jnp.maximum(mx_s[...], jnp.max(jnp.where(masked, NEG_BIG, s).reshape(tk // 8, 8, tq), axis=0))

    idx_dots(0, sa_s)

    def idx_pair(i, carry):
        idx_dots(2 * i + 1, sb_s)
        idx_scores(2 * i, sa_s)
        idx_dots(2 * i + 2, sa_s)
        idx_scores(2 * i + 1, sb_s)
        return carry

    lax.fori_loop(0, n_kt // 2, idx_pair, 0)

    @pl.when(n_kt % 2 == 1)
    def _():
        idx_scores(n_kt - 1, sa_s)

    lo = jnp.min(mn_s[...], axis=0, keepdims=True)
    hi = jnp.max(mx_s[...], axis=0, keepdims=True)
    scale = KEY_TOP / jnp.maximum(hi - lo, KEY_MIN_RANGE)

    def key_body(kj, carry):
        y = jnp.maximum((key_s[kj] - lo) * scale, -1.0)
        rem = y
        for p in (3, 2, 1, 0):
            d = jnp.floor(rem)
            dig_s[p, kj] = d.astype(BF16)
            rem = (rem - d) * 256.0
        key_s[kj] = y - rem * (2.0 ** -32)
        return carry

    lax.fori_loop(0, n_kt, key_body, 0)

    n_search = jnp.where(qi == 0, 0, n_kt)

    def count_active(pred, then=None):
        one, zero = jnp.ones((), BF16), jnp.zeros((), BF16)

        def body(kj, acc):
            a = act_s[kj]
            c = jnp.where(pred(a), one, zero).reshape(tk // 16, 16, tq)
            part = c[0]
            for j in range(1, tk // 16):
                part = part + c[j]
            if then is not None:
                then(kj, a)
            return acc + part.astype(F32)
        acc = lax.fori_loop(0, n_search, body, jnp.zeros((16, tq), F32))
        return jnp.sum(acc, axis=0, keepdims=True)

    def to_digit(v):
        return v.astype(F32).astype(BF16)

    def digit_phase(p, prev_digit, target):
        if p == 3:
            def init_body(kj, carry):
                act_s[kj] = dig_s[3, kj]
                return carry
            lax.fori_loop(0, n_search, init_body, 0)
        else:
            prev = to_digit(prev_digit)

            def narrow(kj, a):
                act_s[kj] = jnp.where(a == prev, dig_s[p, kj], jnp.full((), -1.0, BF16))
            target = target - count_active(lambda a: a > prev, then=narrow)
        n_bits = 7 if p == 3 else 8

        def bit_body(i, dgt):
            cand = dgt | jnp.left_shift(jnp.int32(1), n_bits - 1 - i)
            cand_d = to_digit(cand)
            cnt = count_active(lambda a: a >= cand_d)
            return jnp.where(cnt >= target, cand, dgt)
        return lax.fori_loop(0, n_bits, bit_body, jnp.zeros((1, tq), jnp.int32)), target

    target = jnp.full((1, tq), float(n_sel), F32)
    tau = jnp.zeros((1, tq), F32)
    digit = None
    for p in (3, 2, 1):
        digit, target = digit_phase(p, digit, target)
        tau = tau + digit.astype(F32) * (256.0 ** (p - 3))
    digit_d = to_digit(digit)
    excess = count_active(lambda a: a >= digit_d) - target

    def count_keys(pred):
        def body(kj, acc):
            c = jnp.where(pred(key_s[kj], kpos + kj * tk), 1.0, 0.0)
            return acc + jnp.sum(c.reshape(tk // 8, 8, tq), axis=0)
        acc = lax.fori_loop(0, n_kt, body, jnp.zeros((8, tq), F32))
        return jnp.sum(acc, axis=0, keepdims=True)

    first = qi == 0
    cs_s[...] = jnp.where(lax.broadcasted_iota(jnp.int32, cs_s.shape, 0) == 0, seq, 0)

    @pl.when(jnp.logical_not(first) & (jnp.max(excess) > 0.0))
    def _():
        digit0, target0 = digit_phase(0, digit, target)
        cs_s[1:2, :] = digit0
        tau0 = tau + digit0.astype(F32) * (256.0 ** -3)
        last = to_digit(digit0)
        n_gt = count_active(lambda a: a > last)
        n_eq = count_active(lambda a: a == last)
        need = target0 - n_gt

        @pl.when(jnp.max(n_eq - need) > 0.0)
        def _():
            nbits = seq.bit_length() - 1

            def tie_body(i, d):
                cand = d | jnp.left_shift(jnp.int32(1), nbits - 1 - i)
                cnt = count_keys(lambda k, pos: (k == tau0) & (pos < cand))
                return jnp.where(cnt < need, cand, d)

            cs_s[0:1, :] = lax.fori_loop(0, nbits, tie_body, jnp.zeros((1, tq), jnp.int32))

    tau = jnp.where(first, -0.5, tau + cs_s[1:2, :].astype(F32) * (256.0 ** -3))
    cstar = jnp.where(first, -1, cs_s[0:1, :])

    m_s[...] = jnp.full(m_s.shape, NEG_BIG, F32)
    acc_s[...] = jnp.zeros(acc_s.shape, F32)

    def scores(kj, s_ref, mx_ref):
        kc = jnp.minimum(kj, n_kt - 1)
        ks = pl.multiple_of(kc * tk, tk)
        k = key_s[kc]
        sel = ((k > tau) | ((k == tau) & (kpos + kc * tk <= cstar))) & (kj < n_kt)
        selb = jnp.where(sel, 0.0, NEG_BIG)
        s = _dot_nt(ckv_ref[pl.ds(ks, tk), :], qa_s[...])
        s = s + bias_s[jnp.minimum(qi - kc, 2)] + jnp.concatenate([selb] * nh, axis=1)
        s_ref[...] = s
        mx_ref[...] = jnp.max(s.reshape(tk // 8, 8, nhq), axis=0)

    def update(kj, s_ref, mx_ref):
        m_old = m_s[0:1, :]
        m_new = jnp.maximum(m_old, jnp.max(mx_ref[...], axis=0, keepdims=True))
        alpha = jnp.exp2(m_old - m_new)
        p = jnp.exp2(s_ref[...] - m_new).astype(BF16)
        acc_s[...] = acc_s[...] * alpha + _dot(ckvt_ref[jnp.minimum(kj, n_kt - 1)], p)
        m_s[...] = jnp.broadcast_to(m_new, m_s.shape)

    scores(0, sa_s, ma_s)

    def pair_body(i, carry):
        scores(2 * i + 1, sb_s, mb_s)
        update(2 * i, sa_s, ma_s)
        scores(2 * i + 2, sa_s, ma_s)
        update(2 * i + 1, sb_s, mb_s)
        return carry

    lax.fori_loop(0, n_kt // 2, pair_body, 0)

    @pl.when(n_kt % 2 == 1)
    def _():
        update(n_kt - 1, sa_s, ma_s)

    acc = acc_s[...]
    o_t = acc[:KV_RANK] / acc[KV_RANK:]
    for h in range(nh):
        o_ref[:, h * KV_RANK:(h + 1) * KV_RANK] = jnp.transpose(o_t[:, h * tq:(h + 1) * tq]).astype(BF16)


def _dsa(rel_bias, qabs, qidx, widx, ckv, ckvt, kk, batch, seq):
    n = qabs.shape[0]
    tq = ATT_TILE
    nq = seq // tq
    n_sel = min(TOPK_MAX, seq // 4)
    assert n_sel == tq and seq % tq == 0 and seq & (seq - 1) == 0
    qrow = lambda w: pl.BlockSpec((tq, w), lambda b, q: (b * nq + q, 0))
    brow = lambda w: pl.BlockSpec((seq, w), lambda b, q: (b, 0))
    nhq = N_HEADS_A * tq
    return pl.pallas_call(
        functools.partial(_dsa_kernel, n_sel=n_sel, seq=seq),
        grid=(batch, nq),
        in_specs=[pl.BlockSpec(memory_space=pltpu.SMEM),
                  qrow(N_HEADS_A * KV_RANK), qrow(IDX_HEADS * IDX_DIM), qrow(LANES),
                  brow(KV_RANK), pl.BlockSpec((nq, 2 * KV_RANK, tq), lambda b, q: (b, 0, 0)), brow(LANES)],
        out_specs=qrow(N_HEADS_A * KV_RANK),
        out_shape=jax.ShapeDtypeStruct((n, N_HEADS_A * KV_RANK), BF16),
        scratch_shapes=[pltpu.VMEM((nhq, KV_RANK), BF16), pltpu.VMEM((nhq, LANES), BF16),
                        pltpu.VMEM((nq, tq, tq), F32), pltpu.VMEM((4, nq, tq, tq), BF16),
                        pltpu.VMEM((nq, tq, tq), BF16), pltpu.VMEM((8, tq), jnp.int32),
                        pltpu.VMEM((8, tq), F32), pltpu.VMEM((8, tq), F32),
                        pltpu.VMEM((3, tq, nhq), F32),
                        pltpu.VMEM((8, nhq), F32), pltpu.VMEM((2 * KV_RANK, nhq), F32),
                        pltpu.VMEM((tq, nhq), F32), pltpu.VMEM((tq, nhq), F32),
                        pltpu.VMEM((8, nhq), F32), pltpu.VMEM((8, nhq), F32)],
        compiler_params=pltpu.CompilerParams(dimension_semantics=("arbitrary", "arbitrary"),
                                             vmem_limit_bytes=VMEM_LIMIT),
        name="dsa",
    )(rel_bias, qabs, qidx, widx, ckv, ckvt, kk)


def _stick_kernel(q_ref, k_ref, v_ref, o_ref):
    for j in range(STICK_Q_TILES):
        _stick_tile(pl.program_id(1) * STICK_Q_TILES + j, q_ref.at[j * ATT_TILE:(j + 1) * ATT_TILE],
                    k_ref, v_ref, o_ref.at[j * ATT_TILE:(j + 1) * ATT_TILE])


def _stick_tile(qi, q_ref, k_ref, v_ref, o_ref):
    tq = tk = ATT_TILE
    npair = WIDTH_B // LANES
    lane = lax.broadcasted_iota(jnp.int32, (tq, LANES), 1)
    row = lax.broadcasted_iota(jnp.int32, (tq, tk), 0)
    col = lax.broadcasted_iota(jnp.int32, (tq, tk), 1)
    causal = col < row
    upper = jnp.where(row > col, 1.0, 0.0).astype(BF16)
    upper_ones = jnp.concatenate([upper, jnp.ones((tk, LANES), BF16)], axis=1)
    lo_half = jnp.where(lane < HEAD_DIM, 1.0, 0.0)
    causal2 = jnp.concatenate([causal, causal], axis=0)
    q_pairs = []
    for p in range(npair):
        q2 = q_ref[:, p * LANES:(p + 1) * LANES].astype(F32)
        q_pairs.append(jnp.concatenate([q2 * lo_half, q2 * (1.0 - lo_half)], axis=0).astype(BF16))

    def tile(kj, r_sums, outs, masked):
        ks = pl.multiple_of(kj * tk, tk)
        log_sig, log_1m = [], []
        for p in range(npair):
            z = _dot_nt(q_pairs[p], k_ref[pl.ds(ks, tk), p * LANES:(p + 1) * LANES])
            sp = jnp.maximum(z, 0.0) + jnp.log2(1.0 + jnp.exp2(-jnp.abs(z)))
            lm = jnp.where(causal2, -sp, 0.0) if masked else -sp
            log_sig.append(z - sp)
            log_1m.append(lm.astype(BF16))
        sums = _dot(jnp.concatenate(log_1m, axis=0), upper_ones)
        new_r, new_o = [], []
        for p in range(npair):
            sm = sums[p * 2 * tq:(p + 1) * 2 * tq]
            after = sm[:, :tk] + jnp.concatenate([r_sums[p]] * (tk // LANES), axis=1)
            a = jnp.exp2(log_sig[p] + after)
            if masked:
                a = jnp.where(causal2, a, 0.0)
            res = _dot(a.astype(BF16), v_ref[pl.ds(ks, tk), p * LANES:(p + 1) * LANES])
            new_o.append(outs[p] + jnp.where(lane < HEAD_DIM, res[:tq], res[tq:]))
            new_r.append(r_sums[p] + sm[:, tk:])
        return tuple(new_r), tuple(new_o)

    zeros = jnp.zeros((tq, LANES), F32)
    r_sums, outs = tile(qi, (jnp.zeros((2 * tq, LANES), F32),) * npair, (zeros,) * npair, True)

    def cond(c):
        it, r_sums, _ = c
        return (it < qi) & (jnp.max(functools.reduce(jnp.maximum, r_sums)) > EXP_UNDERFLOW)

    def body(c):
        it, r_sums, outs = c
        r_sums, outs = tile(qi - 1 - it, r_sums, outs, False)
        return it + 1, r_sums, outs

    _, _, outs = lax.while_loop(cond, body, (jnp.int32(0), r_sums, outs))
    for p in range(npair):
        o_ref[:, p * LANES:(p + 1) * LANES] = outs[p].astype(BF16)


def _stick(qb, kb, vb, batch, seq):
    n = qb.shape[0]
    tq = ATT_TILE
    nq = seq // tq
    nqs = nq // STICK_Q_TILES
    qspec = pl.BlockSpec((STICK_Q_TILES * tq, WIDTH_B), lambda b, q: (b * nqs + q, 0))
    kspec = pl.BlockSpec((seq, WIDTH_B), lambda b, q: (b, 0))
    return pl.pallas_call(
        _stick_kernel,
        grid=(batch, nqs),
        in_specs=[qspec, kspec, kspec],
        out_specs=qspec,
        out_shape=jax.ShapeDtypeStruct((n, WIDTH_B), BF16),
        compiler_params=pltpu.CompilerParams(dimension_semantics=("arbitrary",) * 2, vmem_limit_bytes=VMEM_LIMIT),
        name="stick",
    )(qb, kb, vb)


def _merge_kernel(x_ref, ol_ref, ob_ref, sa_ref, sb_ref, wuv_ref, wba_ref, wbb_ref, wout_ref, g_ref,
                  wr_ref, br_ref, x1_ref, h2_ref, comb_ref):
    half = N_HEADS_A * KV_RANK // 2
    oa = jnp.concatenate([_dot(ol_ref[:, :half], wuv_ref[:half, :WIDTH_A // 2]),
                          _dot(ol_ref[:, half:], wuv_ref[half:, WIDTH_A // 2:])], axis=1).astype(BF16)
    ya = _dot(oa, wba_ref[...])
    yb = _dot(ob_ref[...], wbb_ref[...])
    merged = sa_ref[...].astype(F32) * ya + sb_ref[...].astype(F32) * yb
    x1 = x_ref[...] + _dot(merged.astype(BF16), wout_ref[...])
    x1_ref[...] = x1
    h2 = _rms(x1, g_ref[...])
    h2_hi = h2.astype(BF16)
    h2_ref[...] = h2_hi
    h2_lo = (h2 - h2_hi.astype(F32)).astype(BF16)
    hi_part = _dot(h2_hi, wr_ref[...])
    logits = (hi_part[:, :LANES] + _dot(h2_lo, wr_ref[:, :LANES]) + hi_part[:, LANES:]) + br_ref[...]
    lane = lax.broadcasted_iota(jnp.int32, logits.shape, 1).astype(F32)
    ninf = -jnp.inf

    def first_max(v):
        m = jnp.max(v, axis=1, keepdims=True)
        return m, jnp.min(jnp.where(v == m, lane, 1e9), axis=1, keepdims=True)

    gmask = (lane >= N_EXPERTS) & (lane < N_EXPERTS + N_GROUPS)
    gl = jnp.where(gmask, logits, ninf)
    gmax, gidx = first_max(gl)
    p_g = 1.0 / jnp.sum(jnp.where(gmask, jnp.exp(gl - gmax), 0.0), axis=1, keepdims=True)
    e_lo = (gidx - N_EXPERTS) * EXPERTS_PER_GROUP
    el = jnp.where((lane >= e_lo) & (lane < e_lo + EXPERTS_PER_GROUP), logits, ninf)
    v1, i1 = first_max(el)
    el2 = jnp.where(lane == i1, ninf, el)
    v2, i2 = first_max(el2)
    e2 = jnp.exp(v2 - v1)
    w1 = 1.0 / (1.0 + e2)
    comb_ref[...] = jnp.where(lane == i1, w1 * p_g,
                              jnp.where(lane == i2, (e2 * w1) * p_g, jnp.where(lane == gidx, 1.0, 0.0)))


def _merge(x, olat, ob, sa, sb, wuv, wba, wbb, wout, g, wr, br):
    n, d = x.shape
    tm = ROW_TILE
    row = lambda w: pl.BlockSpec((tm, w), lambda i: (i, 0))
    ws = (wuv, wba, wbb, wout, g, wr, br)
    return pl.pallas_call(
        _merge_kernel,
        grid=(n // tm,),
        in_specs=[row(d), row(olat.shape[1]), row(ob.shape[1]), row(d), row(d)] + [_full_spec(w.shape) for w in ws],
        out_specs=[row(d), row(d), row(LANES)],
        out_shape=[jax.ShapeDtypeStruct((n, d), F32), jax.ShapeDtypeStruct((n, d), BF16),
                   jax.ShapeDtypeStruct((n, LANES), F32)],
        compiler_params=pltpu.CompilerParams(dimension_semantics=("arbitrary",), vmem_limit_bytes=VMEM_LIMIT),
        name="merge",
    )(x, olat, ob, sa, sb, *ws)


def _moe_kernel(h_ref, comb_ref, wg_ref, wu_ref, wd_ref, o_ref, slotc_s, slotr_s, split_s, acc_s, rng_s):
    t, d = h_ref.shape
    nsub = MOE_SUBTILES
    ts = t // nsub
    c = MOE_CHUNK
    rb = MOE_RANK_BLOCK
    g = pl.program_id(1)

    @pl.when(g == 0)
    def _():
        lane_b = lax.broadcasted_iota(jnp.int32, (rb, LANES), 1)
        group_lanes = (lane_b >= N_EXPERTS) & (lane_b < N_EXPERTS + N_GROUPS)
        r_i = lax.broadcasted_iota(jnp.int32, (rb, rb), 0)
        c_i = lax.broadcasted_iota(jnp.int32, (rb, rb), 1)
        lower = jnp.where(c_i < r_i, 1.0, 0.0).astype(BF16)
        lane1 = lax.broadcasted_iota(jnp.int32, (1, LANES), 1)
        for sub in range(nsub):
            counts = jnp.zeros((1, LANES), F32)
            ranks = []
            for blk in range(ts // rb):
                rows = slice(sub * ts + blk * rb, sub * ts + (blk + 1) * rb)
                oh = jnp.where(group_lanes, comb_ref[rows, :], 0.0)
                before = _dot(lower, oh.astype(BF16)) + counts
                ranks.append(jnp.sum(before * oh, axis=1, keepdims=True))
                counts = counts + jnp.sum(oh, axis=0, keepdims=True)
            start = jnp.int32(0)
            seg = jnp.zeros((1, LANES), F32)
            for k in range(N_GROUPS):
                n_k = jnp.sum(jnp.where(lane1 == N_EXPERTS + k, counts, 0.0)).astype(jnp.int32)
                seg = jnp.where(lane1 == N_EXPERTS + k, (start * c).astype(F32), seg)
                rng_s[sub * 2 * N_GROUPS + k] = start
                for j in range(-(-ts // c)):
                    start = start + (n_k > j * c).astype(jnp.int32)
                rng_s[sub * 2 * N_GROUPS + N_GROUPS + k] = start
            for blk in range(ts // rb):
                rows = slice(sub * ts + blk * rb, sub * ts + (blk + 1) * rb)
                oh = jnp.where(group_lanes, comb_ref[rows, :], 0.0)
                slot = ranks[blk] + jnp.sum(oh * seg, axis=1, keepdims=True)
                slotc_s[rows, :] = jnp.broadcast_to(slot, (rb, LANES))
        slotr_s[...] = jnp.transpose(slotc_s[...])
        cb = comb_ref[...]
        hi = cb.astype(BF16)
        split_s[...] = jnp.concatenate([hi, (cb - hi.astype(F32)).astype(BF16)], axis=1)
        acc_s[...] = jnp.zeros(acc_s.shape, F32)

    lane = lax.broadcasted_iota(jnp.int32, (nsub * c, LANES), 1)
    first = [rng_s[sub * 2 * N_GROUPS + g] for sub in range(nsub)]
    n_chunks = [rng_s[sub * 2 * N_GROUPS + N_GROUPS + g] - first[sub] for sub in range(nsub)]

    def chunk_body(j, carry):
        bases = [((first[sub] + j) * c).astype(F32) for sub in range(nsub)]
        xs, cw = [], []
        for sub in range(nsub):
            rows = slice(sub * ts, (sub + 1) * ts)
            row_slot = lax.broadcasted_iota(jnp.int32, (c, ts), 0).astype(F32) + bases[sub]
            pc = jnp.where(slotr_s[0:1, rows] == row_slot, 1.0, 0.0).astype(BF16)
            xs.append(_dot(pc, h_ref[rows, :]).astype(BF16))
            cw2 = _dot(pc, split_s[rows, :])
            cw.append(cw2[:, :LANES] + cw2[:, LANES:])
        xs = jnp.concatenate(xs, axis=0)
        cw = jnp.concatenate(cw, axis=0)
        y = jnp.zeros((nsub * c, d), F32)
        for e in range(EXPERTS_PER_GROUP):
            gate = _dot(xs, wg_ref[e])
            hid = gate * jax.nn.sigmoid(gate) * _dot(xs, wu_ref[e])
            ce = jnp.sum(jnp.where(lane == g * EXPERTS_PER_GROUP + e, cw, 0.0), axis=1, keepdims=True)
            y = y + _dot((hid * ce).astype(BF16), wd_ref[e])
        y = y.astype(BF16)
        for sub in range(nsub):
            rows = slice(sub * ts, (sub + 1) * ts)
            col_slot = lax.broadcasted_iota(jnp.int32, (ts, c), 1).astype(F32) + bases[sub]
            pct = jnp.where(jnp.broadcast_to(slotc_s[rows, 0:1], (ts, c)) == col_slot, 1.0, 0.0).astype(BF16)
            acc_s[rows, :] += _dot(pct, y[sub * c:(sub + 1) * c])
        return carry

    lax.fori_loop(0, functools.reduce(jnp.maximum, n_chunks), chunk_body, 0)

    @pl.when(g == N_GROUPS - 1)
    def _():
        o_ref[...] = acc_s[...].astype(BF16)


def _moe(h2, comb, wg, wu, wd):
    n, d = h2.shape
    tm = MOE_ROW_TILE
    row = lambda w: pl.BlockSpec((tm, w), lambda i, g: (i, 0))
    return pl.pallas_call(
        _moe_kernel,
        grid=(n // tm, N_GROUPS),
        in_specs=[row(d), row(LANES),
                  pl.BlockSpec((EXPERTS_PER_GROUP, d, D_EXPERT), lambda i, g: (g, 0, 0)),
                  pl.BlockSpec((EXPERTS_PER_GROUP, d, D_EXPERT), lambda i, g: (g, 0, 0)),
                  pl.BlockSpec((EXPERTS_PER_GROUP, D_EXPERT, d), lambda i, g: (g, 0, 0))],
        out_specs=row(d),
        out_shape=jax.ShapeDtypeStruct((n, d), BF16),
        scratch_shapes=[pltpu.VMEM((tm, LANES), F32), pltpu.VMEM((LANES, tm), F32),
                        pltpu.VMEM((tm, 2 * LANES), BF16), pltpu.VMEM((tm, d), F32),
                        pltpu.SMEM((MOE_SUBTILES * 2 * N_GROUPS,), jnp.int32)],
        compiler_params=pltpu.CompilerParams(dimension_semantics=("arbitrary", "arbitrary"),
                                             vmem_limit_bytes=MOE_VMEM_LIMIT),
        name="moe",
    )(h2, comb, wg, wu, wd)


def _ple_kernel(x_ref, m_ref, p_ref, g_ref, wpg_ref, wple_ref, gf_ref, o_ref, *, last):
    x2 = x_ref[...] + m_ref[...].astype(F32)
    h3 = _rms(x2, g_ref[...]).astype(BF16)
    gate = jax.nn.sigmoid(_dot(h3, wpg_ref[...]))
    x3 = x2 + _dot(p_ref[...].astype(BF16), wple_ref[...]) * gate
    o_ref[...] = _rms(x3, gf_ref[...]) if last else x3


def _ple(x1, moe, p, g, wpg, wple, gf, last):
    n, d = x1.shape
    tm = ROW_TILE
    row = lambda w: pl.BlockSpec((tm, w), lambda i: (i, 0))
    ws = (g, wpg, wple, gf)
    return pl.pallas_call(
        functools.partial(_ple_kernel, last=last),
        grid=(n // tm,),
        in_specs=[row(d), row(d), row(p.shape[1])] + [_full_spec(w.shape) for w in ws],
        out_specs=row(d),
        out_shape=jax.ShapeDtypeStruct((n, d), F32),
        compiler_params=pltpu.CompilerParams(dimension_semantics=("arbitrary",), vmem_limit_bytes=VMEM_LIMIT),
        name="ple",
    )(x1, moe, p, *ws)


def _block_diag(blocks):
    h, r, c = blocks.shape
    eye = jnp.eye(h, dtype=blocks.dtype)
    return (eye[:, None, :, None] * blocks[:, :, None, :]).reshape(h * r, h * c)


def kernel(x, p, attn_norm, w_in, kv_norm, w_uk, w_uv, rel_bias, w_branch_a, w_branch_b, w_out, ffn_norm,
           w_r1, b_r1, w_r2, b_r2, w_gate, w_up, w_down, ple_norm, w_ple_gate, w_ple, final_norm):
    batch, seq, d = x.shape
    n = batch * seq
    depth = w_in.shape[0]
    xf = x.reshape(n, d).astype(F32)
    widths = [WIDTH_A, KV_RANK, IDX_HEADS * IDX_DIM, IDX_DIM, IDX_HEADS, 3 * WIDTH_B, d, d]
    starts = [sum(widths[:k]) for k in range(len(widths))]
    for i in range(depth):
        w_qa, w_ckv, w_qi, w_ki, w_wi, w_qkv, w_ga, w_gb = [
            w_in[i][:, s:s + w].astype(BF16) for s, w in zip(starts, widths)]
        w_kk = jnp.concatenate([w_ki, w_ki], axis=1)
        w_cw = jnp.concatenate([w_ckv, jnp.pad(w_wi, ((0, 0), (0, LANES - IDX_HEADS)))], axis=1)
        wuk_bd = _block_diag(jnp.swapaxes(w_uk[i], 1, 2)).astype(BF16)
        wuv_bd = _block_diag(w_uv[i]).astype(BF16)
        qabs, ckv, ckvt, qidx, kk, widx, qb, kb, vb, sa, sb = _proj(
            xf, attn_norm[i][None].astype(F32), w_qa, wuk_bd, w_cw, kv_norm[i][None].astype(F32),
            w_qi, w_kk, w_qkv, w_ga, w_gb)
        olat = _dsa(rel_bias.astype(F32), qabs, qidx, widx, ckv, ckvt, kk, batch, seq)
        ob = _stick(qb, kb, vb, batch, seq)
        w_r = jnp.concatenate([jnp.transpose(w_r2[i], (1, 0, 2)).reshape(d, N_EXPERTS), w_r1[i]], axis=1)
        w_r = jnp.pad(w_r.astype(F32), ((0, 0), (0, LANES - N_EXPERTS - N_GROUPS)))
        w_r_hi = w_r.astype(BF16)
        w_r_lo = (w_r - w_r_hi.astype(F32)).astype(BF16)
        b_r = jnp.pad(jnp.concatenate([b_r2[i].reshape(-1), b_r1[i]]).astype(F32),
                      (0, LANES - N_EXPERTS - N_GROUPS))[None]
        x1, h2, comb = _merge(xf, olat, ob, sa, sb, wuv_bd, w_branch_a[i].astype(BF16), w_branch_b[i].astype(BF16),
                              w_out[i].astype(BF16), ffn_norm[i][None].astype(F32),
                              jnp.concatenate([w_r_hi, w_r_lo], axis=1), b_r)
        moe = _moe(h2, comb, w_gate[i].astype(BF16), w_up[i].astype(BF16), w_down[i].astype(BF16))
        xf = _ple(x1, moe, p[i].reshape(n, -1).astype(F32), ple_norm[i][None].astype(F32), w_ple_gate[i].astype(BF16),
                  w_ple[i].astype(BF16), final_norm[None].astype(F32), last=(i == depth - 1))
    return xf.reshape(batch, seq, d).astype(x.dtype)
```

```python
import functools
import math

import jax
import jax.numpy as jnp
from jax import lax
from jax.experimental import pallas as pl
from jax.experimental.pallas import tpu as pltpu

D_MODEL = 1024
N_HEADS_A = 8
HEAD_DIM = 64
WIDTH_A = N_HEADS_A * HEAD_DIM
KV_RANK = 128
IDX_HEADS = 8
IDX_DIM = 64
TOPK_MAX = 256
N_HEADS_B = 8
WIDTH_B = N_HEADS_B * HEAD_DIM
N_BUCKETS = 32
MAX_EXACT = N_BUCKETS // 2
MAX_DISTANCE = 128
ATTN_SCALE = HEAD_DIM ** -0.5
IDX_SCALE = (IDX_HEADS ** -0.5) * (IDX_DIM ** -0.5)
N_GROUPS = 4
EXPERTS_PER_GROUP = 8
N_EXPERTS = N_GROUPS * EXPERTS_PER_GROUP
D_EXPERT = 256
PLE_DIM = 256
EPS = 1e-6
LOG2E = 1.4426950408889634

LANES = 128
ROW_TILE = 1024
MOE_ROW_TILE = 2048
MOE_SUBTILES = 4
MOE_CHUNK = 160
MOE_RANK_BLOCK = 256
ATT_TILE = 256
STICK_Q_TILES = 2
VMEM_LIMIT = 56 * 1024 * 1024
MOE_VMEM_LIMIT = 60 * 1024 * 1024
NEG_BIG = -1e30
EXP_UNDERFLOW = -173.0
KEY_TOP = 127.0
KEY_MIN_RANGE = 1e-30

F32 = jnp.float32
BF16 = jnp.bfloat16


def _rms(x, g):
    return x * lax.rsqrt(jnp.mean(x * x, axis=-1, keepdims=True) + EPS) * g


def _dot(a, b):
    return jnp.dot(a, b, preferred_element_type=F32)


def _dot_nt(a, b):
    return lax.dot_general(a, b, (((1,), (1,)), ((), ())), preferred_element_type=F32)


def _full_spec(shape):
    nd = len(shape)
    return pl.BlockSpec(shape, lambda *_: (0,) * nd)


def _proj_kernel(x_ref, g_ref, wqa_ref, wuk_ref, wcw_ref, kvg_ref, wqi_ref, wkk_ref,
                 wqkv_ref, wga_ref, wgb_ref,
                 qabs_ref, ckv_ref, ckvt_ref, qidx_ref, kk_ref, widx_ref, qb_ref, kb_ref, vb_ref, sa_ref, sb_ref):
    h = _rms(x_ref[...], g_ref[...]).astype(BF16)
    qa = _dot(h, wqa_ref[...]).astype(BF16)
    half = WIDTH_A // 2
    qabs = jnp.concatenate([_dot(qa[:, :half], wuk_ref[:half, :N_HEADS_A * KV_RANK // 2]),
                            _dot(qa[:, half:], wuk_ref[half:, N_HEADS_A * KV_RANK // 2:])], axis=1)
    qabs_ref[...] = (qabs * (ATTN_SCALE * LOG2E)).astype(BF16)
    cw = _dot(h, wcw_ref[...])
    c = _rms(cw[:, :KV_RANK], kvg_ref[...])
    ckv_ref[...] = c.astype(BF16)
    c_ext = jnp.concatenate([c, jnp.ones_like(c)], axis=1)
    for j in range(ckvt_ref.shape[0]):
        ckvt_ref[j] = jnp.transpose(c_ext[j * ATT_TILE:(j + 1) * ATT_TILE, :]).astype(BF16)
    qidx_ref[...] = _dot(h, wqi_ref[...]).astype(BF16)
    kk_ref[...] = _dot(h, wkk_ref[...]).astype(BF16)
    widx_ref[...] = cw[:, KV_RANK:] * IDX_SCALE
    qkv = _dot(h, wqkv_ref[...])
    qb_ref[...] = (qkv[:, :WIDTH_B] * (ATTN_SCALE * LOG2E)).astype(BF16)
    kb_ref[...] = qkv[:, WIDTH_B:2 * WIDTH_B].astype(BF16)
    vb_ref[...] = qkv[:, 2 * WIDTH_B:].astype(BF16)
    sa_ref[...] = jax.nn.sigmoid(_dot(h, wga_ref[...])).astype(BF16)
    sb_ref[...] = jax.nn.sigmoid(_dot(h, wgb_ref[...])).astype(BF16)


def _proj(x, g, wqa, wuk, wcw, kvg, wqi, wkk, wqkv, wga, wgb):
    n, d = x.shape
    tm = ROW_TILE
    row = lambda w: pl.BlockSpec((tm, w), lambda i: (i, 0))
    ws = (g, wqa, wuk, wcw, kvg, wqi, wkk, wqkv, wga, wgb)
    outs = [(N_HEADS_A * KV_RANK, BF16), (KV_RANK, BF16), None, (IDX_HEADS * IDX_DIM, BF16), (LANES, BF16),
            (LANES, F32), (WIDTH_B, BF16), (WIDTH_B, BF16), (WIDTH_B, BF16), (d, BF16), (d, BF16)]
    t = ATT_TILE
    ckvt_spec = pl.BlockSpec((tm // t, 2 * KV_RANK, t), lambda i: (i, 0, 0))
    ckvt_shape = jax.ShapeDtypeStruct((n // t, 2 * KV_RANK, t), BF16)
    return pl.pallas_call(
        _proj_kernel,
        grid=(n // tm,),
        in_specs=[row(d)] + [_full_spec(w.shape) for w in ws],
        out_specs=[ckvt_spec if o is None else row(o[0]) for o in outs],
        out_shape=[ckvt_shape if o is None else jax.ShapeDtypeStruct((n, o[0]), o[1]) for o in outs],
        compiler_params=pltpu.CompilerParams(dimension_semantics=("arbitrary",), vmem_limit_bytes=VMEM_LIMIT),
        name="proj",
    )(x, *ws)


def _t5_bucket(dist):
    dist = jnp.maximum(dist, 0)
    d_f = jnp.maximum(dist, 1).astype(F32)
    large = MAX_EXACT + jnp.floor(jnp.log(d_f / MAX_EXACT) / math.log(MAX_DISTANCE / MAX_EXACT)
                                  * (N_BUCKETS - MAX_EXACT)).astype(jnp.int32)
    large = jnp.minimum(large, N_BUCKETS - 1)
    return jnp.where(dist < MAX_EXACT, dist, large)


def _dsa_kernel(relb_ref, qabs_ref, qidx_ref, widx_ref, ckv_ref, ckvt_ref, kk_ref, o_ref,
                qa_s, qi_s, key_s, dig_s, act_s, cs_s, mn_s, mx_s, bias_s, m_s, acc_s, sa_s, sb_s, ma_s, mb_s, *, n_sel, seq):
    tq = tk = ATT_TILE
    nh = N_HEADS_A
    nhq = nh * tq
    qi = pl.program_id(1)
    n_kt = qi + 1
    kpos = lax.broadcasted_iota(jnp.int32, (tk, tq), 0)
    qpos = lax.broadcasted_iota(jnp.int32, (tk, tq), 1)

    @pl.when((pl.program_id(0) == 0) & (qi == 0))
    def _():
        for t in range(2):
            bucket = _t5_bucket(t * tq + qpos - kpos)
            for h in range(nh):
                b = jnp.zeros((tk, tq), F32)
                for bk in range(N_BUCKETS):
                    b = jnp.where(bucket == bk, relb_ref[bk, h], b)
                bias_s[t, :, h * tq:(h + 1) * tq] = (b - relb_ref[N_BUCKETS - 1, h]) * LOG2E

    lane = lax.broadcasted_iota(jnp.int32, (tq, LANES), 1)
    lo_half = jnp.where(lane < IDX_DIM, 1.0, 0.0)
    hi_half = 1.0 - lo_half
    for h in range(nh):
        qa_s[h * tq:(h + 1) * tq, :] = qabs_ref[:, h * KV_RANK:(h + 1) * KV_RANK]
        pair = qidx_ref[:, (h // 2) * LANES:(h // 2 + 1) * LANES].astype(F32)
        qi_s[h * tq:(h + 1) * tq, :] = (pair * (lo_half if h % 2 == 0 else hi_half)).astype(BF16)
    w_t = jnp.transpose(widx_ref[...])
    w_row = jnp.concatenate([w_t[h:h + 1, :] for h in range(nh)], axis=1)

    mn_s[...] = jnp.full(mn_s.shape, -NEG_BIG, F32)
    mx_s[...] = jnp.full(mx_s.shape, NEG_BIG, F32)

    def idx_dots(kj, r_ref):
        ks = pl.multiple_of(jnp.minimum(kj, n_kt - 1) * tk, tk)
        r_ref[...] = _dot_nt(kk_ref[pl.ds(ks, tk), :], qi_s[...])

    def idx_scores(kj, r_ref):
        r = jnp.maximum(r_ref[...], 0.0) * w_row
        s = r[:, :tq]
        for h in range(1, nh):
            s = s + r[:, h * tq:(h + 1) * tq]
        masked = (kpos > qpos) & (kj == qi)
        key_s[kj] = jnp.where(masked, NEG_BIG, s)
        mn_s[...] = jnp.minimum(mn_s[...], jnp.min(jnp.where(masked, -NEG_BIG, s).reshape(tk // 8, 8, tq), axis=0))
        mx_s[...] = jnp.maximum(mx_s[...], jnp.max(jnp.where(masked, NEG_BIG, s).reshape(tk // 8, 8, tq), axis=0))

    idx_dots(0, sa_s)

    def idx_pair(i, carry):
        idx_dots(2 * i + 1, sb_s)
        idx_scores(2 * i, sa_s)
        idx_dots(2 * i + 2, sa_s)
        idx_scores(2 * i + 1, sb_s)
        return carry

    lax.fori_loop(0, n_kt // 2, idx_pair, 0)

    @pl.when(n_kt % 2 == 1)
    def _():
        idx_scores(n_kt - 1, sa_s)

    lo = jnp.min(mn_s[...], axis=0, keepdims=True)
    hi = jnp.max(mx_s[...], axis=0, keepdims=True)
    scale = KEY_TOP / jnp.maximum(hi - lo, KEY_MIN_RANGE)

    def key_body(kj, carry):
        y = jnp.maximum((key_s[kj] - lo) * scale, -1.0)
        rem = y
        for p in (3, 2, 1, 0):
            d = jnp.floor(rem)
            dig_s[p, kj] = d.astype(BF16)
            rem = (rem - d) * 256.0
        key_s[kj] = y - rem * (2.0 ** -32)
        return carry

    lax.fori_loop(0, n_kt, key_body, 0)

    n_search = jnp.where(qi == 0, 0, n_kt)

    def count_active(pred, then=None):
        one, zero = jnp.ones((), BF16), jnp.zeros((), BF16)

        def body(kj, acc):
            a = act_s[kj]
            c = jnp.where(pred(a), one, zero).reshape(tk // 16, 16, tq)
            part = c[0]
            for j in range(1, tk // 16):
                part = part + c[j]
            if then is not None:
                then(kj, a)
            return acc + part.astype(F32)
        acc = lax.fori_loop(0, n_search, body, jnp.zeros((16, tq), F32))
        return jnp.sum(acc, axis=0, keepdims=True)

    def to_digit(v):
        return v.astype(F32).astype(BF16)

    def digit_phase(p, prev_digit, target):
        if p == 3:
            def init_body(kj, carry):
                act_s[kj] = dig_s[3, kj]
                return carry
            lax.fori_loop(0, n_search, init_body, 0)
        else:
            prev = to_digit(prev_digit)

            def narrow(kj, a):
                act_s[kj] = jnp.where(a == prev, dig_s[p, kj], jnp.full((), -1.0, BF16))
            target = target - count_active(lambda a: a > prev, then=narrow)
        n_bits = 7 if p == 3 else 8

        def bit_body(i, dgt):
            cand = dgt | jnp.left_shift(jnp.int32(1), n_bits - 1 - i)
            cand_d = to_digit(cand)
            cnt = count_active(lambda a: a >= cand_d)
            return jnp.where(cnt >= target, cand, dgt)
        return lax.fori_loop(0, n_bits, bit_body, jnp.zeros((1, tq), jnp.int32)), target

    target = jnp.full((1, tq), float(n_sel), F32)
    tau = jnp.zeros((1, tq), F32)
    digit = None
    for p in (3, 2, 1):
        digit, target = digit_phase(p, digit, target)
        tau = tau + digit.astype(F32) * (256.0 ** (p - 3))
    digit_d = to_digit(digit)
    excess = count_active(lambda a: a >= digit_d) - target

    def count_keys(pred):
        def body(kj, acc):
            c = jnp.where(pred(key_s[kj], kpos + kj * tk), 1.0, 0.0)
            return acc + jnp.sum(c.reshape(tk // 8, 8, tq), axis=0)
        acc = lax.fori_loop(0, n_kt, body, jnp.zeros((8, tq), F32))
        return jnp.sum(acc, axis=0, keepdims=True)

    first = qi == 0
    cs_s[...] = jnp.where(lax.broadcasted_iota(jnp.int32, cs_s.shape, 0) == 0, seq, 0)

    @pl.when(jnp.logical_not(first) & (jnp.max(excess) > 0.0))
    def _():
        digit0, target0 = digit_phase(0, digit, target)
        cs_s[1:2, :] = digit0
        tau0 = tau + digit0.astype(F32) * (256.0 ** -3)
        last = to_digit(digit0)
        n_gt = count_active(lambda a: a > last)
        n_eq = count_active(lambda a: a == last)
        need = target0 - n_gt

        @pl.when(jnp.max(n_eq - need) > 0.0)
        def _():
            nbits = seq.bit_length() - 1

            def tie_body(i, d):
                cand = d | jnp.left_shift(jnp.int32(1), nbits - 1 - i)
                cnt = count_keys(lambda k, pos: (k == tau0) & (pos < cand))
                return jnp.where(cnt < need, cand, d)

            cs_s[0:1, :] = lax.fori_loop(0, nbits, tie_body, jnp.zeros((1, tq), jnp.int32))

    tau = jnp.where(first, -0.5, tau + cs_s[1:2, :].astype(F32) * (256.0 ** -3))
    cstar = jnp.where(first, -1, cs_s[0:1, :])

    m_s[...] = jnp.full(m_s.shape, NEG_BIG, F32)
    acc_s[...] = jnp.zeros(acc_s.shape, F32)

    def tile_of(t):
        tc = jnp.minimum(t, n_kt - 1)
        return jnp.where(tc < 2, qi - tc, tc - 2)

    def scores(t, s_ref, mx_ref, near=None):
        kc = tile_of(t)
        ks = pl.multiple_of(kc * tk, tk)
        k = key_s[kc]
        sel = ((k > tau) | ((k == tau) & (kpos + kc * tk <= cstar))) & (t < n_kt)
        selb = jnp.where(sel, 0.0, NEG_BIG)
        s = _dot_nt(ckv_ref[pl.ds(ks, tk), :], qa_s[...])
        if near is not None:
            s = s + bias_s[near]
        s = s + jnp.concatenate([selb] * nh, axis=1)
        s_ref[...] = s
        mx_ref[...] = jnp.max(s.reshape(tk // 8, 8, nhq), axis=0)

    def update(t, s_ref, mx_ref):
        m_old = m_s[0:1, :]
        m_new = jnp.maximum(m_old, jnp.max(mx_ref[...], axis=0, keepdims=True))
        alpha = jnp.exp2(m_old - m_new)
        p = jnp.exp2(s_ref[...] - m_new).astype(BF16)
        acc_s[...] = acc_s[...] * alpha + _dot(ckvt_ref[tile_of(t)], p)
        m_s[...] = jnp.broadcast_to(m_new, m_s.shape)

    scores(0, sa_s, ma_s, near=0)

    @pl.when(n_kt >= 2)
    def _():
        scores(1, sb_s, mb_s, near=1)
        update(0, sa_s, ma_s)
        scores(2, sa_s, ma_s)
        update(1, sb_s, mb_s)

    def pair_body(i, carry):
        scores(2 * i + 1, sb_s, mb_s)
        update(2 * i, sa_s, ma_s)
        scores(2 * i + 2, sa_s, ma_s)
        update(2 * i + 1, sb_s, mb_s)
        return carry

    lax.fori_loop(1, n_kt // 2, pair_body, 0)

    @pl.when(n_kt % 2 == 1)
    def _():
        update(n_kt - 1, sa_s, ma_s)

    acc = acc_s[...]
    o_t = acc[:KV_RANK] / acc[KV_RANK:]
    for h in range(nh):
        o_ref[:, h * KV_RANK:(h + 1) * KV_RANK] = jnp.transpose(o_t[:, h * tq:(h + 1) * tq]).astype(BF16)


def _dsa(rel_bias, qabs, qidx, widx, ckv, ckvt, kk, batch, seq):
    n = qabs.shape[0]
    tq = ATT_TILE
    nq = seq // tq
    n_sel = min(TOPK_MAX, seq // 4)
    assert n_sel == tq and seq % tq == 0 and seq & (seq - 1) == 0
    assert MAX_EXACT * (MAX_DISTANCE / MAX_EXACT) ** ((N_BUCKETS - 1 - MAX_EXACT) / (N_BUCKETS - MAX_EXACT)) < tq
    qrow = lambda w: pl.BlockSpec((tq, w), lambda b, q: (b * nq + q, 0))
    brow = lambda w: pl.BlockSpec((seq, w), lambda b, q: (b, 0))
    nhq = N_HEADS_A * tq
    return pl.pallas_call(
        functools.partial(_dsa_kernel, n_sel=n_sel, seq=seq),
        grid=(batch, nq),
        in_specs=[pl.BlockSpec(memory_space=pltpu.SMEM),
                  qrow(N_HEADS_A * KV_RANK), qrow(IDX_HEADS * IDX_DIM), qrow(LANES),
                  brow(KV_RANK), pl.BlockSpec((nq, 2 * KV_RANK, tq), lambda b, q: (b, 0, 0)), brow(LANES)],
        out_specs=qrow(N_HEADS_A * KV_RANK),
        out_shape=jax.ShapeDtypeStruct((n, N_HEADS_A * KV_RANK), BF16),
        scratch_shapes=[pltpu.VMEM((nhq, KV_RANK), BF16), pltpu.VMEM((nhq, LANES), BF16),
                        pltpu.VMEM((nq, tq, tq), F32), pltpu.VMEM((4, nq, tq, tq), BF16),
                        pltpu.VMEM((nq, tq, tq), BF16), pltpu.VMEM((8, tq), jnp.int32),
                        pltpu.VMEM((8, tq), F32), pltpu.VMEM((8, tq), F32),
                        pltpu.VMEM((2, tq, nhq), F32),
                        pltpu.VMEM((8, nhq), F32), pltpu.VMEM((2 * KV_RANK, nhq), F32),
                        pltpu.VMEM((tq, nhq), F32), pltpu.VMEM((tq, nhq), F32),
                        pltpu.VMEM((8, nhq), F32), pltpu.VMEM((8, nhq), F32)],
        compiler_params=pltpu.CompilerParams(dimension_semantics=("arbitrary", "arbitrary"),
                                             vmem_limit_bytes=VMEM_LIMIT),
        name="dsa",
    )(rel_bias, qabs, qidx, widx, ckv, ckvt, kk)


def _stick_kernel(q_ref, k_ref, v_ref, o_ref):
    for j in range(STICK_Q_TILES):
        _stick_tile(pl.program_id(1) * STICK_Q_TILES + j, q_ref.at[j * ATT_TILE:(j + 1) * ATT_TILE],
                    k_ref, v_ref, o_ref.at[j * ATT_TILE:(j + 1) * ATT_TILE])


def _stick_tile(qi, q_ref, k_ref, v_ref, o_ref):
    tq = tk = ATT_TILE
    npair = WIDTH_B // LANES
    lane = lax.broadcasted_iota(jnp.int32, (tq, LANES), 1)
    row = lax.broadcasted_iota(jnp.int32, (tq, tk), 0)
    col = lax.broadcasted_iota(jnp.int32, (tq, tk), 1)
    causal = col < row
    upper = jnp.where(row > col, 1.0, 0.0).astype(BF16)
    upper_ones = jnp.concatenate([upper, jnp.ones((tk, LANES), BF16)], axis=1)
    lo_half = jnp.where(lane < HEAD_DIM, 1.0, 0.0)
    causal2 = jnp.concatenate([causal, causal], axis=0)
    q_pairs = []
    for p in range(npair):
        q2 = q_ref[:, p * LANES:(p + 1) * LANES].astype(F32)
        q_pairs.append(jnp.concatenate([q2 * lo_half, q2 * (1.0 - lo_half)], axis=0).astype(BF16))

    def tile(kj, r_sums, outs, masked):
        ks = pl.multiple_of(kj * tk, tk)
        log_sig, log_1m = [], []
        for p in range(npair):
            z = _dot_nt(q_pairs[p], k_ref[pl.ds(ks, tk), p * LANES:(p + 1) * LANES])
            sp = jnp.maximum(z, 0.0) + jnp.log2(1.0 + jnp.exp2(-jnp.abs(z)))
            lm = jnp.where(causal2, -sp, 0.0) if masked else -sp
            log_sig.append(z - sp)
            log_1m.append(lm.astype(BF16))
        sums = _dot(jnp.concatenate(log_1m, axis=0), upper_ones)
        new_r, new_o = [], []
        for p in range(npair):
            sm = sums[p * 2 * tq:(p + 1) * 2 * tq]
            after = sm[:, :tk] + jnp.concatenate([r_sums[p]] * (tk // LANES), axis=1)
            a = jnp.exp2(log_sig[p] + after)
            if masked:
                a = jnp.where(causal2, a, 0.0)
            res = _dot(a.astype(BF16), v_ref[pl.ds(ks, tk), p * LANES:(p + 1) * LANES])
            new_o.append(outs[p] + jnp.where(lane < HEAD_DIM, res[:tq], res[tq:]))
            new_r.append(r_sums[p] + sm[:, tk:])
        return tuple(new_r), tuple(new_o)

    zeros = jnp.zeros((tq, LANES), F32)
    r_sums, outs = tile(qi, (jnp.zeros((2 * tq, LANES), F32),) * npair, (zeros,) * npair, True)

    def cond(c):
        it, r_sums, _ = c
        return (it < qi) & (jnp.max(functools.reduce(jnp.maximum, r_sums)) > EXP_UNDERFLOW)

    def body(c):
        it, r_sums, outs = c
        r_sums, outs = tile(qi - 1 - it, r_sums, outs, False)
        return it + 1, r_sums, outs

    _, _, outs = lax.while_loop(cond, body, (jnp.int32(0), r_sums, outs))
    for p in range(npair):
        o_ref[:, p * LANES:(p + 1) * LANES] = outs[p].astype(BF16)


def _stick(qb, kb, vb, batch, seq):
    n = qb.shape[0]
    tq = ATT_TILE
    nq = seq // tq
    nqs = nq // STICK_Q_TILES
    qspec = pl.BlockSpec((STICK_Q_TILES * tq, WIDTH_B), lambda b, q: (b * nqs + q, 0))
    kspec = pl.BlockSpec((seq, WIDTH_B), lambda b, q: (b, 0))
    return pl.pallas_call(
        _stick_kernel,
        grid=(batch, nqs),
        in_specs=[qspec, kspec, kspec],
        out_specs=qspec,
        out_shape=jax.ShapeDtypeStruct((n, WIDTH_B), BF16),
        compiler_params=pltpu.CompilerParams(dimension_semantics=("arbitrary",) * 2, vmem_limit_bytes=VMEM_LIMIT),
        name="stick",
    )(qb, kb, vb)


def _merge_kernel(x_ref, ol_ref, ob_ref, sa_ref, sb_ref, wuv_ref, wba_ref, wbb_ref, wout_ref, g_ref,
                  wr_ref, br_ref, x1_ref, h2_ref, comb_ref):
    half = N_HEADS_A * KV_RANK // 2
    oa = jnp.concatenate([_dot(ol_ref[:, :half], wuv_ref[:half, :WIDTH_A // 2]),
                          _dot(ol_ref[:, half:], wuv_ref[half:, WIDTH_A // 2:])], axis=1).astype(BF16)
    ya = _dot(oa, wba_ref[...])
    yb = _dot(ob_ref[...], wbb_ref[...])
    merged = sa_ref[...].astype(F32) * ya + sb_ref[...].astype(F32) * yb
    x1 = x_ref[...] + _dot(merged.astype(BF16), wout_ref[...])
    x1_ref[...] = x1
    h2 = _rms(x1, g_ref[...])
    h2_hi = h2.astype(BF16)
    h2_ref[...] = h2_hi
    h2_lo = (h2 - h2_hi.astype(F32)).astype(BF16)
    hi_part = _dot(h2_hi, wr_ref[...])
    logits = (hi_part[:, :LANES] + _dot(h2_lo, wr_ref[:, :LANES]) + hi_part[:, LANES:]) + br_ref[...]
    lane = lax.broadcasted_iota(jnp.int32, logits.shape, 1).astype(F32)
    ninf = -jnp.inf

    def first_max(v):
        m = jnp.max(v, axis=1, keepdims=True)
        return m, jnp.min(jnp.where(v == m, lane, 1e9), axis=1, keepdims=True)

    gmask = (lane >= N_EXPERTS) & (lane < N_EXPERTS + N_GROUPS)
    gl = jnp.where(gmask, logits, ninf)
    gmax, gidx = first_max(gl)
    p_g = 1.0 / jnp.sum(jnp.where(gmask, jnp.exp(gl - gmax), 0.0), axis=1, keepdims=True)
    e_lo = (gidx - N_EXPERTS) * EXPERTS_PER_GROUP
    el = jnp.where((lane >= e_lo) & (lane < e_lo + EXPERTS_PER_GROUP), logits, ninf)
    v1, i1 = first_max(el)
    el2 = jnp.where(lane == i1, ninf, el)
    v2, i2 = first_max(el2)
    e2 = jnp.exp(v2 - v1)
    w1 = 1.0 / (1.0 + e2)
    comb_ref[...] = jnp.where(lane == i1, w1 * p_g,
                              jnp.where(lane == i2, (e2 * w1) * p_g, jnp.where(lane == gidx, 1.0, 0.0)))


def _merge(x, olat, ob, sa, sb, wuv, wba, wbb, wout, g, wr, br):
    n, d = x.shape
    tm = ROW_TILE
    row = lambda w: pl.BlockSpec((tm, w), lambda i: (i, 0))
    ws = (wuv, wba, wbb, wout, g, wr, br)
    return pl.pallas_call(
        _merge_kernel,
        grid=(n // tm,),
        in_specs=[row(d), row(olat.shape[1]), row(ob.shape[1]), row(d), row(d)] + [_full_spec(w.shape) for w in ws],
        out_specs=[row(d), row(d), row(LANES)],
        out_shape=[jax.ShapeDtypeStruct((n, d), F32), jax.ShapeDtypeStruct((n, d), BF16),
                   jax.ShapeDtypeStruct((n, LANES), F32)],
        compiler_params=pltpu.CompilerParams(dimension_semantics=("arbitrary",), vmem_limit_bytes=VMEM_LIMIT),
        name="merge",
    )(x, olat, ob, sa, sb, *ws)


def _moe_kernel(h_ref, comb_ref, wg_ref, wu_ref, wd_ref, o_ref, slotc_s, slotr_s, split_s, acc_s, rng_s):
    t, d = h_ref.shape
    nsub = MOE_SUBTILES
    ts = t // nsub
    c = MOE_CHUNK
    rb = MOE_RANK_BLOCK
    g = pl.program_id(1)

    @pl.when(g == 0)
    def _():
        lane_b = lax.broadcasted_iota(jnp.int32, (rb, LANES), 1)
        group_lanes = (lane_b >= N_EXPERTS) & (lane_b < N_EXPERTS + N_GROUPS)
        r_i = lax.broadcasted_iota(jnp.int32, (rb, rb), 0)
        c_i = lax.broadcasted_iota(jnp.int32, (rb, rb), 1)
        lower = jnp.where(c_i < r_i, 1.0, 0.0).astype(BF16)
        lane1 = lax.broadcasted_iota(jnp.int32, (1, LANES), 1)
        for sub in range(nsub):
            counts = jnp.zeros((1, LANES), F32)
            ranks = []
            for blk in range(ts // rb):
                rows = slice(sub * ts + blk * rb, sub * ts + (blk + 1) * rb)
                oh = jnp.where(group_lanes, comb_ref[rows, :], 0.0)
                before = _dot(lower, oh.astype(BF16)) + counts
                ranks.append(jnp.sum(before * oh, axis=1, keepdims=True))
                counts = counts + jnp.sum(oh, axis=0, keepdims=True)
            start = jnp.int32(0)
            seg = jnp.zeros((1, LANES), F32)
            for k in range(N_GROUPS):
                n_k = jnp.sum(jnp.where(lane1 == N_EXPERTS + k, counts, 0.0)).astype(jnp.int32)
                seg = jnp.where(lane1 == N_EXPERTS + k, (start * c).astype(F32), seg)
                rng_s[sub * 2 * N_GROUPS + k] = start
                for j in range(-(-ts // c)):
                    start = start + (n_k > j * c).astype(jnp.int32)
                rng_s[sub * 2 * N_GROUPS + N_GROUPS + k] = start
            for blk in range(ts // rb):
                rows = slice(sub * ts + blk * rb, sub * ts + (blk + 1) * rb)
                oh = jnp.where(group_lanes, comb_ref[rows, :], 0.0)
                slot = ranks[blk] + jnp.sum(oh * seg, axis=1, keepdims=True)
                slotc_s[rows, :] = jnp.broadcast_to(slot, (rb, LANES))
        slotr_s[...] = jnp.transpose(slotc_s[...])
        cb = comb_ref[...]
        hi = cb.astype(BF16)
        split_s[...] = jnp.concatenate([hi, (cb - hi.astype(F32)).astype(BF16)], axis=1)
        acc_s[...] = jnp.zeros(acc_s.shape, F32)

    lane = lax.broadcasted_iota(jnp.int32, (nsub * c, LANES), 1)
    first = [rng_s[sub * 2 * N_GROUPS + g] for sub in range(nsub)]
    n_chunks = [rng_s[sub * 2 * N_GROUPS + N_GROUPS + g] - first[sub] for sub in range(nsub)]

    def chunk_body(j, carry):
        bases = [((first[sub] + j) * c).astype(F32) for sub in range(nsub)]
        xs, cw = [], []
        for sub in range(nsub):
            rows = slice(sub * ts, (sub + 1) * ts)
            row_slot = lax.broadcasted_iota(jnp.int32, (c, ts), 0).astype(F32) + bases[sub]
            pc = jnp.where(slotr_s[0:1, rows] == row_slot, 1.0, 0.0).astype(BF16)
            xs.append(_dot(pc, h_ref[rows, :]).astype(BF16))
            cw2 = _dot(pc, split_s[rows, :])
            cw.append(cw2[:, :LANES] + cw2[:, LANES:])
        xs = jnp.concatenate(xs, axis=0)
        cw = jnp.concatenate(cw, axis=0)
        y = jnp.zeros((nsub * c, d), F32)
        for e in range(EXPERTS_PER_GROUP):
            gate = _dot(xs, wg_ref[e])
            hid = gate * jax.nn.sigmoid(gate) * _dot(xs, wu_ref[e])
            ce = jnp.sum(jnp.where(lane == g * EXPERTS_PER_GROUP + e, cw, 0.0), axis=1, keepdims=True)
            y = y + _dot((hid * ce).astype(BF16), wd_ref[e])
        y = y.astype(BF16)
        for sub in range(nsub):
            rows = slice(sub * ts, (sub + 1) * ts)
            col_slot = lax.broadcasted_iota(jnp.int32, (ts, c), 1).astype(F32) + bases[sub]
            pct = jnp.where(jnp.broadcast_to(slotc_s[rows, 0:1], (ts, c)) == col_slot, 1.0, 0.0).astype(BF16)
            acc_s[rows, :] += _dot(pct, y[sub * c:(sub + 1) * c])
        return carry

    lax.fori_loop(0, functools.reduce(jnp.maximum, n_chunks), chunk_body, 0)

    @pl.when(g == N_GROUPS - 1)
    def _():
        o_ref[...] = acc_s[...].astype(BF16)


def _moe(h2, comb, wg, wu, wd):
    n, d = h2.shape
    tm = MOE_ROW_TILE
    row = lambda w: pl.BlockSpec((tm, w), lambda i, g: (i, 0))
    return pl.pallas_call(
        _moe_kernel,
        grid=(n // tm, N_GROUPS),
        in_specs=[row(d), row(LANES),
                  pl.BlockSpec((EXPERTS_PER_GROUP, d, D_EXPERT), lambda i, g: (g, 0, 0)),
                  pl.BlockSpec((EXPERTS_PER_GROUP, d, D_EXPERT), lambda i, g: (g, 0, 0)),
                  pl.BlockSpec((EXPERTS_PER_GROUP, D_EXPERT, d), lambda i, g: (g, 0, 0))],
        out_specs=row(d),
        out_shape=jax.ShapeDtypeStruct((n, d), BF16),
        scratch_shapes=[pltpu.VMEM((tm, LANES), F32), pltpu.VMEM((LANES, tm), F32),
                        pltpu.VMEM((tm, 2 * LANES), BF16), pltpu.VMEM((tm, d), F32),
                        pltpu.SMEM((MOE_SUBTILES * 2 * N_GROUPS,), jnp.int32)],
        compiler_params=pltpu.CompilerParams(dimension_semantics=("arbitrary", "arbitrary"),
                                             vmem_limit_bytes=MOE_VMEM_LIMIT),
        name="moe",
    )(h2, comb, wg, wu, wd)


def _ple_kernel(x_ref, m_ref, p_ref, g_ref, wpg_ref, wple_ref, gf_ref, o_ref, *, last):
    x2 = x_ref[...] + m_ref[...].astype(F32)
    h3 = _rms(x2, g_ref[...]).astype(BF16)
    gate = jax.nn.sigmoid(_dot(h3, wpg_ref[...]))
    x3 = x2 + _dot(p_ref[...].astype(BF16), wple_ref[...]) * gate
    o_ref[...] = _rms(x3, gf_ref[...]) if last else x3


def _ple(x1, moe, p, g, wpg, wple, gf, last):
    n, d = x1.shape
    tm = ROW_TILE
    row = lambda w: pl.BlockSpec((tm, w), lambda i: (i, 0))
    ws = (g, wpg, wple, gf)
    return pl.pallas_call(
        functools.partial(_ple_kernel, last=last),
        grid=(n // tm,),
        in_specs=[row(d), row(d), row(p.shape[1])] + [_full_spec(w.shape) for w in ws],
        out_specs=row(d),
        out_shape=jax.ShapeDtypeStruct((n, d), F32),
        compiler_params=pltpu.CompilerParams(dimension_semantics=("arbitrary",), vmem_limit_bytes=VMEM_LIMIT),
        name="ple",
    )(x1, moe, p, *ws)


def _block_diag(blocks):
    h, r, c = blocks.shape
    eye = jnp.eye(h, dtype=blocks.dtype)
    return (eye[:, None, :, None] * blocks[:, :, None, :]).reshape(h * r, h * c)


def kernel(x, p, attn_norm, w_in, kv_norm, w_uk, w_uv, rel_bias, w_branch_a, w_branch_b, w_out, ffn_norm,
           w_r1, b_r1, w_r2, b_r2, w_gate, w_up, w_down, ple_norm, w_ple_gate, w_ple, final_norm):
    batch, seq, d = x.shape
    n = batch * seq
    depth = w_in.shape[0]
    xf = x.reshape(n, d).astype(F32)
    widths = [WIDTH_A, KV_RANK, IDX_HEADS * IDX_DIM, IDX_DIM, IDX_HEADS, 3 * WIDTH_B, d, d]
    starts = [sum(widths[:k]) for k in range(len(widths))]
    for i in range(depth):
        w_qa, w_ckv, w_qi, w_ki, w_wi, w_qkv, w_ga, w_gb = [
            w_in[i][:, s:s + w].astype(BF16) for s, w in zip(starts, widths)]
        w_kk = jnp.concatenate([w_ki, w_ki], axis=1)
        w_cw = jnp.concatenate([w_ckv, jnp.pad(w_wi, ((0, 0), (0, LANES - IDX_HEADS)))], axis=1)
        wuk_bd = _block_diag(jnp.swapaxes(w_uk[i], 1, 2)).astype(BF16)
        wuv_bd = _block_diag(w_uv[i]).astype(BF16)
        qabs, ckv, ckvt, qidx, kk, widx, qb, kb, vb, sa, sb = _proj(
            xf, attn_norm[i][None].astype(F32), w_qa, wuk_bd, w_cw, kv_norm[i][None].astype(F32),
            w_qi, w_kk, w_qkv, w_ga, w_gb)
        olat = _dsa(rel_bias.astype(F32), qabs, qidx, widx, ckv, ckvt, kk, batch, seq)
        ob = _stick(qb, kb, vb, batch, seq)
        w_r = jnp.concatenate([jnp.transpose(w_r2[i], (1, 0, 2)).reshape(d, N_EXPERTS), w_r1[i]], axis=1)
        w_r = jnp.pad(w_r.astype(F32), ((0, 0), (0, LANES - N_EXPERTS - N_GROUPS)))
        w_r_hi = w_r.astype(BF16)
        w_r_lo = (w_r - w_r_hi.astype(F32)).astype(BF16)
        b_r = jnp.pad(jnp.concatenate([b_r2[i].reshape(-1), b_r1[i]]).astype(F32),
                      (0, LANES - N_EXPERTS - N_GROUPS))[None]
        x1, h2, comb = _merge(xf, olat, ob, sa, sb, wuv_bd, w_branch_a[i].astype(BF16), w_branch_b[i].astype(BF16),
                              w_out[i].astype(BF16), ffn_norm[i][None].astype(F32),
                              jnp.concatenate([w_r_hi, w_r_lo], axis=1), b_r)
        moe = _moe(h2, comb, w_gate[i].astype(BF16), w_up[i].astype(BF16), w_down[i].astype(BF16))
        xf = _ple(x1, moe, p[i].reshape(n, -1).astype(F32), ple_norm[i][None].astype(F32), w_ple_gate[i].astype(BF16),
                  w_ple[i].astype(BF16), final_norm[None].astype(F32), last=(i == depth - 1))
    return xf.reshape(batch, seq, d).astype(x.dtype)
```

```python
import functools
import math

import jax
import jax.numpy as jnp
from jax import lax
from jax.experimental import pallas as pl
from jax.experimental.pallas import tpu as pltpu

D_MODEL = 1024
N_HEADS_A = 8
HEAD_DIM = 64
WIDTH_A = N_HEADS_A * HEAD_DIM
KV_RANK = 128
IDX_HEADS = 8
IDX_DIM = 64
TOPK_MAX = 256
N_HEADS_B = 8
WIDTH_B = N_HEADS_B * HEAD_DIM
N_BUCKETS = 32
MAX_EXACT = N_BUCKETS // 2
MAX_DISTANCE = 128
ATTN_SCALE = HEAD_DIM ** -0.5
IDX_SCALE = (IDX_HEADS ** -0.5) * (IDX_DIM ** -0.5)
N_GROUPS = 4
EXPERTS_PER_GROUP = 8
N_EXPERTS = N_GROUPS * EXPERTS_PER_GROUP
D_EXPERT = 256
PLE_DIM = 256
EPS = 1e-6
LOG2E = 1.4426950408889634

LANES = 128
ROW_TILE = 1024
MOE_ROW_TILE = 2048
MOE_SUBTILES = 4
MOE_CHUNK = 160
MOE_RANK_BLOCK = 256
ATT_TILE = 256
STICK_Q_TILES = 2
ONES_ROWS = 16
VMEM_LIMIT = 56 * 1024 * 1024
MOE_VMEM_LIMIT = 60 * 1024 * 1024
NEG_BIG = -1e30
EXP_UNDERFLOW = -173.0
KEY_TOP = 127.0
KEY_MIN_RANGE = 1e-30

F32 = jnp.float32
BF16 = jnp.bfloat16


def _rms(x, g):
    return x * lax.rsqrt(jnp.mean(x * x, axis=-1, keepdims=True) + EPS) * g


def _dot(a, b):
    return jnp.dot(a, b, preferred_element_type=F32)


def _dot_nt(a, b):
    return lax.dot_general(a, b, (((1,), (1,)), ((), ())), preferred_element_type=F32)


def _full_spec(shape):
    nd = len(shape)
    return pl.BlockSpec(shape, lambda *_: (0,) * nd)


def _proj_kernel(x_ref, g_ref, wqa_ref, wuk_ref, wcw_ref, kvg_ref, wqi_ref, wkk_ref,
                 wqkv_ref, wga_ref, wgb_ref,
                 qabs_ref, ckv_ref, ckvt_ref, qidx_ref, kk_ref, widx_ref, qb_ref, kb_ref, vb_ref, sa_ref, sb_ref):
    h = _rms(x_ref[...], g_ref[...]).astype(BF16)
    qa = _dot(h, wqa_ref[...]).astype(BF16)
    half = WIDTH_A // 2
    qabs = jnp.concatenate([_dot(qa[:, :half], wuk_ref[:half, :N_HEADS_A * KV_RANK // 2]),
                            _dot(qa[:, half:], wuk_ref[half:, N_HEADS_A * KV_RANK // 2:])], axis=1)
    qabs_ref[...] = (qabs * (ATTN_SCALE * LOG2E)).astype(BF16)
    cw = _dot(h, wcw_ref[...])
    c = _rms(cw[:, :KV_RANK], kvg_ref[...])
    ckv_ref[...] = c.astype(BF16)
    for j in range(ckvt_ref.shape[0]):
        c_t = jnp.transpose(c[j * ATT_TILE:(j + 1) * ATT_TILE, :])
        ckvt_ref[j] = jnp.concatenate([c_t, jnp.ones((ONES_ROWS, ATT_TILE), F32)], axis=0).astype(BF16)
    qidx_ref[...] = _dot(h, wqi_ref[...]).astype(BF16)
    kk_ref[...] = _dot(h, wkk_ref[...]).astype(BF16)
    widx_ref[...] = cw[:, KV_RANK:] * IDX_SCALE
    qkv = _dot(h, wqkv_ref[...])
    qb_ref[...] = (qkv[:, :WIDTH_B] * (ATTN_SCALE * LOG2E)).astype(BF16)
    kb_ref[...] = qkv[:, WIDTH_B:2 * WIDTH_B].astype(BF16)
    vb_ref[...] = qkv[:, 2 * WIDTH_B:].astype(BF16)
    sa_ref[...] = jax.nn.sigmoid(_dot(h, wga_ref[...])).astype(BF16)
    sb_ref[...] = jax.nn.sigmoid(_dot(h, wgb_ref[...])).astype(BF16)


def _proj(x, g, wqa, wuk, wcw, kvg, wqi, wkk, wqkv, wga, wgb):
    n, d = x.shape
    tm = ROW_TILE
    row = lambda w: pl.BlockSpec((tm, w), lambda i: (i, 0))
    ws = (g, wqa, wuk, wcw, kvg, wqi, wkk, wqkv, wga, wgb)
    outs = [(N_HEADS_A * KV_RANK, BF16), (KV_RANK, BF16), None, (IDX_HEADS * IDX_DIM, BF16), (LANES, BF16),
            (LANES, F32), (WIDTH_B, BF16), (WIDTH_B, BF16), (WIDTH_B, BF16), (d, BF16), (d, BF16)]
    t = ATT_TILE
    ckvt_spec = pl.BlockSpec((tm // t, KV_RANK + ONES_ROWS, t), lambda i: (i, 0, 0))
    ckvt_shape = jax.ShapeDtypeStruct((n // t, KV_RANK + ONES_ROWS, t), BF16)
    return pl.pallas_call(
        _proj_kernel,
        grid=(n // tm,),
        in_specs=[row(d)] + [_full_spec(w.shape) for w in ws],
        out_specs=[ckvt_spec if o is None else row(o[0]) for o in outs],
        out_shape=[ckvt_shape if o is None else jax.ShapeDtypeStruct((n, o[0]), o[1]) for o in outs],
        compiler_params=pltpu.CompilerParams(dimension_semantics=("arbitrary",), vmem_limit_bytes=VMEM_LIMIT),
        name="proj",
    )(x, *ws)


def _t5_bucket(dist):
    dist = jnp.maximum(dist, 0)
    d_f = jnp.maximum(dist, 1).astype(F32)
    large = MAX_EXACT + jnp.floor(jnp.log(d_f / MAX_EXACT) / math.log(MAX_DISTANCE / MAX_EXACT)
                                  * (N_BUCKETS - MAX_EXACT)).astype(jnp.int32)
    large = jnp.minimum(large, N_BUCKETS - 1)
    return jnp.where(dist < MAX_EXACT, dist, large)


def _dsa_kernel(relb_ref, qabs_ref, qidx_ref, widx_ref, ckv_ref, ckvt_ref, kk_ref, o_ref,
                qa_s, qi_s, key_s, dig_s, act_s, cs_s, mn_s, mx_s, bias_s, m_s, acc_s, sa_s, sb_s, ma_s, mb_s, *, n_sel, seq):
    tq = tk = ATT_TILE
    nh = N_HEADS_A
    nhq = nh * tq
    qi = pl.program_id(1)
    n_kt = qi + 1
    kpos = lax.broadcasted_iota(jnp.int32, (tk, tq), 0)
    qpos = lax.broadcasted_iota(jnp.int32, (tk, tq), 1)

    @pl.when((pl.program_id(0) == 0) & (qi == 0))
    def _():
        for t in range(2):
            bucket = _t5_bucket(t * tq + qpos - kpos)
            for h in range(nh):
                b = jnp.zeros((tk, tq), F32)
                for bk in range(N_BUCKETS):
                    b = jnp.where(bucket == bk, relb_ref[bk, h], b)
                bias_s[t, :, h * tq:(h + 1) * tq] = (b - relb_ref[N_BUCKETS - 1, h]) * LOG2E

    lane = lax.broadcasted_iota(jnp.int32, (tq, LANES), 1)
    lo_half = jnp.where(lane < IDX_DIM, 1.0, 0.0)
    hi_half = 1.0 - lo_half
    for h in range(nh):
        qa_s[h * tq:(h + 1) * tq, :] = qabs_ref[:, h * KV_RANK:(h + 1) * KV_RANK]
        pair = qidx_ref[:, (h // 2) * LANES:(h // 2 + 1) * LANES].astype(F32)
        qi_s[h * tq:(h + 1) * tq, :] = (pair * (lo_half if h % 2 == 0 else hi_half)).astype(BF16)
    w_t = jnp.transpose(widx_ref[...])
    w_row = jnp.concatenate([w_t[h:h + 1, :] for h in range(nh)], axis=1)

    mn_s[...] = jnp.full(mn_s.shape, -NEG_BIG, F32)
    mx_s[...] = jnp.full(mx_s.shape, NEG_BIG, F32)

    def idx_dots(kj, r_ref):
        ks = pl.multiple_of(jnp.minimum(kj, n_kt - 1) * tk, tk)
        r_ref[...] = _dot_nt(kk_ref[pl.ds(ks, tk), :], qi_s[...])

    def idx_scores(kj, r_ref):
        r = jnp.maximum(r_ref[...], 0.0) * w_row
        s = r[:, :tq]
        for h in range(1, nh):
            s = s + r[:, h * tq:(h + 1) * tq]
        masked = (kpos > qpos) & (kj == qi)
        key_s[kj] = jnp.where(masked, NEG_BIG, s)
        mn_s[...] = jnp.minimum(mn_s[...], jnp.min(jnp.where(masked, -NEG_BIG, s).reshape(tk // 8, 8, tq), axis=0))
        mx_s[...] = jnp.maximum(mx_s[...], jnp.max(jnp.where(masked, NEG_BIG, s).reshape(tk // 8, 8, tq), axis=0))

    idx_dots(0, sa_s)

    def idx_pair(i, carry):
        idx_dots(2 * i + 1, sb_s)
        idx_scores(2 * i, sa_s)
        idx_dots(2 * i + 2, sa_s)
        idx_scores(2 * i + 1, sb_s)
        return carry

    lax.fori_loop(0, n_kt // 2, idx_pair, 0)

    @pl.when(n_kt % 2 == 1)
    def _():
        idx_scores(n_kt - 1, sa_s)

    lo = jnp.min(mn_s[...], axis=0, keepdims=True)
    hi = jnp.max(mx_s[...], axis=0, keepdims=True)
    scale = KEY_TOP / jnp.maximum(hi - lo, KEY_MIN_RANGE)

    def key_body(kj, carry):
        y = jnp.maximum((key_s[kj] - lo) * scale, -1.0)
        rem = y
        for p in (3, 2, 1, 0):
            d = jnp.floor(rem)
            dig_s[p, kj] = d.astype(BF16)
            rem = (rem - d) * 256.0
        key_s[kj] = y - rem * (2.0 ** -32)
        return carry

    lax.fori_loop(0, n_kt, key_body, 0)

    n_search = jnp.where(qi == 0, 0, n_kt)

    def count_active(pred, then=None):
        one, zero = jnp.ones((), BF16), jnp.zeros((), BF16)

        def body(kj, acc):
            a = act_s[kj]
            c = jnp.where(pred(a), one, zero).reshape(tk // 16, 16, tq)
            part = c[0]
            for j in range(1, tk // 16):
                part = part + c[j]
            if then is not None:
                then(kj, a)
            return acc + part.astype(F32)
        acc = lax.fori_loop(0, n_search, body, jnp.zeros((16, tq), F32))
        return jnp.sum(acc, axis=0, keepdims=True)

    def to_digit(v):
        return v.astype(F32).astype(BF16)

    def digit_phase(p, prev_digit, target):
        if p == 3:
            def init_body(kj, carry):
                act_s[kj] = dig_s[3, kj]
                return carry
            lax.fori_loop(0, n_search, init_body, 0)
        else:
            prev = to_digit(prev_digit)

            def narrow(kj, a):
                act_s[kj] = jnp.where(a == prev, dig_s[p, kj], jnp.full((), -1.0, BF16))
            target = target - count_active(lambda a: a > prev, then=narrow)
        n_bits = 7 if p == 3 else 8

        def bit_body(i, dgt):
            cand = dgt | jnp.left_shift(jnp.int32(1), n_bits - 1 - i)
            cand_d = to_digit(cand)
            cnt = count_active(lambda a: a >= cand_d)
            return jnp.where(cnt >= target, cand, dgt)
        return lax.fori_loop(0, n_bits, bit_body, jnp.zeros((1, tq), jnp.int32)), target

    target = jnp.full((1, tq), float(n_sel), F32)
    tau = jnp.zeros((1, tq), F32)
    digit = None
    for p in (3, 2, 1):
        digit, target = digit_phase(p, digit, target)
        tau = tau + digit.astype(F32) * (256.0 ** (p - 3))
    digit_d = to_digit(digit)
    excess = count_active(lambda a: a >= digit_d) - target

    def count_keys(pred):
        def body(kj, acc):
            c = jnp.where(pred(key_s[kj], kpos + kj * tk), 1.0, 0.0)
            return acc + jnp.sum(c.reshape(tk // 8, 8, tq), axis=0)
        acc = lax.fori_loop(0, n_kt, body, jnp.zeros((8, tq), F32))
        return jnp.sum(acc, axis=0, keepdims=True)

    first = qi == 0
    cs_s[...] = jnp.where(lax.broadcasted_iota(jnp.int32, cs_s.shape, 0) == 0, seq, 0)

    @pl.when(jnp.logical_not(first) & (jnp.max(excess) > 0.0))
    def _():
        digit0, target0 = digit_phase(0, digit, target)
        cs_s[1:2, :] = digit0
        tau0 = tau + digit0.astype(F32) * (256.0 ** -3)
        last = to_digit(digit0)
        n_gt = count_active(lambda a: a > last)
        n_eq = count_active(lambda a: a == last)
        need = target0 - n_gt

        @pl.when(jnp.max(n_eq - need) > 0.0)
        def _():
            nbits = seq.bit_length() - 1

            def tie_body(i, d):
                cand = d | jnp.left_shift(jnp.int32(1), nbits - 1 - i)
                cnt = count_keys(lambda k, pos: (k == tau0) & (pos < cand))
                return jnp.where(cnt < need, cand, d)

            cs_s[0:1, :] = lax.fori_loop(0, nbits, tie_body, jnp.zeros((1, tq), jnp.int32))

    tau = jnp.where(first, -0.5, tau + cs_s[1:2, :].astype(F32) * (256.0 ** -3))
    cstar = jnp.where(first, -1, cs_s[0:1, :])

    m_s[...] = jnp.full(m_s.shape, NEG_BIG, F32)
    acc_s[...] = jnp.zeros(acc_s.shape, F32)

    def tile_of(t):
        tc = jnp.minimum(t, n_kt - 1)
        return jnp.where(tc < 2, qi - tc, tc - 2)

    def scores(t, s_ref, mx_ref, near=None):
        kc = tile_of(t)
        ks = pl.multiple_of(kc * tk, tk)
        k = key_s[kc]
        sel = ((k > tau) | ((k == tau) & (kpos + kc * tk <= cstar))) & (t < n_kt)
        selb = jnp.where(sel, 0.0, NEG_BIG)
        s = _dot_nt(ckv_ref[pl.ds(ks, tk), :], qa_s[...])
        if near is not None:
            s = s + bias_s[near]
        s = s + jnp.concatenate([selb] * nh, axis=1)
        s_ref[...] = s
        mx_ref[...] = jnp.max(s.reshape(tk // 8, 8, nhq), axis=0)

    def update(t, s_ref, mx_ref):
        m_old = m_s[0:1, :]
        m_new = jnp.maximum(m_old, jnp.max(mx_ref[...], axis=0, keepdims=True))
        alpha = jnp.exp2(m_old - m_new)
        p = jnp.exp2(s_ref[...] - m_new).astype(BF16)
        acc_s[...] = acc_s[...] * alpha + _dot(ckvt_ref[tile_of(t)], p)
        m_s[...] = jnp.broadcast_to(m_new, m_s.shape)

    scores(0, sa_s, ma_s, near=0)

    @pl.when(n_kt >= 2)
    def _():
        scores(1, sb_s, mb_s, near=1)
        update(0, sa_s, ma_s)
        scores(2, sa_s, ma_s)
        update(1, sb_s, mb_s)

    def pair_body(i, carry):
        scores(2 * i + 1, sb_s, mb_s)
        update(2 * i, sa_s, ma_s)
        scores(2 * i + 2, sa_s, ma_s)
        update(2 * i + 1, sb_s, mb_s)
        return carry

    lax.fori_loop(1, n_kt // 2, pair_body, 0)

    @pl.when(n_kt % 2 == 1)
    def _():
        update(n_kt - 1, sa_s, ma_s)

    acc = acc_s[...]
    o_t = acc[:KV_RANK] / acc[KV_RANK:KV_RANK + 1]
    for h in range(nh):
        o_ref[:, h * KV_RANK:(h + 1) * KV_RANK] = jnp.transpose(o_t[:, h * tq:(h + 1) * tq]).astype(BF16)


def _dsa(rel_bias, qabs, qidx, widx, ckv, ckvt, kk, batch, seq):
    n = qabs.shape[0]
    tq = ATT_TILE
    nq = seq // tq
    n_sel = min(TOPK_MAX, seq // 4)
    assert n_sel == tq and seq % tq == 0 and seq & (seq - 1) == 0
    assert MAX_EXACT * (MAX_DISTANCE / MAX_EXACT) ** ((N_BUCKETS - 1 - MAX_EXACT) / (N_BUCKETS - MAX_EXACT)) < tq
    qrow = lambda w: pl.BlockSpec((tq, w), lambda b, q: (b * nq + q, 0))
    brow = lambda w: pl.BlockSpec((seq, w), lambda b, q: (b, 0))
    nhq = N_HEADS_A * tq
    return pl.pallas_call(
        functools.partial(_dsa_kernel, n_sel=n_sel, seq=seq),
        grid=(batch, nq),
        in_specs=[pl.BlockSpec(memory_space=pltpu.SMEM),
                  qrow(N_HEADS_A * KV_RANK), qrow(IDX_HEADS * IDX_DIM), qrow(LANES),
                  brow(KV_RANK), pl.BlockSpec((nq, KV_RANK + ONES_ROWS, tq), lambda b, q: (b, 0, 0)), brow(LANES)],
        out_specs=qrow(N_HEADS_A * KV_RANK),
        out_shape=jax.ShapeDtypeStruct((n, N_HEADS_A * KV_RANK), BF16),
        scratch_shapes=[pltpu.VMEM((nhq, KV_RANK), BF16), pltpu.VMEM((nhq, LANES), BF16),
                        pltpu.VMEM((nq, tq, tq), F32), pltpu.VMEM((4, nq, tq, tq), BF16),
                        pltpu.VMEM((nq, tq, tq), BF16), pltpu.VMEM((8, tq), jnp.int32),
                        pltpu.VMEM((8, tq), F32), pltpu.VMEM((8, tq), F32),
                        pltpu.VMEM((2, tq, nhq), F32),
                        pltpu.VMEM((8, nhq), F32), pltpu.VMEM((KV_RANK + ONES_ROWS, nhq), F32),
                        pltpu.VMEM((tq, nhq), F32), pltpu.VMEM((tq, nhq), F32),
                        pltpu.VMEM((8, nhq), F32), pltpu.VMEM((8, nhq), F32)],
        compiler_params=pltpu.CompilerParams(dimension_semantics=("arbitrary", "arbitrary"),
                                             vmem_limit_bytes=VMEM_LIMIT),
        name="dsa",
    )(rel_bias, qabs, qidx, widx, ckv, ckvt, kk)


def _stick_kernel(q_ref, k_ref, v_ref, o_ref):
    for j in range(STICK_Q_TILES):
        _stick_tile(pl.program_id(1) * STICK_Q_TILES + j, q_ref.at[j * ATT_TILE:(j + 1) * ATT_TILE],
                    k_ref, v_ref, o_ref.at[j * ATT_TILE:(j + 1) * ATT_TILE])


def _stick_tile(qi, q_ref, k_ref, v_ref, o_ref):
    tq = tk = ATT_TILE
    npair = WIDTH_B // LANES
    lane = lax.broadcasted_iota(jnp.int32, (tq, LANES), 1)
    row = lax.broadcasted_iota(jnp.int32, (tq, tk), 0)
    col = lax.broadcasted_iota(jnp.int32, (tq, tk), 1)
    causal = col < row
    upper = jnp.where(row > col, 1.0, 0.0).astype(BF16)
    upper_ones = jnp.concatenate([upper, jnp.ones((tk, LANES), BF16)], axis=1)
    lo_half = jnp.where(lane < HEAD_DIM, 1.0, 0.0)
    causal2 = jnp.concatenate([causal, causal], axis=0)
    q_pairs = []
    for p in range(npair):
        q2 = q_ref[:, p * LANES:(p + 1) * LANES].astype(F32)
        q_pairs.append(jnp.concatenate([q2 * lo_half, q2 * (1.0 - lo_half)], axis=0).astype(BF16))

    def tile(kj, r_sums, outs, masked):
        ks = pl.multiple_of(kj * tk, tk)
        log_sig, log_1m = [], []
        for p in range(npair):
            z = _dot_nt(q_pairs[p], k_ref[pl.ds(ks, tk), p * LANES:(p + 1) * LANES])
            sp = jnp.maximum(z, 0.0) + jnp.log2(1.0 + jnp.exp2(-jnp.abs(z)))
            lm = jnp.where(causal2, -sp, 0.0) if masked else -sp
            log_sig.append(z - sp)
            log_1m.append(lm.astype(BF16))
        sums = _dot(jnp.concatenate(log_1m, axis=0), upper_ones)
        new_r, new_o = [], []
        for p in range(npair):
            sm = sums[p * 2 * tq:(p + 1) * 2 * tq]
            after = sm[:, :tk] + jnp.concatenate([r_sums[p]] * (tk // LANES), axis=1)
            a = jnp.exp2(log_sig[p] + after)
            if masked:
                a = jnp.where(causal2, a, 0.0)
            res = _dot(a.astype(BF16), v_ref[pl.ds(ks, tk), p * LANES:(p + 1) * LANES])
            new_o.append(outs[p] + jnp.where(lane < HEAD_DIM, res[:tq], res[tq:]))
            new_r.append(r_sums[p] + sm[:, tk:])
        return tuple(new_r), tuple(new_o)

    zeros = jnp.zeros((tq, LANES), F32)
    r_sums, outs = tile(qi, (jnp.zeros((2 * tq, LANES), F32),) * npair, (zeros,) * npair, True)

    def cond(c):
        it, r_sums, _ = c
        return (it < qi) & (jnp.max(functools.reduce(jnp.maximum, r_sums)) > EXP_UNDERFLOW)

    def body(c):
        it, r_sums, outs = c
        r_sums, outs = tile(qi - 1 - it, r_sums, outs, False)
        return it + 1, r_sums, outs

    _, _, outs = lax.while_loop(cond, body, (jnp.int32(0), r_sums, outs))
    for p in range(npair):
        o_ref[:, p * LANES:(p + 1) * LANES] = outs[p].astype(BF16)


def _stick(qb, kb, vb, batch, seq):
    n = qb.shape[0]
    tq = ATT_TILE
    nq = seq // tq
    nqs = nq // STICK_Q_TILES
    qspec = pl.BlockSpec((STICK_Q_TILES * tq, WIDTH_B), lambda b, q: (b * nqs + q, 0))
    kspec = pl.BlockSpec((seq, WIDTH_B), lambda b, q: (b, 0))
    return pl.pallas_call(
        _stick_kernel,
        grid=(batch, nqs),
        in_specs=[qspec, kspec, kspec],
        out_specs=qspec,
        out_shape=jax.ShapeDtypeStruct((n, WIDTH_B), BF16),
        compiler_params=pltpu.CompilerParams(dimension_semantics=("arbitrary",) * 2, vmem_limit_bytes=VMEM_LIMIT),
        name="stick",
    )(qb, kb, vb)


def _merge_kernel(x_ref, ol_ref, ob_ref, sa_ref, sb_ref, wuv_ref, wba_ref, wbb_ref, wout_ref, g_ref,
                  wr_ref, br_ref, x1_ref, h2_ref, comb_ref):
    half = N_HEADS_A * KV_RANK // 2
    oa = jnp.concatenate([_dot(ol_ref[:, :half], wuv_ref[:half, :WIDTH_A // 2]),
                          _dot(ol_ref[:, half:], wuv_ref[half:, WIDTH_A // 2:])], axis=1).astype(BF16)
    ya = _dot(oa, wba_ref[...])
    yb = _dot(ob_ref[...], wbb_ref[...])
    merged = sa_ref[...].astype(F32) * ya + sb_ref[...].astype(F32) * yb
    x1 = x_ref[...] + _dot(merged.astype(BF16), wout_ref[...])
    x1_ref[...] = x1
    h2 = _rms(x1, g_ref[...])
    h2_hi = h2.astype(BF16)
    h2_ref[...] = h2_hi
    h2_lo = (h2 - h2_hi.astype(F32)).astype(BF16)
    hi_part = _dot(h2_hi, wr_ref[...])
    logits = (hi_part[:, :LANES] + _dot(h2_lo, wr_ref[:, :LANES]) + hi_part[:, LANES:]) + br_ref[...]
    lane = lax.broadcasted_iota(jnp.int32, logits.shape, 1).astype(F32)
    ninf = -jnp.inf

    def first_max(v):
        m = jnp.max(v, axis=1, keepdims=True)
        return m, jnp.min(jnp.where(v == m, lane, 1e9), axis=1, keepdims=True)

    gmask = (lane >= N_EXPERTS) & (lane < N_EXPERTS + N_GROUPS)
    gl = jnp.where(gmask, logits, ninf)
    gmax, gidx = first_max(gl)
    p_g = 1.0 / jnp.sum(jnp.where(gmask, jnp.exp(gl - gmax), 0.0), axis=1, keepdims=True)
    e_lo = (gidx - N_EXPERTS) * EXPERTS_PER_GROUP
    el = jnp.where((lane >= e_lo) & (lane < e_lo + EXPERTS_PER_GROUP), logits, ninf)
    v1, i1 = first_max(el)
    el2 = jnp.where(lane == i1, ninf, el)
    v2, i2 = first_max(el2)
    e2 = jnp.exp(v2 - v1)
    w1 = 1.0 / (1.0 + e2)
    comb_ref[...] = jnp.where(lane == i1, w1 * p_g,
                              jnp.where(lane == i2, (e2 * w1) * p_g, jnp.where(lane == gidx, 1.0, 0.0)))


def _merge(x, olat, ob, sa, sb, wuv, wba, wbb, wout, g, wr, br):
    n, d = x.shape
    tm = ROW_TILE
    row = lambda w: pl.BlockSpec((tm, w), lambda i: (i, 0))
    ws = (wuv, wba, wbb, wout, g, wr, br)
    return pl.pallas_call(
        _merge_kernel,
        grid=(n // tm,),
        in_specs=[row(d), row(olat.shape[1]), row(ob.shape[1]), row(d), row(d)] + [_full_spec(w.shape) for w in ws],
        out_specs=[row(d), row(d), row(LANES)],
        out_shape=[jax.ShapeDtypeStruct((n, d), F32), jax.ShapeDtypeStruct((n, d), BF16),
                   jax.ShapeDtypeStruct((n, LANES), F32)],
        compiler_params=pltpu.CompilerParams(dimension_semantics=("arbitrary",), vmem_limit_bytes=VMEM_LIMIT),
        name="merge",
    )(x, olat, ob, sa, sb, *ws)


def _moe_kernel(h_ref, comb_ref, wg_ref, wu_ref, wd_ref, o_ref, slotc_s, slotr_s, split_s, acc_s, rng_s):
    t, d = h_ref.shape
    nsub = MOE_SUBTILES
    ts = t // nsub
    c = MOE_CHUNK
    rb = MOE_RANK_BLOCK
    g = pl.program_id(1)

    @pl.when(g == 0)
    def _():
        lane_b = lax.broadcasted_iota(jnp.int32, (rb, LANES), 1)
        group_lanes = (lane_b >= N_EXPERTS) & (lane_b < N_EXPERTS + N_GROUPS)
        r_i = lax.broadcasted_iota(jnp.int32, (rb, rb), 0)
        c_i = lax.broadcasted_iota(jnp.int32, (rb, rb), 1)
        lower = jnp.where(c_i < r_i, 1.0, 0.0).astype(BF16)
        lane1 = lax.broadcasted_iota(jnp.int32, (1, LANES), 1)
        for sub in range(nsub):
            counts = jnp.zeros((1, LANES), F32)
            ranks = []
            for blk in range(ts // rb):
                rows = slice(sub * ts + blk * rb, sub * ts + (blk + 1) * rb)
                oh = jnp.where(group_lanes, comb_ref[rows, :], 0.0)
                before = _dot(lower, oh.astype(BF16)) + counts
                ranks.append(jnp.sum(before * oh, axis=1, keepdims=True))
                counts = counts + jnp.sum(oh, axis=0, keepdims=True)
            start = jnp.int32(0)
            seg = jnp.zeros((1, LANES), F32)
            for k in range(N_GROUPS):
                n_k = jnp.sum(jnp.where(lane1 == N_EXPERTS + k, counts, 0.0)).astype(jnp.int32)
                seg = jnp.where(lane1 == N_EXPERTS + k, (start * c).astype(F32), seg)
                rng_s[sub * 2 * N_GROUPS + k] = start
                for j in range(-(-ts // c)):
                    start = start + (n_k > j * c).astype(jnp.int32)
                rng_s[sub * 2 * N_GROUPS + N_GROUPS + k] = start
            for blk in range(ts // rb):
                rows = slice(sub * ts + blk * rb, sub * ts + (blk + 1) * rb)
                oh = jnp.where(group_lanes, comb_ref[rows, :], 0.0)
                slot = ranks[blk] + jnp.sum(oh * seg, axis=1, keepdims=True)
                slotc_s[rows, :] = jnp.broadcast_to(slot, (rb, LANES))
        slotr_s[...] = jnp.transpose(slotc_s[...])
        cb = comb_ref[...]
        hi = cb.astype(BF16)
        split_s[...] = jnp.concatenate([hi, (cb - hi.astype(F32)).astype(BF16)], axis=1)
        acc_s[...] = jnp.zeros(acc_s.shape, F32)

    lane = lax.broadcasted_iota(jnp.int32, (nsub * c, LANES), 1)
    first = [rng_s[sub * 2 * N_GROUPS + g] for sub in range(nsub)]
    n_chunks = [rng_s[sub * 2 * N_GROUPS + N_GROUPS + g] - first[sub] for sub in range(nsub)]

    def chunk_body(j, carry):
        bases = [((first[sub] + j) * c).astype(F32) for sub in range(nsub)]
        xs, cw = [], []
        for sub in range(nsub):
            rows = slice(sub * ts, (sub + 1) * ts)
            row_slot = lax.broadcasted_iota(jnp.int32, (c, ts), 0).astype(F32) + bases[sub]
            pc = jnp.where(slotr_s[0:1, rows] == row_slot, 1.0, 0.0).astype(BF16)
            xs.append(_dot(pc, h_ref[rows, :]).astype(BF16))
            cw2 = _dot(pc, split_s[rows, :])
            cw.append(cw2[:, :LANES] + cw2[:, LANES:])
        xs = jnp.concatenate(xs, axis=0)
        cw = jnp.concatenate(cw, axis=0)
        y = jnp.zeros((nsub * c, d), F32)
        for e in range(EXPERTS_PER_GROUP):
            gate = _dot(xs, wg_ref[e])
            hid = gate * jax.nn.sigmoid(gate) * _dot(xs, wu_ref[e])
            ce = jnp.sum(jnp.where(lane == g * EXPERTS_PER_GROUP + e, cw, 0.0), axis=1, keepdims=True)
            y = y + _dot((hid * ce).astype(BF16), wd_ref[e])
        y = y.astype(BF16)
        for sub in range(nsub):
            rows = slice(sub * ts, (sub + 1) * ts)
            col_slot = lax.broadcasted_iota(jnp.int32, (ts, c), 1).astype(F32) + bases[sub]
            pct = jnp.where(jnp.broadcast_to(slotc_s[rows, 0:1], (ts, c)) == col_slot, 1.0, 0.0).astype(BF16)
            acc_s[rows, :] += _dot(pct, y[sub * c:(sub + 1) * c])
        return carry

    lax.fori_loop(0, functools.reduce(jnp.maximum, n_chunks), chunk_body, 0)

    @pl.when(g == N_GROUPS - 1)
    def _():
        o_ref[...] = acc_s[...].astype(BF16)


def _moe(h2, comb, wg, wu, wd):
    n, d = h2.shape
    tm = MOE_ROW_TILE
    row = lambda w: pl.BlockSpec((tm, w), lambda i, g: (i, 0))
    return pl.pallas_call(
        _moe_kernel,
        grid=(n // tm, N_GROUPS),
        in_specs=[row(d), row(LANES),
                  pl.BlockSpec((EXPERTS_PER_GROUP, d, D_EXPERT), lambda i, g: (g, 0, 0)),
                  pl.BlockSpec((EXPERTS_PER_GROUP, d, D_EXPERT), lambda i, g: (g, 0, 0)),
                  pl.BlockSpec((EXPERTS_PER_GROUP, D_EXPERT, d), lambda i, g: (g, 0, 0))],
        out_specs=row(d),
        out_shape=jax.ShapeDtypeStruct((n, d), BF16),
        scratch_shapes=[pltpu.VMEM((tm, LANES), F32), pltpu.VMEM((LANES, tm), F32),
                        pltpu.VMEM((tm, 2 * LANES), BF16), pltpu.VMEM((tm, d), F32),
                        pltpu.SMEM((MOE_SUBTILES * 2 * N_GROUPS,), jnp.int32)],
        compiler_params=pltpu.CompilerParams(dimension_semantics=("arbitrary", "arbitrary"),
                                             vmem_limit_bytes=MOE_VMEM_LIMIT),
        name="moe",
    )(h2, comb, wg, wu, wd)


def _ple_kernel(x_ref, m_ref, p_ref, g_ref, wpg_ref, wple_ref, gf_ref, o_ref, *, last):
    x2 = x_ref[...] + m_ref[...].astype(F32)
    h3 = _rms(x2, g_ref[...]).astype(BF16)
    gate = jax.nn.sigmoid(_dot(h3, wpg_ref[...]))
    x3 = x2 + _dot(p_ref[...].astype(BF16), wple_ref[...]) * gate
    o_ref[...] = _rms(x3, gf_ref[...]) if last else x3


def _ple(x1, moe, p, g, wpg, wple, gf, last):
    n, d = x1.shape
    tm = ROW_TILE
    row = lambda w: pl.BlockSpec((tm, w), lambda i: (i, 0))
    ws = (g, wpg, wple, gf)
    return pl.pallas_call(
        functools.partial(_ple_kernel, last=last),
        grid=(n // tm,),
        in_specs=[row(d), row(d), row(p.shape[1])] + [_full_spec(w.shape) for w in ws],
        out_specs=row(d),
        out_shape=jax.ShapeDtypeStruct((n, d), F32),
        compiler_params=pltpu.CompilerParams(dimension_semantics=("arbitrary",), vmem_limit_bytes=VMEM_LIMIT),
        name="ple",
    )(x1, moe, p, *ws)


def _block_diag(blocks):
    h, r, c = blocks.shape
    eye = jnp.eye(h, dtype=blocks.dtype)
    return (eye[:, None, :, None] * blocks[:, :, None, :]).reshape(h * r, h * c)


def kernel(x, p, attn_norm, w_in, kv_norm, w_uk, w_uv, rel_bias, w_branch_a, w_branch_b, w_out, ffn_norm,
           w_r1, b_r1, w_r2, b_r2, w_gate, w_up, w_down, ple_norm, w_ple_gate, w_ple, final_norm):
    batch, seq, d = x.shape
    n = batch * seq
    depth = w_in.shape[0]
    xf = x.reshape(n, d).astype(F32)
    widths = [WIDTH_A, KV_RANK, IDX_HEADS * IDX_DIM, IDX_DIM, IDX_HEADS, 3 * WIDTH_B, d, d]
    starts = [sum(widths[:k]) for k in range(len(widths))]
    for i in range(depth):
        w_qa, w_ckv, w_qi, w_ki, w_wi, w_qkv, w_ga, w_gb = [
            w_in[i][:, s:s + w].astype(BF16) for s, w in zip(starts, widths)]
        w_kk = jnp.concatenate([w_ki, w_ki], axis=1)
        w_cw = jnp.concatenate([w_ckv, jnp.pad(w_wi, ((0, 0), (0, LANES - IDX_HEADS)))], axis=1)
        wuk_bd = _block_diag(jnp.swapaxes(w_uk[i], 1, 2)).astype(BF16)
        wuv_bd = _block_diag(w_uv[i]).astype(BF16)
        qabs, ckv, ckvt, qidx, kk, widx, qb, kb, vb, sa, sb = _proj(
            xf, attn_norm[i][None].astype(F32), w_qa, wuk_bd, w_cw, kv_norm[i][None].astype(F32),
            w_qi, w_kk, w_qkv, w_ga, w_gb)
        olat = _dsa(rel_bias.astype(F32), qabs, qidx, widx, ckv, ckvt, kk, batch, seq)
        ob = _stick(qb, kb, vb, batch, seq)
        w_r = jnp.concatenate([jnp.transpose(w_r2[i], (1, 0, 2)).reshape(d, N_EXPERTS), w_r1[i]], axis=1)
        w_r = jnp.pad(w_r.astype(F32), ((0, 0), (0, LANES - N_EXPERTS - N_GROUPS)))
        w_r_hi = w_r.astype(BF16)
        w_r_lo = (w_r - w_r_hi.astype(F32)).astype(BF16)
        b_r = jnp.pad(jnp.concatenate([b_r2[i].reshape(-1), b_r1[i]]).astype(F32),
                      (0, LANES - N_EXPERTS - N_GROUPS))[None]
        x1, h2, comb = _merge(xf, olat, ob, sa, sb, wuv_bd, w_branch_a[i].astype(BF16), w_branch_b[i].astype(BF16),
                              w_out[i].astype(BF16), ffn_norm[i][None].astype(F32),
                              jnp.concatenate([w_r_hi, w_r_lo], axis=1), b_r)
        moe = _moe(h2, comb, w_gate[i].astype(BF16), w_up[i].astype(BF16), w_down[i].astype(BF16))
        xf = _ple(x1, moe, p[i].reshape(n, -1).astype(F32), ple_norm[i][None].astype(F32), w_ple_gate[i].astype(BF16),
                  w_ple[i].astype(BF16), final_norm[None].astype(F32), last=(i == depth - 1))
    return xf.reshape(batch, seq, d).astype(x.dtype)
```

```python
import functools
import math

import jax
import jax.numpy as jnp
from jax import lax
from jax.experimental import pallas as pl
from jax.experimental.pallas import tpu as pltpu

D_MODEL = 1024
N_HEADS_A = 8
HEAD_DIM = 64
WIDTH_A = N_HEADS_A * HEAD_DIM
KV_RANK = 128
IDX_HEADS = 8
IDX_DIM = 64
TOPK_MAX = 256
N_HEADS_B = 8
WIDTH_B = N_HEADS_B * HEAD_DIM
N_BUCKETS = 32
MAX_EXACT = N_BUCKETS // 2
MAX_DISTANCE = 128
ATTN_SCALE = HEAD_DIM ** -0.5
IDX_SCALE = (IDX_HEADS ** -0.5) * (IDX_DIM ** -0.5)
N_GROUPS = 4
EXPERTS_PER_GROUP = 8
N_EXPERTS = N_GROUPS * EXPERTS_PER_GROUP
D_EXPERT = 256
PLE_DIM = 256
EPS = 1e-6
LOG2E = 1.4426950408889634

LANES = 128
ROW_TILE = 1024
MOE_ROW_TILE = 2048
MOE_SUBTILES = 4
MOE_CHUNK = 160
MOE_RANK_BLOCK = 256
ATT_TILE = 256
STICK_Q_TILES = 2
STICK_PAIR_GROUP = 2
ONES_ROWS = 16
VMEM_LIMIT = 56 * 1024 * 1024
MOE_VMEM_LIMIT = 60 * 1024 * 1024
NEG_BIG = -1e30
EXP_UNDERFLOW = -173.0
KEY_TOP = 127.0
KEY_MIN_RANGE = 1e-30

F32 = jnp.float32
BF16 = jnp.bfloat16


def _rms(x, g):
    return x * lax.rsqrt(jnp.mean(x * x, axis=-1, keepdims=True) + EPS) * g


def _dot(a, b):
    return jnp.dot(a, b, preferred_element_type=F32)


def _dot_nt(a, b):
    return lax.dot_general(a, b, (((1,), (1,)), ((), ())), preferred_element_type=F32)


def _full_spec(shape):
    nd = len(shape)
    return pl.BlockSpec(shape, lambda *_: (0,) * nd)


def _proj_kernel(x_ref, g_ref, wqa_ref, wuk_ref, wcw_ref, kvg_ref, wqi_ref, wkk_ref,
                 wqkv_ref, wga_ref, wgb_ref,
                 qabs_ref, ckv_ref, ckvt_ref, qidx_ref, kk_ref, widx_ref, qb_ref, kb_ref, vb_ref, sa_ref, sb_ref):
    h = _rms(x_ref[...], g_ref[...]).astype(BF16)
    qa = _dot(h, wqa_ref[...]).astype(BF16)
    half = WIDTH_A // 2
    qabs = jnp.concatenate([_dot(qa[:, :half], wuk_ref[:half, :N_HEADS_A * KV_RANK // 2]),
                            _dot(qa[:, half:], wuk_ref[half:, N_HEADS_A * KV_RANK // 2:])], axis=1)
    qabs_ref[...] = (qabs * (ATTN_SCALE * LOG2E)).astype(BF16)
    cw = _dot(h, wcw_ref[...])
    c = _rms(cw[:, :KV_RANK], kvg_ref[...])
    ckv_ref[...] = c.astype(BF16)
    for j in range(ckvt_ref.shape[0]):
        c_t = jnp.transpose(c[j * ATT_TILE:(j + 1) * ATT_TILE, :])
        ckvt_ref[j] = jnp.concatenate([c_t, jnp.ones((ONES_ROWS, ATT_TILE), F32)], axis=0).astype(BF16)
    qidx_ref[...] = _dot(h, wqi_ref[...]).astype(BF16)
    kk_ref[...] = _dot(h, wkk_ref[...]).astype(BF16)
    widx_ref[...] = cw[:, KV_RANK:] * IDX_SCALE
    qkv = _dot(h, wqkv_ref[...])
    qb_ref[...] = (qkv[:, :WIDTH_B] * (ATTN_SCALE * LOG2E)).astype(BF16)
    kb_ref[...] = qkv[:, WIDTH_B:2 * WIDTH_B].astype(BF16)
    vb_ref[...] = qkv[:, 2 * WIDTH_B:].astype(BF16)
    sa_ref[...] = jax.nn.sigmoid(_dot(h, wga_ref[...])).astype(BF16)
    sb_ref[...] = jax.nn.sigmoid(_dot(h, wgb_ref[...])).astype(BF16)


def _proj(x, g, wqa, wuk, wcw, kvg, wqi, wkk, wqkv, wga, wgb):
    n, d = x.shape
    tm = ROW_TILE
    row = lambda w: pl.BlockSpec((tm, w), lambda i: (i, 0))
    ws = (g, wqa, wuk, wcw, kvg, wqi, wkk, wqkv, wga, wgb)
    outs = [(N_HEADS_A * KV_RANK, BF16), (KV_RANK, BF16), None, (IDX_HEADS * IDX_DIM, BF16), (LANES, BF16),
            (LANES, F32), (WIDTH_B, BF16), (WIDTH_B, BF16), (WIDTH_B, BF16), (d, BF16), (d, BF16)]
    t = ATT_TILE
    ckvt_spec = pl.BlockSpec((tm // t, KV_RANK + ONES_ROWS, t), lambda i: (i, 0, 0))
    ckvt_shape = jax.ShapeDtypeStruct((n // t, KV_RANK + ONES_ROWS, t), BF16)
    return pl.pallas_call(
        _proj_kernel,
        grid=(n // tm,),
        in_specs=[row(d)] + [_full_spec(w.shape) for w in ws],
        out_specs=[ckvt_spec if o is None else row(o[0]) for o in outs],
        out_shape=[ckvt_shape if o is None else jax.ShapeDtypeStruct((n, o[0]), o[1]) for o in outs],
        compiler_params=pltpu.CompilerParams(dimension_semantics=("arbitrary",), vmem_limit_bytes=VMEM_LIMIT),
        name="proj",
    )(x, *ws)


def _t5_bucket(dist):
    dist = jnp.maximum(dist, 0)
    d_f = jnp.maximum(dist, 1).astype(F32)
    large = MAX_EXACT + jnp.floor(jnp.log(d_f / MAX_EXACT) / math.log(MAX_DISTANCE / MAX_EXACT)
                                  * (N_BUCKETS - MAX_EXACT)).astype(jnp.int32)
    large = jnp.minimum(large, N_BUCKETS - 1)
    return jnp.where(dist < MAX_EXACT, dist, large)


def _dsa_kernel(relb_ref, qabs_ref, qidx_ref, widx_ref, ckv_ref, ckvt_ref, kk_ref, o_ref,
                qa_s, qi_s, key_s, dig_s, act_s, cs_s, mn_s, mx_s, bias_s, m_s, acc_s, sa_s, sb_s, ma_s, mb_s, *, n_sel, seq):
    tq = tk = ATT_TILE
    nh = N_HEADS_A
    nhq = nh * tq
    qi = pl.program_id(1)
    n_kt = qi + 1
    kpos = lax.broadcasted_iota(jnp.int32, (tk, tq), 0)
    qpos = lax.broadcasted_iota(jnp.int32, (tk, tq), 1)

    @pl.when((pl.program_id(0) == 0) & (qi == 0))
    def _():
        for t in range(2):
            bucket = _t5_bucket(t * tq + qpos - kpos)
            for h in range(nh):
                b = jnp.zeros((tk, tq), F32)
                for bk in range(N_BUCKETS):
                    b = jnp.where(bucket == bk, relb_ref[bk, h], b)
                bias_s[t, :, h * tq:(h + 1) * tq] = (b - relb_ref[N_BUCKETS - 1, h]) * LOG2E

    lane = lax.broadcasted_iota(jnp.int32, (tq, LANES), 1)
    lo_half = jnp.where(lane < IDX_DIM, 1.0, 0.0)
    hi_half = 1.0 - lo_half
    for h in range(nh):
        qa_s[h * tq:(h + 1) * tq, :] = qabs_ref[:, h * KV_RANK:(h + 1) * KV_RANK]
        pair = qidx_ref[:, (h // 2) * LANES:(h // 2 + 1) * LANES].astype(F32)
        qi_s[h * tq:(h + 1) * tq, :] = (pair * (lo_half if h % 2 == 0 else hi_half)).astype(BF16)
    w_t = jnp.transpose(widx_ref[...])
    w_row = jnp.concatenate([w_t[h:h + 1, :] for h in range(nh)], axis=1)

    mn_s[...] = jnp.full(mn_s.shape, -NEG_BIG, F32)
    mx_s[...] = jnp.full(mx_s.shape, NEG_BIG, F32)

    def idx_dots(kj, r_ref):
        ks = pl.multiple_of(jnp.minimum(kj, n_kt - 1) * tk, tk)
        r_ref[...] = _dot_nt(kk_ref[pl.ds(ks, tk), :], qi_s[...])

    def idx_scores(kj, r_ref):
        r = jnp.maximum(r_ref[...], 0.0) * w_row
        s = r[:, :tq]
        for h in range(1, nh):
            s = s + r[:, h * tq:(h + 1) * tq]
        masked = (kpos > qpos) & (kj == qi)
        key_s[kj] = jnp.where(masked, NEG_BIG, s)
        mn_s[...] = jnp.minimum(mn_s[...], jnp.min(jnp.where(masked, -NEG_BIG, s).reshape(tk // 8, 8, tq), axis=0))
        mx_s[...] = jnp.maximum(mx_s[...], jnp.max(jnp.where(masked, NEG_BIG, s).reshape(tk // 8, 8, tq), axis=0))

    idx_dots(0, sa_s)

    def idx_pair(i, carry):
        idx_dots(2 * i + 1, sb_s)
        idx_scores(2 * i, sa_s)
        idx_dots(2 * i + 2, sa_s)
        idx_scores(2 * i + 1, sb_s)
        return carry

    lax.fori_loop(0, n_kt // 2, idx_pair, 0)

    @pl.when(n_kt % 2 == 1)
    def _():
        idx_scores(n_kt - 1, sa_s)

    lo = jnp.min(mn_s[...], axis=0, keepdims=True)
    hi = jnp.max(mx_s[...], axis=0, keepdims=True)
    scale = KEY_TOP / jnp.maximum(hi - lo, KEY_MIN_RANGE)

    def key_body(kj, carry):
        y = jnp.maximum((key_s[kj] - lo) * scale, -1.0)
        rem = y
        for p in (3, 2, 1, 0):
            d = jnp.floor(rem)
            dig_s[p, kj] = d.astype(BF16)
            rem = (rem - d) * 256.0
        key_s[kj] = y - rem * (2.0 ** -32)
        return carry

    lax.fori_loop(0, n_kt, key_body, 0)

    n_search = jnp.where(qi == 0, 0, n_kt)

    def count_active(pred, then=None):
        one, zero = jnp.ones((), BF16), jnp.zeros((), BF16)

        def body(kj, acc):
            a = act_s[kj]
            c = jnp.where(pred(a), one, zero).reshape(tk // 16, 16, tq)
            part = c[0]
            for j in range(1, tk // 16):
                part = part + c[j]
            if then is not None:
                then(kj, a)
            return acc + part.astype(F32)
        acc = lax.fori_loop(0, n_search, body, jnp.zeros((16, tq), F32))
        return jnp.sum(acc, axis=0, keepdims=True)

    def to_digit(v):
        return v.astype(F32).astype(BF16)

    def digit_phase(p, prev_digit, target):
        if p == 3:
            def init_body(kj, carry):
                act_s[kj] = dig_s[3, kj]
                return carry
            lax.fori_loop(0, n_search, init_body, 0)
        else:
            prev = to_digit(prev_digit)

            def narrow(kj, a):
                act_s[kj] = jnp.where(a == prev, dig_s[p, kj], jnp.full((), -1.0, BF16))
            target = target - count_active(lambda a: a > prev, then=narrow)
        n_bits = 7 if p == 3 else 8

        def bit_body(i, dgt):
            cand = dgt | jnp.left_shift(jnp.int32(1), n_bits - 1 - i)
            cand_d = to_digit(cand)
            cnt = count_active(lambda a: a >= cand_d)
            return jnp.where(cnt >= target, cand, dgt)
        return lax.fori_loop(0, n_bits, bit_body, jnp.zeros((1, tq), jnp.int32)), target

    target = jnp.full((1, tq), float(n_sel), F32)
    tau = jnp.zeros((1, tq), F32)
    digit = None
    for p in (3, 2, 1):
        digit, target = digit_phase(p, digit, target)
        tau = tau + digit.astype(F32) * (256.0 ** (p - 3))
    digit_d = to_digit(digit)
    excess = count_active(lambda a: a >= digit_d) - target

    def count_keys(pred):
        def body(kj, acc):
            c = jnp.where(pred(key_s[kj], kpos + kj * tk), 1.0, 0.0)
            return acc + jnp.sum(c.reshape(tk // 8, 8, tq), axis=0)
        acc = lax.fori_loop(0, n_kt, body, jnp.zeros((8, tq), F32))
        return jnp.sum(acc, axis=0, keepdims=True)

    first = qi == 0
    cs_s[...] = jnp.where(lax.broadcasted_iota(jnp.int32, cs_s.shape, 0) == 0, seq, 0)

    @pl.when(jnp.logical_not(first) & (jnp.max(excess) > 0.0))
    def _():
        digit0, target0 = digit_phase(0, digit, target)
        cs_s[1:2, :] = digit0
        tau0 = tau + digit0.astype(F32) * (256.0 ** -3)
        last = to_digit(digit0)
        n_gt = count_active(lambda a: a > last)
        n_eq = count_active(lambda a: a == last)
        need = target0 - n_gt

        @pl.when(jnp.max(n_eq - need) > 0.0)
        def _():
            nbits = seq.bit_length() - 1

            def tie_body(i, d):
                cand = d | jnp.left_shift(jnp.int32(1), nbits - 1 - i)
                cnt = count_keys(lambda k, pos: (k == tau0) & (pos < cand))
                return jnp.where(cnt < need, cand, d)

            cs_s[0:1, :] = lax.fori_loop(0, nbits, tie_body, jnp.zeros((1, tq), jnp.int32))

    tau = jnp.where(first, -0.5, tau + cs_s[1:2, :].astype(F32) * (256.0 ** -3))
    cstar = jnp.where(first, -1, cs_s[0:1, :])

    m_s[...] = jnp.full(m_s.shape, NEG_BIG, F32)
    acc_s[...] = jnp.zeros(acc_s.shape, F32)

    def tile_of(t):
        tc = jnp.minimum(t, n_kt - 1)
        return jnp.where(tc < 2, qi - tc, tc - 2)

    def scores(t, s_ref, mx_ref, near=None):
        kc = tile_of(t)
        ks = pl.multiple_of(kc * tk, tk)
        k = key_s[kc]
        sel = ((k > tau) | ((k == tau) & (kpos + kc * tk <= cstar))) & (t < n_kt)
        selb = jnp.where(sel, 0.0, NEG_BIG)
        s = _dot_nt(ckv_ref[pl.ds(ks, tk), :], qa_s[...])
        if near is not None:
            s = s + bias_s[near]
        s = s + jnp.concatenate([selb] * nh, axis=1)
        s_ref[...] = s
        mx_ref[...] = jnp.max(s.reshape(tk // 8, 8, nhq), axis=0)

    def update(t, s_ref, mx_ref):
        m_old = m_s[0:1, :]
        m_new = jnp.maximum(m_old, jnp.max(mx_ref[...], axis=0, keepdims=True))
        alpha = jnp.exp2(m_old - m_new)
        p = jnp.exp2(s_ref[...] - m_new).astype(BF16)
        acc_s[...] = acc_s[...] * alpha + _dot(ckvt_ref[tile_of(t)], p)
        m_s[...] = jnp.broadcast_to(m_new, m_s.shape)

    scores(0, sa_s, ma_s, near=0)

    @pl.when(n_kt >= 2)
    def _():
        scores(1, sb_s, mb_s, near=1)
        update(0, sa_s, ma_s)
        scores(2, sa_s, ma_s)
        update(1, sb_s, mb_s)

    def pair_body(i, carry):
        scores(2 * i + 1, sb_s, mb_s)
        update(2 * i, sa_s, ma_s)
        scores(2 * i + 2, sa_s, ma_s)
        update(2 * i + 1, sb_s, mb_s)
        return carry

    lax.fori_loop(1, n_kt // 2, pair_body, 0)

    @pl.when(n_kt % 2 == 1)
    def _():
        update(n_kt - 1, sa_s, ma_s)

    acc = acc_s[...]
    o_t = acc[:KV_RANK] / acc[KV_RANK:KV_RANK + 1]
    for h in range(nh):
        o_ref[:, h * KV_RANK:(h + 1) * KV_RANK] = jnp.transpose(o_t[:, h * tq:(h + 1) * tq]).astype(BF16)


def _dsa(rel_bias, qabs, qidx, widx, ckv, ckvt, kk, batch, seq):
    n = qabs.shape[0]
    tq = ATT_TILE
    nq = seq // tq
    n_sel = min(TOPK_MAX, seq // 4)
    assert n_sel == tq and seq % tq == 0 and seq & (seq - 1) == 0
    assert MAX_EXACT * (MAX_DISTANCE / MAX_EXACT) ** ((N_BUCKETS - 1 - MAX_EXACT) / (N_BUCKETS - MAX_EXACT)) < tq
    qrow = lambda w: pl.BlockSpec((tq, w), lambda b, q: (b * nq + q, 0))
    brow = lambda w: pl.BlockSpec((seq, w), lambda b, q: (b, 0))
    nhq = N_HEADS_A * tq
    return pl.pallas_call(
        functools.partial(_dsa_kernel, n_sel=n_sel, seq=seq),
        grid=(batch, nq),
        in_specs=[pl.BlockSpec(memory_space=pltpu.SMEM),
                  qrow(N_HEADS_A * KV_RANK), qrow(IDX_HEADS * IDX_DIM), qrow(LANES),
                  brow(KV_RANK), pl.BlockSpec((nq, KV_RANK + ONES_ROWS, tq), lambda b, q: (b, 0, 0)), brow(LANES)],
        out_specs=qrow(N_HEADS_A * KV_RANK),
        out_shape=jax.ShapeDtypeStruct((n, N_HEADS_A * KV_RANK), BF16),
        scratch_shapes=[pltpu.VMEM((nhq, KV_RANK), BF16), pltpu.VMEM((nhq, LANES), BF16),
                        pltpu.VMEM((nq, tq, tq), F32), pltpu.VMEM((4, nq, tq, tq), BF16),
                        pltpu.VMEM((nq, tq, tq), BF16), pltpu.VMEM((8, tq), jnp.int32),
                        pltpu.VMEM((8, tq), F32), pltpu.VMEM((8, tq), F32),
                        pltpu.VMEM((2, tq, nhq), F32),
                        pltpu.VMEM((8, nhq), F32), pltpu.VMEM((KV_RANK + ONES_ROWS, nhq), F32),
                        pltpu.VMEM((tq, nhq), F32), pltpu.VMEM((tq, nhq), F32),
                        pltpu.VMEM((8, nhq), F32), pltpu.VMEM((8, nhq), F32)],
        compiler_params=pltpu.CompilerParams(dimension_semantics=("arbitrary", "arbitrary"),
                                             vmem_limit_bytes=VMEM_LIMIT),
        name="dsa",
    )(rel_bias, qabs, qidx, widx, ckv, ckvt, kk)


def _stick_kernel(q_ref, k_ref, v_ref, o_ref):
    for j in range(STICK_Q_TILES):
        _stick_tile(pl.program_id(1) * STICK_Q_TILES + j, q_ref.at[j * ATT_TILE:(j + 1) * ATT_TILE],
                    k_ref, v_ref, o_ref.at[j * ATT_TILE:(j + 1) * ATT_TILE])


def _stick_tile(qi, q_ref, k_ref, v_ref, o_ref):
    tq = tk = ATT_TILE
    npair = WIDTH_B // LANES
    lane = lax.broadcasted_iota(jnp.int32, (tq, LANES), 1)
    row = lax.broadcasted_iota(jnp.int32, (tq, tk), 0)
    col = lax.broadcasted_iota(jnp.int32, (tq, tk), 1)
    causal = col < row
    upper = jnp.where(row > col, 1.0, 0.0).astype(BF16)
    upper_ones = jnp.concatenate([upper, jnp.ones((tk, LANES), BF16)], axis=1)
    lo_half = jnp.where(lane < HEAD_DIM, 1.0, 0.0)
    causal2 = jnp.concatenate([causal, causal], axis=0)
    q_pairs = []
    for p in range(npair):
        q2 = q_ref[:, p * LANES:(p + 1) * LANES].astype(F32)
        q_pairs.append(jnp.concatenate([q2 * lo_half, q2 * (1.0 - lo_half)], axis=0).astype(BF16))

    def tile(kj, r_sums, outs, masked):
        ks = pl.multiple_of(kj * tk, tk)
        new_r, new_o = [], []
        for grp in range(0, npair, STICK_PAIR_GROUP):
            pairs = range(grp, grp + STICK_PAIR_GROUP)
            log_sig, log_1m = [], []
            for p in pairs:
                z = _dot_nt(q_pairs[p], k_ref[pl.ds(ks, tk), p * LANES:(p + 1) * LANES])
                sp = jnp.maximum(z, 0.0) + jnp.log2(1.0 + jnp.exp2(-jnp.abs(z)))
                lm = jnp.where(causal2, -sp, 0.0) if masked else -sp
                log_sig.append(z - sp)
                log_1m.append(lm.astype(BF16))
            sums = _dot(jnp.concatenate(log_1m, axis=0), upper_ones)
            for i, p in enumerate(pairs):
                sm = sums[i * 2 * tq:(i + 1) * 2 * tq]
                after = sm[:, :tk] + jnp.concatenate([r_sums[p]] * (tk // LANES), axis=1)
                a = jnp.exp2(log_sig[i] + after)
                if masked:
                    a = jnp.where(causal2, a, 0.0)
                res = _dot(a.astype(BF16), v_ref[pl.ds(ks, tk), p * LANES:(p + 1) * LANES])
                new_o.append(outs[p] + jnp.where(lane < HEAD_DIM, res[:tq], res[tq:]))
                new_r.append(r_sums[p] + sm[:, tk:])
        return tuple(new_r), tuple(new_o)

    zeros = jnp.zeros((tq, LANES), F32)
    r_sums, outs = tile(qi, (jnp.zeros((2 * tq, LANES), F32),) * npair, (zeros,) * npair, True)

    def cond(c):
        it, r_sums, _ = c
        return (it < qi) & (jnp.max(functools.reduce(jnp.maximum, r_sums)) > EXP_UNDERFLOW)

    def body(c):
        it, r_sums, outs = c
        r_sums, outs = tile(qi - 1 - it, r_sums, outs, False)
        return it + 1, r_sums, outs

    _, _, outs = lax.while_loop(cond, body, (jnp.int32(0), r_sums, outs))
    for p in range(npair):
        o_ref[:, p * LANES:(p + 1) * LANES] = outs[p].astype(BF16)


def _stick(qb, kb, vb, batch, seq):
    n = qb.shape[0]
    tq = ATT_TILE
    nq = seq // tq
    nqs = nq // STICK_Q_TILES
    qspec = pl.BlockSpec((STICK_Q_TILES * tq, WIDTH_B), lambda b, q: (b * nqs + q, 0))
    kspec = pl.BlockSpec((seq, WIDTH_B), lambda b, q: (b, 0))
    return pl.pallas_call(
        _stick_kernel,
        grid=(batch, nqs),
        in_specs=[qspec, kspec, kspec],
        out_specs=qspec,
        out_shape=jax.ShapeDtypeStruct((n, WIDTH_B), BF16),
        compiler_params=pltpu.CompilerParams(dimension_semantics=("arbitrary",) * 2, vmem_limit_bytes=VMEM_LIMIT),
        name="stick",
    )(qb, kb, vb)


def _merge_kernel(x_ref, ol_ref, ob_ref, sa_ref, sb_ref, wuv_ref, wba_ref, wbb_ref, wout_ref, g_ref,
                  wr_ref, br_ref, x1_ref, h2_ref, comb_ref):
    half = N_HEADS_A * KV_RANK // 2
    oa = jnp.concatenate([_dot(ol_ref[:, :half], wuv_ref[:half, :WIDTH_A // 2]),
                          _dot(ol_ref[:, half:], wuv_ref[half:, WIDTH_A // 2:])], axis=1).astype(BF16)
    ya = _dot(oa, wba_ref[...])
    yb = _dot(ob_ref[...], wbb_ref[...])
    merged = sa_ref[...].astype(F32) * ya + sb_ref[...].astype(F32) * yb
    x1 = x_ref[...] + _dot(merged.astype(BF16), wout_ref[...])
    x1_ref[...] = x1
    h2 = _rms(x1, g_ref[...])
    h2_hi = h2.astype(BF16)
    h2_ref[...] = h2_hi
    h2_lo = (h2 - h2_hi.astype(F32)).astype(BF16)
    hi_part = _dot(h2_hi, wr_ref[...])
    logits = (hi_part[:, :LANES] + _dot(h2_lo, wr_ref[:, :LANES]) + hi_part[:, LANES:]) + br_ref[...]
    lane = lax.broadcasted_iota(jnp.int32, logits.shape, 1).astype(F32)
    ninf = -jnp.inf

    def first_max(v):
        m = jnp.max(v, axis=1, keepdims=True)
        return m, jnp.min(jnp.where(v == m, lane, 1e9), axis=1, keepdims=True)

    gmask = (lane >= N_EXPERTS) & (lane < N_EXPERTS + N_GROUPS)
    gl = jnp.where(gmask, logits, ninf)
    gmax, gidx = first_max(gl)
    p_g = 1.0 / jnp.sum(jnp.where(gmask, jnp.exp(gl - gmax), 0.0), axis=1, keepdims=True)
    e_lo = (gidx - N_EXPERTS) * EXPERTS_PER_GROUP
    el = jnp.where((lane >= e_lo) & (lane < e_lo + EXPERTS_PER_GROUP), logits, ninf)
    v1, i1 = first_max(el)
    el2 = jnp.where(lane == i1, ninf, el)
    v2, i2 = first_max(el2)
    e2 = jnp.exp(v2 - v1)
    w1 = 1.0 / (1.0 + e2)
    comb_ref[...] = jnp.where(lane == i1, w1 * p_g,
                              jnp.where(lane == i2, (e2 * w1) * p_g, jnp.where(lane == gidx, 1.0, 0.0)))


def _merge(x, olat, ob, sa, sb, wuv, wba, wbb, wout, g, wr, br):
    n, d = x.shape
    tm = ROW_TILE
    row = lambda w: pl.BlockSpec((tm, w), lambda i: (i, 0))
    ws = (wuv, wba, wbb, wout, g, wr, br)
    return pl.pallas_call(
        _merge_kernel,
        grid=(n // tm,),
        in_specs=[row(d), row(olat.shape[1]), row(ob.shape[1]), row(d), row(d)] + [_full_spec(w.shape) for w in ws],
        out_specs=[row(d), row(d), row(LANES)],
        out_shape=[jax.ShapeDtypeStruct((n, d), F32), jax.ShapeDtypeStruct((n, d), BF16),
                   jax.ShapeDtypeStruct((n, LANES), F32)],
        compiler_params=pltpu.CompilerParams(dimension_semantics=("arbitrary",), vmem_limit_bytes=VMEM_LIMIT),
        name="merge",
    )(x, olat, ob, sa, sb, *ws)


def _moe_kernel(h_ref, comb_ref, wg_ref, wu_ref, wd_ref, o_ref, slotc_s, slotr_s, split_s, acc_s, rng_s):
    t, d = h_ref.shape
    nsub = MOE_SUBTILES
    ts = t // nsub
    c = MOE_CHUNK
    rb = MOE_RANK_BLOCK
    g = pl.program_id(1)

    @pl.when(g == 0)
    def _():
        lane_b = lax.broadcasted_iota(jnp.int32, (rb, LANES), 1)
        group_lanes = (lane_b >= N_EXPERTS) & (lane_b < N_EXPERTS + N_GROUPS)
        r_i = lax.broadcasted_iota(jnp.int32, (rb, rb), 0)
        c_i = lax.broadcasted_iota(jnp.int32, (rb, rb), 1)
        lower = jnp.where(c_i < r_i, 1.0, 0.0).astype(BF16)
        lane1 = lax.broadcasted_iota(jnp.int32, (1, LANES), 1)
        for sub in range(nsub):
            counts = jnp.zeros((1, LANES), F32)
            ranks = []
            for blk in range(ts // rb):
                rows = slice(sub * ts + blk * rb, sub * ts + (blk + 1) * rb)
                oh = jnp.where(group_lanes, comb_ref[rows, :], 0.0)
                before = _dot(lower, oh.astype(BF16)) + counts
                ranks.append(jnp.sum(before * oh, axis=1, keepdims=True))
                counts = counts + jnp.sum(oh, axis=0, keepdims=True)
            start = jnp.int32(0)
            seg = jnp.zeros((1, LANES), F32)
            for k in range(N_GROUPS):
                n_k = jnp.sum(jnp.where(lane1 == N_EXPERTS + k, counts, 0.0)).astype(jnp.int32)
                seg = jnp.where(lane1 == N_EXPERTS + k, (start * c).astype(F32), seg)
                rng_s[sub * 2 * N_GROUPS + k] = start
                for j in range(-(-ts // c)):
                    start = start + (n_k > j * c).astype(jnp.int32)
                rng_s[sub * 2 * N_GROUPS + N_GROUPS + k] = start
            for blk in range(ts // rb):
                rows = slice(sub * ts + blk * rb, sub * ts + (blk + 1) * rb)
                oh = jnp.where(group_lanes, comb_ref[rows, :], 0.0)
                slot = ranks[blk] + jnp.sum(oh * seg, axis=1, keepdims=True)
                slotc_s[rows, :] = jnp.broadcast_to(slot, (rb, LANES))
        slotr_s[...] = jnp.transpose(slotc_s[...])
        cb = comb_ref[...]
        hi = cb.astype(BF16)
        split_s[...] = jnp.concatenate([hi, (cb - hi.astype(F32)).astype(BF16)], axis=1)
        acc_s[...] = jnp.zeros(acc_s.shape, F32)

    lane = lax.broadcasted_iota(jnp.int32, (nsub * c, LANES), 1)
    first = [rng_s[sub * 2 * N_GROUPS + g] for sub in range(nsub)]
    n_chunks = [rng_s[sub * 2 * N_GROUPS + N_GROUPS + g] - first[sub] for sub in range(nsub)]

    def chunk_body(j, carry):
        bases = [((first[sub] + j) * c).astype(F32) for sub in range(nsub)]
        xs, cw = [], []
        for sub in range(nsub):
            rows = slice(sub * ts, (sub + 1) * ts)
            row_slot = lax.broadcasted_iota(jnp.int32, (c, ts), 0).astype(F32) + bases[sub]
            pc = jnp.where(slotr_s[0:1, rows] == row_slot, 1.0, 0.0).astype(BF16)
            xs.append(_dot(pc, h_ref[rows, :]).astype(BF16))
            cw2 = _dot(pc, split_s[rows, :])
            cw.append(cw2[:, :LANES] + cw2[:, LANES:])
        xs = jnp.concatenate(xs, axis=0)
        cw = jnp.concatenate(cw, axis=0)
        y = jnp.zeros((nsub * c, d), F32)
        for e in range(EXPERTS_PER_GROUP):
            gate = _dot(xs, wg_ref[e])
            hid = gate * jax.nn.sigmoid(gate) * _dot(xs, wu_ref[e])
            ce = jnp.sum(jnp.where(lane == g * EXPERTS_PER_GROUP + e, cw, 0.0), axis=1, keepdims=True)
            y = y + _dot((hid * ce).astype(BF16), wd_ref[e])
        y = y.astype(BF16)
        for sub in range(nsub):
            rows = slice(sub * ts, (sub + 1) * ts)
            col_slot = lax.broadcasted_iota(jnp.int32, (ts, c), 1).astype(F32) + bases[sub]
            pct = jnp.where(jnp.broadcast_to(slotc_s[rows, 0:1], (ts, c)) == col_slot, 1.0, 0.0).astype(BF16)
            acc_s[rows, :] += _dot(pct, y[sub * c:(sub + 1) * c])
        return carry

    lax.fori_loop(0, functools.reduce(jnp.maximum, n_chunks), chunk_body, 0)

    @pl.when(g == N_GROUPS - 1)
    def _():
        o_ref[...] = acc_s[...].astype(BF16)


def _moe(h2, comb, wg, wu, wd):
    n, d = h2.shape
    tm = MOE_ROW_TILE
    row = lambda w: pl.BlockSpec((tm, w), lambda i, g: (i, 0))
    return pl.pallas_call(
        _moe_kernel,
        grid=(n // tm, N_GROUPS),
        in_specs=[row(d), row(LANES),
                  pl.BlockSpec((EXPERTS_PER_GROUP, d, D_EXPERT), lambda i, g: (g, 0, 0)),
                  pl.BlockSpec((EXPERTS_PER_GROUP, d, D_EXPERT), lambda i, g: (g, 0, 0)),
                  pl.BlockSpec((EXPERTS_PER_GROUP, D_EXPERT, d), lambda i, g: (g, 0, 0))],
        out_specs=row(d),
        out_shape=jax.ShapeDtypeStruct((n, d), BF16),
        scratch_shapes=[pltpu.VMEM((tm, LANES), F32), pltpu.VMEM((LANES, tm), F32),
                        pltpu.VMEM((tm, 2 * LANES), BF16), pltpu.VMEM((tm, d), F32),
                        pltpu.SMEM((MOE_SUBTILES * 2 * N_GROUPS,), jnp.int32)],
        compiler_params=pltpu.CompilerParams(dimension_semantics=("arbitrary", "arbitrary"),
                                             vmem_limit_bytes=MOE_VMEM_LIMIT),
        name="moe",
    )(h2, comb, wg, wu, wd)


def _ple_kernel(x_ref, m_ref, p_ref, g_ref, wpg_ref, wple_ref, gf_ref, o_ref, *, last):
    x2 = x_ref[...] + m_ref[...].astype(F32)
    h3 = _rms(x2, g_ref[...]).astype(BF16)
    gate = jax.nn.sigmoid(_dot(h3, wpg_ref[...]))
    x3 = x2 + _dot(p_ref[...].astype(BF16), wple_ref[...]) * gate
    o_ref[...] = _rms(x3, gf_ref[...]) if last else x3


def _ple(x1, moe, p, g, wpg, wple, gf, last):
    n, d = x1.shape
    tm = ROW_TILE
    row = lambda w: pl.BlockSpec((tm, w), lambda i: (i, 0))
    ws = (g, wpg, wple, gf)
    return pl.pallas_call(
        functools.partial(_ple_kernel, last=last),
        grid=(n // tm,),
        in_specs=[row(d), row(d), row(p.shape[1])] + [_full_spec(w.shape) for w in ws],
        out_specs=row(d),
        out_shape=jax.ShapeDtypeStruct((n, d), F32),
        compiler_params=pltpu.CompilerParams(dimension_semantics=("arbitrary",), vmem_limit_bytes=VMEM_LIMIT),
        name="ple",
    )(x1, moe, p, *ws)


def _block_diag(blocks):
    h, r, c = blocks.shape
    eye = jnp.eye(h, dtype=blocks.dtype)
    return (eye[:, None, :, None] * blocks[:, :, None, :]).reshape(h * r, h * c)


def kernel(x, p, attn_norm, w_in, kv_norm, w_uk, w_uv, rel_bias, w_branch_a, w_branch_b, w_out, ffn_norm,
           w_r1, b_r1, w_r2, b_r2, w_gate, w_up, w_down, ple_norm, w_ple_gate, w_ple, final_norm):
    batch, seq, d = x.shape
    n = batch * seq
    depth = w_in.shape[0]
    xf = x.reshape(n, d).astype(F32)
    widths = [WIDTH_A, KV_RANK, IDX_HEADS * IDX_DIM, IDX_DIM, IDX_HEADS, 3 * WIDTH_B, d, d]
    starts = [sum(widths[:k]) for k in range(len(widths))]
    for i in range(depth):
        w_qa, w_ckv, w_qi, w_ki, w_wi, w_qkv, w_ga, w_gb = [
            w_in[i][:, s:s + w].astype(BF16) for s, w in zip(starts, widths)]
        w_kk = jnp.concatenate([w_ki, w_ki], axis=1)
        w_cw = jnp.concatenate([w_ckv, jnp.pad(w_wi, ((0, 0), (0, LANES - IDX_HEADS)))], axis=1)
        wuk_bd = _block_diag(jnp.swapaxes(w_uk[i], 1, 2)).astype(BF16)
        wuv_bd = _block_diag(w_uv[i]).astype(BF16)
        qabs, ckv, ckvt, qidx, kk, widx, qb, kb, vb, sa, sb = _proj(
            xf, attn_norm[i][None].astype(F32), w_qa, wuk_bd, w_cw, kv_norm[i][None].astype(F32),
            w_qi, w_kk, w_qkv, w_ga, w_gb)
        olat = _dsa(rel_bias.astype(F32), qabs, qidx, widx, ckv, ckvt, kk, batch, seq)
        ob = _stick(qb, kb, vb, batch, seq)
        w_r = jnp.concatenate([jnp.transpose(w_r2[i], (1, 0, 2)).reshape(d, N_EXPERTS), w_r1[i]], axis=1)
        w_r = jnp.pad(w_r.astype(F32), ((0, 0), (0, LANES - N_EXPERTS - N_GROUPS)))
        w_r_hi = w_r.astype(BF16)
        w_r_lo = (w_r - w_r_hi.astype(F32)).astype(BF16)
        b_r = jnp.pad(jnp.concatenate([b_r2[i].reshape(-1), b_r1[i]]).astype(F32),
                      (0, LANES - N_EXPERTS - N_GROUPS))[None]
        x1, h2, comb = _merge(xf, olat, ob, sa, sb, wuv_bd, w_branch_a[i].astype(BF16), w_branch_b[i].astype(BF16),
                              w_out[i].astype(BF16), ffn_norm[i][None].astype(F32),
                              jnp.concatenate([w_r_hi, w_r_lo], axis=1), b_r)
        moe = _moe(h2, comb, w_gate[i].astype(BF16), w_up[i].astype(BF16), w_down[i].astype(BF16))
        xf = _ple(x1, moe, p[i].reshape(n, -1).astype(F32), ple_norm[i][None].astype(F32), w_ple_gate[i].astype(BF16),
                  w_ple[i].astype(BF16), final_norm[None].astype(F32), last=(i == depth - 1))
    return xf.reshape(batch, seq, d).astype(x.dtype)
```
